```python
import math
import jax, jax.numpy as jnp
from jax import lax
import numpy as np

D_MODEL = 1024
BATCH = 2
SEQ = 16384
DEPTH = 1
DEC_BATCH = 8
DEC_SEQ = 8192
PAST_LEN = 128

HY_WIDTH = D_MODEL // 2
HY_ORDER = 2
HY_EMB_DIM = 33
HY_BANDS = (HY_EMB_DIM - 1) // 2
HY_FILTER_WIDTH = 64
HY_SHORT = 3
HY_FAST_DECAY = 0.3
HY_SLOW_DECAY = 1.5
HY_DECAY_TARGET = 1e-2
RET_WIDTH = D_MODEL // 2
RET_HEADS = 4
RET_HEAD_DIM = RET_WIDTH // RET_HEADS
RET_CHUNK = 128
ROPE_BASE = 10000.0
IN_WIDTH = 3 * HY_WIDTH + 4 * RET_WIDTH + 2 * D_MODEL
N_EXPERTS = 16
EXPERT_FF = 1408
CAPACITY_FACTOR = 2
EPS = 1e-6

kernel_name = "hyena_retention_ec_moe_encoder"


def _rms_norm(x, g):
    x32 = x.astype(jnp.float32)
    y = x32 * lax.rsqrt(jnp.mean(x32 * x32, axis=-1, keepdims=True) + EPS)
    return (y * g.astype(jnp.float32)).astype(x.dtype)


def _hyena_filter_freq(L, w1, b1, f1, w2, b2, f2, w3):
    f32 = jnp.float32
    t = jnp.linspace(0.0, 1.0, L, dtype=f32)
    ang = 2.0 * math.pi * jnp.arange(L, dtype=f32) / L
    bands = jnp.linspace(1e-4, HY_BANDS - 1, HY_BANDS, dtype=f32)
    phase = ang[:, None] * bands[None, :]
    z = jnp.concatenate([t[:, None], jnp.cos(phase), -jnp.sin(phase)], axis=-1)
    a = jnp.sin(f1.astype(f32) * (z @ w1.astype(f32) + b1.astype(f32)))
    a = jnp.sin(f2.astype(f32) * (a @ w2.astype(f32) + b2.astype(f32)))
    h = (a @ w3.astype(f32)).reshape(L, HY_ORDER, 2, HY_WIDTH)
    deltas = jnp.abs(jnp.linspace(math.log(HY_DECAY_TARGET) / HY_SLOW_DECAY,
                                  math.log(HY_DECAY_TARGET) / HY_FAST_DECAY, HY_WIDTH, dtype=f32))
    h = h * jnp.exp(-t[:, None] * deltas[None, :])[:, None, None, :]
    fwd, bwd = h[:, :, 0], h[:, :, 1]
    k = jnp.concatenate([fwd, jnp.zeros((1, HY_ORDER, HY_WIDTH), f32), bwd[:0:-1]], axis=0)
    k = k / jnp.sum(jnp.abs(k), axis=0, keepdims=True)
    return jnp.fft.rfft(k, axis=0)


def _long_conv(u, kf, bias):
    L = u.shape[1]
    u32 = u.astype(jnp.float32)
    uf = jnp.fft.rfft(u32, n=2 * L, axis=1)
    y = jnp.fft.irfft(uf * kf[None], n=2 * L, axis=1)[:, :L]
    return (y + u32 * bias.astype(jnp.float32)).astype(u.dtype)


def _hyena_branch(z, conv_w, conv_b, kf, skip):
    zp = jnp.pad(z, ((0, 0), (1, 1), (0, 0)))
    zc = zp[:, :-2] * conv_w[0] + zp[:, 1:-1] * conv_w[1] + zp[:, 2:] * conv_w[2] + conv_b
    x1, x2, v = jnp.split(zc, 3, axis=-1)
    y = x1 * _long_conv(v, kf[:, 0], skip[0])
    y = x2 * _long_conv(y, kf[:, 1], skip[1])
    return y


def _rotary(x, L):
    half = RET_HEAD_DIM // 2
    inv_freq = ROPE_BASE ** (-jnp.arange(half, dtype=jnp.float32) / half)
    ang = jnp.arange(L, dtype=jnp.float32)[:, None] * inv_freq[None, :]
    cos = jnp.cos(ang)[:, None, :].astype(x.dtype)
    sin = jnp.sin(ang)[:, None, :].astype(x.dtype)
    xa, xb = x[..., :half], x[..., half:]
    return jnp.concatenate([xa * cos - xb * sin, xb * cos + xa * sin], axis=-1)


def _retention_dir(q, k, v, log_gamma, strict):
    B, nc, C, H, Dh = q.shape
    dt = q.dtype
    pos = jnp.arange(C, dtype=jnp.float32)
    lag = pos[:, None] - pos[None, :]
    keep = (lag > 0) if strict else (lag >= 0)
    decay = jnp.where(keep[None], jnp.exp(log_gamma[:, None, None] * jnp.maximum(lag, 0.0)[None]), 0.0).astype(dt)
    scores = jnp.einsum('bnchd,bnmhd->bnhcm', q, k) * decay[None, None]
    inner = jnp.einsum('bnhcm,bnmhd->bnchd', scores, v)
    k_w = k * jnp.exp(log_gamma[None, :] * (C - 1 - pos)[:, None]).astype(dt)[:, :, None]
    chunk_state = jnp.einsum('bnchk,bnchv->nbhkv', k_w, v)
    carry_decay = jnp.exp(log_gamma * C).astype(dt)[None, :, None, None]

    def step(state, s_chunk):
        return state * carry_decay + s_chunk, state

    _, prev = lax.scan(step, jnp.zeros((B, H, Dh, Dh), dt), chunk_state)
    q_w = q * jnp.exp(log_gamma[None, :] * (pos + 1.0)[:, None]).astype(dt)[:, :, None]
    cross = jnp.einsum('bnchk,nbhkv->bnchv', q_w, prev)
    return inner + cross


def _retention_branch(z, decay_logit):
    B, L, _ = z.shape
    q, k, v, g = jnp.split(z, 4, axis=-1)
    q = _rotary(q.reshape(B, L, RET_HEADS, RET_HEAD_DIM), L)
    k = _rotary(k.reshape(B, L, RET_HEADS, RET_HEAD_DIM), L) * (RET_HEAD_DIM ** -0.5)
    v = v.reshape(B, L, RET_HEADS, RET_HEAD_DIM)
    log_g = jax.nn.log_sigmoid(decay_logit.astype(jnp.float32))
    nc = L // RET_CHUNK
    chunk = lambda a: a.reshape(B, nc, RET_CHUNK, RET_HEADS, RET_HEAD_DIM)
    flip = lambda a: a[:, ::-1]
    fwd = _retention_dir(chunk(q), chunk(k), chunk(v), log_g[0], False)
    bwd = _retention_dir(chunk(flip(q)), chunk(flip(k)), chunk(flip(v)), log_g[1], True)
    o = fwd.reshape(B, L, RET_HEADS, RET_HEAD_DIM) + flip(bwd.reshape(B, L, RET_HEADS, RET_HEAD_DIM))
    o32 = o.astype(jnp.float32)
    o = (o32 * lax.rsqrt(jnp.mean(o32 * o32, axis=-1, keepdims=True) + EPS)).astype(z.dtype)
    return jax.nn.silu(g) * o.reshape(B, L, RET_WIDTH)


def _expert_choice_ffn(h, w_router, w_gate, w_up, w_down):
    B, L, D = h.shape
    n_tok = B * L
    cap = CAPACITY_FACTOR * n_tok // N_EXPERTS
    hf = h.reshape(n_tok, D)
    probs = jax.nn.softmax((hf @ w_router).astype(jnp.float32), axis=-1)
    gates, idx = lax.top_k(probs.T, cap)
    xe = hf[idx]
    hid = jax.nn.silu(jnp.einsum('ecd,edf->ecf', xe, w_gate)) * jnp.einsum('ecd,edf->ecf', xe, w_up)
    ye = jnp.einsum('ecf,efd->ecd', hid, w_down) * gates[..., None].astype(h.dtype)
    y = jnp.zeros((n_tok, D), h.dtype).at[idx.reshape(-1)].add(ye.reshape(-1, D))
    return y.reshape(B, L, D)


def _trunk(x, params):
    (norm1_g, w_in, hy_conv_w, hy_conv_b, hy_w1, hy_b1, hy_freq1, hy_w2, hy_b2, hy_freq2, hy_w3,
     hy_skip, ret_decay_logit, w_hy_out, w_ret_out, w_o, norm2_g, w_router, w_gate, w_up, w_down,
     norm_f_g) = params
    L = x.shape[1]
    for layer in range(DEPTH):
        h = _rms_norm(x, norm1_g[layer])
        z = h @ w_in[layer]
        z_hy = z[..., :3 * HY_WIDTH]
        z_ret = z[..., 3 * HY_WIDTH:3 * HY_WIDTH + 4 * RET_WIDTH]
        g_hy, g_ret = jnp.split(z[..., 3 * HY_WIDTH + 4 * RET_WIDTH:], 2, axis=-1)
        kf = _hyena_filter_freq(L, hy_w1[layer], hy_b1[layer], hy_freq1[layer], hy_w2[layer],
                                hy_b2[layer], hy_freq2[layer], hy_w3[layer])
        y_hy = _hyena_branch(z_hy, hy_conv_w[layer], hy_conv_b[layer], kf, hy_skip[layer]) @ w_hy_out[layer]
        y_ret = _retention_branch(z_ret, ret_decay_logit[layer]) @ w_ret_out[layer]
        merged = jax.nn.sigmoid(g_hy) * y_hy + jax.nn.sigmoid(g_ret) * y_ret
        x = x + merged @ w_o[layer]
        x = x + _expert_choice_ffn(_rms_norm(x, norm2_g[layer]), w_router[layer], w_gate[layer],
                                   w_up[layer], w_down[layer])
    return _rms_norm(x, norm_f_g)


def setup_inputs(seed: int = 0) -> dict:
    key = jax.random.key(seed)
    ks = jax.random.split(key, 32)
    f32 = jnp.float32
    nrm = lambda k, shape, scale: jax.random.normal(k, shape, f32) * scale
    base_logit = jnp.asarray(np.log(2.0 ** (5.0 + np.arange(RET_HEADS)) - 1.0), f32)
    return {
        "x_prompt": nrm(ks[0], (BATCH, SEQ, D_MODEL), 1.0),
        "x_sample": nrm(ks[1], (DEC_BATCH, DEC_SEQ, D_MODEL), 1.0),
        "norm1_g": 1.0 + nrm(ks[2], (DEPTH, D_MODEL), 0.02),
        "w_in": nrm(ks[3], (DEPTH, D_MODEL, IN_WIDTH), D_MODEL ** -0.5),
        "hy_conv_w": nrm(ks[4], (DEPTH, HY_SHORT, 3 * HY_WIDTH), HY_SHORT ** -0.5),
        "hy_conv_b": nrm(ks[5], (DEPTH, 3 * HY_WIDTH), 0.02),
        "hy_w1": nrm(ks[6], (DEPTH, HY_EMB_DIM, HY_FILTER_WIDTH), HY_EMB_DIM ** -0.5),
        "hy_b1": nrm(ks[7], (DEPTH, HY_FILTER_WIDTH), 0.02),
        "hy_freq1": 1.0 + nrm(ks[8], (DEPTH, HY_FILTER_WIDTH), 0.02),
        "hy_w2": nrm(ks[9], (DEPTH, HY_FILTER_WIDTH, HY_FILTER_WIDTH), HY_FILTER_WIDTH ** -0.5),
        "hy_b2": nrm(ks[10], (DEPTH, HY_FILTER_WIDTH), 0.02),
        "hy_freq2": 1.0 + nrm(ks[11], (DEPTH, HY_FILTER_WIDTH), 0.02),
        "hy_w3": nrm(ks[12], (DEPTH, HY_FILTER_WIDTH, HY_ORDER * 2 * HY_WIDTH), HY_FILTER_WIDTH ** -0.5),
        "hy_skip": nrm(ks[13], (DEPTH, HY_ORDER, HY_WIDTH), 1.0),
        "ret_decay_logit": base_logit[None, None, :] + nrm(ks[14], (DEPTH, 2, RET_HEADS), 0.01),
        "w_hy_out": nrm(ks[15], (DEPTH, HY_WIDTH, D_MODEL), HY_WIDTH ** -0.5),
        "w_ret_out": nrm(ks[16], (DEPTH, RET_WIDTH, D_MODEL), RET_WIDTH ** -0.5),
        "w_o": nrm(ks[17], (DEPTH, D_MODEL, D_MODEL), D_MODEL ** -0.5),
        "norm2_g": 1.0 + nrm(ks[18], (DEPTH, D_MODEL), 0.02),
        "w_router": nrm(ks[19], (DEPTH, D_MODEL, N_EXPERTS), D_MODEL ** -0.5),
        "w_gate": nrm(ks[20], (DEPTH, N_EXPERTS, D_MODEL, EXPERT_FF), D_MODEL ** -0.5),
        "w_up": nrm(ks[21], (DEPTH, N_EXPERTS, D_MODEL, EXPERT_FF), D_MODEL ** -0.5),
        "w_down": nrm(ks[22], (DEPTH, N_EXPERTS, EXPERT_FF, D_MODEL), EXPERT_FF ** -0.5),
        "norm_f_g": 1.0 + nrm(ks[23], (D_MODEL,), 0.02),
    }


def reference(x_prompt, x_sample, norm1_g, w_in, hy_conv_w, hy_conv_b, hy_w1, hy_b1, hy_freq1, hy_w2,
              hy_b2, hy_freq2, hy_w3, hy_skip, ret_decay_logit, w_hy_out, w_ret_out, w_o, norm2_g,
              w_router, w_gate, w_up, w_down, norm_f_g):
    params = (norm1_g, w_in, hy_conv_w, hy_conv_b, hy_w1, hy_b1, hy_freq1, hy_w2, hy_b2, hy_freq2, hy_w3,
              hy_skip, ret_decay_logit, w_hy_out, w_ret_out, w_o, norm2_g, w_router, w_gate, w_up, w_down,
              norm_f_g)
    y_prompt = _trunk(x_prompt, params)
    y_sample = _trunk(x_sample, params)
    return (y_prompt, y_sample)
```

```python
import functools
import math

import numpy as np
import jax
import jax.numpy as jnp
from jax import lax
from jax.experimental import pallas as pl
from jax.experimental.pallas import tpu as pltpu

F32 = jnp.float32
BF16 = jnp.bfloat16
I32 = jnp.int32

D_MODEL = 1024
HY_WIDTH = 512
HY_ORDER = 2
HY_BANDS = 16
HY_FILTER_WIDTH = 64
HY_FAST_DECAY = 0.3
HY_SLOW_DECAY = 1.5
HY_DECAY_TARGET = 1e-2
RET_WIDTH = 512
RET_HEADS = 4
RET_HEAD_DIM = 128
RET_CHUNK = 128
ROPE_BASE = 10000.0
N_EXPERTS = 16
EXPERT_FF = 1408
CAPACITY_FACTOR = 2
EPS = 1e-6

LANES = 128
EMB_PAD = 128
DFT_N1_MAX = 128
COMBINE_ALIGN = 16
COMBINE_WIN = LANES + COMBINE_ALIGN


def _cparams(sem, vmem_mb=48):
    return pltpu.CompilerParams(dimension_semantics=sem, vmem_limit_bytes=vmem_mb * 1024 * 1024)


def _dot(a, b):
    return jnp.dot(a, b, preferred_element_type=F32)


def _dot_nt(a, b):
    return lax.dot_general(a, b, (((1,), (1,)), ((), ())), preferred_element_type=F32)


def _dot_tn(a, b):
    return lax.dot_general(a, b, (((0,), (0,)), ((), ())), preferred_element_type=F32)


def _split(a):
    hi = a.astype(BF16)
    lo = (a - hi.astype(F32)).astype(BF16)
    return hi, lo


def _dot_const(m_hi, m_lo, x, passes):
    xh = x.astype(BF16)
    r = _dot(m_hi, xh)
    if passes >= 3:
        xl = (x - xh.astype(F32)).astype(BF16)
        r = r + _dot(m_lo, xh) + _dot(m_hi, xl)
    return r


def _dot3(a, b):
    ah, al = _split(a)
    bh, bl = _split(b)
    return _dot(ah, bh) + _dot(al, bh) + _dot(ah, bl)


def _inproj_body(x_ref, g_ref, w_ref, zhy_ref, zret_ref, zg_ref):
    x = x_ref[...]
    ms = jnp.mean(x * x, axis=-1, keepdims=True)
    h = (x * lax.rsqrt(ms + EPS) * g_ref[...]).astype(BF16)
    n_hy = zhy_ref.shape[1]
    n_ret = zret_ref.shape[1]
    zhy_ref[...] = _dot(h, w_ref[:, :n_hy])
    zret_ref[...] = _dot(h, w_ref[:, n_hy:n_hy + n_ret]).astype(BF16)
    zg_ref[...] = _dot(h, w_ref[:, n_hy + n_ret:]).astype(BF16)


def _inproj(x2, g, w_bf, tm=256):
    n_tok = x2.shape[0]
    n_hy, n_ret, n_g = 3 * HY_WIDTH, 4 * RET_WIDTH, 2 * D_MODEL
    return pl.pallas_call(
        _inproj_body,
        grid=(n_tok // tm,),
        in_specs=[pl.BlockSpec((tm, D_MODEL), lambda i: (i, 0)),
                  pl.BlockSpec((1, D_MODEL), lambda i: (0, 0)),
                  pl.BlockSpec((D_MODEL, n_hy + n_ret + n_g), lambda i: (0, 0))],
        out_specs=[pl.BlockSpec((tm, n_hy), lambda i: (i, 0)),
                   pl.BlockSpec((tm, n_ret), lambda i: (i, 0)),
                   pl.BlockSpec((tm, n_g), lambda i: (i, 0))],
        out_shape=[jax.ShapeDtypeStruct((n_tok, n_hy), F32),
                   jax.ShapeDtypeStruct((n_tok, n_ret), BF16),
                   jax.ShapeDtypeStruct((n_tok, n_g), BF16)],
        compiler_params=_cparams(("parallel",)),
        name="inproj",
    )(x2, g, w_bf)


def _shortconv_body(z_ref, zp_ref, zn_ref, w_ref, b_ref, o_ref):
    tl = z_ref.shape[0]
    i = pl.program_id(1)
    z = z_ref[...]
    prev_row = jnp.where(i > 0, zp_ref[7:8, :], 0.0)
    next_row = jnp.where(i < pl.num_programs(1) - 1, zn_ref[0:1, :], 0.0)
    row = lax.broadcasted_iota(I32, (tl, 1), 0)
    zm1 = jnp.where(row == 0, prev_row, pltpu.roll(z, 1, axis=0))
    zp1 = jnp.where(row == tl - 1, next_row, pltpu.roll(z, tl - 1, axis=0))
    o_ref[...] = zm1 * w_ref[0:1, :] + z * w_ref[1:2, :] + zp1 * w_ref[2:3, :] + b_ref[...]


def _shortconv(z3, w, b, tl=256):
    bsz, seq, width = z3.shape
    tl = min(tl, seq)
    nb8 = seq // 8
    r8 = tl // 8
    return pl.pallas_call(
        _shortconv_body,
        grid=(bsz, seq // tl),
        in_specs=[pl.BlockSpec((None, tl, width), lambda b_, i: (b_, i, 0)),
                  pl.BlockSpec((None, 8, width), lambda b_, i: (b_, jnp.maximum(i * r8 - 1, 0), 0)),
                  pl.BlockSpec((None, 8, width), lambda b_, i: (b_, jnp.minimum((i + 1) * r8, nb8 - 1), 0)),
                  pl.BlockSpec((3, width), lambda b_, i: (0, 0)),
                  pl.BlockSpec((1, width), lambda b_, i: (0, 0))],
        out_specs=pl.BlockSpec((None, tl, width), lambda b_, i: (b_, i, 0)),
        out_shape=jax.ShapeDtypeStruct((bsz, seq, width), F32),
        compiler_params=_cparams(("parallel", "parallel")),
        name="shortconv",
    )(z3, z3, z3, w, b)


def _fft_dims(seq):
    n = 2 * seq
    n1 = min(DFT_N1_MAX, 1 << (n.bit_length() // 2))
    n2 = n // n1
    assert n1 * n2 == n and n1 % 2 == 0
    return n, n1, n2


def _hi_lo_const(m):
    m = np.asarray(m, np.float64)
    hi = jnp.asarray(m, F32).astype(BF16)
    lo = (jnp.asarray(m, F32) - hi.astype(F32)).astype(BF16)
    return hi, lo


def _fft_tables(seq):
    n, n1, n2 = _fft_dims(seq)
    r1 = n1 // 2
    a1 = 2.0 * np.pi * np.outer(np.arange(n1), np.arange(n1)) / n1
    fr, fi = np.cos(a1), -np.sin(a1)
    mat_a = np.block([[fr[:, :r1], -fi[:, :r1]], [fi[:, :r1], fr[:, :r1]]])
    mat_a_real = np.concatenate([fr, fi], axis=0)
    mat_c = np.block([[fr[:r1], fi[:r1]], [-fi[:r1], fr[:r1]]])
    a2 = 2.0 * np.pi * np.outer(np.arange(n2), np.arange(n2)) / n2
    gr, gi = np.cos(a2), -np.sin(a2)
    mat_f = np.block([[gr, -gi], [gi, gr]])
    mat_i = np.block([[gr, gi], [-gi, gr]])
    at = 2.0 * np.pi * (np.outer(np.arange(n1), np.arange(n2)) % n) / n
    tw_r = jnp.broadcast_to(jnp.asarray(np.cos(at), F32)[:, :, None], (n1, n2, LANES))
    tw_i = jnp.broadcast_to(jnp.asarray(-np.sin(at), F32)[:, :, None], (n1, n2, LANES))
    return dict(n=n, n1=n1, n2=n2, mat_a=_hi_lo_const(mat_a), mat_a_real=_hi_lo_const(mat_a_real),
                mat_c=_hi_lo_const(mat_c), mat_f=_hi_lo_const(mat_f), mat_i=_hi_lo_const(mat_i),
                tw_r=tw_r, tw_i=tw_i)


def _lane_tile(t, width):
    return jnp.concatenate([t] * (width // LANES), axis=1) if width > LANES else t


def _filter_body(z_ref, w1_ref, b1_ref, f1_ref, w2_ref, b2_ref, f2_ref, w3_ref, dl_ref, h_ref, s_ref, *, seq):
    i = pl.program_id(0)
    tl = z_ref.shape[0]
    z = z_ref[...]
    a = jnp.sin(f1_ref[...] * (_dot3(z, w1_ref[...]) + b1_ref[...]))
    a = jnp.sin(f2_ref[...] * (_dot3(a, w2_ref[...]) + b2_ref[...]))
    h = _dot3(a, w3_ref[...])
    win = jnp.exp(-z[:, 0:1] * dl_ref[...])
    h = h * jnp.concatenate([win] * HY_ORDER, axis=1)
    row = i * tl + lax.broadcasted_iota(I32, (tl, 1), 0)
    h = jnp.where(row == seq, 0.0, h)
    h_ref[...] = h

    @pl.when(i == 0)
    def _():
        s_ref[...] = jnp.zeros_like(s_ref)

    s_ref[...] += jnp.broadcast_to(jnp.sum(jnp.abs(h), axis=0, keepdims=True), s_ref.shape)


def _hyena_filter_time(seq, w1, b1, f1, w2, b2, f2, w3, tl=512):
    n = 2 * seq
    tl = min(tl, seq)
    idx = jnp.arange(n, dtype=I32)
    p = jnp.where(idx < seq, idx, n - idx)
    p = jnp.minimum(p, seq - 1)
    t = jnp.linspace(0.0, 1.0, seq, dtype=F32)[p]
    ang = (2.0 * math.pi * jnp.arange(seq, dtype=F32) / seq)[p]
    bands = jnp.linspace(1e-4, HY_BANDS - 1, HY_BANDS, dtype=F32)
    phase = ang[:, None] * bands[None, :]
    emb = jnp.concatenate([t[:, None], jnp.cos(phase), -jnp.sin(phase)], axis=-1)
    emb = jnp.pad(emb, ((0, 0), (0, EMB_PAD - emb.shape[1])))
    w1p = jnp.pad(w1.astype(F32), ((0, EMB_PAD - w1.shape[0]), (0, 0)))
    w3d = w3.astype(F32).reshape(HY_FILTER_WIDTH, HY_ORDER, 2, HY_WIDTH).transpose(2, 0, 1, 3)
    w3d = w3d.reshape(2, HY_FILTER_WIDTH, HY_ORDER * HY_WIDTH)
    deltas = jnp.abs(jnp.linspace(math.log(HY_DECAY_TARGET) / HY_SLOW_DECAY,
                                  math.log(HY_DECAY_TARGET) / HY_FAST_DECAY, HY_WIDTH, dtype=F32))[None, :]
    fw = HY_FILTER_WIDTH
    nblk_half = seq // tl
    const = lambda i: (0, 0)
    return pl.pallas_call(
        functools.partial(_filter_body, seq=seq),
        grid=(n // tl,),
        in_specs=[pl.BlockSpec((tl, EMB_PAD), lambda i: (i, 0)),
                  pl.BlockSpec((EMB_PAD, fw), const), pl.BlockSpec((1, fw), const), pl.BlockSpec((1, fw), const),
                  pl.BlockSpec((fw, fw), const), pl.BlockSpec((1, fw), const), pl.BlockSpec((1, fw), const),
                  pl.BlockSpec((None, fw, HY_ORDER * HY_WIDTH), lambda i: (i // nblk_half, 0, 0)),
                  pl.BlockSpec((1, HY_WIDTH), const)],
        out_specs=[pl.BlockSpec((tl, HY_ORDER * HY_WIDTH), lambda i: (i, 0)),
                   pl.BlockSpec((8, HY_ORDER * HY_WIDTH), const)],
        out_shape=[jax.ShapeDtypeStruct((n, HY_ORDER * HY_WIDTH), F32),
                   jax.ShapeDtypeStruct((8, HY_ORDER * HY_WIDTH), F32)],
        compiler_params=_cparams(("arbitrary",)),
        name="hyena_filter",
    )(emb, w1p, b1.astype(F32)[None], f1.astype(F32)[None], w2.astype(F32), b2.astype(F32)[None],
      f2.astype(F32)[None], w3d, deltas)


def _stage_a_real_body(h_ref, mh_ref, ml_ref, a_ref, *, passes):
    n1 = h_ref.shape[0]
    for j in range(h_ref.shape[1]):
        r = _dot_const(mh_ref[...], ml_ref[...], h_ref[:, j, :], passes)
        a_ref[0, :, j, :] = r[:n1]
        a_ref[1, :, j, :] = r[n1:]


def _stage_a_real(h, tabs, passes, n2c=8, cw=512):
    n1, n2 = tabs["n1"], tabs["n2"]
    c = h.shape[1]
    h3 = h.reshape(n1, n2, c)
    mh, ml = tabs["mat_a_real"]
    return pl.pallas_call(
        functools.partial(_stage_a_real_body, passes=passes),
        grid=(n2 // n2c, c // cw),
        in_specs=[pl.BlockSpec((n1, n2c, cw), lambda j, k: (0, j, k)),
                  pl.BlockSpec(mh.shape, lambda j, k: (0, 0)),
                  pl.BlockSpec(ml.shape, lambda j, k: (0, 0))],
        out_specs=pl.BlockSpec((2, n1, n2c, cw), lambda j, k: (0, 0, j, k)),
        out_shape=jax.ShapeDtypeStruct((2, n1, n2, c), F32),
        compiler_params=_cparams(("parallel", "parallel")),
        name="dft_a_filter",
    )(h3, mh, ml)


def _stage_a_body(u_ref, mh_ref, ml_ref, a_ref, *, passes):
    n1 = a_ref.shape[1]
    for j in range(u_ref.shape[2]):
        x = jnp.concatenate([u_ref[0, :, j, :], u_ref[1, :, j, :]], axis=0)
        r = _dot_const(mh_ref[...], ml_ref[...], x, passes)
        a_ref[0, :, j, :] = r[:n1]
        a_ref[1, :, j, :] = r[n1:]


def _stage_a(u5, col, tabs, passes, n2c=8):
    n1, n2 = tabs["n1"], tabs["n2"]
    npair, _, r1, _, _ = u5.shape
    cw = HY_WIDTH
    mh, ml = tabs["mat_a"]
    return pl.pallas_call(
        functools.partial(_stage_a_body, passes=passes),
        grid=(npair, n2 // n2c),
        in_specs=[pl.BlockSpec((None, 2, r1, n2c, cw), lambda p, j: (p, 0, 0, j, col)),
                  pl.BlockSpec(mh.shape, lambda p, j: (0, 0)),
                  pl.BlockSpec(ml.shape, lambda p, j: (0, 0))],
        out_specs=pl.BlockSpec((None, 2, n1, n2c, cw), lambda p, j: (p, 0, 0, j, 0)),
        out_shape=jax.ShapeDtypeStruct((npair, 2, n1, n2, cw), F32),
        compiler_params=_cparams(("parallel", "parallel")),
        name="dft_a",
    )(u5, mh, ml)


def _twiddle(ar, ai, tr, ti, conj):
    if conj:
        return ar * tr + ai * ti, ai * tr - ar * ti
    return ar * tr - ai * ti, ar * ti + ai * tr


def _stage_b_filter_body(a_ref, tr_ref, ti_ref, gh_ref, gl_ref, s_ref, kr_ref, ki_ref, *, passes, n):
    n2, c = a_ref.shape[1], a_ref.shape[2]
    tr = _lane_tile(tr_ref[...], c)
    ti = _lane_tile(ti_ref[...], c)
    br, bi = _twiddle(a_ref[0], a_ref[1], tr, ti, False)
    d = _dot_const(gh_ref[...], gl_ref[...], jnp.concatenate([br, bi], axis=0), passes)
    scale = 1.0 / (s_ref[0:1, :] * float(n))
    kr_ref[...] = d[:n2] * scale
    ki_ref[...] = d[n2:] * scale


def _stage_b_filter(a, s, tabs, passes):
    n, n1, n2 = tabs["n"], tabs["n1"], tabs["n2"]
    c = a.shape[-1]
    gh, gl = tabs["mat_f"]
    return pl.pallas_call(
        functools.partial(_stage_b_filter_body, passes=passes, n=n),
        grid=(n1,),
        in_specs=[pl.BlockSpec((2, None, n2, c), lambda k: (0, k, 0, 0)),
                  pl.BlockSpec((None, n2, LANES), lambda k: (k, 0, 0)),
                  pl.BlockSpec((None, n2, LANES), lambda k: (k, 0, 0)),
                  pl.BlockSpec(gh.shape, lambda k: (0, 0)),
                  pl.BlockSpec(gl.shape, lambda k: (0, 0)),
                  pl.BlockSpec(s.shape, lambda k: (0, 0))],
        out_specs=[pl.BlockSpec((None, n2, c), lambda k: (k, 0, 0)),
                   pl.BlockSpec((None, n2, c), lambda k: (k, 0, 0))],
        out_shape=[jax.ShapeDtypeStruct((n1, n2, c), F32), jax.ShapeDtypeStruct((n1, n2, c), F32)],
        compiler_params=_cparams(("parallel",)),
        name="dft_b_filter",
    )(a, tabs["tw_r"], tabs["tw_i"], gh, gl, s)


def _stage_b_body(a_ref, kr_ref, ki_ref, tr_ref, ti_ref, gfh_ref, gfl_ref, gih_ref, gil_ref, c_ref, *, passes):
    n2, c = a_ref.shape[1], a_ref.shape[2]
    tr = _lane_tile(tr_ref[...], c)
    ti = _lane_tile(ti_ref[...], c)
    br, bi = _twiddle(a_ref[0], a_ref[1], tr, ti, False)
    d = _dot_const(gfh_ref[...], gfl_ref[...], jnp.concatenate([br, bi], axis=0), passes)
    dr, di = d[:n2], d[n2:]
    kr, ki = kr_ref[...], ki_ref[...]
    yr = dr * kr - di * ki
    yi = dr * ki + di * kr
    e = _dot_const(gih_ref[...], gil_ref[...], jnp.concatenate([yr, yi], axis=0), passes)
    cr, ci = _twiddle(e[:n2], e[n2:], tr, ti, True)
    c_ref[0] = cr
    c_ref[1] = ci


def _stage_b(a, kr, ki, order, tabs, passes):
    n1, n2 = tabs["n1"], tabs["n2"]
    npair = a.shape[0]
    cw = HY_WIDTH
    gfh, gfl = tabs["mat_f"]
    gih, gil = tabs["mat_i"]
    const = lambda k, p: (0, 0)
    return pl.pallas_call(
        functools.partial(_stage_b_body, passes=passes),
        grid=(n1, npair),
        in_specs=[pl.BlockSpec((None, 2, None, n2, cw), lambda k, p: (p, 0, k, 0, 0)),
                  pl.BlockSpec((None, n2, cw), lambda k, p: (k, 0, order)),
                  pl.BlockSpec((None, n2, cw), lambda k, p: (k, 0, order)),
                  pl.BlockSpec((None, n2, LANES), lambda k, p: (k, 0, 0)),
                  pl.BlockSpec((None, n2, LANES), lambda k, p: (k, 0, 0)),
                  pl.BlockSpec(gfh.shape, const), pl.BlockSpec(gfl.shape, const),
                  pl.BlockSpec(gih.shape, const), pl.BlockSpec(gil.shape, const)],
        out_specs=pl.BlockSpec((None, 2, None, n2, cw), lambda k, p: (p, 0, k, 0, 0)),
        out_shape=jax.ShapeDtypeStruct(a.shape, F32),
        compiler_params=_cparams(("parallel", "parallel")),
        name="dft_b",
    )(a, kr, ki, tabs["tw_r"], tabs["tw_i"], gfh, gfl, gih, gil)


def _stage_c_body(c_ref, u_ref, g_ref, skip_ref, mch_ref, mcl_ref, mah_ref, mal_ref, y_ref, *rest, passes, fuse_a):
    r1 = u_ref.shape[1]
    n1 = 2 * r1
    for j in range(c_ref.shape[2]):
        cc = jnp.concatenate([c_ref[0, :, j, :], c_ref[1, :, j, :]], axis=0)
        y = _dot_const(mch_ref[...], mcl_ref[...], cc, passes)
        u = jnp.concatenate([u_ref[0, :, j, :], u_ref[1, :, j, :]], axis=0)
        g = jnp.concatenate([g_ref[0, :, j, :], g_ref[1, :, j, :]], axis=0)
        yo = g * (y + u * skip_ref[...])
        y_ref[0, :, j, :] = yo[:r1].astype(y_ref.dtype)
        y_ref[1, :, j, :] = yo[r1:].astype(y_ref.dtype)
        if fuse_a:
            a_ref = rest[0]
            r = _dot_const(mah_ref[...], mal_ref[...], yo, passes)
            a_ref[0, :, j, :] = r[:n1]
            a_ref[1, :, j, :] = r[n1:]


def _stage_c(c, u5, u_col, g5, g_col, skip, tabs, passes, fuse_a, out_dtype, n2c=8):
    n1, n2 = tabs["n1"], tabs["n2"]
    npair, _, r1, _, _ = u5.shape
    cw = HY_WIDTH
    mch, mcl = tabs["mat_c"]
    mah, mal = tabs["mat_a"]
    const = lambda p, j: (0, 0)
    out_shape = [jax.ShapeDtypeStruct((npair, 2, r1, n2, cw), out_dtype)]
    out_specs = [pl.BlockSpec((None, 2, r1, n2c, cw), lambda p, j: (p, 0, 0, j, 0))]
    if fuse_a:
        out_shape.append(jax.ShapeDtypeStruct((npair, 2, n1, n2, cw), F32))
        out_specs.append(pl.BlockSpec((None, 2, n1, n2c, cw), lambda p, j: (p, 0, 0, j, 0)))
    return pl.pallas_call(
        functools.partial(_stage_c_body, passes=passes, fuse_a=fuse_a),
        grid=(npair, n2 // n2c),
        in_specs=[pl.BlockSpec((None, 2, n1, n2c, cw), lambda p, j: (p, 0, 0, j, 0)),
                  pl.BlockSpec((None, 2, r1, n2c, cw), lambda p, j: (p, 0, 0, j, u_col)),
                  pl.BlockSpec((None, 2, r1, n2c, cw), lambda p, j: (p, 0, 0, j, g_col)),
                  pl.BlockSpec((1, cw), const),
                  pl.BlockSpec(mch.shape, const), pl.BlockSpec(mcl.shape, const),
                  pl.BlockSpec(mah.shape, const), pl.BlockSpec(mal.shape, const)],
        out_specs=out_specs,
        out_shape=out_shape,
        compiler_params=_cparams(("parallel", "parallel")),
        name="dft_c",
    )(c, u5, g5, skip, mch, mcl, mah, mal)


def _hyena(zc3, kr, ki, skip, tabs, passes):
    bsz, seq, _ = zc3.shape
    n1, n2 = tabs["n1"], tabs["n2"]
    r1 = n1 // 2
    z5 = zc3.reshape(bsz // 2, 2, r1, n2, 3 * HY_WIDTH)
    a = _stage_a(z5, 2, tabs, passes)
    c = _stage_b(a, kr, ki, 0, tabs, passes)
    y1, a2 = _stage_c(c, z5, 2, z5, 0, skip[0:1].astype(F32), tabs, passes, True, F32)
    c2 = _stage_b(a2, kr, ki, 1, tabs, passes)
    (y2,) = _stage_c(c2, y1, 0, z5, 1, skip[1:2].astype(F32), tabs, passes, False, BF16)
    return y2.reshape(bsz * seq, HY_WIDTH)


def _retention_body(sc_ref, qf_ref, kf_ref, vf_ref, cf_ref, sf_ref, qb_ref, kb_ref, vb_ref, cb_ref, sb_ref,
                    of_ref, ob_ref, state, dmat, qwt, kwt):
    ch = RET_CHUNK
    dh = RET_HEAD_DIM

    @pl.when(pl.program_id(1) == 0)
    def _():
        state[...] = jnp.zeros_like(state)
        ci = lax.broadcasted_iota(I32, (ch, ch), 0).astype(F32)
        mi = lax.broadcasted_iota(I32, (ch, ch), 1).astype(F32)
        lag = ci - mi
        for d in range(2):
            for h in range(RET_HEADS):
                lg = sc_ref[d * RET_HEADS + h]
                if d == 0:
                    dmat[d, h] = jnp.where(lag >= 0, jnp.exp(lg * jnp.maximum(lag, 0.0)), 0.0)
                    qwt[d, h] = jnp.exp(lg * (ci + 1.0))
                    kwt[d, h] = jnp.exp(lg * (ch - 1.0 - ci))
                else:
                    dmat[d, h] = jnp.where(lag < 0, jnp.exp(lg * jnp.maximum(-lag, 0.0)), 0.0)
                    qwt[d, h] = jnp.exp(lg * (ch - ci))
                    kwt[d, h] = jnp.exp(lg * ci)

    scale = RET_HEAD_DIM ** -0.5
    for d, (q_ref, k_ref, v_ref, c_ref, s_ref, o_ref) in enumerate(
            ((qf_ref, kf_ref, vf_ref, cf_ref, sf_ref, of_ref), (qb_ref, kb_ref, vb_ref, cb_ref, sb_ref, ob_ref))):
        cosf = c_ref[...]
        sinf = s_ref[...]
        for h in range(RET_HEADS):
            sl = slice(h * dh, (h + 1) * dh)
            q = q_ref[:, sl].astype(F32)
            k = k_ref[:, sl].astype(F32)
            v = v_ref[:, sl]
            q = q * cosf + pltpu.roll(q, dh // 2, axis=1) * sinf
            k = (k * cosf + pltpu.roll(k, dh // 2, axis=1) * sinf) * scale
            s = _dot_nt(q.astype(BF16), k.astype(BF16)) * dmat[d, h]
            inner = _dot(s.astype(BF16), v)
            st = state[d, h]
            cross = _dot((q * qwt[d, h]).astype(BF16), st.astype(BF16))
            o_ref[:, sl] = inner + cross
            cdec = sc_ref[2 * RET_HEADS + d * RET_HEADS + h]
            state[d, h] = st * cdec + _dot_tn((k * kwt[d, h]).astype(BF16), v)


def _retention(zret3, decay_logit):
    bsz, seq, _ = zret3.shape
    ch = RET_CHUNK
    nc = seq // ch
    half = RET_HEAD_DIM // 2
    inv_freq = ROPE_BASE ** (-jnp.arange(half, dtype=F32) / half)
    ang = jnp.arange(seq, dtype=F32)[:, None] * inv_freq[None, :]
    cosf = jnp.concatenate([jnp.cos(ang), jnp.cos(ang)], axis=1)
    sinf = jnp.concatenate([-jnp.sin(ang), jnp.sin(ang)], axis=1)
    log_g = jax.nn.log_sigmoid(decay_logit.astype(F32)).reshape(-1)
    scal = jnp.concatenate([log_g, jnp.exp(log_g * ch)])
    w = RET_WIDTH
    fwd = lambda col: pl.BlockSpec((None, ch, w), lambda b, n, sc: (b, n, col))
    bwd = lambda col: pl.BlockSpec((None, ch, w), lambda b, n, sc: (b, nc - 1 - n, col))
    rope_f = pl.BlockSpec((ch, RET_HEAD_DIM), lambda b, n, sc: (n, 0))
    rope_b = pl.BlockSpec((ch, RET_HEAD_DIM), lambda b, n, sc: (nc - 1 - n, 0))
    grid_spec = pltpu.PrefetchScalarGridSpec(
        num_scalar_prefetch=1,
        grid=(bsz, nc),
        in_specs=[fwd(0), fwd(1), fwd(2), rope_f, rope_f, bwd(0), bwd(1), bwd(2), rope_b, rope_b],
        out_specs=[pl.BlockSpec((None, ch, w), lambda b, n, sc: (b, n, 0)),
                   pl.BlockSpec((None, ch, w), lambda b, n, sc: (b, nc - 1 - n, 0))],
        scratch_shapes=[pltpu.VMEM((2, RET_HEADS, RET_HEAD_DIM, RET_HEAD_DIM), F32),
                        pltpu.VMEM((2, RET_HEADS, ch, ch), F32),
                        pltpu.VMEM((2, RET_HEADS, ch, RET_HEAD_DIM), F32),
                        pltpu.VMEM((2, RET_HEADS, ch, RET_HEAD_DIM), F32)],
    )
    return pl.pallas_call(
        _retention_body,
        grid_spec=grid_spec,
        out_shape=[jax.ShapeDtypeStruct((bsz, seq, w), F32), jax.ShapeDtypeStruct((bsz, seq, w), F32)],
        compiler_params=_cparams(("arbitrary", "arbitrary")),
        name="retention",
    )(scal, zret3, zret3, zret3, cosf, sinf, zret3, zret3, zret3, cosf, sinf)


def _outproj_body(x_ref, yhy_ref, of_ref, ob_ref, gr_ref, ghy_ref, gret_ref, whyo_ref, wreto_ref, wo_ref,
                  n2g_ref, wrt_ref, x1_ref, h2_ref, pt_ref):
    dh = RET_HEAD_DIM
    o = of_ref[...] + ob_ref[...]
    parts = []
    for h in range(RET_HEADS):
        oh = o[:, h * dh:(h + 1) * dh]
        parts.append(oh * lax.rsqrt(jnp.mean(oh * oh, axis=-1, keepdims=True) + EPS))
    on = jnp.concatenate(parts, axis=1)
    gr = gr_ref[...].astype(F32)
    ret = (gr * jax.nn.sigmoid(gr)) * on
    y_ret = _dot(ret.astype(BF16), wreto_ref[...])
    y_hy = _dot(yhy_ref[...], whyo_ref[...])
    merged = jax.nn.sigmoid(ghy_ref[...].astype(F32)) * y_hy + jax.nn.sigmoid(gret_ref[...].astype(F32)) * y_ret
    x1 = x_ref[...] + _dot(merged.astype(BF16), wo_ref[...])
    x1_ref[...] = x1
    h2 = x1 * lax.rsqrt(jnp.mean(x1 * x1, axis=-1, keepdims=True) + EPS) * n2g_ref[...]
    h2_ref[...] = h2
    logits = _dot_nt(wrt_ref[...], h2.astype(BF16))
    m = jnp.max(logits, axis=0, keepdims=True)
    e = jnp.exp(logits - m)
    pt_ref[...] = e / jnp.sum(e, axis=0, keepdims=True)


def _outproj(x2, yhy, o_f, o_b, zret, zg, whyo, wreto, wo, n2g, wrt, tm=256):
    n_tok = x2.shape[0]
    d = D_MODEL
    row = lambda w, col=0: pl.BlockSpec((tm, w), lambda i: (i, col))
    const = lambda shape: pl.BlockSpec(shape, lambda i: (0, 0))
    return pl.pallas_call(
        _outproj_body,
        grid=(n_tok // tm,),
        in_specs=[row(d), row(HY_WIDTH), row(RET_WIDTH), row(RET_WIDTH), row(RET_WIDTH, 3), row(d, 0), row(d, 1),
                  const((HY_WIDTH, d)), const((RET_WIDTH, d)), const((d, d)), const((1, d)), const((N_EXPERTS, d))],
        out_specs=[row(d), row(d), pl.BlockSpec((N_EXPERTS, tm), lambda i: (0, i))],
        out_shape=[jax.ShapeDtypeStruct((n_tok, d), F32), jax.ShapeDtypeStruct((n_tok, d), F32),
                   jax.ShapeDtypeStruct((N_EXPERTS, n_tok), F32)],
        compiler_params=_cparams(("parallel",)),
        name="outproj_router",
    )(x2, yhy, o_f, o_b, zret, zg, zg, whyo, wreto, wo, n2g, wrt)


def _select_body(p_ref, upper_ref, lower_ref, lowinc_ref, pos_ref, idx_ref, lo_ref, *, cap):
    rows = p_ref.shape[0]
    p = p_ref[...]
    bits = lax.bitcast_convert_type(p, I32)

    def bit_step(i, thr):
        cand = thr | jnp.left_shift(jnp.int32(1), 30 - i)
        cnt = jnp.sum((bits >= cand).astype(F32), keepdims=True)
        return jnp.where(cnt >= cap, cand, thr)

    thr = lax.fori_loop(0, 31, bit_step, jnp.zeros((1, 1), I32))
    gt = bits > thr
    eq = bits == thr
    need = cap - jnp.sum(gt.astype(F32), keepdims=True)

    def prefix(mask_f):
        incl = _dot(mask_f.astype(BF16), upper_ref[...])
        tot = jnp.broadcast_to(incl[:, LANES - 1:LANES], incl.shape)
        base = _dot(lower_ref[...], tot.astype(BF16))
        return incl, base, tot

    eq_f = eq.astype(F32)
    incl_e, base_e, _ = prefix(eq_f)
    sel = gt | (eq & (base_e + incl_e - eq_f < need))
    sel_f = sel.astype(F32)
    incl, base, tot = prefix(sel_f)
    pos_ref[...] = jnp.where(sel, (base + incl - 1.0).astype(I32), -1)
    lo_ref[...] = base.astype(I32)

    rowend = base + tot
    incl_t = _dot_nt(lowinc_ref[...], sel_f.astype(BF16)).astype(BF16)
    r_iota = lax.broadcasted_iota(I32, (rows, LANES), 0).astype(F32)

    def slot_tile(ts, carry):
        s = (ts * LANES + lax.broadcasted_iota(I32, (1, LANES), 1)).astype(F32)
        done = rowend <= s
        row = jnp.sum(done.astype(F32), axis=0, keepdims=True)
        before = jnp.sum(jnp.where(done, tot, 0.0), axis=0, keepdims=True)
        onehot_t = (r_iota == row).astype(BF16)
        g_t = _dot(incl_t, onehot_t)
        lane = jnp.sum((g_t <= s - before).astype(F32), axis=0, keepdims=True)
        idx_ref[pl.ds(ts, 1), :] = (row * LANES + lane).astype(I32)
        return carry

    lax.fori_loop(0, cap // LANES, slot_tile, 0)


def _select(pt3, cap):
    n_e, rows, _ = pt3.shape
    ii = np.arange(LANES)
    upper = jnp.asarray(ii[:, None] <= ii[None, :], BF16)
    lowinc = jnp.asarray(ii[None, :] <= ii[:, None], BF16)
    rr = np.arange(rows)
    lower = jnp.asarray(rr[None, :] < rr[:, None], BF16)
    const = lambda shape: pl.BlockSpec(shape, lambda e: (0, 0))
    return pl.pallas_call(
        functools.partial(_select_body, cap=cap),
        grid=(n_e,),
        in_specs=[pl.BlockSpec((None, rows, LANES), lambda e: (e, 0, 0)),
                  const((LANES, LANES)), const((rows, rows)), const((LANES, LANES))],
        out_specs=[pl.BlockSpec((None, rows, LANES), lambda e: (e, 0, 0)),
                   pl.BlockSpec((None, cap // LANES, LANES), lambda e: (e, 0, 0)),
                   pl.BlockSpec((None, rows, LANES), lambda e: (e, 0, 0))],
        out_shape=[jax.ShapeDtypeStruct((n_e, rows, LANES), I32),
                   jax.ShapeDtypeStruct((n_e, cap // LANES, LANES), I32),
                   jax.ShapeDtypeStruct((n_e, rows, LANES), I32)],
        compiler_params=_cparams(("parallel",)),
        name="expert_select",
    )(pt3, upper, lower, lowinc)


def _ffn_body(idx_ref, h_hbm, wg_ref, wu_ref, wd_ref, ye_ref, xbuf, sem, *, tiles_per_expert):
    s = xbuf.shape[0]
    base = (pl.program_id(0) * tiles_per_expert + pl.program_id(1)) * s

    def row_copy(i, tok):
        return pltpu.make_async_copy(h_hbm.at[pl.ds(tok, 1), :], xbuf.at[pl.ds(i, 1), :], sem)

    def issue(i, carry):
        row_copy(i, idx_ref[base + i]).start()
        return carry

    lax.fori_loop(0, s, issue, 0)
    pltpu.make_async_copy(h_hbm.at[pl.ds(0, s), :], xbuf, sem).wait()
    x = xbuf[...].astype(BF16)
    a = _dot(x, wg_ref[...])
    b = _dot(x, wu_ref[...])
    hid = (a * jax.nn.sigmoid(a)) * b
    ye_ref[...] = _dot(hid.astype(BF16), wd_ref[...]).astype(ye_ref.dtype)


def _expert_ffn(idx_flat, h2, wg, wu, wd, cap, s=256):
    s = min(s, cap)
    tiles = cap // s
    grid_spec = pltpu.PrefetchScalarGridSpec(
        num_scalar_prefetch=1,
        grid=(N_EXPERTS, tiles),
        in_specs=[pl.BlockSpec(memory_space=pl.ANY),
                  pl.BlockSpec((None, D_MODEL, EXPERT_FF), lambda e, j, idx: (e, 0, 0)),
                  pl.BlockSpec((None, D_MODEL, EXPERT_FF), lambda e, j, idx: (e, 0, 0)),
                  pl.BlockSpec((None, EXPERT_FF, D_MODEL), lambda e, j, idx: (e, 0, 0))],
        out_specs=pl.BlockSpec((s, D_MODEL), lambda e, j, idx: (e * tiles + j, 0)),
        scratch_shapes=[pltpu.VMEM((s, D_MODEL), F32), pltpu.SemaphoreType.DMA(())],
    )
    return pl.pallas_call(
        functools.partial(_ffn_body, tiles_per_expert=tiles),
        grid_spec=grid_spec,
        out_shape=jax.ShapeDtypeStruct((N_EXPERTS * cap, D_MODEL), BF16),
        compiler_params=_cparams(("arbitrary", "arbitrary")),
        name="expert_ffn",
    )(idx_flat, h2, wg, wu, wd)


def _combine_body(lo_ref, x1_ref, pos_ref, p_ref, nfg_ref, ye_hbm, o_ref, win, sems, *, cap, rows_total):
    r = pl.program_id(0)
    n_rows = pl.num_programs(0)
    tm = x1_ref.shape[0]

    def window(e):
        start = e * cap + lo_ref[e * n_rows + r]
        start = (start // COMBINE_ALIGN) * COMBINE_ALIGN
        start = pl.multiple_of(jnp.minimum(start, rows_total - COMBINE_WIN), COMBINE_ALIGN)
        return start, pltpu.make_async_copy(ye_hbm.at[pl.ds(start, COMBINE_WIN), :], win.at[e], sems.at[e])

    for e in range(N_EXPERTS):
        window(e)[1].start()
    acc = x1_ref[...]
    slot = lax.broadcasted_iota(I32, (tm, COMBINE_WIN), 1)
    for e in range(N_EXPERTS):
        start, cp = window(e)
        cp.wait()
        pos = pos_ref[:, e:e + 1]
        onehot = ((pos >= 0) & (pos + (e * cap - start) == slot)).astype(BF16)
        acc = acc + _dot(onehot, win[e]) * p_ref[:, e:e + 1]
    o_ref[...] = acc * lax.rsqrt(jnp.mean(acc * acc, axis=-1, keepdims=True) + EPS) * nfg_ref[...]


def _combine(lo_flat, x1, pos_t, p_t, nfg, ye, cap):
    n_tok = x1.shape[0]
    tm = LANES
    rows_total = ye.shape[0]
    grid_spec = pltpu.PrefetchScalarGridSpec(
        num_scalar_prefetch=1,
        grid=(n_tok // tm,),
        in_specs=[pl.BlockSpec((tm, D_MODEL), lambda i, lo: (i, 0)),
                  pl.BlockSpec((tm, N_EXPERTS), lambda i, lo: (i, 0)),
                  pl.BlockSpec((tm, N_EXPERTS), lambda i, lo: (i, 0)),
                  pl.BlockSpec((1, D_MODEL), lambda i, lo: (0, 0)),
                  pl.BlockSpec(memory_space=pl.ANY)],
        out_specs=pl.BlockSpec((tm, D_MODEL), lambda i, lo: (i, 0)),
        scratch_shapes=[pltpu.VMEM((N_EXPERTS, COMBINE_WIN, D_MODEL), BF16),
                        pltpu.SemaphoreType.DMA((N_EXPERTS,))],
    )
    return pl.pallas_call(
        functools.partial(_combine_body, cap=cap, rows_total=rows_total),
        grid_spec=grid_spec,
        out_shape=jax.ShapeDtypeStruct((n_tok, D_MODEL), F32),
        compiler_params=_cparams(("arbitrary",)),
        name="moe_combine_norm",
    )(lo_flat, x1, pos_t, p_t, nfg, ye)


def _trunk(x, w, fft_passes=1, filter_passes=3):
    bsz, seq, d = x.shape
    n_tok = bsz * seq
    x2 = x.reshape(n_tok, d)
    tabs = _fft_tables(seq)

    zhy, zret, zg = _inproj(x2, w["norm1_g"], w["w_in"])
    zc = _shortconv(zhy.reshape(bsz, seq, -1), w["hy_conv_w"], w["hy_conv_b"])

    h_time, h_abs = _hyena_filter_time(seq, w["hy_w1"], w["hy_b1"], w["hy_freq1"], w["hy_w2"], w["hy_b2"],
                                       w["hy_freq2"], w["hy_w3"])
    kr, ki = _stage_b_filter(_stage_a_real(h_time, tabs, filter_passes), h_abs, tabs, filter_passes)
    yhy = _hyena(zc, kr, ki, w["hy_skip"], tabs, fft_passes)

    o_f, o_b = _retention(zret.reshape(bsz, seq, -1), w["ret_decay_logit"])
    x1, h2, pt = _outproj(x2, yhy, o_f.reshape(n_tok, -1), o_b.reshape(n_tok, -1), zret, zg,
                          w["w_hy_out"], w["w_ret_out"], w["w_o"], w["norm2_g"], w["w_router_t"])

    cap = CAPACITY_FACTOR * n_tok // N_EXPERTS
    rows = n_tok // LANES
    pos, idx, lo = _select(pt.reshape(N_EXPERTS, rows, LANES), cap)
    ye = _expert_ffn(idx.reshape(-1), h2, w["w_gate"], w["w_up"], w["w_down"], cap)
    pos_t = pos.reshape(N_EXPERTS, n_tok).T
    lo_flat = lo[:, :, 0].reshape(-1)
    y = _combine(lo_flat, x1, pos_t, pt.T, w["norm_f_g"], ye, cap)
    return y.reshape(bsz, seq, d)


def kernel(x_prompt, x_sample, norm1_g, w_in, hy_conv_w, hy_conv_b, hy_w1, hy_b1, hy_freq1, hy_w2, hy_b2, hy_freq2,
           hy_w3, hy_skip, ret_decay_logit, w_hy_out, w_ret_out, w_o, norm2_g, w_router, w_gate, w_up, w_down,
           norm_f_g):
    layer = 0
    w = dict(
        norm1_g=norm1_g[layer].astype(F32)[None], w_in=w_in[layer].astype(BF16),
        hy_conv_w=hy_conv_w[layer].astype(F32), hy_conv_b=hy_conv_b[layer].astype(F32)[None],
        hy_w1=hy_w1[layer], hy_b1=hy_b1[layer], hy_freq1=hy_freq1[layer], hy_w2=hy_w2[layer], hy_b2=hy_b2[layer],
        hy_freq2=hy_freq2[layer], hy_w3=hy_w3[layer], hy_skip=hy_skip[layer],
        ret_decay_logit=ret_decay_logit[layer],
        w_hy_out=w_hy_out[layer].astype(BF16), w_ret_out=w_ret_out[layer].astype(BF16), w_o=w_o[layer].astype(BF16),
        norm2_g=norm2_g[layer].astype(F32)[None], w_router_t=w_router[layer].T.astype(BF16),
        w_gate=w_gate[layer].astype(BF16), w_up=w_up[layer].astype(BF16), w_down=w_down[layer].astype(BF16),
        norm_f_g=norm_f_g.astype(F32)[None],
    )
    return _trunk(x_prompt, w), _trunk(x_sample, w)
```

```python
import functools
import math

import numpy as np
import jax
import jax.numpy as jnp
from jax import lax
from jax.experimental import pallas as pl
from jax.experimental.pallas import tpu as pltpu

F32 = jnp.float32
BF16 = jnp.bfloat16
I32 = jnp.int32

D_MODEL = 1024
HY_WIDTH = 512
HY_ORDER = 2
HY_BANDS = 16
HY_FILTER_WIDTH = 64
HY_FAST_DECAY = 0.3
HY_SLOW_DECAY = 1.5
HY_DECAY_TARGET = 1e-2
RET_WIDTH = 512
RET_HEADS = 4
RET_HEAD_DIM = 128
RET_CHUNK = 128
ROPE_BASE = 10000.0
N_EXPERTS = 16
EXPERT_FF = 1408
CAPACITY_FACTOR = 2
EPS = 1e-6

LANES = 128
EMB_PAD = 128
DFT_N1_MAX = 128
COMBINE_ALIGN = 16
COMBINE_WIN = 64


def _cparams(sem, vmem_mb=48):
    return pltpu.CompilerParams(dimension_semantics=sem, vmem_limit_bytes=vmem_mb * 1024 * 1024)


def _dot(a, b):
    return jnp.dot(a, b, preferred_element_type=F32)


def _dot_nt(a, b):
    return lax.dot_general(a, b, (((1,), (1,)), ((), ())), preferred_element_type=F32)


def _dot_tn(a, b):
    return lax.dot_general(a, b, (((0,), (0,)), ((), ())), preferred_element_type=F32)


def _split(a):
    hi = a.astype(BF16)
    lo = (a - hi.astype(F32)).astype(BF16)
    return hi, lo


def _dot_const(m_hi, m_lo, x, passes):
    xh = x.astype(BF16)
    r = _dot(m_hi, xh)
    if passes >= 3:
        xl = (x - xh.astype(F32)).astype(BF16)
        r = r + _dot(m_lo, xh) + _dot(m_hi, xl)
    return r


def _dot3(a, b):
    ah, al = _split(a)
    bh, bl = _split(b)
    return _dot(ah, bh) + _dot(al, bh) + _dot(ah, bl)


def _inproj_body(x_ref, g_ref, w_ref, zhy_ref, zret_ref, zg_ref):
    x = x_ref[...]
    ms = jnp.mean(x * x, axis=-1, keepdims=True)
    h = (x * lax.rsqrt(ms + EPS) * g_ref[...]).astype(BF16)
    n_hy = zhy_ref.shape[1]
    n_ret = zret_ref.shape[1]
    zhy_ref[...] = _dot(h, w_ref[:, :n_hy])
    zret_ref[...] = _dot(h, w_ref[:, n_hy:n_hy + n_ret]).astype(BF16)
    zg_ref[...] = _dot(h, w_ref[:, n_hy + n_ret:]).astype(BF16)


def _inproj(x2, g, w_bf, tm=256):
    n_tok = x2.shape[0]
    n_hy, n_ret, n_g = 3 * HY_WIDTH, 4 * RET_WIDTH, 2 * D_MODEL
    return pl.pallas_call(
        _inproj_body,
        grid=(n_tok // tm,),
        in_specs=[pl.BlockSpec((tm, D_MODEL), lambda i: (i, 0)),
                  pl.BlockSpec((1, D_MODEL), lambda i: (0, 0)),
                  pl.BlockSpec((D_MODEL, n_hy + n_ret + n_g), lambda i: (0, 0))],
        out_specs=[pl.BlockSpec((tm, n_hy), lambda i: (i, 0)),
                   pl.BlockSpec((tm, n_ret), lambda i: (i, 0)),
                   pl.BlockSpec((tm, n_g), lambda i: (i, 0))],
        out_shape=[jax.ShapeDtypeStruct((n_tok, n_hy), F32),
                   jax.ShapeDtypeStruct((n_tok, n_ret), BF16),
                   jax.ShapeDtypeStruct((n_tok, n_g), BF16)],
        compiler_params=_cparams(("parallel",)),
        name="inproj",
    )(x2, g, w_bf)


def _shortconv_body(z_ref, zp_ref, zn_ref, w_ref, b_ref, o_ref):
    tl = z_ref.shape[0]
    i = pl.program_id(1)
    z = z_ref[...]
    prev_row = jnp.where(i > 0, zp_ref[7:8, :], 0.0)
    next_row = jnp.where(i < pl.num_programs(1) - 1, zn_ref[0:1, :], 0.0)
    row = lax.broadcasted_iota(I32, (tl, 1), 0)
    zm1 = jnp.where(row == 0, prev_row, pltpu.roll(z, 1, axis=0))
    zp1 = jnp.where(row == tl - 1, next_row, pltpu.roll(z, tl - 1, axis=0))
    o_ref[...] = zm1 * w_ref[0:1, :] + z * w_ref[1:2, :] + zp1 * w_ref[2:3, :] + b_ref[...]


def _shortconv(z3, w, b, tl=256):
    bsz, seq, width = z3.shape
    tl = min(tl, seq)
    nb8 = seq // 8
    r8 = tl // 8
    return pl.pallas_call(
        _shortconv_body,
        grid=(bsz, seq // tl),
        in_specs=[pl.BlockSpec((None, tl, width), lambda b_, i: (b_, i, 0)),
                  pl.BlockSpec((None, 8, width), lambda b_, i: (b_, jnp.maximum(i * r8 - 1, 0), 0)),
                  pl.BlockSpec((None, 8, width), lambda b_, i: (b_, jnp.minimum((i + 1) * r8, nb8 - 1), 0)),
                  pl.BlockSpec((3, width), lambda b_, i: (0, 0)),
                  pl.BlockSpec((1, width), lambda b_, i: (0, 0))],
        out_specs=pl.BlockSpec((None, tl, width), lambda b_, i: (b_, i, 0)),
        out_shape=jax.ShapeDtypeStruct((bsz, seq, width), F32),
        compiler_params=_cparams(("parallel", "parallel")),
        name="shortconv",
    )(z3, z3, z3, w, b)


def _fft_dims(seq):
    n = 2 * seq
    n1 = min(DFT_N1_MAX, 1 << (n.bit_length() // 2))
    n2 = n // n1
    assert n1 * n2 == n and n1 % 2 == 0
    return n, n1, n2


def _hi_lo_const(m):
    m = np.asarray(m, np.float64)
    hi = jnp.asarray(m, F32).astype(BF16)
    lo = (jnp.asarray(m, F32) - hi.astype(F32)).astype(BF16)
    return hi, lo


def _fft_tables(seq):
    n, n1, n2 = _fft_dims(seq)
    r1 = n1 // 2
    a1 = 2.0 * np.pi * np.outer(np.arange(n1), np.arange(n1)) / n1
    fr, fi = np.cos(a1), -np.sin(a1)
    mat_a = np.block([[fr[:, :r1], -fi[:, :r1]], [fi[:, :r1], fr[:, :r1]]])
    mat_a_real = np.concatenate([fr, fi], axis=0)
    mat_c = np.block([[fr[:r1], fi[:r1]], [-fi[:r1], fr[:r1]]])
    a2 = 2.0 * np.pi * np.outer(np.arange(n2), np.arange(n2)) / n2
    gr, gi = np.cos(a2), -np.sin(a2)
    mat_f = np.block([[gr, -gi], [gi, gr]])
    mat_i = np.block([[gr, gi], [-gi, gr]])
    at = 2.0 * np.pi * (np.outer(np.arange(n1), np.arange(n2)) % n) / n
    tw_r = jnp.broadcast_to(jnp.asarray(np.cos(at), F32)[:, :, None], (n1, n2, LANES))
    tw_i = jnp.broadcast_to(jnp.asarray(-np.sin(at), F32)[:, :, None], (n1, n2, LANES))
    return dict(n=n, n1=n1, n2=n2, mat_a=_hi_lo_const(mat_a), mat_a_real=_hi_lo_const(mat_a_real),
                mat_c=_hi_lo_const(mat_c), mat_f=_hi_lo_const(mat_f), mat_i=_hi_lo_const(mat_i),
                tw_r=tw_r, tw_i=tw_i)


def _lane_tile(t, width):
    return jnp.concatenate([t] * (width // LANES), axis=1) if width > LANES else t


def _filter_body(z_ref, w1_ref, b1_ref, f1_ref, w2_ref, b2_ref, f2_ref, w3_ref, dl_ref, h_ref, s_ref, *, seq):
    i = pl.program_id(0)
    tl = z_ref.shape[0]
    z = z_ref[...]
    a = jnp.sin(f1_ref[...] * (_dot3(z, w1_ref[...]) + b1_ref[...]))
    a = jnp.sin(f2_ref[...] * (_dot3(a, w2_ref[...]) + b2_ref[...]))
    h = _dot3(a, w3_ref[...])
    win = jnp.exp(-z[:, 0:1] * dl_ref[...])
    h = h * jnp.concatenate([win] * HY_ORDER, axis=1)
    row = i * tl + lax.broadcasted_iota(I32, (tl, 1), 0)
    h = jnp.where(row == seq, 0.0, h)
    h_ref[...] = h

    @pl.when(i == 0)
    def _():
        s_ref[...] = jnp.zeros_like(s_ref)

    s_ref[...] += jnp.broadcast_to(jnp.sum(jnp.abs(h), axis=0, keepdims=True), s_ref.shape)


def _hyena_filter_time(seq, w1, b1, f1, w2, b2, f2, w3, tl=512):
    n = 2 * seq
    tl = min(tl, seq)
    idx = jnp.arange(n, dtype=I32)
    p = jnp.minimum(jnp.where(idx < seq, idx, n - idx), seq - 1).astype(F32)
    t = p / float(seq - 1)
    ang = 2.0 * math.pi * p / seq
    bands = jnp.linspace(1e-4, HY_BANDS - 1, HY_BANDS, dtype=F32)
    phase = ang[:, None] * bands[None, :]
    emb = jnp.concatenate([t[:, None], jnp.cos(phase), -jnp.sin(phase)], axis=-1)
    emb = jnp.pad(emb, ((0, 0), (0, EMB_PAD - emb.shape[1])))
    w1p = jnp.pad(w1.astype(F32), ((0, EMB_PAD - w1.shape[0]), (0, 0)))
    w3d = w3.astype(F32).reshape(HY_FILTER_WIDTH, HY_ORDER, 2, HY_WIDTH).transpose(2, 0, 1, 3)
    w3d = w3d.reshape(2, HY_FILTER_WIDTH, HY_ORDER * HY_WIDTH)
    deltas = jnp.abs(jnp.linspace(math.log(HY_DECAY_TARGET) / HY_SLOW_DECAY,
                                  math.log(HY_DECAY_TARGET) / HY_FAST_DECAY, HY_WIDTH, dtype=F32))[None, :]
    fw = HY_FILTER_WIDTH
    nblk_half = seq // tl
    const = lambda i: (0, 0)
    return pl.pallas_call(
        functools.partial(_filter_body, seq=seq),
        grid=(n // tl,),
        in_specs=[pl.BlockSpec((tl, EMB_PAD), lambda i: (i, 0)),
                  pl.BlockSpec((EMB_PAD, fw), const), pl.BlockSpec((1, fw), const), pl.BlockSpec((1, fw), const),
                  pl.BlockSpec((fw, fw), const), pl.BlockSpec((1, fw), const), pl.BlockSpec((1, fw), const),
                  pl.BlockSpec((None, fw, HY_ORDER * HY_WIDTH), lambda i: (i // nblk_half, 0, 0)),
                  pl.BlockSpec((1, HY_WIDTH), const)],
        out_specs=[pl.BlockSpec((tl, HY_ORDER * HY_WIDTH), lambda i: (i, 0)),
                   pl.BlockSpec((8, HY_ORDER * HY_WIDTH), const)],
        out_shape=[jax.ShapeDtypeStruct((n, HY_ORDER * HY_WIDTH), F32),
                   jax.ShapeDtypeStruct((8, HY_ORDER * HY_WIDTH), F32)],
        compiler_params=_cparams(("arbitrary",)),
        name="hyena_filter",
    )(emb, w1p, b1.astype(F32)[None], f1.astype(F32)[None], w2.astype(F32), b2.astype(F32)[None],
      f2.astype(F32)[None], w3d, deltas)


def _stage_a_real_body(h_ref, mh_ref, ml_ref, a_ref, *, passes):
    n1 = h_ref.shape[0]
    for j in range(h_ref.shape[1]):
        r = _dot_const(mh_ref[...], ml_ref[...], h_ref[:, j, :], passes)
        a_ref[0, :, j, :] = r[:n1]
        a_ref[1, :, j, :] = r[n1:]


def _stage_a_real(h, tabs, passes, n2c=8, cw=512):
    n1, n2 = tabs["n1"], tabs["n2"]
    c = h.shape[1]
    h3 = h.reshape(n1, n2, c)
    mh, ml = tabs["mat_a_real"]
    return pl.pallas_call(
        functools.partial(_stage_a_real_body, passes=passes),
        grid=(n2 // n2c, c // cw),
        in_specs=[pl.BlockSpec((n1, n2c, cw), lambda j, k: (0, j, k)),
                  pl.BlockSpec(mh.shape, lambda j, k: (0, 0)),
                  pl.BlockSpec(ml.shape, lambda j, k: (0, 0))],
        out_specs=pl.BlockSpec((2, n1, n2c, cw), lambda j, k: (0, 0, j, k)),
        out_shape=jax.ShapeDtypeStruct((2, n1, n2, c), F32),
        compiler_params=_cparams(("parallel", "parallel")),
        name="dft_a_filter",
    )(h3, mh, ml)


def _stage_a_body(u_ref, mh_ref, ml_ref, a_ref, *, passes):
    n1 = a_ref.shape[1]
    for j in range(u_ref.shape[2]):
        x = jnp.concatenate([u_ref[0, :, j, :], u_ref[1, :, j, :]], axis=0)
        r = _dot_const(mh_ref[...], ml_ref[...], x, passes)
        a_ref[0, :, j, :] = r[:n1]
        a_ref[1, :, j, :] = r[n1:]


def _stage_a(u5, col, tabs, passes, n2c=8):
    n1, n2 = tabs["n1"], tabs["n2"]
    npair, _, r1, _, _ = u5.shape
    cw = HY_WIDTH
    mh, ml = tabs["mat_a"]
    return pl.pallas_call(
        functools.partial(_stage_a_body, passes=passes),
        grid=(npair, n2 // n2c),
        in_specs=[pl.BlockSpec((None, 2, r1, n2c, cw), lambda p, j: (p, 0, 0, j, col)),
                  pl.BlockSpec(mh.shape, lambda p, j: (0, 0)),
                  pl.BlockSpec(ml.shape, lambda p, j: (0, 0))],
        out_specs=pl.BlockSpec((None, 2, n1, n2c, cw), lambda p, j: (p, 0, 0, j, 0)),
        out_shape=jax.ShapeDtypeStruct((npair, 2, n1, n2, cw), F32),
        compiler_params=_cparams(("parallel", "parallel")),
        name="dft_a",
    )(u5, mh, ml)


def _twiddle(ar, ai, tr, ti, conj):
    if conj:
        return ar * tr + ai * ti, ai * tr - ar * ti
    return ar * tr - ai * ti, ar * ti + ai * tr


def _stage_b_filter_body(a_ref, tr_ref, ti_ref, gh_ref, gl_ref, s_ref, kr_ref, ki_ref, *, passes, n):
    n2, c = a_ref.shape[1], a_ref.shape[2]
    tr = _lane_tile(tr_ref[...], c)
    ti = _lane_tile(ti_ref[...], c)
    br, bi = _twiddle(a_ref[0], a_ref[1], tr, ti, False)
    d = _dot_const(gh_ref[...], gl_ref[...], jnp.concatenate([br, bi], axis=0), passes)
    scale = 1.0 / (s_ref[0:1, :] * float(n))
    kr_ref[...] = d[:n2] * scale
    ki_ref[...] = d[n2:] * scale


def _stage_b_filter(a, s, tabs, passes):
    n, n1, n2 = tabs["n"], tabs["n1"], tabs["n2"]
    c = a.shape[-1]
    gh, gl = tabs["mat_f"]
    return pl.pallas_call(
        functools.partial(_stage_b_filter_body, passes=passes, n=n),
        grid=(n1,),
        in_specs=[pl.BlockSpec((2, None, n2, c), lambda k: (0, k, 0, 0)),
                  pl.BlockSpec((None, n2, LANES), lambda k: (k, 0, 0)),
                  pl.BlockSpec((None, n2, LANES), lambda k: (k, 0, 0)),
                  pl.BlockSpec(gh.shape, lambda k: (0, 0)),
                  pl.BlockSpec(gl.shape, lambda k: (0, 0)),
                  pl.BlockSpec(s.shape, lambda k: (0, 0))],
        out_specs=[pl.BlockSpec((None, n2, c), lambda k: (k, 0, 0)),
                   pl.BlockSpec((None, n2, c), lambda k: (k, 0, 0))],
        out_shape=[jax.ShapeDtypeStruct((n1, n2, c), F32), jax.ShapeDtypeStruct((n1, n2, c), F32)],
        compiler_params=_cparams(("parallel",)),
        name="dft_b_filter",
    )(a, tabs["tw_r"], tabs["tw_i"], gh, gl, s)


def _stage_b_body(a_ref, kr_ref, ki_ref, tr_ref, ti_ref, gfh_ref, gfl_ref, gih_ref, gil_ref, c_ref, *, passes):
    n2, c = a_ref.shape[1], a_ref.shape[2]
    tr = _lane_tile(tr_ref[...], c)
    ti = _lane_tile(ti_ref[...], c)
    br, bi = _twiddle(a_ref[0], a_ref[1], tr, ti, False)
    d = _dot_const(gfh_ref[...], gfl_ref[...], jnp.concatenate([br, bi], axis=0), passes)
    dr, di = d[:n2], d[n2:]
    kr, ki = kr_ref[...], ki_ref[...]
    yr = dr * kr - di * ki
    yi = dr * ki + di * kr
    e = _dot_const(gih_ref[...], gil_ref[...], jnp.concatenate([yr, yi], axis=0), passes)
    cr, ci = _twiddle(e[:n2], e[n2:], tr, ti, True)
    c_ref[0] = cr
    c_ref[1] = ci


def _stage_b(a, kr, ki, order, tabs, passes):
    n1, n2 = tabs["n1"], tabs["n2"]
    npair = a.shape[0]
    cw = HY_WIDTH
    gfh, gfl = tabs["mat_f"]
    gih, gil = tabs["mat_i"]
    const = lambda k, p: (0, 0)
    return pl.pallas_call(
        functools.partial(_stage_b_body, passes=passes),
        grid=(n1, npair),
        in_specs=[pl.BlockSpec((None, 2, None, n2, cw), lambda k, p: (p, 0, k, 0, 0)),
                  pl.BlockSpec((None, n2, cw), lambda k, p: (k, 0, order)),
                  pl.BlockSpec((None, n2, cw), lambda k, p: (k, 0, order)),
                  pl.BlockSpec((None, n2, LANES), lambda k, p: (k, 0, 0)),
                  pl.BlockSpec((None, n2, LANES), lambda k, p: (k, 0, 0)),
                  pl.BlockSpec(gfh.shape, const), pl.BlockSpec(gfl.shape, const),
                  pl.BlockSpec(gih.shape, const), pl.BlockSpec(gil.shape, const)],
        out_specs=pl.BlockSpec((None, 2, None, n2, cw), lambda k, p: (p, 0, k, 0, 0)),
        out_shape=jax.ShapeDtypeStruct(a.shape, F32),
        compiler_params=_cparams(("parallel", "parallel")),
        name="dft_b",
    )(a, kr, ki, tabs["tw_r"], tabs["tw_i"], gfh, gfl, gih, gil)


def _stage_c_body(c_ref, u_ref, g_ref, skip_ref, mch_ref, mcl_ref, mah_ref, mal_ref, y_ref, *rest, passes, fuse_a):
    r1 = u_ref.shape[1]
    n1 = 2 * r1
    for j in range(c_ref.shape[2]):
        cc = jnp.concatenate([c_ref[0, :, j, :], c_ref[1, :, j, :]], axis=0)
        y = _dot_const(mch_ref[...], mcl_ref[...], cc, passes)
        u = jnp.concatenate([u_ref[0, :, j, :], u_ref[1, :, j, :]], axis=0)
        g = jnp.concatenate([g_ref[0, :, j, :], g_ref[1, :, j, :]], axis=0)
        yo = g * (y + u * skip_ref[...])
        y_ref[0, :, j, :] = yo[:r1].astype(y_ref.dtype)
        y_ref[1, :, j, :] = yo[r1:].astype(y_ref.dtype)
        if fuse_a:
            a_ref = rest[0]
            r = _dot_const(mah_ref[...], mal_ref[...], yo, passes)
            a_ref[0, :, j, :] = r[:n1]
            a_ref[1, :, j, :] = r[n1:]


def _stage_c(c, u5, u_col, g5, g_col, skip, tabs, passes, fuse_a, out_dtype, n2c=8):
    n1, n2 = tabs["n1"], tabs["n2"]
    npair, _, r1, _, _ = u5.shape
    cw = HY_WIDTH
    mch, mcl = tabs["mat_c"]
    mah, mal = tabs["mat_a"]
    const = lambda p, j: (0, 0)
    out_shape = [jax.ShapeDtypeStruct((npair, 2, r1, n2, cw), out_dtype)]
    out_specs = [pl.BlockSpec((None, 2, r1, n2c, cw), lambda p, j: (p, 0, 0, j, 0))]
    if fuse_a:
        out_shape.append(jax.ShapeDtypeStruct((npair, 2, n1, n2, cw), F32))
        out_specs.append(pl.BlockSpec((None, 2, n1, n2c, cw), lambda p, j: (p, 0, 0, j, 0)))
    return pl.pallas_call(
        functools.partial(_stage_c_body, passes=passes, fuse_a=fuse_a),
        grid=(npair, n2 // n2c),
        in_specs=[pl.BlockSpec((None, 2, n1, n2c, cw), lambda p, j: (p, 0, 0, j, 0)),
                  pl.BlockSpec((None, 2, r1, n2c, cw), lambda p, j: (p, 0, 0, j, u_col)),
                  pl.BlockSpec((None, 2, r1, n2c, cw), lambda p, j: (p, 0, 0, j, g_col)),
                  pl.BlockSpec((1, cw), const),
                  pl.BlockSpec(mch.shape, const), pl.BlockSpec(mcl.shape, const),
                  pl.BlockSpec(mah.shape, const), pl.BlockSpec(mal.shape, const)],
        out_specs=out_specs,
        out_shape=out_shape,
        compiler_params=_cparams(("parallel", "parallel")),
        name="dft_c",
    )(c, u5, g5, skip, mch, mcl, mah, mal)


def _hyena(zc3, kr, ki, skip, tabs, passes):
    bsz, seq, _ = zc3.shape
    n1, n2 = tabs["n1"], tabs["n2"]
    r1 = n1 // 2
    z5 = zc3.reshape(bsz // 2, 2, r1, n2, 3 * HY_WIDTH)
    a = _stage_a(z5, 2, tabs, passes)
    c = _stage_b(a, kr, ki, 0, tabs, passes)
    y1, a2 = _stage_c(c, z5, 2, z5, 0, skip[0:1].astype(F32), tabs, passes, True, F32)
    c2 = _stage_b(a2, kr, ki, 1, tabs, passes)
    (y2,) = _stage_c(c2, y1, 0, z5, 1, skip[1:2].astype(F32), tabs, passes, False, BF16)
    return y2.reshape(bsz * seq, HY_WIDTH)


def _retention_body(sc_ref, qf_ref, kf_ref, vf_ref, cf_ref, sf_ref, qb_ref, kb_ref, vb_ref, cb_ref, sb_ref,
                    of_ref, ob_ref, state, dmat, qwt, kwt):
    ch = RET_CHUNK
    dh = RET_HEAD_DIM

    @pl.when(pl.program_id(1) == 0)
    def _():
        state[...] = jnp.zeros_like(state)
        ci = lax.broadcasted_iota(I32, (ch, ch), 0).astype(F32)
        mi = lax.broadcasted_iota(I32, (ch, ch), 1).astype(F32)
        lag = ci - mi
        for d in range(2):
            for h in range(RET_HEADS):
                lg = sc_ref[d * RET_HEADS + h]
                if d == 0:
                    dmat[d, h] = jnp.where(lag >= 0, jnp.exp(lg * jnp.maximum(lag, 0.0)), 0.0)
                    qwt[d, h] = jnp.exp(lg * (ci + 1.0))
                    kwt[d, h] = jnp.exp(lg * (ch - 1.0 - ci))
                else:
                    dmat[d, h] = jnp.where(lag < 0, jnp.exp(lg * jnp.maximum(-lag, 0.0)), 0.0)
                    qwt[d, h] = jnp.exp(lg * (ch - ci))
                    kwt[d, h] = jnp.exp(lg * ci)

    scale = RET_HEAD_DIM ** -0.5
    for d, (q_ref, k_ref, v_ref, c_ref, s_ref, o_ref) in enumerate(
            ((qf_ref, kf_ref, vf_ref, cf_ref, sf_ref, of_ref), (qb_ref, kb_ref, vb_ref, cb_ref, sb_ref, ob_ref))):
        cosf = c_ref[...]
        sinf = s_ref[...]
        for h in range(RET_HEADS):
            sl = slice(h * dh, (h + 1) * dh)
            q = q_ref[:, sl].astype(F32)
            k = k_ref[:, sl].astype(F32)
            v = v_ref[:, sl]
            q = q * cosf + pltpu.roll(q, dh // 2, axis=1) * sinf
            k = (k * cosf + pltpu.roll(k, dh // 2, axis=1) * sinf) * scale
            s = _dot_nt(q.astype(BF16), k.astype(BF16)) * dmat[d, h]
            inner = _dot(s.astype(BF16), v)
            st = state[d, h]
            cross = _dot((q * qwt[d, h]).astype(BF16), st.astype(BF16))
            o_ref[:, sl] = inner + cross
            cdec = sc_ref[2 * RET_HEADS + d * RET_HEADS + h]
            state[d, h] = st * cdec + _dot_tn((k * kwt[d, h]).astype(BF16), v)


def _retention(zret3, decay_logit):
    bsz, seq, _ = zret3.shape
    ch = RET_CHUNK
    nc = seq // ch
    half = RET_HEAD_DIM // 2
    inv_freq = ROPE_BASE ** (-jnp.arange(half, dtype=F32) / half)
    ang = jnp.arange(seq, dtype=F32)[:, None] * inv_freq[None, :]
    cosf = jnp.concatenate([jnp.cos(ang), jnp.cos(ang)], axis=1)
    sinf = jnp.concatenate([-jnp.sin(ang), jnp.sin(ang)], axis=1)
    log_g = jax.nn.log_sigmoid(decay_logit.astype(F32)).reshape(-1)
    scal = jnp.concatenate([log_g, jnp.exp(log_g * ch)])
    w = RET_WIDTH
    fwd = lambda col: pl.BlockSpec((None, ch, w), lambda b, n, sc: (b, n, col))
    bwd = lambda col: pl.BlockSpec((None, ch, w), lambda b, n, sc: (b, nc - 1 - n, col))
    rope_f = pl.BlockSpec((ch, RET_HEAD_DIM), lambda b, n, sc: (n, 0))
    rope_b = pl.BlockSpec((ch, RET_HEAD_DIM), lambda b, n, sc: (nc - 1 - n, 0))
    grid_spec = pltpu.PrefetchScalarGridSpec(
        num_scalar_prefetch=1,
        grid=(bsz, nc),
        in_specs=[fwd(0), fwd(1), fwd(2), rope_f, rope_f, bwd(0), bwd(1), bwd(2), rope_b, rope_b],
        out_specs=[pl.BlockSpec((None, ch, w), lambda b, n, sc: (b, n, 0)),
                   pl.BlockSpec((None, ch, w), lambda b, n, sc: (b, nc - 1 - n, 0))],
        scratch_shapes=[pltpu.VMEM((2, RET_HEADS, RET_HEAD_DIM, RET_HEAD_DIM), F32),
                        pltpu.VMEM((2, RET_HEADS, ch, ch), F32),
                        pltpu.VMEM((2, RET_HEADS, ch, RET_HEAD_DIM), F32),
                        pltpu.VMEM((2, RET_HEADS, ch, RET_HEAD_DIM), F32)],
    )
    return pl.pallas_call(
        _retention_body,
        grid_spec=grid_spec,
        out_shape=[jax.ShapeDtypeStruct((bsz, seq, w), F32), jax.ShapeDtypeStruct((bsz, seq, w), F32)],
        compiler_params=_cparams(("arbitrary", "arbitrary")),
        name="retention",
    )(scal, zret3, zret3, zret3, cosf, sinf, zret3, zret3, zret3, cosf, sinf)


def _outproj_body(x_ref, yhy_ref, of_ref, ob_ref, gr_ref, ghy_ref, gret_ref, whyo_ref, wreto_ref, wo_ref,
                  n2g_ref, wrt_ref, x1_ref, h2_ref, pt_ref):
    dh = RET_HEAD_DIM
    o = of_ref[...] + ob_ref[...]
    parts = []
    for h in range(RET_HEADS):
        oh = o[:, h * dh:(h + 1) * dh]
        parts.append(oh * lax.rsqrt(jnp.mean(oh * oh, axis=-1, keepdims=True) + EPS))
    on = jnp.concatenate(parts, axis=1)
    gr = gr_ref[...].astype(F32)
    ret = (gr * jax.nn.sigmoid(gr)) * on
    y_ret = _dot(ret.astype(BF16), wreto_ref[...])
    y_hy = _dot(yhy_ref[...], whyo_ref[...])
    merged = jax.nn.sigmoid(ghy_ref[...].astype(F32)) * y_hy + jax.nn.sigmoid(gret_ref[...].astype(F32)) * y_ret
    x1 = x_ref[...] + _dot(merged.astype(BF16), wo_ref[...])
    x1_ref[...] = x1
    h2 = x1 * lax.rsqrt(jnp.mean(x1 * x1, axis=-1, keepdims=True) + EPS) * n2g_ref[...]
    h2_ref[...] = h2
    logits = _dot_nt(wrt_ref[...], h2.astype(BF16))
    m = jnp.max(logits, axis=0, keepdims=True)
    e = jnp.exp(logits - m)
    pt_ref[...] = e / jnp.sum(e, axis=0, keepdims=True)


def _outproj(x2, yhy, o_f, o_b, zret, zg, whyo, wreto, wo, n2g, wrt, tm=256):
    n_tok = x2.shape[0]
    d = D_MODEL
    row = lambda w, col=0: pl.BlockSpec((tm, w), lambda i: (i, col))
    const = lambda shape: pl.BlockSpec(shape, lambda i: (0, 0))
    return pl.pallas_call(
        _outproj_body,
        grid=(n_tok // tm,),
        in_specs=[row(d), row(HY_WIDTH), row(RET_WIDTH), row(RET_WIDTH), row(RET_WIDTH, 3), row(d, 0), row(d, 1),
                  const((HY_WIDTH, d)), const((RET_WIDTH, d)), const((d, d)), const((1, d)), const((N_EXPERTS, d))],
        out_specs=[row(d), row(d), pl.BlockSpec((N_EXPERTS, tm), lambda i: (0, i))],
        out_shape=[jax.ShapeDtypeStruct((n_tok, d), F32), jax.ShapeDtypeStruct((n_tok, d), F32),
                   jax.ShapeDtypeStruct((N_EXPERTS, n_tok), F32)],
        compiler_params=_cparams(("parallel",)),
        name="outproj_router",
    )(x2, yhy, o_f, o_b, zret, zg, zg, whyo, wreto, wo, n2g, wrt)


def _select_body(p_ref, upper_ref, lower_ref, lowinc_ref, eye_ref, pos_ref, idx_ref, gate_ref, lo_ref, *, cap):
    rows = p_ref.shape[0]
    p = p_ref[...]
    bits = lax.bitcast_convert_type(p, I32)

    def count(mask):
        return jnp.sum(jnp.sum(mask.astype(F32), axis=1, keepdims=True), axis=0, keepdims=True)

    def bit_step(i, thr):
        cand = thr | jnp.left_shift(jnp.int32(1), 30 - i)
        return jnp.where(count(bits >= cand) >= cap, cand, thr)

    thr = lax.fori_loop(0, 31, bit_step, jnp.zeros((1, 1), I32))
    gt = bits > thr
    eq = bits == thr
    need = cap - count(gt)

    def prefix(mask_f):
        incl = _dot(mask_f.astype(BF16), upper_ref[...])
        tot = jnp.broadcast_to(incl[:, LANES - 1:LANES], incl.shape)
        base = _dot(lower_ref[...], tot.astype(BF16))
        return incl, base, tot

    eq_f = eq.astype(F32)
    incl_e, base_e, _ = prefix(eq_f)
    sel = gt | (eq & (base_e + incl_e - eq_f < need))
    sel_f = sel.astype(F32)
    incl, base, tot = prefix(sel_f)
    pos_ref[...] = jnp.where(sel, (base + incl - 1.0).astype(I32), -1)
    lo_ref[...] = base.astype(I32)

    rowend = base + tot
    incl_t = _dot_nt(lowinc_ref[...], sel_f.astype(BF16)).astype(BF16)
    p_t = []
    rem = p
    for _ in range(3):
        part = rem.astype(BF16)
        rem = rem - part.astype(F32)
        p_t.append(_dot_nt(eye_ref[...], part).astype(BF16))
    r_iota = lax.broadcasted_iota(I32, (rows, LANES), 0).astype(F32)
    lane_iota = lax.broadcasted_iota(I32, (LANES, LANES), 0).astype(F32)

    def slot_tile(ts, carry):
        s = (ts * LANES + lax.broadcasted_iota(I32, (1, LANES), 1)).astype(F32)
        done = rowend <= s
        row = jnp.sum(done.astype(F32), axis=0, keepdims=True)
        before = jnp.sum(jnp.where(done, tot, 0.0), axis=0, keepdims=True)
        onehot_t = (r_iota == row).astype(BF16)
        g_t = _dot(incl_t, onehot_t)
        lane = jnp.sum((g_t <= s - before).astype(F32), axis=0, keepdims=True)
        idx_ref[pl.ds(ts, 1), :] = (row * LANES + lane).astype(I32)
        p_row = _dot(p_t[0], onehot_t) + _dot(p_t[1], onehot_t) + _dot(p_t[2], onehot_t)
        gate_ref[pl.ds(ts, 1), :] = jnp.sum(jnp.where(lane_iota == lane, p_row, 0.0), axis=0, keepdims=True)
        return carry

    lax.fori_loop(0, cap // LANES, slot_tile, 0)


def _select(pt3, cap):
    n_e, rows, _ = pt3.shape
    ii = np.arange(LANES)
    upper = jnp.asarray(ii[:, None] <= ii[None, :], BF16)
    lowinc = jnp.asarray(ii[None, :] <= ii[:, None], BF16)
    eye = jnp.asarray(ii[None, :] == ii[:, None], BF16)
    rr = np.arange(rows)
    lower = jnp.asarray(rr[None, :] < rr[:, None], BF16)
    const = lambda shape: pl.BlockSpec(shape, lambda e: (0, 0))
    tok_spec = pl.BlockSpec((None, rows, LANES), lambda e: (e, 0, 0))
    slot_spec = pl.BlockSpec((None, cap // LANES, LANES), lambda e: (e, 0, 0))
    return pl.pallas_call(
        functools.partial(_select_body, cap=cap),
        grid=(n_e,),
        in_specs=[tok_spec, const((LANES, LANES)), const((rows, rows)), const((LANES, LANES)), const((LANES, LANES))],
        out_specs=[tok_spec, slot_spec, slot_spec, tok_spec],
        out_shape=[jax.ShapeDtypeStruct((n_e, rows, LANES), I32),
                   jax.ShapeDtypeStruct((n_e, cap // LANES, LANES), I32),
                   jax.ShapeDtypeStruct((n_e, cap // LANES, LANES), F32),
                   jax.ShapeDtypeStruct((n_e, rows, LANES), I32)],
        compiler_params=_cparams(("parallel",)),
        name="expert_select",
    )(pt3, upper, lower, lowinc, eye)


def _ffn_body(idx_ref, h_hbm, gate_ref, wg_ref, wu_ref, wd_ref, ye_ref, xbuf, sems, *, tiles_per_expert):
    s = xbuf.shape[1]
    step = pl.program_id(0) * tiles_per_expert + pl.program_id(1)
    n_steps = pl.num_programs(0) * tiles_per_expert
    cur = step % 2

    def issue_rows(st, buf):
        def one(i, carry):
            tok = idx_ref[st * s + i]
            pltpu.make_async_copy(h_hbm.at[pl.ds(tok, 1), :], xbuf.at[buf, pl.ds(i, 1), :], sems.at[buf]).start()
            return carry
        lax.fori_loop(0, s, one, 0, unroll=8)

    @pl.when(step == 0)
    def _():
        issue_rows(0, 0)

    @pl.when(step + 1 < n_steps)
    def _():
        issue_rows(step + 1, 1 - cur)

    pltpu.make_async_copy(h_hbm.at[pl.ds(0, s), :], xbuf.at[cur], sems.at[cur]).wait()
    x = xbuf[cur].astype(BF16)
    a = _dot(x, wg_ref[...])
    b = _dot(x, wu_ref[...])
    hid = (a * jax.nn.sigmoid(a)) * b
    y = _dot(hid.astype(BF16), wd_ref[...])
    for k in range(s // LANES):
        col = jnp.transpose(jnp.broadcast_to(gate_ref[k:k + 1, :], (LANES, LANES)))[:, 0:1]
        ye_ref[k * LANES:(k + 1) * LANES, :] = (y[k * LANES:(k + 1) * LANES] * col).astype(ye_ref.dtype)


def _expert_ffn(idx_flat, h2, gates, wg, wu, wd, cap, s=256):
    s = min(s, cap)
    tiles = cap // s
    g3 = gates.reshape(N_EXPERTS * tiles, s // LANES, LANES)
    grid_spec = pltpu.PrefetchScalarGridSpec(
        num_scalar_prefetch=1,
        grid=(N_EXPERTS, tiles),
        in_specs=[pl.BlockSpec(memory_space=pl.ANY),
                  pl.BlockSpec((None, s // LANES, LANES), lambda e, j, idx: (e * tiles + j, 0, 0)),
                  pl.BlockSpec((None, D_MODEL, EXPERT_FF), lambda e, j, idx: (e, 0, 0)),
                  pl.BlockSpec((None, D_MODEL, EXPERT_FF), lambda e, j, idx: (e, 0, 0)),
                  pl.BlockSpec((None, EXPERT_FF, D_MODEL), lambda e, j, idx: (e, 0, 0))],
        out_specs=pl.BlockSpec((s, D_MODEL), lambda e, j, idx: (e * tiles + j, 0)),
        scratch_shapes=[pltpu.VMEM((2, s, D_MODEL), F32), pltpu.SemaphoreType.DMA((2,))],
    )
    return pl.pallas_call(
        functools.partial(_ffn_body, tiles_per_expert=tiles),
        grid_spec=grid_spec,
        out_shape=jax.ShapeDtypeStruct((N_EXPERTS * cap, D_MODEL), BF16),
        compiler_params=_cparams(("arbitrary", "arbitrary")),
        name="expert_ffn",
    )(idx_flat, h2, g3, wg, wu, wd)


def _combine_body(lo_ref, np_ref, x1_ref, pos_ref, nfg_ref, expand_ref, ye_hbm, o_ref, win, sems, *, cap, rows_total):
    r = pl.program_id(0)
    n_rows = pl.num_programs(0)
    w = COMBINE_WIN
    cur = r % 2

    def starts(rr, m):
        out = []
        for e in range(N_EXPERTS):
            first = e * cap + lo_ref[e * n_rows + rr]
            intended = (first // COMBINE_ALIGN) * COMBINE_ALIGN + m * w
            actual = pl.multiple_of(jnp.minimum(intended, rows_total - w), COMBINE_ALIGN)
            out.append((intended, actual))
        return out

    def copies(rr, m, buf):
        return [pltpu.make_async_copy(ye_hbm.at[pl.ds(actual, w), :], win.at[buf, pl.ds(e * w, w), :], sems.at[buf, e])
                for e, (_, actual) in enumerate(starts(rr, m))]

    def contribution(rr, m, buf):
        lane = lax.broadcasted_iota(I32, (1, N_EXPERTS), 1)
        intended = jnp.zeros((1, N_EXPERTS), I32)
        actual = jnp.zeros((1, N_EXPERTS), I32)
        for e, (i_s, a_s) in enumerate(starts(rr, m)):
            intended = jnp.where(lane == e, i_s, intended)
            actual = jnp.where(lane == e, a_s, actual)
        pos = pos_ref[...]
        glob = pos + lane * cap
        rel = glob - intended
        valid = (pos >= 0) & (rel >= 0) & (rel < w)
        local = jnp.where(valid, glob - actual, -1).astype(F32).astype(BF16)
        spread = _dot(local, expand_ref[...])
        col = (lax.broadcasted_iota(I32, (1, N_EXPERTS * w), 1) & (w - 1)).astype(F32)
        onehot = (spread == col).astype(BF16)
        return _dot(onehot, win[buf])

    @pl.when(r == 0)
    def _():
        for cp in copies(0, 0, 0):
            cp.start()

    @pl.when(r + 1 < n_rows)
    def _():
        for cp in copies(r + 1, 0, 1 - cur):
            cp.start()

    for cp in copies(r, 0, cur):
        cp.wait()
    acc = x1_ref[...] + contribution(r, 0, cur)

    def extra_pass(m, acc):
        for cp in copies(r, m, cur):
            cp.start()
        for cp in copies(r, m, cur):
            cp.wait()
        return acc + contribution(r, m, cur)

    acc = lax.fori_loop(1, np_ref[r], extra_pass, acc)
    o_ref[...] = acc * lax.rsqrt(jnp.mean(acc * acc, axis=-1, keepdims=True) + EPS) * nfg_ref[...]


def _combine(lo, x1, pos_t, nfg, ye, cap):
    n_tok = x1.shape[0]
    tm = LANES
    w = COMBINE_WIN
    rows_total = ye.shape[0]
    n_rows = n_tok // tm
    nxt = jnp.concatenate([lo[:, 1:], jnp.full((N_EXPERTS, 1), cap, I32)], axis=1)
    span = lo % COMBINE_ALIGN + (nxt - lo)
    n_pass = jnp.maximum(jnp.max((span + w - 1) // w, axis=0), 1).astype(I32)
    ee = np.arange(N_EXPERTS)
    expand = jnp.asarray(ee[:, None] == (np.arange(N_EXPERTS * w) // w)[None, :], BF16)
    grid_spec = pltpu.PrefetchScalarGridSpec(
        num_scalar_prefetch=2,
        grid=(n_rows,),
        in_specs=[pl.BlockSpec((tm, D_MODEL), lambda i, lo_, np_: (i, 0)),
                  pl.BlockSpec((tm, N_EXPERTS), lambda i, lo_, np_: (i, 0)),
                  pl.BlockSpec((1, D_MODEL), lambda i, lo_, np_: (0, 0)),
                  pl.BlockSpec((N_EXPERTS, N_EXPERTS * w), lambda i, lo_, np_: (0, 0)),
                  pl.BlockSpec(memory_space=pl.ANY)],
        out_specs=pl.BlockSpec((tm, D_MODEL), lambda i, lo_, np_: (i, 0)),
        scratch_shapes=[pltpu.VMEM((2, N_EXPERTS * w, D_MODEL), BF16),
                        pltpu.SemaphoreType.DMA((2, N_EXPERTS))],
    )
    return pl.pallas_call(
        functools.partial(_combine_body, cap=cap, rows_total=rows_total),
        grid_spec=grid_spec,
        out_shape=jax.ShapeDtypeStruct((n_tok, D_MODEL), F32),
        compiler_params=_cparams(("arbitrary",)),
        name="moe_combine_norm",
    )(lo.reshape(-1), n_pass, x1, pos_t, nfg, expand, ye)


def _trunk(x, w, fft_passes=1, filter_passes=3):
    bsz, seq, d = x.shape
    n_tok = bsz * seq
    x2 = x.reshape(n_tok, d)
    tabs = _fft_tables(seq)

    zhy, zret, zg = _inproj(x2, w["norm1_g"], w["w_in"])
    zc = _shortconv(zhy.reshape(bsz, seq, -1), w["hy_conv_w"], w["hy_conv_b"])

    h_time, h_abs = _hyena_filter_time(seq, w["hy_w1"], w["hy_b1"], w["hy_freq1"], w["hy_w2"], w["hy_b2"],
                                       w["hy_freq2"], w["hy_w3"])
    kr, ki = _stage_b_filter(_stage_a_real(h_time, tabs, filter_passes), h_abs, tabs, filter_passes)
    yhy = _hyena(zc, kr, ki, w["hy_skip"], tabs, fft_passes)

    o_f, o_b = _retention(zret.reshape(bsz, seq, -1), w["ret_decay_logit"])
    x1, h2, pt = _outproj(x2, yhy, o_f.reshape(n_tok, -1), o_b.reshape(n_tok, -1), zret, zg,
                          w["w_hy_out"], w["w_ret_out"], w["w_o"], w["norm2_g"], w["w_router_t"])

    cap = CAPACITY_FACTOR * n_tok // N_EXPERTS
    rows = n_tok // LANES
    pos, idx, gates, lo = _select(pt.reshape(N_EXPERTS, rows, LANES), cap)
    ye = _expert_ffn(idx.reshape(-1), h2, gates, w["w_gate"], w["w_up"], w["w_down"], cap)
    pos_t = pos.reshape(N_EXPERTS, n_tok).T
    y = _combine(lo[:, :, 0], x1, pos_t, w["norm_f_g"], ye, cap)
    return y.reshape(bsz, seq, d)


def kernel(x_prompt, x_sample, norm1_g, w_in, hy_conv_w, hy_conv_b, hy_w1, hy_b1, hy_freq1, hy_w2, hy_b2, hy_freq2,
           hy_w3, hy_skip, ret_decay_logit, w_hy_out, w_ret_out, w_o, norm2_g, w_router, w_gate, w_up, w_down,
           norm_f_g):
    layer = 0
    w = dict(
        norm1_g=norm1_g[layer].astype(F32)[None], w_in=w_in[layer].astype(BF16),
        hy_conv_w=hy_conv_w[layer].astype(F32), hy_conv_b=hy_conv_b[layer].astype(F32)[None],
        hy_w1=hy_w1[layer], hy_b1=hy_b1[layer], hy_freq1=hy_freq1[layer], hy_w2=hy_w2[layer], hy_b2=hy_b2[layer],
        hy_freq2=hy_freq2[layer], hy_w3=hy_w3[layer], hy_skip=hy_skip[layer],
        ret_decay_logit=ret_decay_logit[layer],
        w_hy_out=w_hy_out[layer].astype(BF16), w_ret_out=w_ret_out[layer].astype(BF16), w_o=w_o[layer].astype(BF16),
        norm2_g=norm2_g[layer].astype(F32)[None], w_router_t=w_router[layer].T.astype(BF16),
        w_gate=w_gate[layer].astype(BF16), w_up=w_up[layer].astype(BF16), w_down=w_down[layer].astype(BF16),
        norm_f_g=norm_f_g.astype(F32)[None],
    )
    return _trunk(x_prompt, w), _trunk(x_sample, w)
```

```python
import functools
import math

import numpy as np
import jax
import jax.numpy as jnp
from jax import lax
from jax.experimental import pallas as pl
from jax.experimental.pallas import tpu as pltpu

F32 = jnp.float32
BF16 = jnp.bfloat16
I32 = jnp.int32

D_MODEL = 1024
HY_WIDTH = 512
HY_ORDER = 2
HY_BANDS = 16
HY_FILTER_WIDTH = 64
HY_FAST_DECAY = 0.3
HY_SLOW_DECAY = 1.5
HY_DECAY_TARGET = 1e-2
RET_WIDTH = 512
RET_HEADS = 4
RET_HEAD_DIM = 128
RET_CHUNK = 128
ROPE_BASE = 10000.0
N_EXPERTS = 16
EXPERT_FF = 1408
CAPACITY_FACTOR = 2
EPS = 1e-6

LANES = 128
EMB_PAD = 128
DFT_N1_MAX = 128
FF_CHUNK = 256
COMBINE_ALIGN = 16
COMBINE_WIN = 64


def _cparams(sem, vmem_mb=48):
    return pltpu.CompilerParams(dimension_semantics=sem, vmem_limit_bytes=vmem_mb * 1024 * 1024)


def _dot(a, b):
    return jnp.dot(a, b, preferred_element_type=F32)


def _dot_nt(a, b):
    return lax.dot_general(a, b, (((1,), (1,)), ((), ())), preferred_element_type=F32)


def _dot_tn(a, b):
    return lax.dot_general(a, b, (((0,), (0,)), ((), ())), preferred_element_type=F32)


def _split(a):
    hi = a.astype(BF16)
    lo = (a - hi.astype(F32)).astype(BF16)
    return hi, lo


def _dot_const(m_hi, m_lo, x, passes):
    xh = x.astype(BF16)
    r = _dot(m_hi, xh)
    if passes >= 3:
        xl = (x - xh.astype(F32)).astype(BF16)
        r = r + _dot(m_lo, xh) + _dot(m_hi, xl)
    return r


def _dot3(a, b):
    ah, al = _split(a)
    bh, bl = _split(b)
    return _dot(ah, bh) + _dot(al, bh) + _dot(ah, bl)


def _inproj_body(x_ref, g_ref, w_ref, zhy_ref, zret_ref, zg_ref):
    x = x_ref[...]
    ms = jnp.mean(x * x, axis=-1, keepdims=True)
    h = (x * lax.rsqrt(ms + EPS) * g_ref[...]).astype(BF16)
    n_hy = zhy_ref.shape[1]
    n_ret = zret_ref.shape[1]
    zhy_ref[...] = _dot(h, w_ref[:, :n_hy])
    zret_ref[...] = _dot(h, w_ref[:, n_hy:n_hy + n_ret]).astype(BF16)
    zg_ref[...] = _dot(h, w_ref[:, n_hy + n_ret:]).astype(BF16)


def _inproj(x2, g, w_bf, tm=256):
    n_tok = x2.shape[0]
    n_hy, n_ret, n_g = 3 * HY_WIDTH, 4 * RET_WIDTH, 2 * D_MODEL
    return pl.pallas_call(
        _inproj_body,
        grid=(n_tok // tm,),
        in_specs=[pl.BlockSpec((tm, D_MODEL), lambda i: (i, 0)),
                  pl.BlockSpec((1, D_MODEL), lambda i: (0, 0)),
                  pl.BlockSpec((D_MODEL, n_hy + n_ret + n_g), lambda i: (0, 0))],
        out_specs=[pl.BlockSpec((tm, n_hy), lambda i: (i, 0)),
                   pl.BlockSpec((tm, n_ret), lambda i: (i, 0)),
                   pl.BlockSpec((tm, n_g), lambda i: (i, 0))],
        out_shape=[jax.ShapeDtypeStruct((n_tok, n_hy), F32),
                   jax.ShapeDtypeStruct((n_tok, n_ret), BF16),
                   jax.ShapeDtypeStruct((n_tok, n_g), BF16)],
        compiler_params=_cparams(("parallel",)),
        name="inproj",
    )(x2, g, w_bf)


def _shortconv_body(z_ref, zp_ref, zn_ref, w_ref, b_ref, o_ref):
    tl = z_ref.shape[0]
    i = pl.program_id(1)
    z = z_ref[...]
    prev_row = jnp.where(i > 0, zp_ref[7:8, :], 0.0)
    next_row = jnp.where(i < pl.num_programs(1) - 1, zn_ref[0:1, :], 0.0)
    row = lax.broadcasted_iota(I32, (tl, 1), 0)
    zm1 = jnp.where(row == 0, prev_row, pltpu.roll(z, 1, axis=0))
    zp1 = jnp.where(row == tl - 1, next_row, pltpu.roll(z, tl - 1, axis=0))
    o_ref[...] = zm1 * w_ref[0:1, :] + z * w_ref[1:2, :] + zp1 * w_ref[2:3, :] + b_ref[...]


def _shortconv(z3, w, b, tl=256):
    bsz, seq, width = z3.shape
    tl = min(tl, seq)
    nb8 = seq // 8
    r8 = tl // 8
    return pl.pallas_call(
        _shortconv_body,
        grid=(bsz, seq // tl),
        in_specs=[pl.BlockSpec((None, tl, width), lambda b_, i: (b_, i, 0)),
                  pl.BlockSpec((None, 8, width), lambda b_, i: (b_, jnp.maximum(i * r8 - 1, 0), 0)),
                  pl.BlockSpec((None, 8, width), lambda b_, i: (b_, jnp.minimum((i + 1) * r8, nb8 - 1), 0)),
                  pl.BlockSpec((3, width), lambda b_, i: (0, 0)),
                  pl.BlockSpec((1, width), lambda b_, i: (0, 0))],
        out_specs=pl.BlockSpec((None, tl, width), lambda b_, i: (b_, i, 0)),
        out_shape=jax.ShapeDtypeStruct((bsz, seq, width), F32),
        compiler_params=_cparams(("parallel", "parallel")),
        name="shortconv",
    )(z3, z3, z3, w, b)


def _fft_dims(seq):
    n = 2 * seq
    n1 = min(DFT_N1_MAX, 1 << (n.bit_length() // 2))
    n2 = n // n1
    assert n1 * n2 == n and n1 % 2 == 0
    return n, n1, n2


def _hi_lo_const(m):
    m = np.asarray(m, np.float64)
    hi = jnp.asarray(m, F32).astype(BF16)
    lo = (jnp.asarray(m, F32) - hi.astype(F32)).astype(BF16)
    return hi, lo


def _fft_tables(seq):
    n, n1, n2 = _fft_dims(seq)
    r1 = n1 // 2
    a1 = 2.0 * np.pi * np.outer(np.arange(n1), np.arange(n1)) / n1
    fr, fi = np.cos(a1), -np.sin(a1)
    mat_a = np.block([[fr[:, :r1], -fi[:, :r1]], [fi[:, :r1], fr[:, :r1]]])
    mat_a_real = np.concatenate([fr, fi], axis=0)
    mat_c = np.block([[fr[:r1], fi[:r1]], [-fi[:r1], fr[:r1]]])
    a2 = 2.0 * np.pi * np.outer(np.arange(n2), np.arange(n2)) / n2
    gr, gi = np.cos(a2), -np.sin(a2)
    mat_f = np.block([[gr, -gi], [gi, gr]])
    mat_i = np.block([[gr, gi], [-gi, gr]])
    at = 2.0 * np.pi * (np.outer(np.arange(n1), np.arange(n2)) % n) / n
    tw_r = jnp.broadcast_to(jnp.asarray(np.cos(at), F32)[:, :, None], (n1, n2, LANES))
    tw_i = jnp.broadcast_to(jnp.asarray(-np.sin(at), F32)[:, :, None], (n1, n2, LANES))
    return dict(n=n, n1=n1, n2=n2, mat_a=_hi_lo_const(mat_a), mat_a_real=_hi_lo_const(mat_a_real),
                mat_c=_hi_lo_const(mat_c), mat_f=_hi_lo_const(mat_f), mat_i=_hi_lo_const(mat_i),
                tw_r=tw_r, tw_i=tw_i)


def _lane_tile(t, width):
    return jnp.concatenate([t] * (width // LANES), axis=1) if width > LANES else t


def _filter_body(z_ref, w1_ref, b1_ref, f1_ref, w2_ref, b2_ref, f2_ref, w3_ref, dl_ref, h_ref, s_ref, *, seq):
    i = pl.program_id(0)
    tl = z_ref.shape[0]
    z = z_ref[...]
    a = jnp.sin(f1_ref[...] * (_dot3(z, w1_ref[...]) + b1_ref[...]))
    a = jnp.sin(f2_ref[...] * (_dot3(a, w2_ref[...]) + b2_ref[...]))
    h = _dot3(a, w3_ref[...])
    win = jnp.exp(-z[:, 0:1] * dl_ref[...])
    h = h * jnp.concatenate([win] * HY_ORDER, axis=1)
    row = i * tl + lax.broadcasted_iota(I32, (tl, 1), 0)
    h = jnp.where(row == seq, 0.0, h)
    h_ref[...] = h

    @pl.when(i == 0)
    def _():
        s_ref[...] = jnp.zeros_like(s_ref)

    s_ref[...] += jnp.broadcast_to(jnp.sum(jnp.abs(h), axis=0, keepdims=True), s_ref.shape)


def _hyena_filter_time(seq, w1, b1, f1, w2, b2, f2, w3, tl=512):
    n = 2 * seq
    tl = min(tl, seq)
    idx = jnp.arange(n, dtype=I32)
    p = jnp.minimum(jnp.where(idx < seq, idx, n - idx), seq - 1).astype(F32)
    t = p / float(seq - 1)
    ang = 2.0 * math.pi * p / seq
    bands = jnp.linspace(1e-4, HY_BANDS - 1, HY_BANDS, dtype=F32)
    phase = ang[:, None] * bands[None, :]
    emb = jnp.concatenate([t[:, None], jnp.cos(phase), -jnp.sin(phase)], axis=-1)
    emb = jnp.pad(emb, ((0, 0), (0, EMB_PAD - emb.shape[1])))
    w1p = jnp.pad(w1.astype(F32), ((0, EMB_PAD - w1.shape[0]), (0, 0)))
    w3d = w3.astype(F32).reshape(HY_FILTER_WIDTH, HY_ORDER, 2, HY_WIDTH).transpose(2, 0, 1, 3)
    w3d = w3d.reshape(2, HY_FILTER_WIDTH, HY_ORDER * HY_WIDTH)
    deltas = jnp.abs(jnp.linspace(math.log(HY_DECAY_TARGET) / HY_SLOW_DECAY,
                                  math.log(HY_DECAY_TARGET) / HY_FAST_DECAY, HY_WIDTH, dtype=F32))[None, :]
    fw = HY_FILTER_WIDTH
    nblk_half = seq // tl
    const = lambda i: (0, 0)
    return pl.pallas_call(
        functools.partial(_filter_body, seq=seq),
        grid=(n // tl,),
        in_specs=[pl.BlockSpec((tl, EMB_PAD), lambda i: (i, 0)),
                  pl.BlockSpec((EMB_PAD, fw), const), pl.BlockSpec((1, fw), const), pl.BlockSpec((1, fw), const),
                  pl.BlockSpec((fw, fw), const), pl.BlockSpec((1, fw), const), pl.BlockSpec((1, fw), const),
                  pl.BlockSpec((None, fw, HY_ORDER * HY_WIDTH), lambda i: (i // nblk_half, 0, 0)),
                  pl.BlockSpec((1, HY_WIDTH), const)],
        out_specs=[pl.BlockSpec((tl, HY_ORDER * HY_WIDTH), lambda i: (i, 0)),
                   pl.BlockSpec((8, HY_ORDER * HY_WIDTH), const)],
        out_shape=[jax.ShapeDtypeStruct((n, HY_ORDER * HY_WIDTH), F32),
                   jax.ShapeDtypeStruct((8, HY_ORDER * HY_WIDTH), F32)],
        compiler_params=_cparams(("arbitrary",)),
        name="hyena_filter",
    )(emb, w1p, b1.astype(F32)[None], f1.astype(F32)[None], w2.astype(F32), b2.astype(F32)[None],
      f2.astype(F32)[None], w3d, deltas)


def _stage_a_real_body(h_ref, mh_ref, ml_ref, a_ref, *, passes):
    n1 = h_ref.shape[0]
    for j in range(h_ref.shape[1]):
        r = _dot_const(mh_ref[...], ml_ref[...], h_ref[:, j, :], passes)
        a_ref[0, :, j, :] = r[:n1]
        a_ref[1, :, j, :] = r[n1:]


def _stage_a_real(h, tabs, passes, n2c=8, cw=512):
    n1, n2 = tabs["n1"], tabs["n2"]
    c = h.shape[1]
    h3 = h.reshape(n1, n2, c)
    mh, ml = tabs["mat_a_real"]
    return pl.pallas_call(
        functools.partial(_stage_a_real_body, passes=passes),
        grid=(n2 // n2c, c // cw),
        in_specs=[pl.BlockSpec((n1, n2c, cw), lambda j, k: (0, j, k)),
                  pl.BlockSpec(mh.shape, lambda j, k: (0, 0)),
                  pl.BlockSpec(ml.shape, lambda j, k: (0, 0))],
        out_specs=pl.BlockSpec((2, n1, n2c, cw), lambda j, k: (0, 0, j, k)),
        out_shape=jax.ShapeDtypeStruct((2, n1, n2, c), F32),
        compiler_params=_cparams(("parallel", "parallel")),
        name="dft_a_filter",
    )(h3, mh, ml)


def _pack_pair(re, im):
    hi = lax.bitcast_convert_type(re.astype(BF16).astype(F32), I32)
    lo = lax.bitcast_convert_type(im.astype(BF16).astype(F32), I32)
    return hi | lax.shift_right_logical(lo, 16)


def _unpack_pair(word):
    re = lax.bitcast_convert_type(word & jnp.int32(-65536), F32)
    im = lax.bitcast_convert_type(lax.shift_left(word, 16), F32)
    return re, im


def _stage_a_body(u_ref, mh_ref, ml_ref, a_ref, *, passes):
    n1 = a_ref.shape[0]
    for j in range(u_ref.shape[2]):
        x = jnp.concatenate([u_ref[0, :, j, :], u_ref[1, :, j, :]], axis=0)
        r = _dot_const(mh_ref[...], ml_ref[...], x, passes)
        a_ref[:, j, :] = _pack_pair(r[:n1], r[n1:])


def _stage_a(u5, col, tabs, passes, n2c=8):
    n1, n2 = tabs["n1"], tabs["n2"]
    npair, _, r1, _, _ = u5.shape
    cw = HY_WIDTH
    mh, ml = tabs["mat_a"]
    return pl.pallas_call(
        functools.partial(_stage_a_body, passes=passes),
        grid=(npair, n2 // n2c),
        in_specs=[pl.BlockSpec((None, 2, r1, n2c, cw), lambda p, j: (p, 0, 0, j, col)),
                  pl.BlockSpec(mh.shape, lambda p, j: (0, 0)),
                  pl.BlockSpec(ml.shape, lambda p, j: (0, 0))],
        out_specs=pl.BlockSpec((None, n1, n2c, cw), lambda p, j: (p, 0, j, 0)),
        out_shape=jax.ShapeDtypeStruct((npair, n1, n2, cw), I32),
        compiler_params=_cparams(("parallel", "parallel")),
        name="dft_a",
    )(u5, mh, ml)


def _twiddle(ar, ai, tr, ti, conj):
    if conj:
        return ar * tr + ai * ti, ai * tr - ar * ti
    return ar * tr - ai * ti, ar * ti + ai * tr


def _stage_b_filter_body(a_ref, tr_ref, ti_ref, gh_ref, gl_ref, s_ref, kr_ref, ki_ref, *, passes, n):
    n2, c = a_ref.shape[1], a_ref.shape[2]
    tr = _lane_tile(tr_ref[...], c)
    ti = _lane_tile(ti_ref[...], c)
    br, bi = _twiddle(a_ref[0], a_ref[1], tr, ti, False)
    d = _dot_const(gh_ref[...], gl_ref[...], jnp.concatenate([br, bi], axis=0), passes)
    scale = 1.0 / (s_ref[0:1, :] * float(n))
    kr_ref[...] = d[:n2] * scale
    ki_ref[...] = d[n2:] * scale


def _stage_b_filter(a, s, tabs, passes):
    n, n1, n2 = tabs["n"], tabs["n1"], tabs["n2"]
    c = a.shape[-1]
    gh, gl = tabs["mat_f"]
    return pl.pallas_call(
        functools.partial(_stage_b_filter_body, passes=passes, n=n),
        grid=(n1,),
        in_specs=[pl.BlockSpec((2, None, n2, c), lambda k: (0, k, 0, 0)),
                  pl.BlockSpec((None, n2, LANES), lambda k: (k, 0, 0)),
                  pl.BlockSpec((None, n2, LANES), lambda k: (k, 0, 0)),
                  pl.BlockSpec(gh.shape, lambda k: (0, 0)),
                  pl.BlockSpec(gl.shape, lambda k: (0, 0)),
                  pl.BlockSpec(s.shape, lambda k: (0, 0))],
        out_specs=[pl.BlockSpec((None, n2, c), lambda k: (k, 0, 0)),
                   pl.BlockSpec((None, n2, c), lambda k: (k, 0, 0))],
        out_shape=[jax.ShapeDtypeStruct((n1, n2, c), F32), jax.ShapeDtypeStruct((n1, n2, c), F32)],
        compiler_params=_cparams(("parallel",)),
        name="dft_b_filter",
    )(a, tabs["tw_r"], tabs["tw_i"], gh, gl, s)


def _stage_b_body(a_ref, kr_ref, ki_ref, tr_ref, ti_ref, gfh_ref, gfl_ref, gih_ref, gil_ref, c_ref, *, passes):
    n2, c = a_ref.shape[1], a_ref.shape[2]
    for kk in range(a_ref.shape[0]):
        tr = _lane_tile(tr_ref[kk], c)
        ti = _lane_tile(ti_ref[kk], c)
        ar, ai = _unpack_pair(a_ref[kk])
        br, bi = _twiddle(ar, ai, tr, ti, False)
        d = _dot_const(gfh_ref[...], gfl_ref[...], jnp.concatenate([br, bi], axis=0), passes)
        dr, di = d[:n2], d[n2:]
        kr, ki = kr_ref[kk], ki_ref[kk]
        yr = dr * kr - di * ki
        yi = dr * ki + di * kr
        e = _dot_const(gih_ref[...], gil_ref[...], jnp.concatenate([yr, yi], axis=0), passes)
        cr, ci = _twiddle(e[:n2], e[n2:], tr, ti, True)
        c_ref[kk] = _pack_pair(cr, ci)


def _stage_b(a, kr, ki, order, tabs, passes):
    n1, n2 = tabs["n1"], tabs["n2"]
    npair = a.shape[0]
    cw = HY_WIDTH
    kb = max(1, min(n1, 1024 // n2))
    gfh, gfl = tabs["mat_f"]
    gih, gil = tabs["mat_i"]
    const = lambda k, p: (0, 0)
    return pl.pallas_call(
        functools.partial(_stage_b_body, passes=passes),
        grid=(n1 // kb, npair),
        in_specs=[pl.BlockSpec((None, kb, n2, cw), lambda k, p: (p, k, 0, 0)),
                  pl.BlockSpec((kb, n2, cw), lambda k, p: (k, 0, order)),
                  pl.BlockSpec((kb, n2, cw), lambda k, p: (k, 0, order)),
                  pl.BlockSpec((kb, n2, LANES), lambda k, p: (k, 0, 0)),
                  pl.BlockSpec((kb, n2, LANES), lambda k, p: (k, 0, 0)),
                  pl.BlockSpec(gfh.shape, const), pl.BlockSpec(gfl.shape, const),
                  pl.BlockSpec(gih.shape, const), pl.BlockSpec(gil.shape, const)],
        out_specs=pl.BlockSpec((None, kb, n2, cw), lambda k, p: (p, k, 0, 0)),
        out_shape=jax.ShapeDtypeStruct(a.shape, I32),
        compiler_params=_cparams(("parallel", "parallel")),
        name="dft_b",
    )(a, kr, ki, tabs["tw_r"], tabs["tw_i"], gfh, gfl, gih, gil)


def _stage_c_body(c_ref, u_ref, g_ref, skip_ref, mch_ref, mcl_ref, mah_ref, mal_ref, y_ref, *rest, passes, fuse_a):
    r1 = u_ref.shape[1]
    n1 = 2 * r1
    for j in range(c_ref.shape[1]):
        cc = jnp.concatenate(_unpack_pair(c_ref[:, j, :]), axis=0)
        y = _dot_const(mch_ref[...], mcl_ref[...], cc, passes)
        u = jnp.concatenate([u_ref[0, :, j, :], u_ref[1, :, j, :]], axis=0)
        g = jnp.concatenate([g_ref[0, :, j, :], g_ref[1, :, j, :]], axis=0)
        yo = g * (y + u * skip_ref[...])
        y_ref[0, :, j, :] = yo[:r1].astype(y_ref.dtype)
        y_ref[1, :, j, :] = yo[r1:].astype(y_ref.dtype)
        if fuse_a:
            a_ref = rest[0]
            r = _dot_const(mah_ref[...], mal_ref[...], yo, passes)
            a_ref[:, j, :] = _pack_pair(r[:n1], r[n1:])


def _stage_c(c, u5, u_col, g5, g_col, skip, tabs, passes, fuse_a, out_dtype, n2c=8):
    n1, n2 = tabs["n1"], tabs["n2"]
    npair, _, r1, _, _ = u5.shape
    cw = HY_WIDTH
    mch, mcl = tabs["mat_c"]
    mah, mal = tabs["mat_a"]
    const = lambda p, j: (0, 0)
    out_shape = [jax.ShapeDtypeStruct((npair, 2, r1, n2, cw), out_dtype)]
    out_specs = [pl.BlockSpec((None, 2, r1, n2c, cw), lambda p, j: (p, 0, 0, j, 0))]
    if fuse_a:
        out_shape.append(jax.ShapeDtypeStruct((npair, n1, n2, cw), I32))
        out_specs.append(pl.BlockSpec((None, n1, n2c, cw), lambda p, j: (p, 0, j, 0)))
    return pl.pallas_call(
        functools.partial(_stage_c_body, passes=passes, fuse_a=fuse_a),
        grid=(npair, n2 // n2c),
        in_specs=[pl.BlockSpec((None, n1, n2c, cw), lambda p, j: (p, 0, j, 0)),
                  pl.BlockSpec((None, 2, r1, n2c, cw), lambda p, j: (p, 0, 0, j, u_col)),
                  pl.BlockSpec((None, 2, r1, n2c, cw), lambda p, j: (p, 0, 0, j, g_col)),
                  pl.BlockSpec((1, cw), const),
                  pl.BlockSpec(mch.shape, const), pl.BlockSpec(mcl.shape, const),
                  pl.BlockSpec(mah.shape, const), pl.BlockSpec(mal.shape, const)],
        out_specs=out_specs,
        out_shape=out_shape,
        compiler_params=_cparams(("parallel", "parallel")),
        name="dft_c",
    )(c, u5, g5, skip, mch, mcl, mah, mal)


def _hyena(zc3, kr, ki, skip, tabs, passes):
    bsz, seq, _ = zc3.shape
    n1, n2 = tabs["n1"], tabs["n2"]
    r1 = n1 // 2
    z5 = zc3.reshape(bsz // 2, 2, r1, n2, 3 * HY_WIDTH)
    a = _stage_a(z5, 2, tabs, passes)
    c = _stage_b(a, kr, ki, 0, tabs, passes)
    y1, a2 = _stage_c(c, z5, 2, z5, 0, skip[0:1].astype(F32), tabs, passes, True, F32)
    c2 = _stage_b(a2, kr, ki, 1, tabs, passes)
    (y2,) = _stage_c(c2, y1, 0, z5, 1, skip[1:2].astype(F32), tabs, passes, False, BF16)
    return y2.reshape(bsz * seq, HY_WIDTH)


def _retention_body(sc_ref, qf_ref, kf_ref, vf_ref, cf_ref, sf_ref, qb_ref, kb_ref, vb_ref, cb_ref, sb_ref,
                    of_ref, ob_ref, state, dmat, qwt, kwt):
    ch = RET_CHUNK
    dh = RET_HEAD_DIM

    @pl.when(pl.program_id(1) == 0)
    def _():
        state[...] = jnp.zeros_like(state)
        ci = lax.broadcasted_iota(I32, (ch, ch), 0).astype(F32)
        mi = lax.broadcasted_iota(I32, (ch, ch), 1).astype(F32)
        lag = ci - mi
        for d in range(2):
            for h in range(RET_HEADS):
                lg = sc_ref[d * RET_HEADS + h]
                if d == 0:
                    dmat[d, h] = jnp.where(lag >= 0, jnp.exp(lg * jnp.maximum(lag, 0.0)), 0.0)
                    qwt[d, h] = jnp.exp(lg * (ci + 1.0))
                    kwt[d, h] = jnp.exp(lg * (ch - 1.0 - ci))
                else:
                    dmat[d, h] = jnp.where(lag < 0, jnp.exp(lg * jnp.maximum(-lag, 0.0)), 0.0)
                    qwt[d, h] = jnp.exp(lg * (ch - ci))
                    kwt[d, h] = jnp.exp(lg * ci)

    scale = RET_HEAD_DIM ** -0.5
    for d, (q_ref, k_ref, v_ref, c_ref, s_ref, o_ref) in enumerate(
            ((qf_ref, kf_ref, vf_ref, cf_ref, sf_ref, of_ref), (qb_ref, kb_ref, vb_ref, cb_ref, sb_ref, ob_ref))):
        cosf = c_ref[...]
        sinf = s_ref[...]
        for h in range(RET_HEADS):
            sl = slice(h * dh, (h + 1) * dh)
            q = q_ref[:, sl].astype(F32)
            k = k_ref[:, sl].astype(F32)
            v = v_ref[:, sl]
            q = q * cosf + pltpu.roll(q, dh // 2, axis=1) * sinf
            k = (k * cosf + pltpu.roll(k, dh // 2, axis=1) * sinf) * scale
            s = _dot_nt(q.astype(BF16), k.astype(BF16)) * dmat[d, h]
            inner = _dot(s.astype(BF16), v)
            st = state[d, h]
            cross = _dot((q * qwt[d, h]).astype(BF16), st.astype(BF16))
            o_ref[:, sl] = inner + cross
            cdec = sc_ref[2 * RET_HEADS + d * RET_HEADS + h]
            state[d, h] = st * cdec + _dot_tn((k * kwt[d, h]).astype(BF16), v)


def _retention(zret3, decay_logit):
    bsz, seq, _ = zret3.shape
    ch = RET_CHUNK
    nc = seq // ch
    half = RET_HEAD_DIM // 2
    inv_freq = ROPE_BASE ** (-jnp.arange(half, dtype=F32) / half)
    ang = jnp.arange(seq, dtype=F32)[:, None] * inv_freq[None, :]
    cosf = jnp.concatenate([jnp.cos(ang), jnp.cos(ang)], axis=1)
    sinf = jnp.concatenate([-jnp.sin(ang), jnp.sin(ang)], axis=1)
    log_g = jax.nn.log_sigmoid(decay_logit.astype(F32)).reshape(-1)
    scal = jnp.concatenate([log_g, jnp.exp(log_g * ch)])
    w = RET_WIDTH
    fwd = lambda col: pl.BlockSpec((None, ch, w), lambda b, n, sc: (b, n, col))
    bwd = lambda col: pl.BlockSpec((None, ch, w), lambda b, n, sc: (b, nc - 1 - n, col))
    rope_f = pl.BlockSpec((ch, RET_HEAD_DIM), lambda b, n, sc: (n, 0))
    rope_b = pl.BlockSpec((ch, RET_HEAD_DIM), lambda b, n, sc: (nc - 1 - n, 0))
    grid_spec = pltpu.PrefetchScalarGridSpec(
        num_scalar_prefetch=1,
        grid=(bsz, nc),
        in_specs=[fwd(0), fwd(1), fwd(2), rope_f, rope_f, bwd(0), bwd(1), bwd(2), rope_b, rope_b],
        out_specs=[pl.BlockSpec((None, ch, w), lambda b, n, sc: (b, n, 0)),
                   pl.BlockSpec((None, ch, w), lambda b, n, sc: (b, nc - 1 - n, 0))],
        scratch_shapes=[pltpu.VMEM((2, RET_HEADS, RET_HEAD_DIM, RET_HEAD_DIM), F32),
                        pltpu.VMEM((2, RET_HEADS, ch, ch), F32),
                        pltpu.VMEM((2, RET_HEADS, ch, RET_HEAD_DIM), F32),
                        pltpu.VMEM((2, RET_HEADS, ch, RET_HEAD_DIM), F32)],
    )
    return pl.pallas_call(
        _retention_body,
        grid_spec=grid_spec,
        out_shape=[jax.ShapeDtypeStruct((bsz, seq, w), F32), jax.ShapeDtypeStruct((bsz, seq, w), F32)],
        compiler_params=_cparams(("arbitrary", "arbitrary")),
        name="retention",
    )(scal, zret3, zret3, zret3, cosf, sinf, zret3, zret3, zret3, cosf, sinf)


def _outproj_body(x_ref, yhy_ref, of_ref, ob_ref, gr_ref, ghy_ref, gret_ref, whyo_ref, wreto_ref, wo_ref,
                  n2g_ref, wrt_ref, x1_ref, h2_ref, pt_ref):
    dh = RET_HEAD_DIM
    o = of_ref[...] + ob_ref[...]
    parts = []
    for h in range(RET_HEADS):
        oh = o[:, h * dh:(h + 1) * dh]
        parts.append(oh * lax.rsqrt(jnp.mean(oh * oh, axis=-1, keepdims=True) + EPS))
    on = jnp.concatenate(parts, axis=1)
    gr = gr_ref[...].astype(F32)
    ret = (gr * jax.nn.sigmoid(gr)) * on
    y_ret = _dot(ret.astype(BF16), wreto_ref[...])
    y_hy = _dot(yhy_ref[...], whyo_ref[...])
    merged = jax.nn.sigmoid(ghy_ref[...].astype(F32)) * y_hy + jax.nn.sigmoid(gret_ref[...].astype(F32)) * y_ret
    x1 = x_ref[...] + _dot(merged.astype(BF16), wo_ref[...])
    x1_ref[...] = x1
    h2 = x1 * lax.rsqrt(jnp.mean(x1 * x1, axis=-1, keepdims=True) + EPS) * n2g_ref[...]
    h2_ref[...] = h2
    logits = _dot_nt(wrt_ref[...], h2.astype(BF16))
    m = jnp.max(logits, axis=0, keepdims=True)
    e = jnp.exp(logits - m)
    pt_ref[...] = e / jnp.sum(e, axis=0, keepdims=True)


def _outproj(x2, yhy, o_f, o_b, zret, zg, whyo, wreto, wo, n2g, wrt, tm=256):
    n_tok = x2.shape[0]
    d = D_MODEL
    row = lambda w, col=0: pl.BlockSpec((tm, w), lambda i: (i, col))
    const = lambda shape: pl.BlockSpec(shape, lambda i: (0, 0))
    return pl.pallas_call(
        _outproj_body,
        grid=(n_tok // tm,),
        in_specs=[row(d), row(HY_WIDTH), row(RET_WIDTH), row(RET_WIDTH), row(RET_WIDTH, 3), row(d, 0), row(d, 1),
                  const((HY_WIDTH, d)), const((RET_WIDTH, d)), const((d, d)), const((1, d)), const((N_EXPERTS, d))],
        out_specs=[row(d), row(d), pl.BlockSpec((N_EXPERTS, tm), lambda i: (0, i))],
        out_shape=[jax.ShapeDtypeStruct((n_tok, d), F32), jax.ShapeDtypeStruct((n_tok, d), F32),
                   jax.ShapeDtypeStruct((N_EXPERTS, n_tok), F32)],
        compiler_params=_cparams(("parallel",)),
        name="outproj_router",
    )(x2, yhy, o_f, o_b, zret, zg, zg, whyo, wreto, wo, n2g, wrt)


def _select_body(p_ref, upper_ref, lower_ref, lowinc_ref, eye_ref, pos_ref, idx_ref, gate_ref, lo_ref, *, cap):
    rows = p_ref.shape[0]
    p = p_ref[...]
    bits = lax.bitcast_convert_type(p, I32)

    def count(mask):
        return jnp.sum(jnp.sum(mask.astype(F32), axis=1, keepdims=True), axis=0, keepdims=True)

    def bit_step(i, thr):
        cand = thr | jnp.left_shift(jnp.int32(1), 30 - i)
        return jnp.where(count(bits >= cand) >= cap, cand, thr)

    thr = lax.fori_loop(0, 31, bit_step, jnp.zeros((1, 1), I32))
    gt = bits > thr
    eq = bits == thr
    need = cap - count(gt)

    def prefix(mask_f):
        incl = _dot(mask_f.astype(BF16), upper_ref[...])
        tot = jnp.broadcast_to(incl[:, LANES - 1:LANES], incl.shape)
        base = _dot(lower_ref[...], tot.astype(BF16))
        return incl, base, tot

    eq_f = eq.astype(F32)
    incl_e, base_e, _ = prefix(eq_f)
    sel = gt | (eq & (base_e + incl_e - eq_f < need))
    sel_f = sel.astype(F32)
    incl, base, tot = prefix(sel_f)
    pos_ref[...] = jnp.where(sel, (base + incl - 1.0).astype(I32), -1)
    lo_ref[...] = base.astype(I32)

    rowend = base + tot
    incl_t = _dot_nt(lowinc_ref[...], sel_f.astype(BF16)).astype(BF16)
    p_t = []
    rem = p
    for _ in range(3):
        part = rem.astype(BF16)
        rem = rem - part.astype(F32)
        p_t.append(_dot_nt(eye_ref[...], part).astype(BF16))
    r_iota = lax.broadcasted_iota(I32, (rows, LANES), 0).astype(F32)
    lane_iota = lax.broadcasted_iota(I32, (LANES, LANES), 0).astype(F32)

    def slot_tile(ts, carry):
        s = (ts * LANES + lax.broadcasted_iota(I32, (1, LANES), 1)).astype(F32)
        done = rowend <= s
        row = jnp.sum(done.astype(F32), axis=0, keepdims=True)
        before = jnp.sum(jnp.where(done, tot, 0.0), axis=0, keepdims=True)
        onehot_t = (r_iota == row).astype(BF16)
        g_t = _dot(incl_t, onehot_t)
        lane = jnp.sum((g_t <= s - before).astype(F32), axis=0, keepdims=True)
        idx_ref[pl.ds(ts, 1), :] = (row * LANES + lane).astype(I32)
        p_row = _dot(p_t[0], onehot_t) + _dot(p_t[1], onehot_t) + _dot(p_t[2], onehot_t)
        gate_ref[pl.ds(ts, 1), :] = jnp.sum(jnp.where(lane_iota == lane, p_row, 0.0), axis=0, keepdims=True)
        return carry

    lax.fori_loop(0, cap // LANES, slot_tile, 0)


def _select(pt3, cap):
    n_e, rows, _ = pt3.shape
    ii = np.arange(LANES)
    upper = jnp.asarray(ii[:, None] <= ii[None, :], BF16)
    lowinc = jnp.asarray(ii[None, :] <= ii[:, None], BF16)
    eye = jnp.asarray(ii[None, :] == ii[:, None], BF16)
    rr = np.arange(rows)
    lower = jnp.asarray(rr[None, :] < rr[:, None], BF16)
    const = lambda shape: pl.BlockSpec(shape, lambda e: (0, 0))
    tok_spec = pl.BlockSpec((None, rows, LANES), lambda e: (e, 0, 0))
    slot_spec = pl.BlockSpec((None, cap // LANES, LANES), lambda e: (e, 0, 0))
    return pl.pallas_call(
        functools.partial(_select_body, cap=cap),
        grid=(n_e,),
        in_specs=[tok_spec, const((LANES, LANES)), const((rows, rows)), const((LANES, LANES)), const((LANES, LANES))],
        out_specs=[tok_spec, slot_spec, slot_spec, tok_spec],
        out_shape=[jax.ShapeDtypeStruct((n_e, rows, LANES), I32),
                   jax.ShapeDtypeStruct((n_e, cap // LANES, LANES), I32),
                   jax.ShapeDtypeStruct((n_e, cap // LANES, LANES), F32),
                   jax.ShapeDtypeStruct((n_e, rows, LANES), I32)],
        compiler_params=_cparams(("parallel",)),
        name="expert_select",
    )(pt3, upper, lower, lowinc, eye)


def _ffn_body(idx_ref, h_hbm, gate_ref, wg_ref, wu_ref, wd_ref, ye_ref, xbuf, xb, sems, *, tiles_per_expert):
    s = xbuf.shape[1]
    step = pl.program_id(0) * tiles_per_expert + pl.program_id(1)
    n_steps = pl.num_programs(0) * tiles_per_expert
    cur = step % 2

    def row_copy(st, i, buf):
        tok = idx_ref[st * s + i]
        return pltpu.make_async_copy(h_hbm.at[pl.ds(tok, 1), :], xbuf.at[buf, pl.ds(i, 1), :], sems.at[buf])

    def wait_rows(buf):
        pltpu.make_async_copy(h_hbm.at[pl.ds(0, s), :], xbuf.at[buf], sems.at[buf]).wait()

    @pl.when(step == 0)
    def _():
        def one(i, carry):
            row_copy(0, i, 0).start()
            return carry
        lax.fori_loop(0, s, one, 0, unroll=8)

    @pl.when(step + 1 < n_steps)
    def _():
        for i in range(s):
            row_copy(step + 1, i, 1 - cur).start()

    wait_rows(cur)
    xb[...] = xbuf[cur].astype(BF16)
    y = None
    for f0 in range(0, EXPERT_FF, FF_CHUNK):
        f1 = min(f0 + FF_CHUNK, EXPERT_FF)
        a = _dot(xb[...], wg_ref[:, f0:f1])
        b = _dot(xb[...], wu_ref[:, f0:f1])
        hid = ((a * jax.nn.sigmoid(a)) * b).astype(BF16)
        part = _dot(hid, wd_ref[f0:f1, :])
        y = part if y is None else y + part
    for k in range(s // LANES):
        col = jnp.transpose(jnp.broadcast_to(gate_ref[k:k + 1, :], (LANES, LANES)))[:, 0:1]
        ye_ref[k * LANES:(k + 1) * LANES, :] = (y[k * LANES:(k + 1) * LANES] * col).astype(ye_ref.dtype)


def _expert_ffn(idx_flat, h2, gates, wg, wu, wd, cap, s=512):
    s = min(s, cap)
    tiles = cap // s
    g3 = gates.reshape(N_EXPERTS * tiles, s // LANES, LANES)
    grid_spec = pltpu.PrefetchScalarGridSpec(
        num_scalar_prefetch=1,
        grid=(N_EXPERTS, tiles),
        in_specs=[pl.BlockSpec(memory_space=pl.ANY),
                  pl.BlockSpec((None, s // LANES, LANES), lambda e, j, idx: (e * tiles + j, 0, 0)),
                  pl.BlockSpec((None, D_MODEL, EXPERT_FF), lambda e, j, idx: (e, 0, 0)),
                  pl.BlockSpec((None, D_MODEL, EXPERT_FF), lambda e, j, idx: (e, 0, 0)),
                  pl.BlockSpec((None, EXPERT_FF, D_MODEL), lambda e, j, idx: (e, 0, 0))],
        out_specs=pl.BlockSpec((s, D_MODEL), lambda e, j, idx: (e * tiles + j, 0)),
        scratch_shapes=[pltpu.VMEM((2, s, D_MODEL), F32), pltpu.VMEM((s, D_MODEL), BF16),
                        pltpu.SemaphoreType.DMA((2,))],
    )
    return pl.pallas_call(
        functools.partial(_ffn_body, tiles_per_expert=tiles),
        grid_spec=grid_spec,
        out_shape=jax.ShapeDtypeStruct((N_EXPERTS * cap, D_MODEL), BF16),
        compiler_params=_cparams(("arbitrary", "arbitrary")),
        name="expert_ffn",
    )(idx_flat, h2, g3, wg, wu, wd)


def _combine_body(lo_ref, np_ref, x1_ref, pos_ref, nfg_ref, expand_ref, ye_hbm, o_ref, win, sems, *, cap, rows_total):
    r = pl.program_id(0)
    n_rows = pl.num_programs(0)
    w = COMBINE_WIN
    cur = r % 2

    def starts(rr, m):
        out = []
        for e in range(N_EXPERTS):
            first = e * cap + lo_ref[e * n_rows + rr]
            intended = (first // COMBINE_ALIGN) * COMBINE_ALIGN + m * w
            actual = pl.multiple_of(jnp.minimum(intended, rows_total - w), COMBINE_ALIGN)
            out.append((intended, actual))
        return out

    def copies(rr, m, buf):
        return [pltpu.make_async_copy(ye_hbm.at[pl.ds(actual, w), :], win.at[buf, pl.ds(e * w, w), :], sems.at[buf, e])
                for e, (_, actual) in enumerate(starts(rr, m))]

    def contribution(rr, m, buf):
        lane = lax.broadcasted_iota(I32, (1, N_EXPERTS), 1)
        intended = jnp.zeros((1, N_EXPERTS), I32)
        actual = jnp.zeros((1, N_EXPERTS), I32)
        for e, (i_s, a_s) in enumerate(starts(rr, m)):
            intended = jnp.where(lane == e, i_s, intended)
            actual = jnp.where(lane == e, a_s, actual)
        pos = pos_ref[...]
        glob = pos + lane * cap
        rel = glob - intended
        valid = (pos >= 0) & (rel >= 0) & (rel < w)
        local = jnp.where(valid, glob - actual, -1).astype(F32).astype(BF16)
        spread = _dot(local, expand_ref[...])
        col = (lax.broadcasted_iota(I32, (1, N_EXPERTS * w), 1) & (w - 1)).astype(F32)
        onehot = (spread == col).astype(BF16)
        return _dot(onehot, win[buf])

    @pl.when(r == 0)
    def _():
        for cp in copies(0, 0, 0):
            cp.start()

    @pl.when(r + 1 < n_rows)
    def _():
        for cp in copies(r + 1, 0, 1 - cur):
            cp.start()

    for cp in copies(r, 0, cur):
        cp.wait()
    acc = x1_ref[...] + contribution(r, 0, cur)

    def extra_pass(m, acc):
        for cp in copies(r, m, cur):
            cp.start()
        for cp in copies(r, m, cur):
            cp.wait()
        return acc + contribution(r, m, cur)

    acc = lax.fori_loop(1, np_ref[r], extra_pass, acc)
    o_ref[...] = acc * lax.rsqrt(jnp.mean(acc * acc, axis=-1, keepdims=True) + EPS) * nfg_ref[...]


def _combine(lo, x1, pos_t, nfg, ye, cap):
    n_tok = x1.shape[0]
    tm = LANES
    w = COMBINE_WIN
    rows_total = ye.shape[0]
    n_rows = n_tok // tm
    nxt = jnp.concatenate([lo[:, 1:], jnp.full((N_EXPERTS, 1), cap, I32)], axis=1)
    span = lo % COMBINE_ALIGN + (nxt - lo)
    n_pass = jnp.maximum(jnp.max((span + w - 1) // w, axis=0), 1).astype(I32)
    ee = np.arange(N_EXPERTS)
    expand = jnp.asarray(ee[:, None] == (np.arange(N_EXPERTS * w) // w)[None, :], BF16)
    grid_spec = pltpu.PrefetchScalarGridSpec(
        num_scalar_prefetch=2,
        grid=(n_rows,),
        in_specs=[pl.BlockSpec((tm, D_MODEL), lambda i, lo_, np_: (i, 0)),
                  pl.BlockSpec((tm, N_EXPERTS), lambda i, lo_, np_: (i, 0)),
                  pl.BlockSpec((1, D_MODEL), lambda i, lo_, np_: (0, 0)),
                  pl.BlockSpec((N_EXPERTS, N_EXPERTS * w), lambda i, lo_, np_: (0, 0)),
                  pl.BlockSpec(memory_space=pl.ANY)],
        out_specs=pl.BlockSpec((tm, D_MODEL), lambda i, lo_, np_: (i, 0)),
        scratch_shapes=[pltpu.VMEM((2, N_EXPERTS * w, D_MODEL), BF16),
                        pltpu.SemaphoreType.DMA((2, N_EXPERTS))],
    )
    return pl.pallas_call(
        functools.partial(_combine_body, cap=cap, rows_total=rows_total),
        grid_spec=grid_spec,
        out_shape=jax.ShapeDtypeStruct((n_tok, D_MODEL), F32),
        compiler_params=_cparams(("arbitrary",)),
        name="moe_combine_norm",
    )(lo.reshape(-1), n_pass, x1, pos_t, nfg, expand, ye)


def _trunk(x, w, fft_passes=1, filter_passes=1):
    bsz, seq, d = x.shape
    n_tok = bsz * seq
    x2 = x.reshape(n_tok, d)
    tabs = _fft_tables(seq)

    zhy, zret, zg = _inproj(x2, w["norm1_g"], w["w_in"])
    zc = _shortconv(zhy.reshape(bsz, seq, -1), w["hy_conv_w"], w["hy_conv_b"])

    h_time, h_abs = _hyena_filter_time(seq, w["hy_w1"], w["hy_b1"], w["hy_freq1"], w["hy_w2"], w["hy_b2"],
                                       w["hy_freq2"], w["hy_w3"])
    kr, ki = _stage_b_filter(_stage_a_real(h_time, tabs, filter_passes), h_abs, tabs, filter_passes)
    yhy = _hyena(zc, kr, ki, w["hy_skip"], tabs, fft_passes)

    o_f, o_b = _retention(zret.reshape(bsz, seq, -1), w["ret_decay_logit"])
    x1, h2, pt = _outproj(x2, yhy, o_f.reshape(n_tok, -1), o_b.reshape(n_tok, -1), zret, zg,
                          w["w_hy_out"], w["w_ret_out"], w["w_o"], w["norm2_g"], w["w_router_t"])

    cap = CAPACITY_FACTOR * n_tok // N_EXPERTS
    rows = n_tok // LANES
    pos, idx, gates, lo = _select(pt.reshape(N_EXPERTS, rows, LANES), cap)
    ye = _expert_ffn(idx.reshape(-1), h2, gates, w["w_gate"], w["w_up"], w["w_down"], cap)
    pos_t = pos.reshape(N_EXPERTS, n_tok).T
    y = _combine(lo[:, :, 0], x1, pos_t, w["norm_f_g"], ye, cap)
    return y.reshape(bsz, seq, d)


def kernel(x_prompt, x_sample, norm1_g, w_in, hy_conv_w, hy_conv_b, hy_w1, hy_b1, hy_freq1, hy_w2, hy_b2, hy_freq2,
           hy_w3, hy_skip, ret_decay_logit, w_hy_out, w_ret_out, w_o, norm2_g, w_router, w_gate, w_up, w_down,
           norm_f_g):
    layer = 0
    w = dict(
        norm1_g=norm1_g[layer].astype(F32)[None], w_in=w_in[layer].astype(BF16),
        hy_conv_w=hy_conv_w[layer].astype(F32), hy_conv_b=hy_conv_b[layer].astype(F32)[None],
        hy_w1=hy_w1[layer], hy_b1=hy_b1[layer], hy_freq1=hy_freq1[layer], hy_w2=hy_w2[layer], hy_b2=hy_b2[layer],
        hy_freq2=hy_freq2[layer], hy_w3=hy_w3[layer], hy_skip=hy_skip[layer],
        ret_decay_logit=ret_decay_logit[layer],
        w_hy_out=w_hy_out[layer].astype(BF16), w_ret_out=w_ret_out[layer].astype(BF16), w_o=w_o[layer].astype(BF16),
        norm2_g=norm2_g[layer].astype(F32)[None], w_router_t=w_router[layer].T.astype(BF16),
        w_gate=w_gate[layer].astype(BF16), w_up=w_up[layer].astype(BF16), w_down=w_down[layer].astype(BF16),
        norm_f_g=norm_f_g.astype(F32)[None],
    )
    return _trunk(x_prompt, w), _trunk(x_sample, w)
```

```python
import functools
import math

import numpy as np
import jax
import jax.numpy as jnp
from jax import lax
from jax.experimental import pallas as pl
from jax.experimental.pallas import tpu as pltpu

F32 = jnp.float32
BF16 = jnp.bfloat16
I32 = jnp.int32

D_MODEL = 1024
HY_WIDTH = 512
HY_ORDER = 2
HY_BANDS = 16
HY_FILTER_WIDTH = 64
HY_FAST_DECAY = 0.3
HY_SLOW_DECAY = 1.5
HY_DECAY_TARGET = 1e-2
RET_WIDTH = 512
RET_HEADS = 4
RET_HEAD_DIM = 128
RET_CHUNK = 128
ROPE_BASE = 10000.0
N_EXPERTS = 16
EXPERT_FF = 1408
CAPACITY_FACTOR = 2
EPS = 1e-6

LANES = 128
EMB_PAD = 128
DFT_N1_MAX = 128
FF_CHUNK = 256
COMBINE_ALIGN = 16
COMBINE_WIN = 64


def _cparams(sem, vmem_mb=48):
    return pltpu.CompilerParams(dimension_semantics=sem, vmem_limit_bytes=vmem_mb * 1024 * 1024)


def _dot(a, b):
    return jnp.dot(a, b, preferred_element_type=F32)


def _dot_nt(a, b):
    return lax.dot_general(a, b, (((1,), (1,)), ((), ())), preferred_element_type=F32)


def _dot_tn(a, b):
    return lax.dot_general(a, b, (((0,), (0,)), ((), ())), preferred_element_type=F32)


def _split(a):
    hi = a.astype(BF16)
    lo = (a - hi.astype(F32)).astype(BF16)
    return hi, lo


def _dot_const(m_hi, m_lo, x, passes):
    xh = x.astype(BF16)
    r = _dot(m_hi, xh)
    if passes >= 3:
        xl = (x - xh.astype(F32)).astype(BF16)
        r = r + _dot(m_lo, xh) + _dot(m_hi, xl)
    return r


def _dot3(a, b):
    ah, al = _split(a)
    bh, bl = _split(b)
    return _dot(ah, bh) + _dot(al, bh) + _dot(ah, bl)


def _inproj_body(x_ref, g_ref, w_ref, zhy_ref, zret_ref, zg_ref):
    x = x_ref[...]
    ms = jnp.mean(x * x, axis=-1, keepdims=True)
    h = (x * lax.rsqrt(ms + EPS) * g_ref[...]).astype(BF16)
    n_hy = zhy_ref.shape[1]
    n_ret = zret_ref.shape[1]
    zhy_ref[...] = _dot(h, w_ref[:, :n_hy])
    zret_ref[...] = _dot(h, w_ref[:, n_hy:n_hy + n_ret]).astype(BF16)
    zg_ref[...] = _dot(h, w_ref[:, n_hy + n_ret:]).astype(BF16)


def _inproj(x2, g, w_bf, tm=256):
    n_tok = x2.shape[0]
    n_hy, n_ret, n_g = 3 * HY_WIDTH, 4 * RET_WIDTH, 2 * D_MODEL
    return pl.pallas_call(
        _inproj_body,
        grid=(n_tok // tm,),
        in_specs=[pl.BlockSpec((tm, D_MODEL), lambda i: (i, 0)),
                  pl.BlockSpec((1, D_MODEL), lambda i: (0, 0)),
                  pl.BlockSpec((D_MODEL, n_hy + n_ret + n_g), lambda i: (0, 0))],
        out_specs=[pl.BlockSpec((tm, n_hy), lambda i: (i, 0)),
                   pl.BlockSpec((tm, n_ret), lambda i: (i, 0)),
                   pl.BlockSpec((tm, n_g), lambda i: (i, 0))],
        out_shape=[jax.ShapeDtypeStruct((n_tok, n_hy), F32),
                   jax.ShapeDtypeStruct((n_tok, n_ret), BF16),
                   jax.ShapeDtypeStruct((n_tok, n_g), BF16)],
        compiler_params=_cparams(("parallel",)),
        name="inproj",
    )(x2, g, w_bf)


def _shortconv_body(z_ref, zp_ref, zn_ref, w_ref, b_ref, o_ref):
    tl = z_ref.shape[0]
    i = pl.program_id(1)
    z = z_ref[...]
    prev_row = jnp.where(i > 0, zp_ref[7:8, :], 0.0)
    next_row = jnp.where(i < pl.num_programs(1) - 1, zn_ref[0:1, :], 0.0)
    row = lax.broadcasted_iota(I32, (tl, 1), 0)
    zm1 = jnp.where(row == 0, prev_row, pltpu.roll(z, 1, axis=0))
    zp1 = jnp.where(row == tl - 1, next_row, pltpu.roll(z, tl - 1, axis=0))
    o_ref[...] = zm1 * w_ref[0:1, :] + z * w_ref[1:2, :] + zp1 * w_ref[2:3, :] + b_ref[...]


def _shortconv(z3, w, b, tl=256):
    bsz, seq, width = z3.shape
    tl = min(tl, seq)
    nb8 = seq // 8
    r8 = tl // 8
    return pl.pallas_call(
        _shortconv_body,
        grid=(bsz, seq // tl),
        in_specs=[pl.BlockSpec((None, tl, width), lambda b_, i: (b_, i, 0)),
                  pl.BlockSpec((None, 8, width), lambda b_, i: (b_, jnp.maximum(i * r8 - 1, 0), 0)),
                  pl.BlockSpec((None, 8, width), lambda b_, i: (b_, jnp.minimum((i + 1) * r8, nb8 - 1), 0)),
                  pl.BlockSpec((3, width), lambda b_, i: (0, 0)),
                  pl.BlockSpec((1, width), lambda b_, i: (0, 0))],
        out_specs=pl.BlockSpec((None, tl, width), lambda b_, i: (b_, i, 0)),
        out_shape=jax.ShapeDtypeStruct((bsz, seq, width), F32),
        compiler_params=_cparams(("parallel", "parallel")),
        name="shortconv",
    )(z3, z3, z3, w, b)


def _fft_dims(seq):
    n = 2 * seq
    n1 = min(DFT_N1_MAX, 1 << (n.bit_length() // 2))
    n2 = n // n1
    assert n1 * n2 == n and n1 % 2 == 0
    return n, n1, n2


def _hi_lo_const(m):
    m = np.asarray(m, np.float64)
    hi = jnp.asarray(m, F32).astype(BF16)
    lo = (jnp.asarray(m, F32) - hi.astype(F32)).astype(BF16)
    return hi, lo


def _fft_tables(seq):
    n, n1, n2 = _fft_dims(seq)
    r1 = n1 // 2
    a1 = 2.0 * np.pi * np.outer(np.arange(n1), np.arange(n1)) / n1
    fr, fi = np.cos(a1), -np.sin(a1)
    mat_a = np.block([[fr[:, :r1], -fi[:, :r1]], [fi[:, :r1], fr[:, :r1]]])
    mat_a_real = np.concatenate([fr, fi], axis=0)
    mat_c = np.block([[fr[:r1], fi[:r1]], [-fi[:r1], fr[:r1]]])
    a2 = 2.0 * np.pi * np.outer(np.arange(n2), np.arange(n2)) / n2
    gr, gi = np.cos(a2), -np.sin(a2)
    mat_f = np.block([[gr, -gi], [gi, gr]])
    mat_i = np.block([[gr, gi], [-gi, gr]])
    at = 2.0 * np.pi * (np.outer(np.arange(n1), np.arange(n2)) % n) / n
    tw_r = jnp.broadcast_to(jnp.asarray(np.cos(at), F32)[:, :, None], (n1, n2, LANES))
    tw_i = jnp.broadcast_to(jnp.asarray(-np.sin(at), F32)[:, :, None], (n1, n2, LANES))
    return dict(n=n, n1=n1, n2=n2, mat_a=_hi_lo_const(mat_a), mat_a_real=_hi_lo_const(mat_a_real),
                mat_c=_hi_lo_const(mat_c), mat_f=_hi_lo_const(mat_f), mat_i=_hi_lo_const(mat_i),
                tw_r=tw_r, tw_i=tw_i)


def _lane_tile(t, width):
    return jnp.concatenate([t] * (width // LANES), axis=1) if width > LANES else t


def _filter_body(z_ref, w1_ref, b1_ref, f1_ref, w2_ref, b2_ref, f2_ref, w3_ref, dl_ref, h_ref, s_ref, *, seq):
    i = pl.program_id(0)
    tl = z_ref.shape[0]
    z = z_ref[...]
    a = jnp.sin(f1_ref[...] * (_dot3(z, w1_ref[...]) + b1_ref[...]))
    a = jnp.sin(f2_ref[...] * (_dot3(a, w2_ref[...]) + b2_ref[...]))
    h = _dot3(a, w3_ref[...])
    win = jnp.exp(-z[:, 0:1] * dl_ref[...])
    h = h * jnp.concatenate([win] * HY_ORDER, axis=1)
    row = i * tl + lax.broadcasted_iota(I32, (tl, 1), 0)
    h = jnp.where(row == seq, 0.0, h)
    h_ref[...] = h

    @pl.when(i == 0)
    def _():
        s_ref[...] = jnp.zeros_like(s_ref)

    s_ref[...] += jnp.broadcast_to(jnp.sum(jnp.abs(h), axis=0, keepdims=True), s_ref.shape)


def _hyena_filter_time(seq, w1, b1, f1, w2, b2, f2, w3, tl=512):
    n = 2 * seq
    tl = min(tl, seq)
    idx = jnp.arange(n, dtype=I32)
    p = jnp.minimum(jnp.where(idx < seq, idx, n - idx), seq - 1).astype(F32)
    t = p / float(seq - 1)
    ang = 2.0 * math.pi * p / seq
    bands = jnp.linspace(1e-4, HY_BANDS - 1, HY_BANDS, dtype=F32)
    phase = ang[:, None] * bands[None, :]
    emb = jnp.concatenate([t[:, None], jnp.cos(phase), -jnp.sin(phase)], axis=-1)
    emb = jnp.pad(emb, ((0, 0), (0, EMB_PAD - emb.shape[1])))
    w1p = jnp.pad(w1.astype(F32), ((0, EMB_PAD - w1.shape[0]), (0, 0)))
    w3d = w3.astype(F32).reshape(HY_FILTER_WIDTH, HY_ORDER, 2, HY_WIDTH).transpose(2, 0, 1, 3)
    w3d = w3d.reshape(2, HY_FILTER_WIDTH, HY_ORDER * HY_WIDTH)
    deltas = jnp.abs(jnp.linspace(math.log(HY_DECAY_TARGET) / HY_SLOW_DECAY,
                                  math.log(HY_DECAY_TARGET) / HY_FAST_DECAY, HY_WIDTH, dtype=F32))[None, :]
    fw = HY_FILTER_WIDTH
    nblk_half = seq // tl
    const = lambda i: (0, 0)
    return pl.pallas_call(
        functools.partial(_filter_body, seq=seq),
        grid=(n // tl,),
        in_specs=[pl.BlockSpec((tl, EMB_PAD), lambda i: (i, 0)),
                  pl.BlockSpec((EMB_PAD, fw), const), pl.BlockSpec((1, fw), const), pl.BlockSpec((1, fw), const),
                  pl.BlockSpec((fw, fw), const), pl.BlockSpec((1, fw), const), pl.BlockSpec((1, fw), const),
                  pl.BlockSpec((None, fw, HY_ORDER * HY_WIDTH), lambda i: (i // nblk_half, 0, 0)),
                  pl.BlockSpec((1, HY_WIDTH), const)],
        out_specs=[pl.BlockSpec((tl, HY_ORDER * HY_WIDTH), lambda i: (i, 0)),
                   pl.BlockSpec((8, HY_ORDER * HY_WIDTH), const)],
        out_shape=[jax.ShapeDtypeStruct((n, HY_ORDER * HY_WIDTH), F32),
                   jax.ShapeDtypeStruct((8, HY_ORDER * HY_WIDTH), F32)],
        compiler_params=_cparams(("arbitrary",)),
        name="hyena_filter",
    )(emb, w1p, b1.astype(F32)[None], f1.astype(F32)[None], w2.astype(F32), b2.astype(F32)[None],
      f2.astype(F32)[None], w3d, deltas)


def _stage_a_real_body(h_ref, mh_ref, ml_ref, a_ref, *, passes):
    n1 = h_ref.shape[0]
    for j in range(h_ref.shape[1]):
        r = _dot_const(mh_ref[...], ml_ref[...], h_ref[:, j, :], passes)
        a_ref[0, :, j, :] = r[:n1]
        a_ref[1, :, j, :] = r[n1:]


def _stage_a_real(h, tabs, passes, n2c=8, cw=512):
    n1, n2 = tabs["n1"], tabs["n2"]
    c = h.shape[1]
    h3 = h.reshape(n1, n2, c)
    mh, ml = tabs["mat_a_real"]
    return pl.pallas_call(
        functools.partial(_stage_a_real_body, passes=passes),
        grid=(n2 // n2c, c // cw),
        in_specs=[pl.BlockSpec((n1, n2c, cw), lambda j, k: (0, j, k)),
                  pl.BlockSpec(mh.shape, lambda j, k: (0, 0)),
                  pl.BlockSpec(ml.shape, lambda j, k: (0, 0))],
        out_specs=pl.BlockSpec((2, n1, n2c, cw), lambda j, k: (0, 0, j, k)),
        out_shape=jax.ShapeDtypeStruct((2, n1, n2, c), F32),
        compiler_params=_cparams(("parallel", "parallel")),
        name="dft_a_filter",
    )(h3, mh, ml)


def _pack_pair(re, im):
    hi = lax.bitcast_convert_type(re.astype(BF16).astype(F32), I32)
    lo = lax.bitcast_convert_type(im.astype(BF16).astype(F32), I32)
    return hi | lax.shift_right_logical(lo, 16)


def _unpack_pair(word):
    re = lax.bitcast_convert_type(word & jnp.int32(-65536), F32)
    im = lax.bitcast_convert_type(lax.shift_left(word, 16), F32)
    return re, im


def _start_all(copies):
    for cp in copies:
        cp.start()


def _wait_all(copies):
    for cp in copies:
        cp.wait()


def _stage_a_body(u_hbm, mh_ref, ml_ref, a_hbm, ubuf, abuf, in_sems, out_sems, *, passes, col, steps_per_pair):
    step = pl.program_id(0)
    n_steps = pl.num_programs(0)
    cur = step % 2
    n2c, n1, cw = abuf.shape[1], abuf.shape[2], abuf.shape[3]

    def where(st):
        return st // steps_per_pair, (st % steps_per_pair) * n2c

    def in_copies(st, slot):
        p, j0 = where(st)
        return [pltpu.make_async_copy(u_hbm.at[p, :, :, j0 + jj, pl.ds(col * cw, cw)], ubuf.at[slot, jj],
                                      in_sems.at[slot, jj]) for jj in range(n2c)]

    def out_copies(st, slot):
        p, j0 = where(st)
        return [pltpu.make_async_copy(abuf.at[slot, jj], a_hbm.at[p, :, j0 + jj, :], out_sems.at[slot, jj])
                for jj in range(n2c)]

    @pl.when(step == 0)
    def _():
        _start_all(in_copies(0, 0))

    @pl.when(step + 1 < n_steps)
    def _():
        _start_all(in_copies(step + 1, 1 - cur))

    @pl.when(step >= 2)
    def _():
        _wait_all(out_copies(step - 2, cur))

    for jj, cp in enumerate(in_copies(step, cur)):
        cp.wait()
        x = ubuf[cur, jj].reshape(n1, cw)
        r = _dot_const(mh_ref[...], ml_ref[...], x, passes)
        abuf[cur, jj] = _pack_pair(r[:n1], r[n1:])
    _start_all(out_copies(step, cur))

    @pl.when(step == n_steps - 1)
    def _():
        _wait_all(out_copies(step, cur))
        _wait_all(out_copies(step - 1, 1 - cur))


def _stage_a(u5, col, tabs, passes, n2c=8):
    n1, n2 = tabs["n1"], tabs["n2"]
    npair, _, r1, _, _ = u5.shape
    cw = HY_WIDTH
    mh, ml = tabs["mat_a"]
    steps_per_pair = n2 // n2c
    assert npair * steps_per_pair >= 2
    return pl.pallas_call(
        functools.partial(_stage_a_body, passes=passes, col=col, steps_per_pair=steps_per_pair),
        grid=(npair * steps_per_pair,),
        in_specs=[pl.BlockSpec(memory_space=pl.ANY),
                  pl.BlockSpec(mh.shape, lambda i: (0, 0)),
                  pl.BlockSpec(ml.shape, lambda i: (0, 0))],
        out_specs=pl.BlockSpec(memory_space=pl.ANY),
        out_shape=jax.ShapeDtypeStruct((npair, n1, n2, cw), I32),
        scratch_shapes=[pltpu.VMEM((2, n2c, 2, r1, cw), F32), pltpu.VMEM((2, n2c, n1, cw), I32),
                        pltpu.SemaphoreType.DMA((2, n2c)), pltpu.SemaphoreType.DMA((2, n2c))],
        compiler_params=_cparams(("arbitrary",)),
        name="dft_a",
    )(u5, mh, ml)


def _twiddle(ar, ai, tr, ti, conj):
    if conj:
        return ar * tr + ai * ti, ai * tr - ar * ti
    return ar * tr - ai * ti, ar * ti + ai * tr


def _stage_b_filter_body(a_ref, tr_ref, ti_ref, gh_ref, gl_ref, s_ref, kr_ref, ki_ref, *, passes, n):
    n2, c = a_ref.shape[1], a_ref.shape[2]
    tr = _lane_tile(tr_ref[...], c)
    ti = _lane_tile(ti_ref[...], c)
    br, bi = _twiddle(a_ref[0], a_ref[1], tr, ti, False)
    d = _dot_const(gh_ref[...], gl_ref[...], jnp.concatenate([br, bi], axis=0), passes)
    scale = 1.0 / (s_ref[0:1, :] * float(n))
    kr_ref[...] = d[:n2] * scale
    ki_ref[...] = d[n2:] * scale


def _stage_b_filter(a, s, tabs, passes):
    n, n1, n2 = tabs["n"], tabs["n1"], tabs["n2"]
    c = a.shape[-1]
    gh, gl = tabs["mat_f"]
    return pl.pallas_call(
        functools.partial(_stage_b_filter_body, passes=passes, n=n),
        grid=(n1,),
        in_specs=[pl.BlockSpec((2, None, n2, c), lambda k: (0, k, 0, 0)),
                  pl.BlockSpec((None, n2, LANES), lambda k: (k, 0, 0)),
                  pl.BlockSpec((None, n2, LANES), lambda k: (k, 0, 0)),
                  pl.BlockSpec(gh.shape, lambda k: (0, 0)),
                  pl.BlockSpec(gl.shape, lambda k: (0, 0)),
                  pl.BlockSpec(s.shape, lambda k: (0, 0))],
        out_specs=[pl.BlockSpec((None, n2, c), lambda k: (k, 0, 0)),
                   pl.BlockSpec((None, n2, c), lambda k: (k, 0, 0))],
        out_shape=[jax.ShapeDtypeStruct((n1, n2, c), F32), jax.ShapeDtypeStruct((n1, n2, c), F32)],
        compiler_params=_cparams(("parallel",)),
        name="dft_b_filter",
    )(a, tabs["tw_r"], tabs["tw_i"], gh, gl, s)


def _stage_b_body(a_ref, kr_ref, ki_ref, tr_ref, ti_ref, gfh_ref, gfl_ref, gih_ref, gil_ref, c_ref, *, passes):
    n2, c = a_ref.shape[1], a_ref.shape[2]
    for kk in range(a_ref.shape[0]):
        tr = _lane_tile(tr_ref[kk], c)
        ti = _lane_tile(ti_ref[kk], c)
        ar, ai = _unpack_pair(a_ref[kk])
        br, bi = _twiddle(ar, ai, tr, ti, False)
        d = _dot_const(gfh_ref[...], gfl_ref[...], jnp.concatenate([br, bi], axis=0), passes)
        dr, di = d[:n2], d[n2:]
        kr, ki = kr_ref[kk], ki_ref[kk]
        yr = dr * kr - di * ki
        yi = dr * ki + di * kr
        e = _dot_const(gih_ref[...], gil_ref[...], jnp.concatenate([yr, yi], axis=0), passes)
        cr, ci = _twiddle(e[:n2], e[n2:], tr, ti, True)
        c_ref[kk] = _pack_pair(cr, ci)


def _stage_b(a, kr, ki, order, tabs, passes):
    n1, n2 = tabs["n1"], tabs["n2"]
    npair = a.shape[0]
    cw = HY_WIDTH
    kb = max(1, min(n1, 1024 // n2))
    gfh, gfl = tabs["mat_f"]
    gih, gil = tabs["mat_i"]
    const = lambda k, p: (0, 0)
    return pl.pallas_call(
        functools.partial(_stage_b_body, passes=passes),
        grid=(n1 // kb, npair),
        in_specs=[pl.BlockSpec((None, kb, n2, cw), lambda k, p: (p, k, 0, 0)),
                  pl.BlockSpec((kb, n2, cw), lambda k, p: (k, 0, order)),
                  pl.BlockSpec((kb, n2, cw), lambda k, p: (k, 0, order)),
                  pl.BlockSpec((kb, n2, LANES), lambda k, p: (k, 0, 0)),
                  pl.BlockSpec((kb, n2, LANES), lambda k, p: (k, 0, 0)),
                  pl.BlockSpec(gfh.shape, const), pl.BlockSpec(gfl.shape, const),
                  pl.BlockSpec(gih.shape, const), pl.BlockSpec(gil.shape, const)],
        out_specs=pl.BlockSpec((None, kb, n2, cw), lambda k, p: (p, k, 0, 0)),
        out_shape=jax.ShapeDtypeStruct(a.shape, I32),
        compiler_params=_cparams(("parallel", "parallel")),
        name="dft_b",
    )(a, kr, ki, tabs["tw_r"], tabs["tw_i"], gfh, gfl, gih, gil)


def _stage_c_body(c_ref, u_ref, g_ref, skip_ref, mch_ref, mcl_ref, mah_ref, mal_ref, y_ref, *rest, passes, fuse_a):
    r1 = u_ref.shape[1]
    n1 = 2 * r1
    for j in range(c_ref.shape[1]):
        cc = jnp.concatenate(_unpack_pair(c_ref[:, j, :]), axis=0)
        y = _dot_const(mch_ref[...], mcl_ref[...], cc, passes)
        u = jnp.concatenate([u_ref[0, :, j, :], u_ref[1, :, j, :]], axis=0)
        g = jnp.concatenate([g_ref[0, :, j, :], g_ref[1, :, j, :]], axis=0)
        yo = g * (y + u * skip_ref[...])
        y_ref[0, :, j, :] = yo[:r1].astype(y_ref.dtype)
        y_ref[1, :, j, :] = yo[r1:].astype(y_ref.dtype)
        if fuse_a:
            a_ref = rest[0]
            r = _dot_const(mah_ref[...], mal_ref[...], yo, passes)
            a_ref[:, j, :] = _pack_pair(r[:n1], r[n1:])


def _stage_c(c, u5, u_col, g5, g_col, skip, tabs, passes, fuse_a, out_dtype, n2c=8):
    n1, n2 = tabs["n1"], tabs["n2"]
    npair, _, r1, _, _ = u5.shape
    cw = HY_WIDTH
    mch, mcl = tabs["mat_c"]
    mah, mal = tabs["mat_a"]
    const = lambda p, j: (0, 0)
    out_shape = [jax.ShapeDtypeStruct((npair, 2, r1, n2, cw), out_dtype)]
    out_specs = [pl.BlockSpec((None, 2, r1, n2c, cw), lambda p, j: (p, 0, 0, j, 0))]
    if fuse_a:
        out_shape.append(jax.ShapeDtypeStruct((npair, n1, n2, cw), I32))
        out_specs.append(pl.BlockSpec((None, n1, n2c, cw), lambda p, j: (p, 0, j, 0)))
    return pl.pallas_call(
        functools.partial(_stage_c_body, passes=passes, fuse_a=fuse_a),
        grid=(npair, n2 // n2c),
        in_specs=[pl.BlockSpec((None, n1, n2c, cw), lambda p, j: (p, 0, j, 0)),
                  pl.BlockSpec((None, 2, r1, n2c, cw), lambda p, j: (p, 0, 0, j, u_col)),
                  pl.BlockSpec((None, 2, r1, n2c, cw), lambda p, j: (p, 0, 0, j, g_col)),
                  pl.BlockSpec((1, cw), const),
                  pl.BlockSpec(mch.shape, const), pl.BlockSpec(mcl.shape, const),
                  pl.BlockSpec(mah.shape, const), pl.BlockSpec(mal.shape, const)],
        out_specs=out_specs,
        out_shape=out_shape,
        compiler_params=_cparams(("parallel", "parallel")),
        name="dft_c",
    )(c, u5, g5, skip, mch, mcl, mah, mal)


def _hyena(zc3, kr, ki, skip, tabs, passes):
    bsz, seq, _ = zc3.shape
    n1, n2 = tabs["n1"], tabs["n2"]
    r1 = n1 // 2
    z5 = zc3.reshape(bsz // 2, 2, r1, n2, 3 * HY_WIDTH)
    a = _stage_a(z5, 2, tabs, passes)
    c = _stage_b(a, kr, ki, 0, tabs, passes)
    y1, a2 = _stage_c(c, z5, 2, z5, 0, skip[0:1].astype(F32), tabs, passes, True, F32)
    c2 = _stage_b(a2, kr, ki, 1, tabs, passes)
    (y2,) = _stage_c(c2, y1, 0, z5, 1, skip[1:2].astype(F32), tabs, passes, False, BF16)
    return y2.reshape(bsz * seq, HY_WIDTH)


def _retention_body(sc_ref, qf_ref, kf_ref, vf_ref, cf_ref, sf_ref, qb_ref, kb_ref, vb_ref, cb_ref, sb_ref,
                    of_ref, ob_ref, state, dmat, qwt, kwt):
    ch = RET_CHUNK
    dh = RET_HEAD_DIM

    @pl.when(pl.program_id(1) == 0)
    def _():
        state[...] = jnp.zeros_like(state)
        ci = lax.broadcasted_iota(I32, (ch, ch), 0).astype(F32)
        mi = lax.broadcasted_iota(I32, (ch, ch), 1).astype(F32)
        lag = ci - mi
        for d in range(2):
            for h in range(RET_HEADS):
                lg = sc_ref[d * RET_HEADS + h]
                if d == 0:
                    dmat[d, h] = jnp.where(lag >= 0, jnp.exp(lg * jnp.maximum(lag, 0.0)), 0.0)
                    qwt[d, h] = jnp.exp(lg * (ci + 1.0))
                    kwt[d, h] = jnp.exp(lg * (ch - 1.0 - ci))
                else:
                    dmat[d, h] = jnp.where(lag < 0, jnp.exp(lg * jnp.maximum(-lag, 0.0)), 0.0)
                    qwt[d, h] = jnp.exp(lg * (ch - ci))
                    kwt[d, h] = jnp.exp(lg * ci)

    scale = RET_HEAD_DIM ** -0.5
    refs = ((qf_ref, kf_ref, vf_ref, cf_ref, sf_ref, of_ref), (qb_ref, kb_ref, vb_ref, cb_ref, sb_ref, ob_ref))
    work = [(d, h) for d in range(2) for h in range(RET_HEADS)]
    qs, ks, qws, kws, vs = {}, {}, {}, {}, {}
    for d, h in work:
        q_ref, k_ref, v_ref, c_ref, s_ref, _ = refs[d]
        sl = slice(h * dh, (h + 1) * dh)
        cosf = c_ref[...]
        sinf = s_ref[...]
        q = q_ref[:, sl].astype(F32)
        k = k_ref[:, sl].astype(F32)
        q = q * cosf + pltpu.roll(q, dh // 2, axis=1) * sinf
        k = (k * cosf + pltpu.roll(k, dh // 2, axis=1) * sinf) * scale
        qs[d, h], ks[d, h] = q.astype(BF16), k.astype(BF16)
        qws[d, h], kws[d, h] = (q * qwt[d, h]).astype(BF16), (k * kwt[d, h]).astype(BF16)
        vs[d, h] = v_ref[:, sl]
    scores = {dh_: _dot_nt(qs[dh_], ks[dh_]) for dh_ in work}
    for d, h in work:
        lhs = jnp.concatenate([(scores[d, h] * dmat[d, h]).astype(BF16), qws[d, h]], axis=1)
        rhs = jnp.concatenate([vs[d, h], state[d, h].astype(BF16)], axis=0)
        refs[d][5][:, h * dh:(h + 1) * dh] = _dot(lhs, rhs)
    for d, h in work:
        cdec = sc_ref[2 * RET_HEADS + d * RET_HEADS + h]
        state[d, h] = state[d, h] * cdec + _dot_tn(kws[d, h], vs[d, h])


def _retention(zret3, decay_logit):
    bsz, seq, _ = zret3.shape
    ch = RET_CHUNK
    nc = seq // ch
    half = RET_HEAD_DIM // 2
    inv_freq = ROPE_BASE ** (-jnp.arange(half, dtype=F32) / half)
    ang = jnp.arange(seq, dtype=F32)[:, None] * inv_freq[None, :]
    cosf = jnp.concatenate([jnp.cos(ang), jnp.cos(ang)], axis=1)
    sinf = jnp.concatenate([-jnp.sin(ang), jnp.sin(ang)], axis=1)
    log_g = jax.nn.log_sigmoid(decay_logit.astype(F32)).reshape(-1)
    scal = jnp.concatenate([log_g, jnp.exp(log_g * ch)])
    w = RET_WIDTH
    fwd = lambda col: pl.BlockSpec((None, ch, w), lambda b, n, sc: (b, n, col))
    bwd = lambda col: pl.BlockSpec((None, ch, w), lambda b, n, sc: (b, nc - 1 - n, col))
    rope_f = pl.BlockSpec((ch, RET_HEAD_DIM), lambda b, n, sc: (n, 0))
    rope_b = pl.BlockSpec((ch, RET_HEAD_DIM), lambda b, n, sc: (nc - 1 - n, 0))
    grid_spec = pltpu.PrefetchScalarGridSpec(
        num_scalar_prefetch=1,
        grid=(bsz, nc),
        in_specs=[fwd(0), fwd(1), fwd(2), rope_f, rope_f, bwd(0), bwd(1), bwd(2), rope_b, rope_b],
        out_specs=[pl.BlockSpec((None, ch, w), lambda b, n, sc: (b, n, 0)),
                   pl.BlockSpec((None, ch, w), lambda b, n, sc: (b, nc - 1 - n, 0))],
        scratch_shapes=[pltpu.VMEM((2, RET_HEADS, RET_HEAD_DIM, RET_HEAD_DIM), F32),
                        pltpu.VMEM((2, RET_HEADS, ch, ch), F32),
                        pltpu.VMEM((2, RET_HEADS, ch, RET_HEAD_DIM), F32),
                        pltpu.VMEM((2, RET_HEADS, ch, RET_HEAD_DIM), F32)],
    )
    return pl.pallas_call(
        _retention_body,
        grid_spec=grid_spec,
        out_shape=[jax.ShapeDtypeStruct((bsz, seq, w), F32), jax.ShapeDtypeStruct((bsz, seq, w), F32)],
        compiler_params=_cparams(("arbitrary", "arbitrary")),
        name="retention",
    )(scal, zret3, zret3, zret3, cosf, sinf, zret3, zret3, zret3, cosf, sinf)


def _outproj_body(x_ref, yhy_ref, of_ref, ob_ref, gr_ref, ghy_ref, gret_ref, whyo_ref, wreto_ref, wo_ref,
                  n2g_ref, wrt_ref, x1_ref, h2_ref, pt_ref):
    dh = RET_HEAD_DIM
    o = of_ref[...] + ob_ref[...]
    parts = []
    for h in range(RET_HEADS):
        oh = o[:, h * dh:(h + 1) * dh]
        parts.append(oh * lax.rsqrt(jnp.mean(oh * oh, axis=-1, keepdims=True) + EPS))
    on = jnp.concatenate(parts, axis=1)
    gr = gr_ref[...].astype(F32)
    ret = (gr * jax.nn.sigmoid(gr)) * on
    y_ret = _dot(ret.astype(BF16), wreto_ref[...])
    y_hy = _dot(yhy_ref[...], whyo_ref[...])
    merged = jax.nn.sigmoid(ghy_ref[...].astype(F32)) * y_hy + jax.nn.sigmoid(gret_ref[...].astype(F32)) * y_ret
    x1 = x_ref[...] + _dot(merged.astype(BF16), wo_ref[...])
    x1_ref[...] = x1
    h2 = x1 * lax.rsqrt(jnp.mean(x1 * x1, axis=-1, keepdims=True) + EPS) * n2g_ref[...]
    h2_ref[...] = h2
    logits = _dot_nt(wrt_ref[...], h2.astype(BF16))
    m = jnp.max(logits, axis=0, keepdims=True)
    e = jnp.exp(logits - m)
    pt_ref[...] = e / jnp.sum(e, axis=0, keepdims=True)


def _outproj(x2, yhy, o_f, o_b, zret, zg, whyo, wreto, wo, n2g, wrt, tm=256):
    n_tok = x2.shape[0]
    d = D_MODEL
    row = lambda w, col=0: pl.BlockSpec((tm, w), lambda i: (i, col))
    const = lambda shape: pl.BlockSpec(shape, lambda i: (0, 0))
    return pl.pallas_call(
        _outproj_body,
        grid=(n_tok // tm,),
        in_specs=[row(d), row(HY_WIDTH), row(RET_WIDTH), row(RET_WIDTH), row(RET_WIDTH, 3), row(d, 0), row(d, 1),
                  const((HY_WIDTH, d)), const((RET_WIDTH, d)), const((d, d)), const((1, d)), const((N_EXPERTS, d))],
        out_specs=[row(d), row(d), pl.BlockSpec((N_EXPERTS, tm), lambda i: (0, i))],
        out_shape=[jax.ShapeDtypeStruct((n_tok, d), F32), jax.ShapeDtypeStruct((n_tok, d), F32),
                   jax.ShapeDtypeStruct((N_EXPERTS, n_tok), F32)],
        compiler_params=_cparams(("parallel",)),
        name="outproj_router",
    )(x2, yhy, o_f, o_b, zret, zg, zg, whyo, wreto, wo, n2g, wrt)


def _select_body(p_ref, upper_ref, lower_ref, lowinc_ref, eye_ref, pos_ref, idx_ref, gate_ref, lo_ref, *, cap):
    rows = p_ref.shape[0]
    p = p_ref[...]
    bits = lax.bitcast_convert_type(p, I32)

    def count(mask):
        return jnp.sum(jnp.sum(mask.astype(F32), axis=1, keepdims=True), axis=0, keepdims=True)

    def bit_step(i, thr):
        cand = thr | jnp.left_shift(jnp.int32(1), 30 - i)
        return jnp.where(count(bits >= cand) >= cap, cand, thr)

    thr = lax.fori_loop(0, 31, bit_step, jnp.zeros((1, 1), I32))
    gt = bits > thr
    eq = bits == thr
    need = cap - count(gt)

    def prefix(mask_f):
        incl = _dot(mask_f.astype(BF16), upper_ref[...])
        tot = jnp.broadcast_to(incl[:, LANES - 1:LANES], incl.shape)
        base = _dot(lower_ref[...], tot.astype(BF16))
        return incl, base, tot

    eq_f = eq.astype(F32)
    incl_e, base_e, _ = prefix(eq_f)
    sel = gt | (eq & (base_e + incl_e - eq_f < need))
    sel_f = sel.astype(F32)
    incl, base, tot = prefix(sel_f)
    pos_ref[...] = jnp.where(sel, (base + incl - 1.0).astype(I32), -1)
    lo_ref[...] = base.astype(I32)

    rowend = base + tot
    incl_t = _dot_nt(lowinc_ref[...], sel_f.astype(BF16)).astype(BF16)
    p_t = []
    rem = p
    for _ in range(3):
        part = rem.astype(BF16)
        rem = rem - part.astype(F32)
        p_t.append(_dot_nt(eye_ref[...], part).astype(BF16))
    table = jnp.concatenate([incl_t] + p_t, axis=0)
    r_iota = lax.broadcasted_iota(I32, (rows, LANES), 0).astype(F32)
    lane_iota = lax.broadcasted_iota(I32, (LANES, LANES), 0).astype(F32)

    def slot_tile(ts, carry):
        s = (ts * LANES + lax.broadcasted_iota(I32, (1, LANES), 1)).astype(F32)
        done = rowend <= s
        row = jnp.sum(done.astype(F32), axis=0, keepdims=True)
        before = jnp.sum(jnp.where(done, tot, 0.0), axis=0, keepdims=True)
        onehot_t = (r_iota == row).astype(BF16)
        got = _dot(table, onehot_t)
        g_t = got[:LANES]
        lane = jnp.sum((g_t <= s - before).astype(F32), axis=0, keepdims=True)
        idx_ref[pl.ds(ts, 1), :] = (row * LANES + lane).astype(I32)
        p_row = got[LANES:2 * LANES] + got[2 * LANES:3 * LANES] + got[3 * LANES:]
        gate_ref[pl.ds(ts, 1), :] = jnp.sum(jnp.where(lane_iota == lane, p_row, 0.0), axis=0, keepdims=True)
        return carry

    lax.fori_loop(0, cap // LANES, slot_tile, 0)


def _select(pt3, cap):
    n_e, rows, _ = pt3.shape
    ii = np.arange(LANES)
    upper = jnp.asarray(ii[:, None] <= ii[None, :], BF16)
    lowinc = jnp.asarray(ii[None, :] <= ii[:, None], BF16)
    eye = jnp.asarray(ii[None, :] == ii[:, None], BF16)
    rr = np.arange(rows)
    lower = jnp.asarray(rr[None, :] < rr[:, None], BF16)
    const = lambda shape: pl.BlockSpec(shape, lambda e: (0, 0))
    tok_spec = pl.BlockSpec((None, rows, LANES), lambda e: (e, 0, 0))
    slot_spec = pl.BlockSpec((None, cap // LANES, LANES), lambda e: (e, 0, 0))
    return pl.pallas_call(
        functools.partial(_select_body, cap=cap),
        grid=(n_e,),
        in_specs=[tok_spec, const((LANES, LANES)), const((rows, rows)), const((LANES, LANES)), const((LANES, LANES))],
        out_specs=[tok_spec, slot_spec, slot_spec, tok_spec],
        out_shape=[jax.ShapeDtypeStruct((n_e, rows, LANES), I32),
                   jax.ShapeDtypeStruct((n_e, cap // LANES, LANES), I32),
                   jax.ShapeDtypeStruct((n_e, cap // LANES, LANES), F32),
                   jax.ShapeDtypeStruct((n_e, rows, LANES), I32)],
        compiler_params=_cparams(("parallel",)),
        name="expert_select",
    )(pt3, upper, lower, lowinc, eye)


def _ffn_body(idx_ref, h_hbm, gate_ref, wg_ref, wu_ref, wd_ref, ye_ref, xbuf, xb, sems, *, tiles_per_expert):
    s = xbuf.shape[1]
    step = pl.program_id(0) * tiles_per_expert + pl.program_id(1)
    n_steps = pl.num_programs(0) * tiles_per_expert
    cur = step % 2

    def row_copy(st, i, buf):
        tok = idx_ref[st * s + i]
        return pltpu.make_async_copy(h_hbm.at[pl.ds(tok, 1), :], xbuf.at[buf, pl.ds(i, 1), :], sems.at[buf])

    def wait_rows(buf):
        pltpu.make_async_copy(h_hbm.at[pl.ds(0, s), :], xbuf.at[buf], sems.at[buf]).wait()

    @pl.when(step == 0)
    def _():
        def one(i, carry):
            row_copy(0, i, 0).start()
            return carry
        lax.fori_loop(0, s, one, 0, unroll=8)

    @pl.when(step + 1 < n_steps)
    def _():
        for i in range(s):
            row_copy(step + 1, i, 1 - cur).start()

    wait_rows(cur)
    xb[...] = xbuf[cur].astype(BF16)
    y = None
    for f0 in range(0, EXPERT_FF, FF_CHUNK):
        f1 = min(f0 + FF_CHUNK, EXPERT_FF)
        a = _dot(xb[...], wg_ref[:, f0:f1])
        b = _dot(xb[...], wu_ref[:, f0:f1])
        hid = ((a * jax.nn.sigmoid(a)) * b).astype(BF16)
        part = _dot(hid, wd_ref[f0:f1, :])
        y = part if y is None else y + part
    for k in range(s // LANES):
        col = jnp.transpose(jnp.broadcast_to(gate_ref[k:k + 1, :], (LANES, LANES)))[:, 0:1]
        ye_ref[k * LANES:(k + 1) * LANES, :] = (y[k * LANES:(k + 1) * LANES] * col).astype(ye_ref.dtype)


def _expert_ffn(idx_flat, h2, gates, wg, wu, wd, cap, s=512):
    s = min(s, cap)
    tiles = cap // s
    g3 = gates.reshape(N_EXPERTS * tiles, s // LANES, LANES)
    grid_spec = pltpu.PrefetchScalarGridSpec(
        num_scalar_prefetch=1,
        grid=(N_EXPERTS, tiles),
        in_specs=[pl.BlockSpec(memory_space=pl.ANY),
                  pl.BlockSpec((None, s // LANES, LANES), lambda e, j, idx: (e * tiles + j, 0, 0)),
                  pl.BlockSpec((None, D_MODEL, EXPERT_FF), lambda e, j, idx: (e, 0, 0)),
                  pl.BlockSpec((None, D_MODEL, EXPERT_FF), lambda e, j, idx: (e, 0, 0)),
                  pl.BlockSpec((None, EXPERT_FF, D_MODEL), lambda e, j, idx: (e, 0, 0))],
        out_specs=pl.BlockSpec((s, D_MODEL), lambda e, j, idx: (e * tiles + j, 0)),
        scratch_shapes=[pltpu.VMEM((2, s, D_MODEL), F32), pltpu.VMEM((s, D_MODEL), BF16),
                        pltpu.SemaphoreType.DMA((2,))],
    )
    return pl.pallas_call(
        functools.partial(_ffn_body, tiles_per_expert=tiles),
        grid_spec=grid_spec,
        out_shape=jax.ShapeDtypeStruct((N_EXPERTS * cap, D_MODEL), BF16),
        compiler_params=_cparams(("arbitrary", "arbitrary")),
        name="expert_ffn",
    )(idx_flat, h2, g3, wg, wu, wd)


def _combine_body(lo_ref, np_ref, x1_ref, pos_ref, nfg_ref, expand_ref, ye_hbm, o_ref, win, sems, *, cap, rows_total):
    r = pl.program_id(0)
    n_rows = pl.num_programs(0)
    w = COMBINE_WIN
    cur = r % 2

    def starts(rr, m):
        out = []
        for e in range(N_EXPERTS):
            first = e * cap + lo_ref[e * n_rows + rr]
            intended = (first // COMBINE_ALIGN) * COMBINE_ALIGN + m * w
            actual = pl.multiple_of(jnp.minimum(intended, rows_total - w), COMBINE_ALIGN)
            out.append((intended, actual))
        return out

    def copies(rr, m, buf):
        return [pltpu.make_async_copy(ye_hbm.at[pl.ds(actual, w), :], win.at[buf, pl.ds(e * w, w), :], sems.at[buf, e])
                for e, (_, actual) in enumerate(starts(rr, m))]

    def contribution(rr, m, buf):
        lane = lax.broadcasted_iota(I32, (1, N_EXPERTS), 1)
        intended = jnp.zeros((1, N_EXPERTS), I32)
        actual = jnp.zeros((1, N_EXPERTS), I32)
        for e, (i_s, a_s) in enumerate(starts(rr, m)):
            intended = jnp.where(lane == e, i_s, intended)
            actual = jnp.where(lane == e, a_s, actual)
        pos = pos_ref[...]
        glob = pos + lane * cap
        rel = glob - intended
        valid = (pos >= 0) & (rel >= 0) & (rel < w)
        local = jnp.where(valid, glob - actual, -1).astype(F32).astype(BF16)
        spread = _dot(local, expand_ref[...])
        col = (lax.broadcasted_iota(I32, (1, N_EXPERTS * w), 1) & (w - 1)).astype(F32)
        onehot = (spread == col).astype(BF16)
        return _dot(onehot, win[buf])

    @pl.when(r == 0)
    def _():
        for cp in copies(0, 0, 0):
            cp.start()

    @pl.when(r + 1 < n_rows)
    def _():
        for cp in copies(r + 1, 0, 1 - cur):
            cp.start()

    for cp in copies(r, 0, cur):
        cp.wait()
    acc = x1_ref[...] + contribution(r, 0, cur)

    def extra_pass(m, acc):
        for cp in copies(r, m, cur):
            cp.start()
        for cp in copies(r, m, cur):
            cp.wait()
        return acc + contribution(r, m, cur)

    acc = lax.fori_loop(1, np_ref[r], extra_pass, acc)
    o_ref[...] = acc * lax.rsqrt(jnp.mean(acc * acc, axis=-1, keepdims=True) + EPS) * nfg_ref[...]


def _combine(lo, x1, pos_t, nfg, ye, cap):
    n_tok = x1.shape[0]
    tm = LANES
    w = COMBINE_WIN
    rows_total = ye.shape[0]
    n_rows = n_tok // tm
    nxt = jnp.concatenate([lo[:, 1:], jnp.full((N_EXPERTS, 1), cap, I32)], axis=1)
    span = lo % COMBINE_ALIGN + (nxt - lo)
    n_pass = jnp.maximum(jnp.max((span + w - 1) // w, axis=0), 1).astype(I32)
    ee = np.arange(N_EXPERTS)
    expand = jnp.asarray(ee[:, None] == (np.arange(N_EXPERTS * w) // w)[None, :], BF16)
    grid_spec = pltpu.PrefetchScalarGridSpec(
        num_scalar_prefetch=2,
        grid=(n_rows,),
        in_specs=[pl.BlockSpec((tm, D_MODEL), lambda i, lo_, np_: (i, 0)),
                  pl.BlockSpec((tm, N_EXPERTS), lambda i, lo_, np_: (i, 0)),
                  pl.BlockSpec((1, D_MODEL), lambda i, lo_, np_: (0, 0)),
                  pl.BlockSpec((N_EXPERTS, N_EXPERTS * w), lambda i, lo_, np_: (0, 0)),
                  pl.BlockSpec(memory_space=pl.ANY)],
        out_specs=pl.BlockSpec((tm, D_MODEL), lambda i, lo_, np_: (i, 0)),
        scratch_shapes=[pltpu.VMEM((2, N_EXPERTS * w, D_MODEL), BF16),
                        pltpu.SemaphoreType.DMA((2, N_EXPERTS))],
    )
    return pl.pallas_call(
        functools.partial(_combine_body, cap=cap, rows_total=rows_total),
        grid_spec=grid_spec,
        out_shape=jax.ShapeDtypeStruct((n_tok, D_MODEL), F32),
        compiler_params=_cparams(("arbitrary",)),
        name="moe_combine_norm",
    )(lo.reshape(-1), n_pass, x1, pos_t, nfg, expand, ye)


def _trunk(x, w, fft_passes=1, filter_passes=1):
    bsz, seq, d = x.shape
    n_tok = bsz * seq
    x2 = x.reshape(n_tok, d)
    tabs = _fft_tables(seq)

    zhy, zret, zg = _inproj(x2, w["norm1_g"], w["w_in"])
    zc = _shortconv(zhy.reshape(bsz, seq, -1), w["hy_conv_w"], w["hy_conv_b"])

    h_time, h_abs = _hyena_filter_time(seq, w["hy_w1"], w["hy_b1"], w["hy_freq1"], w["hy_w2"], w["hy_b2"],
                                       w["hy_freq2"], w["hy_w3"])
    kr, ki = _stage_b_filter(_stage_a_real(h_time, tabs, filter_passes), h_abs, tabs, filter_passes)
    yhy = _hyena(zc, kr, ki, w["hy_skip"], tabs, fft_passes)

    o_f, o_b = _retention(zret.reshape(bsz, seq, -1), w["ret_decay_logit"])
    x1, h2, pt = _outproj(x2, yhy, o_f.reshape(n_tok, -1), o_b.reshape(n_tok, -1), zret, zg,
                          w["w_hy_out"], w["w_ret_out"], w["w_o"], w["norm2_g"], w["w_router_t"])

    cap = CAPACITY_FACTOR * n_tok // N_EXPERTS
    rows = n_tok // LANES
    pos, idx, gates, lo = _select(pt.reshape(N_EXPERTS, rows, LANES), cap)
    ye = _expert_ffn(idx.reshape(-1), h2, gates, w["w_gate"], w["w_up"], w["w_down"], cap)
    pos_t = pos.reshape(N_EXPERTS, n_tok).T
    y = _combine(lo[:, :, 0], x1, pos_t, w["norm_f_g"], ye, cap)
    return y.reshape(bsz, seq, d)


def kernel(x_prompt, x_sample, norm1_g, w_in, hy_conv_w, hy_conv_b, hy_w1, hy_b1, hy_freq1, hy_w2, hy_b2, hy_freq2,
           hy_w3, hy_skip, ret_decay_logit, w_hy_out, w_ret_out, w_o, norm2_g, w_router, w_gate, w_up, w_down,
           norm_f_g):
    layer = 0
    w = dict(
        norm1_g=norm1_g[layer].astype(F32)[None], w_in=w_in[layer].astype(BF16),
        hy_conv_w=hy_conv_w[layer].astype(F32), hy_conv_b=hy_conv_b[layer].astype(F32)[None],
        hy_w1=hy_w1[layer], hy_b1=hy_b1[layer], hy_freq1=hy_freq1[layer], hy_w2=hy_w2[layer], hy_b2=hy_b2[layer],
        hy_freq2=hy_freq2[layer], hy_w3=hy_w3[layer], hy_skip=hy_skip[layer],
        ret_decay_logit=ret_decay_logit[layer],
        w_hy_out=w_hy_out[layer].astype(BF16), w_ret_out=w_ret_out[layer].astype(BF16), w_o=w_o[layer].astype(BF16),
        norm2_g=norm2_g[layer].astype(F32)[None], w_router_t=w_router[layer].T.astype(BF16),
        w_gate=w_gate[layer].astype(BF16), w_up=w_up[layer].astype(BF16), w_down=w_down[layer].astype(BF16),
        norm_f_g=norm_f_g.astype(F32)[None],
    )
    return _trunk(x_prompt, w), _trunk(x_sample, w)
```

```python
import functools
import math

import numpy as np
import jax
import jax.numpy as jnp
from jax import lax
from jax.experimental import pallas as pl
from jax.experimental.pallas import tpu as pltpu

F32 = jnp.float32
BF16 = jnp.bfloat16
I32 = jnp.int32

D_MODEL = 1024
HY_WIDTH = 512
HY_ORDER = 2
HY_BANDS = 16
HY_FILTER_WIDTH = 64
HY_FAST_DECAY = 0.3
HY_SLOW_DECAY = 1.5
HY_DECAY_TARGET = 1e-2
RET_WIDTH = 512
RET_HEADS = 4
RET_HEAD_DIM = 128
RET_CHUNK = 128
ROPE_BASE = 10000.0
N_EXPERTS = 16
EXPERT_FF = 1408
CAPACITY_FACTOR = 2
EPS = 1e-6

LANES = 128
EMB_PAD = 128
DFT_N1_MAX = 128
FF_CHUNK = 256
COMBINE_ALIGN = 16
COMBINE_WIN = 64


def _cparams(sem, vmem_mb=48):
    return pltpu.CompilerParams(dimension_semantics=sem, vmem_limit_bytes=vmem_mb * 1024 * 1024)


def _dot(a, b):
    return jnp.dot(a, b, preferred_element_type=F32)


def _dot_nt(a, b):
    return lax.dot_general(a, b, (((1,), (1,)), ((), ())), preferred_element_type=F32)


def _dot_tn(a, b):
    return lax.dot_general(a, b, (((0,), (0,)), ((), ())), preferred_element_type=F32)


def _split(a):
    hi = a.astype(BF16)
    lo = (a - hi.astype(F32)).astype(BF16)
    return hi, lo


def _dot_const(m_hi, m_lo, x, passes):
    xh = x.astype(BF16)
    r = _dot(m_hi, xh)
    if passes >= 3:
        xl = (x - xh.astype(F32)).astype(BF16)
        r = r + _dot(m_lo, xh) + _dot(m_hi, xl)
    return r


def _dot3(a, b):
    ah, al = _split(a)
    bh, bl = _split(b)
    return _dot(ah, bh) + _dot(al, bh) + _dot(ah, bl)


def _inproj_body(x_ref, g_ref, w_ref, zhy_ref, zret_ref, zg_ref):
    x = x_ref[...]
    ms = jnp.mean(x * x, axis=-1, keepdims=True)
    h = (x * lax.rsqrt(ms + EPS) * g_ref[...]).astype(BF16)
    n_hy = zhy_ref.shape[1]
    n_ret = zret_ref.shape[1]
    zhy_ref[...] = _dot(h, w_ref[:, :n_hy])
    zret_ref[...] = _dot(h, w_ref[:, n_hy:n_hy + n_ret]).astype(BF16)
    zg_ref[...] = _dot(h, w_ref[:, n_hy + n_ret:]).astype(BF16)


def _inproj(x2, g, w_bf, tm=256):
    n_tok = x2.shape[0]
    n_hy, n_ret, n_g = 3 * HY_WIDTH, 4 * RET_WIDTH, 2 * D_MODEL
    return pl.pallas_call(
        _inproj_body,
        grid=(n_tok // tm,),
        in_specs=[pl.BlockSpec((tm, D_MODEL), lambda i: (i, 0)),
                  pl.BlockSpec((1, D_MODEL), lambda i: (0, 0)),
                  pl.BlockSpec((D_MODEL, n_hy + n_ret + n_g), lambda i: (0, 0))],
        out_specs=[pl.BlockSpec((tm, n_hy), lambda i: (i, 0)),
                   pl.BlockSpec((tm, n_ret), lambda i: (i, 0)),
                   pl.BlockSpec((tm, n_g), lambda i: (i, 0))],
        out_shape=[jax.ShapeDtypeStruct((n_tok, n_hy), F32),
                   jax.ShapeDtypeStruct((n_tok, n_ret), BF16),
                   jax.ShapeDtypeStruct((n_tok, n_g), BF16)],
        compiler_params=_cparams(("parallel",)),
        name="inproj",
    )(x2, g, w_bf)


def _shortconv_body(z_ref, zp_ref, zn_ref, w_ref, b_ref, o_ref):
    tl = z_ref.shape[0]
    i = pl.program_id(1)
    z = z_ref[...]
    prev_row = jnp.where(i > 0, zp_ref[7:8, :], 0.0)
    next_row = jnp.where(i < pl.num_programs(1) - 1, zn_ref[0:1, :], 0.0)
    row = lax.broadcasted_iota(I32, (tl, 1), 0)
    zm1 = jnp.where(row == 0, prev_row, pltpu.roll(z, 1, axis=0))
    zp1 = jnp.where(row == tl - 1, next_row, pltpu.roll(z, tl - 1, axis=0))
    o_ref[...] = zm1 * w_ref[0:1, :] + z * w_ref[1:2, :] + zp1 * w_ref[2:3, :] + b_ref[...]


def _shortconv(z3, w, b, tl=256):
    bsz, seq, width = z3.shape
    tl = min(tl, seq)
    nb8 = seq // 8
    r8 = tl // 8
    return pl.pallas_call(
        _shortconv_body,
        grid=(bsz, seq // tl),
        in_specs=[pl.BlockSpec((None, tl, width), lambda b_, i: (b_, i, 0)),
                  pl.BlockSpec((None, 8, width), lambda b_, i: (b_, jnp.maximum(i * r8 - 1, 0), 0)),
                  pl.BlockSpec((None, 8, width), lambda b_, i: (b_, jnp.minimum((i + 1) * r8, nb8 - 1), 0)),
                  pl.BlockSpec((3, width), lambda b_, i: (0, 0)),
                  pl.BlockSpec((1, width), lambda b_, i: (0, 0))],
        out_specs=pl.BlockSpec((None, tl, width), lambda b_, i: (b_, i, 0)),
        out_shape=jax.ShapeDtypeStruct((bsz, seq, width), F32),
        compiler_params=_cparams(("parallel", "parallel")),
        name="shortconv",
    )(z3, z3, z3, w, b)


def _fft_dims(seq):
    n = 2 * seq
    n1 = min(DFT_N1_MAX, 1 << (n.bit_length() // 2))
    n2 = n // n1
    assert n1 * n2 == n and n1 % 2 == 0
    return n, n1, n2


def _hi_lo_const(m):
    m = np.asarray(m, np.float64)
    hi = jnp.asarray(m, F32).astype(BF16)
    lo = (jnp.asarray(m, F32) - hi.astype(F32)).astype(BF16)
    return hi, lo


def _fft_tables(seq):
    n, n1, n2 = _fft_dims(seq)
    r1 = n1 // 2
    a1 = 2.0 * np.pi * np.outer(np.arange(n1), np.arange(n1)) / n1
    fr, fi = np.cos(a1), -np.sin(a1)
    mat_a = np.block([[fr[:, :r1], -fi[:, :r1]], [fi[:, :r1], fr[:, :r1]]])
    mat_a_real = np.concatenate([fr, fi], axis=0)
    mat_c = np.block([[fr[:r1], fi[:r1]], [-fi[:r1], fr[:r1]]])
    a2 = 2.0 * np.pi * np.outer(np.arange(n2), np.arange(n2)) / n2
    gr, gi = np.cos(a2), -np.sin(a2)
    mat_f = np.block([[gr, -gi], [gi, gr]])
    mat_i = np.block([[gr, gi], [-gi, gr]])
    at = 2.0 * np.pi * (np.outer(np.arange(n1), np.arange(n2)) % n) / n
    tw_r = jnp.broadcast_to(jnp.asarray(np.cos(at), F32)[:, :, None], (n1, n2, LANES))
    tw_i = jnp.broadcast_to(jnp.asarray(-np.sin(at), F32)[:, :, None], (n1, n2, LANES))
    return dict(n=n, n1=n1, n2=n2, mat_a=_hi_lo_const(mat_a), mat_a_real=_hi_lo_const(mat_a_real),
                mat_c=_hi_lo_const(mat_c), mat_f=_hi_lo_const(mat_f), mat_i=_hi_lo_const(mat_i),
                tw_r=tw_r, tw_i=tw_i)


def _lane_tile(t, width):
    return jnp.concatenate([t] * (width // LANES), axis=1) if width > LANES else t


def _filter_body(z_ref, w1_ref, b1_ref, f1_ref, w2_ref, b2_ref, f2_ref, w3_ref, dl_ref, h_ref, s_ref, *, seq):
    i = pl.program_id(0)
    tl = z_ref.shape[0]
    z = z_ref[...]
    a = jnp.sin(f1_ref[...] * (_dot3(z, w1_ref[...]) + b1_ref[...]))
    a = jnp.sin(f2_ref[...] * (_dot3(a, w2_ref[...]) + b2_ref[...]))
    h = _dot3(a, w3_ref[...])
    win = jnp.exp(-z[:, 0:1] * dl_ref[...])
    h = h * jnp.concatenate([win] * HY_ORDER, axis=1)
    row = i * tl + lax.broadcasted_iota(I32, (tl, 1), 0)
    h = jnp.where(row == seq, 0.0, h)
    h_ref[...] = h

    @pl.when(i == 0)
    def _():
        s_ref[...] = jnp.zeros_like(s_ref)

    s_ref[...] += jnp.broadcast_to(jnp.sum(jnp.abs(h), axis=0, keepdims=True), s_ref.shape)


def _hyena_filter_time(seq, w1, b1, f1, w2, b2, f2, w3, tl=512):
    n = 2 * seq
    tl = min(tl, seq)
    idx = jnp.arange(n, dtype=I32)
    p = jnp.minimum(jnp.where(idx < seq, idx, n - idx), seq - 1).astype(F32)
    t = p / float(seq - 1)
    ang = 2.0 * math.pi * p / seq
    bands = jnp.linspace(1e-4, HY_BANDS - 1, HY_BANDS, dtype=F32)
    phase = ang[:, None] * bands[None, :]
    emb = jnp.concatenate([t[:, None], jnp.cos(phase), -jnp.sin(phase)], axis=-1)
    emb = jnp.pad(emb, ((0, 0), (0, EMB_PAD - emb.shape[1])))
    w1p = jnp.pad(w1.astype(F32), ((0, EMB_PAD - w1.shape[0]), (0, 0)))
    w3d = w3.astype(F32).reshape(HY_FILTER_WIDTH, HY_ORDER, 2, HY_WIDTH).transpose(2, 0, 1, 3)
    w3d = w3d.reshape(2, HY_FILTER_WIDTH, HY_ORDER * HY_WIDTH)
    deltas = jnp.abs(jnp.linspace(math.log(HY_DECAY_TARGET) / HY_SLOW_DECAY,
                                  math.log(HY_DECAY_TARGET) / HY_FAST_DECAY, HY_WIDTH, dtype=F32))[None, :]
    fw = HY_FILTER_WIDTH
    nblk_half = seq // tl
    const = lambda i: (0, 0)
    return pl.pallas_call(
        functools.partial(_filter_body, seq=seq),
        grid=(n // tl,),
        in_specs=[pl.BlockSpec((tl, EMB_PAD), lambda i: (i, 0)),
                  pl.BlockSpec((EMB_PAD, fw), const), pl.BlockSpec((1, fw), const), pl.BlockSpec((1, fw), const),
                  pl.BlockSpec((fw, fw), const), pl.BlockSpec((1, fw), const), pl.BlockSpec((1, fw), const),
                  pl.BlockSpec((None, fw, HY_ORDER * HY_WIDTH), lambda i: (i // nblk_half, 0, 0)),
                  pl.BlockSpec((1, HY_WIDTH), const)],
        out_specs=[pl.BlockSpec((tl, HY_ORDER * HY_WIDTH), lambda i: (i, 0)),
                   pl.BlockSpec((8, HY_ORDER * HY_WIDTH), const)],
        out_shape=[jax.ShapeDtypeStruct((n, HY_ORDER * HY_WIDTH), F32),
                   jax.ShapeDtypeStruct((8, HY_ORDER * HY_WIDTH), F32)],
        compiler_params=_cparams(("arbitrary",)),
        name="hyena_filter",
    )(emb, w1p, b1.astype(F32)[None], f1.astype(F32)[None], w2.astype(F32), b2.astype(F32)[None],
      f2.astype(F32)[None], w3d, deltas)


def _stage_a_real_body(h_ref, mh_ref, ml_ref, a_ref, *, passes):
    n1 = h_ref.shape[0]
    h_t = jnp.swapaxes(h_ref[...], 0, 1)
    rs = [_dot_const(mh_ref[...], ml_ref[...], h_t[j], passes) for j in range(h_ref.shape[1])]
    r = jnp.swapaxes(jnp.stack(rs, axis=0), 0, 1)
    a_ref[0] = r[:n1]
    a_ref[1] = r[n1:]


def _stage_a_real(h, tabs, passes, n2c=8, cw=512):
    n1, n2 = tabs["n1"], tabs["n2"]
    c = h.shape[1]
    h3 = h.reshape(n1, n2, c)
    mh, ml = tabs["mat_a_real"]
    return pl.pallas_call(
        functools.partial(_stage_a_real_body, passes=passes),
        grid=(n2 // n2c, c // cw),
        in_specs=[pl.BlockSpec((n1, n2c, cw), lambda j, k: (0, j, k)),
                  pl.BlockSpec(mh.shape, lambda j, k: (0, 0)),
                  pl.BlockSpec(ml.shape, lambda j, k: (0, 0))],
        out_specs=pl.BlockSpec((2, n1, n2c, cw), lambda j, k: (0, 0, j, k)),
        out_shape=jax.ShapeDtypeStruct((2, n1, n2, c), F32),
        compiler_params=_cparams(("parallel", "parallel")),
        name="dft_a_filter",
    )(h3, mh, ml)


def _pack_pair(re, im):
    hi = lax.bitcast_convert_type(re.astype(BF16).astype(F32), I32)
    lo = lax.bitcast_convert_type(im.astype(BF16).astype(F32), I32)
    return lax.bitcast_convert_type(hi | lax.shift_right_logical(lo, 16), F32)


def _unpack_pair(packed):
    word = lax.bitcast_convert_type(packed, I32)
    re = lax.bitcast_convert_type(word & jnp.int32(-65536), F32)
    im = lax.bitcast_convert_type(lax.shift_left(word, 16), F32)
    return re, im


def _stage_a_body(u_ref, mh_ref, ml_ref, a_ref, *, passes):
    _, r1, n2c, cw = u_ref.shape
    n1 = 2 * r1
    u_t = jnp.swapaxes(u_ref[...].reshape(n1, n2c, cw), 0, 1)
    packed = []
    for j in range(n2c):
        r = _dot_const(mh_ref[...], ml_ref[...], u_t[j], passes)
        packed.append(_pack_pair(r[:n1], r[n1:]))
    a_ref[...] = jnp.swapaxes(jnp.stack(packed, axis=0), 0, 1)


def _stage_a(u5, col, tabs, passes, n2c=8):
    n1, n2 = tabs["n1"], tabs["n2"]
    npair, _, r1, _, _ = u5.shape
    cw = HY_WIDTH
    mh, ml = tabs["mat_a"]
    return pl.pallas_call(
        functools.partial(_stage_a_body, passes=passes),
        grid=(npair, n2 // n2c),
        in_specs=[pl.BlockSpec((None, 2, r1, n2c, cw), lambda p, j: (p, 0, 0, j, col)),
                  pl.BlockSpec(mh.shape, lambda p, j: (0, 0)),
                  pl.BlockSpec(ml.shape, lambda p, j: (0, 0))],
        out_specs=pl.BlockSpec((None, n1, n2c, cw), lambda p, j: (p, 0, j, 0)),
        out_shape=jax.ShapeDtypeStruct((npair, n1, n2, cw), F32),
        compiler_params=_cparams(("parallel", "parallel")),
        name="dft_a",
    )(u5, mh, ml)


def _twiddle(ar, ai, tr, ti, conj):
    if conj:
        return ar * tr + ai * ti, ai * tr - ar * ti
    return ar * tr - ai * ti, ar * ti + ai * tr


def _stage_b_filter_body(a_ref, tr_ref, ti_ref, gh_ref, gl_ref, s_ref, kr_ref, ki_ref, *, passes, n):
    n2, c = a_ref.shape[1], a_ref.shape[2]
    tr = _lane_tile(tr_ref[...], c)
    ti = _lane_tile(ti_ref[...], c)
    br, bi = _twiddle(a_ref[0], a_ref[1], tr, ti, False)
    d = _dot_const(gh_ref[...], gl_ref[...], jnp.concatenate([br, bi], axis=0), passes)
    scale = 1.0 / (s_ref[0:1, :] * float(n))
    kr_ref[...] = d[:n2] * scale
    ki_ref[...] = d[n2:] * scale


def _stage_b_filter(a, s, tabs, passes):
    n, n1, n2 = tabs["n"], tabs["n1"], tabs["n2"]
    c = a.shape[-1]
    gh, gl = tabs["mat_f"]
    return pl.pallas_call(
        functools.partial(_stage_b_filter_body, passes=passes, n=n),
        grid=(n1,),
        in_specs=[pl.BlockSpec((2, None, n2, c), lambda k: (0, k, 0, 0)),
                  pl.BlockSpec((None, n2, LANES), lambda k: (k, 0, 0)),
                  pl.BlockSpec((None, n2, LANES), lambda k: (k, 0, 0)),
                  pl.BlockSpec(gh.shape, lambda k: (0, 0)),
                  pl.BlockSpec(gl.shape, lambda k: (0, 0)),
                  pl.BlockSpec(s.shape, lambda k: (0, 0))],
        out_specs=[pl.BlockSpec((None, n2, c), lambda k: (k, 0, 0)),
                   pl.BlockSpec((None, n2, c), lambda k: (k, 0, 0))],
        out_shape=[jax.ShapeDtypeStruct((n1, n2, c), F32), jax.ShapeDtypeStruct((n1, n2, c), F32)],
        compiler_params=_cparams(("parallel",)),
        name="dft_b_filter",
    )(a, tabs["tw_r"], tabs["tw_i"], gh, gl, s)


def _stage_b_body(a_ref, kr_ref, ki_ref, tr_ref, ti_ref, gfh_ref, gfl_ref, gih_ref, gil_ref, c_ref, *, passes):
    n2, c = a_ref.shape[1], a_ref.shape[2]
    for kk in range(a_ref.shape[0]):
        tr = _lane_tile(tr_ref[kk], c)
        ti = _lane_tile(ti_ref[kk], c)
        ar, ai = _unpack_pair(a_ref[kk])
        br, bi = _twiddle(ar, ai, tr, ti, False)
        d = _dot_const(gfh_ref[...], gfl_ref[...], jnp.concatenate([br, bi], axis=0), passes)
        dr, di = d[:n2], d[n2:]
        kr, ki = kr_ref[kk], ki_ref[kk]
        yr = dr * kr - di * ki
        yi = dr * ki + di * kr
        e = _dot_const(gih_ref[...], gil_ref[...], jnp.concatenate([yr, yi], axis=0), passes)
        cr, ci = _twiddle(e[:n2], e[n2:], tr, ti, True)
        c_ref[kk] = _pack_pair(cr, ci)


def _stage_b(a, kr, ki, order, tabs, passes):
    n1, n2 = tabs["n1"], tabs["n2"]
    npair = a.shape[0]
    cw = HY_WIDTH
    kb = max(1, min(n1, 1024 // n2))
    gfh, gfl = tabs["mat_f"]
    gih, gil = tabs["mat_i"]
    const = lambda k, p: (0, 0)
    return pl.pallas_call(
        functools.partial(_stage_b_body, passes=passes),
        grid=(n1 // kb, npair),
        in_specs=[pl.BlockSpec((None, kb, n2, cw), lambda k, p: (p, k, 0, 0)),
                  pl.BlockSpec((kb, n2, cw), lambda k, p: (k, 0, order)),
                  pl.BlockSpec((kb, n2, cw), lambda k, p: (k, 0, order)),
                  pl.BlockSpec((kb, n2, LANES), lambda k, p: (k, 0, 0)),
                  pl.BlockSpec((kb, n2, LANES), lambda k, p: (k, 0, 0)),
                  pl.BlockSpec(gfh.shape, const), pl.BlockSpec(gfl.shape, const),
                  pl.BlockSpec(gih.shape, const), pl.BlockSpec(gil.shape, const)],
        out_specs=pl.BlockSpec((None, kb, n2, cw), lambda k, p: (p, k, 0, 0)),
        out_shape=jax.ShapeDtypeStruct(a.shape, F32),
        compiler_params=_cparams(("parallel", "parallel")),
        name="dft_b",
    )(a, kr, ki, tabs["tw_r"], tabs["tw_i"], gfh, gfl, gih, gil)


def _stage_c_body(c_ref, u_ref, g_ref, skip_ref, mch_ref, mcl_ref, mah_ref, mal_ref, y_ref, *rest, passes, fuse_a):
    _, r1, n2c, cw = u_ref.shape
    n1 = 2 * r1
    ys = []
    c_t = jnp.swapaxes(c_ref[...], 0, 1)
    for j in range(n2c):
        cc = jnp.concatenate(_unpack_pair(c_t[j]), axis=0)
        ys.append(_dot_const(mch_ref[...], mcl_ref[...], cc, passes))
    y = jnp.swapaxes(jnp.stack(ys, axis=0), 0, 1)
    u = u_ref[...].reshape(n1, n2c, cw)
    g = g_ref[...].reshape(n1, n2c, cw)
    yo = g * (y + u * skip_ref[...].reshape(1, 1, cw))
    y_ref[...] = yo.reshape(2, r1, n2c, cw).astype(y_ref.dtype)
    if fuse_a:
        a_ref = rest[0]
        yo_t = jnp.swapaxes(yo, 0, 1)
        packed = []
        for j in range(n2c):
            r = _dot_const(mah_ref[...], mal_ref[...], yo_t[j], passes)
            packed.append(_pack_pair(r[:n1], r[n1:]))
        a_ref[...] = jnp.swapaxes(jnp.stack(packed, axis=0), 0, 1)


def _stage_c(c, u5, u_col, g5, g_col, skip, tabs, passes, fuse_a, out_dtype, n2c=8):
    n1, n2 = tabs["n1"], tabs["n2"]
    npair, _, r1, _, _ = u5.shape
    cw = HY_WIDTH
    mch, mcl = tabs["mat_c"]
    mah, mal = tabs["mat_a"]
    const = lambda p, j: (0, 0)
    out_shape = [jax.ShapeDtypeStruct((npair, 2, r1, n2, cw), out_dtype)]
    out_specs = [pl.BlockSpec((None, 2, r1, n2c, cw), lambda p, j: (p, 0, 0, j, 0))]
    if fuse_a:
        out_shape.append(jax.ShapeDtypeStruct((npair, n1, n2, cw), F32))
        out_specs.append(pl.BlockSpec((None, n1, n2c, cw), lambda p, j: (p, 0, j, 0)))
    return pl.pallas_call(
        functools.partial(_stage_c_body, passes=passes, fuse_a=fuse_a),
        grid=(npair, n2 // n2c),
        in_specs=[pl.BlockSpec((None, n1, n2c, cw), lambda p, j: (p, 0, j, 0)),
                  pl.BlockSpec((None, 2, r1, n2c, cw), lambda p, j: (p, 0, 0, j, u_col)),
                  pl.BlockSpec((None, 2, r1, n2c, cw), lambda p, j: (p, 0, 0, j, g_col)),
                  pl.BlockSpec((1, cw), const),
                  pl.BlockSpec(mch.shape, const), pl.BlockSpec(mcl.shape, const),
                  pl.BlockSpec(mah.shape, const), pl.BlockSpec(mal.shape, const)],
        out_specs=out_specs,
        out_shape=out_shape,
        compiler_params=_cparams(("parallel", "parallel")),
        name="dft_c",
    )(c, u5, g5, skip, mch, mcl, mah, mal)


def _hyena(zc3, kr, ki, skip, tabs, passes):
    bsz, seq, _ = zc3.shape
    n1, n2 = tabs["n1"], tabs["n2"]
    r1 = n1 // 2
    z5 = zc3.reshape(bsz // 2, 2, r1, n2, 3 * HY_WIDTH)
    a = _stage_a(z5, 2, tabs, passes)
    c = _stage_b(a, kr, ki, 0, tabs, passes)
    y1, a2 = _stage_c(c, z5, 2, z5, 0, skip[0:1].astype(F32), tabs, passes, True, F32)
    c2 = _stage_b(a2, kr, ki, 1, tabs, passes)
    (y2,) = _stage_c(c2, y1, 0, z5, 1, skip[1:2].astype(F32), tabs, passes, False, BF16)
    return y2.reshape(bsz * seq, HY_WIDTH)


def _retention_body(sc_ref, qf_ref, kf_ref, vf_ref, cf_ref, sf_ref, qb_ref, kb_ref, vb_ref, cb_ref, sb_ref,
                    of_ref, ob_ref, state, dmat, qwt, kwt):
    ch = RET_CHUNK
    dh = RET_HEAD_DIM

    @pl.when(pl.program_id(1) == 0)
    def _():
        state[...] = jnp.zeros_like(state)
        ci = lax.broadcasted_iota(I32, (ch, ch), 0).astype(F32)
        mi = lax.broadcasted_iota(I32, (ch, ch), 1).astype(F32)
        lag = ci - mi
        for d in range(2):
            for h in range(RET_HEADS):
                lg = sc_ref[d * RET_HEADS + h]
                if d == 0:
                    dmat[d, h] = jnp.where(lag >= 0, jnp.exp(lg * jnp.maximum(lag, 0.0)), 0.0)
                    qwt[d, h] = jnp.exp(lg * (ci + 1.0))
                    kwt[d, h] = jnp.exp(lg * (ch - 1.0 - ci))
                else:
                    dmat[d, h] = jnp.where(lag < 0, jnp.exp(lg * jnp.maximum(-lag, 0.0)), 0.0)
                    qwt[d, h] = jnp.exp(lg * (ch - ci))
                    kwt[d, h] = jnp.exp(lg * ci)

    scale = RET_HEAD_DIM ** -0.5
    refs = ((qf_ref, kf_ref, vf_ref, cf_ref, sf_ref, of_ref), (qb_ref, kb_ref, vb_ref, cb_ref, sb_ref, ob_ref))
    work = [(d, h) for d in range(2) for h in range(RET_HEADS)]
    qs, ks, qws, kws, vs = {}, {}, {}, {}, {}
    for d, h in work:
        q_ref, k_ref, v_ref, c_ref, s_ref, _ = refs[d]
        sl = slice(h * dh, (h + 1) * dh)
        cosf = c_ref[...]
        sinf = s_ref[...]
        q = q_ref[:, sl].astype(F32)
        k = k_ref[:, sl].astype(F32)
        q = q * cosf + pltpu.roll(q, dh // 2, axis=1) * sinf
        k = (k * cosf + pltpu.roll(k, dh // 2, axis=1) * sinf) * scale
        qs[d, h], ks[d, h] = q.astype(BF16), k.astype(BF16)
        qws[d, h], kws[d, h] = (q * qwt[d, h]).astype(BF16), (k * kwt[d, h]).astype(BF16)
        vs[d, h] = v_ref[:, sl]
    scores = {dh_: _dot_nt(qs[dh_], ks[dh_]) for dh_ in work}
    for d, h in work:
        lhs = jnp.concatenate([(scores[d, h] * dmat[d, h]).astype(BF16), qws[d, h]], axis=1)
        rhs = jnp.concatenate([vs[d, h], state[d, h].astype(BF16)], axis=0)
        refs[d][5][:, h * dh:(h + 1) * dh] = _dot(lhs, rhs)
    for d, h in work:
        cdec = sc_ref[2 * RET_HEADS + d * RET_HEADS + h]
        state[d, h] = state[d, h] * cdec + _dot_tn(kws[d, h], vs[d, h])


def _retention(zret3, decay_logit):
    bsz, seq, _ = zret3.shape
    ch = RET_CHUNK
    nc = seq // ch
    half = RET_HEAD_DIM // 2
    inv_freq = ROPE_BASE ** (-jnp.arange(half, dtype=F32) / half)
    ang = jnp.arange(seq, dtype=F32)[:, None] * inv_freq[None, :]
    cosf = jnp.concatenate([jnp.cos(ang), jnp.cos(ang)], axis=1)
    sinf = jnp.concatenate([-jnp.sin(ang), jnp.sin(ang)], axis=1)
    log_g = jax.nn.log_sigmoid(decay_logit.astype(F32)).reshape(-1)
    scal = jnp.concatenate([log_g, jnp.exp(log_g * ch)])
    w = RET_WIDTH
    fwd = lambda col: pl.BlockSpec((None, ch, w), lambda b, n, sc: (b, n, col))
    bwd = lambda col: pl.BlockSpec((None, ch, w), lambda b, n, sc: (b, nc - 1 - n, col))
    rope_f = pl.BlockSpec((ch, RET_HEAD_DIM), lambda b, n, sc: (n, 0))
    rope_b = pl.BlockSpec((ch, RET_HEAD_DIM), lambda b, n, sc: (nc - 1 - n, 0))
    grid_spec = pltpu.PrefetchScalarGridSpec(
        num_scalar_prefetch=1,
        grid=(bsz, nc),
        in_specs=[fwd(0), fwd(1), fwd(2), rope_f, rope_f, bwd(0), bwd(1), bwd(2), rope_b, rope_b],
        out_specs=[pl.BlockSpec((None, ch, w), lambda b, n, sc: (b, n, 0)),
                   pl.BlockSpec((None, ch, w), lambda b, n, sc: (b, nc - 1 - n, 0))],
        scratch_shapes=[pltpu.VMEM((2, RET_HEADS, RET_HEAD_DIM, RET_HEAD_DIM), F32),
                        pltpu.VMEM((2, RET_HEADS, ch, ch), F32),
                        pltpu.VMEM((2, RET_HEADS, ch, RET_HEAD_DIM), F32),
                        pltpu.VMEM((2, RET_HEADS, ch, RET_HEAD_DIM), F32)],
    )
    return pl.pallas_call(
        _retention_body,
        grid_spec=grid_spec,
        out_shape=[jax.ShapeDtypeStruct((bsz, seq, w), F32), jax.ShapeDtypeStruct((bsz, seq, w), F32)],
        compiler_params=_cparams(("arbitrary", "arbitrary")),
        name="retention",
    )(scal, zret3, zret3, zret3, cosf, sinf, zret3, zret3, zret3, cosf, sinf)


def _outproj_body(x_ref, yhy_ref, of_ref, ob_ref, gr_ref, ghy_ref, gret_ref, whyo_ref, wreto_ref, wo_ref,
                  n2g_ref, wrt_ref, x1_ref, h2_ref, pt_ref):
    dh = RET_HEAD_DIM
    o = of_ref[...] + ob_ref[...]
    parts = []
    for h in range(RET_HEADS):
        oh = o[:, h * dh:(h + 1) * dh]
        parts.append(oh * lax.rsqrt(jnp.mean(oh * oh, axis=-1, keepdims=True) + EPS))
    on = jnp.concatenate(parts, axis=1)
    gr = gr_ref[...].astype(F32)
    ret = (gr * jax.nn.sigmoid(gr)) * on
    y_ret = _dot(ret.astype(BF16), wreto_ref[...])
    y_hy = _dot(yhy_ref[...], whyo_ref[...])
    merged = jax.nn.sigmoid(ghy_ref[...].astype(F32)) * y_hy + jax.nn.sigmoid(gret_ref[...].astype(F32)) * y_ret
    x1 = x_ref[...] + _dot(merged.astype(BF16), wo_ref[...])
    x1_ref[...] = x1
    h2 = x1 * lax.rsqrt(jnp.mean(x1 * x1, axis=-1, keepdims=True) + EPS) * n2g_ref[...]
    h2_ref[...] = h2
    logits = _dot_nt(wrt_ref[...], h2.astype(BF16))
    m = jnp.max(logits, axis=0, keepdims=True)
    e = jnp.exp(logits - m)
    pt_ref[...] = e / jnp.sum(e, axis=0, keepdims=True)


def _outproj(x2, yhy, o_f, o_b, zret, zg, whyo, wreto, wo, n2g, wrt, tm=256):
    n_tok = x2.shape[0]
    d = D_MODEL
    row = lambda w, col=0: pl.BlockSpec((tm, w), lambda i: (i, col))
    const = lambda shape: pl.BlockSpec(shape, lambda i: (0, 0))
    return pl.pallas_call(
        _outproj_body,
        grid=(n_tok // tm,),
        in_specs=[row(d), row(HY_WIDTH), row(RET_WIDTH), row(RET_WIDTH), row(RET_WIDTH, 3), row(d, 0), row(d, 1),
                  const((HY_WIDTH, d)), const((RET_WIDTH, d)), const((d, d)), const((1, d)), const((N_EXPERTS, d))],
        out_specs=[row(d), row(d), pl.BlockSpec((N_EXPERTS, tm), lambda i: (0, i))],
        out_shape=[jax.ShapeDtypeStruct((n_tok, d), F32), jax.ShapeDtypeStruct((n_tok, d), F32),
                   jax.ShapeDtypeStruct((N_EXPERTS, n_tok), F32)],
        compiler_params=_cparams(("parallel",)),
        name="outproj_router",
    )(x2, yhy, o_f, o_b, zret, zg, zg, whyo, wreto, wo, n2g, wrt)


def _select_body(p_ref, upper_ref, lower_ref, lowinc_ref, eye_ref, pos_ref, idx_ref, gate_ref, lo_ref, *, cap):
    rows = p_ref.shape[0]
    p = p_ref[...]
    bits = lax.bitcast_convert_type(p, I32)

    def count(mask):
        return jnp.sum(jnp.sum(mask.astype(F32), axis=1, keepdims=True), axis=0, keepdims=True)

    def bit_step(i, thr):
        cand = thr | jnp.left_shift(jnp.int32(1), 30 - i)
        return jnp.where(count(bits >= cand) >= cap, cand, thr)

    thr = lax.fori_loop(0, 31, bit_step, jnp.zeros((1, 1), I32))
    gt = bits > thr
    eq = bits == thr
    need = cap - count(gt)

    def prefix(mask_f):
        incl = _dot(mask_f.astype(BF16), upper_ref[...])
        tot = jnp.broadcast_to(incl[:, LANES - 1:LANES], incl.shape)
        base = _dot(lower_ref[...], tot.astype(BF16))
        return incl, base, tot

    eq_f = eq.astype(F32)
    incl_e, base_e, _ = prefix(eq_f)
    sel = gt | (eq & (base_e + incl_e - eq_f < need))
    sel_f = sel.astype(F32)
    incl, base, tot = prefix(sel_f)
    pos_ref[...] = jnp.where(sel, (base + incl - 1.0).astype(I32), -1)
    lo_ref[...] = base.astype(I32)

    rowend = base + tot
    incl_t = _dot_nt(lowinc_ref[...], sel_f.astype(BF16)).astype(BF16)
    p_t = []
    rem = p
    for _ in range(3):
        part = rem.astype(BF16)
        rem = rem - part.astype(F32)
        p_t.append(_dot_nt(eye_ref[...], part).astype(BF16))
    table = jnp.concatenate([incl_t] + p_t, axis=0)
    r_iota = lax.broadcasted_iota(I32, (rows, LANES), 0).astype(F32)
    lane_iota = lax.broadcasted_iota(I32, (LANES, LANES), 0).astype(F32)

    def slot_tile(ts, carry):
        s = (ts * LANES + lax.broadcasted_iota(I32, (1, LANES), 1)).astype(F32)
        done = rowend <= s
        row = jnp.sum(done.astype(F32), axis=0, keepdims=True)
        before = jnp.sum(jnp.where(done, tot, 0.0), axis=0, keepdims=True)
        onehot_t = (r_iota == row).astype(BF16)
        got = _dot(table, onehot_t)
        g_t = got[:LANES]
        lane = jnp.sum((g_t <= s - before).astype(F32), axis=0, keepdims=True)
        idx_ref[pl.ds(ts, 1), :] = (row * LANES + lane).astype(I32)
        p_row = got[LANES:2 * LANES] + got[2 * LANES:3 * LANES] + got[3 * LANES:]
        gate_ref[pl.ds(ts, 1), :] = jnp.sum(jnp.where(lane_iota == lane, p_row, 0.0), axis=0, keepdims=True)
        return carry

    lax.fori_loop(0, cap // LANES, slot_tile, 0)


def _select(pt3, cap):
    n_e, rows, _ = pt3.shape
    ii = np.arange(LANES)
    upper = jnp.asarray(ii[:, None] <= ii[None, :], BF16)
    lowinc = jnp.asarray(ii[None, :] <= ii[:, None], BF16)
    eye = jnp.asarray(ii[None, :] == ii[:, None], BF16)
    rr = np.arange(rows)
    lower = jnp.asarray(rr[None, :] < rr[:, None], BF16)
    const = lambda shape: pl.BlockSpec(shape, lambda e: (0, 0))
    tok_spec = pl.BlockSpec((None, rows, LANES), lambda e: (e, 0, 0))
    slot_spec = pl.BlockSpec((None, cap // LANES, LANES), lambda e: (e, 0, 0))
    return pl.pallas_call(
        functools.partial(_select_body, cap=cap),
        grid=(n_e,),
        in_specs=[tok_spec, const((LANES, LANES)), const((rows, rows)), const((LANES, LANES)), const((LANES, LANES))],
        out_specs=[tok_spec, slot_spec, slot_spec, tok_spec],
        out_shape=[jax.ShapeDtypeStruct((n_e, rows, LANES), I32),
                   jax.ShapeDtypeStruct((n_e, cap // LANES, LANES), I32),
                   jax.ShapeDtypeStruct((n_e, cap // LANES, LANES), F32),
                   jax.ShapeDtypeStruct((n_e, rows, LANES), I32)],
        compiler_params=_cparams(("parallel",)),
        name="expert_select",
    )(pt3, upper, lower, lowinc, eye)


def _ffn_body(idx_ref, h_hbm, gate_ref, wg_ref, wu_ref, wd_ref, ye_ref, xbuf, xb, sems, *, tiles_per_expert):
    s = xbuf.shape[1]
    step = pl.program_id(0) * tiles_per_expert + pl.program_id(1)
    n_steps = pl.num_programs(0) * tiles_per_expert
    cur = step % 2

    def row_copy(st, i, buf):
        tok = idx_ref[st * s + i]
        return pltpu.make_async_copy(h_hbm.at[pl.ds(tok, 1), :], xbuf.at[buf, pl.ds(i, 1), :], sems.at[buf])

    def wait_rows(buf):
        pltpu.make_async_copy(h_hbm.at[pl.ds(0, s), :], xbuf.at[buf], sems.at[buf]).wait()

    @pl.when(step == 0)
    def _():
        def one(i, carry):
            row_copy(0, i, 0).start()
            return carry
        lax.fori_loop(0, s, one, 0, unroll=8)

    @pl.when(step + 1 < n_steps)
    def _():
        for i in range(s):
            row_copy(step + 1, i, 1 - cur).start()

    wait_rows(cur)
    xb[...] = xbuf[cur].astype(BF16)
    y = None
    for f0 in range(0, EXPERT_FF, FF_CHUNK):
        f1 = min(f0 + FF_CHUNK, EXPERT_FF)
        a = _dot(xb[...], wg_ref[:, f0:f1])
        b = _dot(xb[...], wu_ref[:, f0:f1])
        hid = ((a * jax.nn.sigmoid(a)) * b).astype(BF16)
        part = _dot(hid, wd_ref[f0:f1, :])
        y = part if y is None else y + part
    for k in range(s // LANES):
        col = jnp.transpose(jnp.broadcast_to(gate_ref[k:k + 1, :], (LANES, LANES)))[:, 0:1]
        ye_ref[k * LANES:(k + 1) * LANES, :] = (y[k * LANES:(k + 1) * LANES] * col).astype(ye_ref.dtype)


def _expert_ffn(idx_flat, h2, gates, wg, wu, wd, cap, s=512):
    s = min(s, cap)
    tiles = cap // s
    g3 = gates.reshape(N_EXPERTS * tiles, s // LANES, LANES)
    grid_spec = pltpu.PrefetchScalarGridSpec(
        num_scalar_prefetch=1,
        grid=(N_EXPERTS, tiles),
        in_specs=[pl.BlockSpec(memory_space=pl.ANY),
                  pl.BlockSpec((None, s // LANES, LANES), lambda e, j, idx: (e * tiles + j, 0, 0)),
                  pl.BlockSpec((None, D_MODEL, EXPERT_FF), lambda e, j, idx: (e, 0, 0)),
                  pl.BlockSpec((None, D_MODEL, EXPERT_FF), lambda e, j, idx: (e, 0, 0)),
                  pl.BlockSpec((None, EXPERT_FF, D_MODEL), lambda e, j, idx: (e, 0, 0))],
        out_specs=pl.BlockSpec((s, D_MODEL), lambda e, j, idx: (e * tiles + j, 0)),
        scratch_shapes=[pltpu.VMEM((2, s, D_MODEL), F32), pltpu.VMEM((s, D_MODEL), BF16),
                        pltpu.SemaphoreType.DMA((2,))],
    )
    return pl.pallas_call(
        functools.partial(_ffn_body, tiles_per_expert=tiles),
        grid_spec=grid_spec,
        out_shape=jax.ShapeDtypeStruct((N_EXPERTS * cap, D_MODEL), BF16),
        compiler_params=_cparams(("arbitrary", "arbitrary")),
        name="expert_ffn",
    )(idx_flat, h2, g3, wg, wu, wd)


def _combine_body(lo_ref, np_ref, x1_ref, pos_ref, nfg_ref, expand_ref, ye_hbm, o_ref, win, sems, *, cap, rows_total):
    r = pl.program_id(0)
    n_rows = pl.num_programs(0)
    w = COMBINE_WIN
    cur = r % 2

    def starts(rr, m):
        out = []
        for e in range(N_EXPERTS):
            first = e * cap + lo_ref[e * n_rows + rr]
            intended = (first // COMBINE_ALIGN) * COMBINE_ALIGN + m * w
            actual = pl.multiple_of(jnp.minimum(intended, rows_total - w), COMBINE_ALIGN)
            out.append((intended, actual))
        return out

    def copies(rr, m, buf):
        return [pltpu.make_async_copy(ye_hbm.at[pl.ds(actual, w), :], win.at[buf, pl.ds(e * w, w), :], sems.at[buf, e])
                for e, (_, actual) in enumerate(starts(rr, m))]

    def contribution(rr, m, buf):
        lane = lax.broadcasted_iota(I32, (1, N_EXPERTS), 1)
        intended = jnp.zeros((1, N_EXPERTS), I32)
        actual = jnp.zeros((1, N_EXPERTS), I32)
        for e, (i_s, a_s) in enumerate(starts(rr, m)):
            intended = jnp.where(lane == e, i_s, intended)
            actual = jnp.where(lane == e, a_s, actual)
        pos = pos_ref[...]
        glob = pos + lane * cap
        rel = glob - intended
        valid = (pos >= 0) & (rel >= 0) & (rel < w)
        local = jnp.where(valid, glob - actual, -1).astype(F32).astype(BF16)
        spread = _dot(local, expand_ref[...])
        col = (lax.broadcasted_iota(I32, (1, N_EXPERTS * w), 1) & (w - 1)).astype(F32)
        onehot = (spread == col).astype(BF16)
        return _dot(onehot, win[buf])

    @pl.when(r == 0)
    def _():
        for cp in copies(0, 0, 0):
            cp.start()

    @pl.when(r + 1 < n_rows)
    def _():
        for cp in copies(r + 1, 0, 1 - cur):
            cp.start()

    for cp in copies(r, 0, cur):
        cp.wait()
    acc = x1_ref[...] + contribution(r, 0, cur)

    def extra_pass(m, acc):
        for cp in copies(r, m, cur):
            cp.start()
        for cp in copies(r, m, cur):
            cp.wait()
        return acc + contribution(r, m, cur)

    acc = lax.fori_loop(1, np_ref[r], extra_pass, acc)
    o_ref[...] = acc * lax.rsqrt(jnp.mean(acc * acc, axis=-1, keepdims=True) + EPS) * nfg_ref[...]


def _combine(lo, x1, pos_t, nfg, ye, cap):
    n_tok = x1.shape[0]
    tm = LANES
    w = COMBINE_WIN
    rows_total = ye.shape[0]
    n_rows = n_tok // tm
    nxt = jnp.concatenate([lo[:, 1:], jnp.full((N_EXPERTS, 1), cap, I32)], axis=1)
    span = lo % COMBINE_ALIGN + (nxt - lo)
    n_pass = jnp.maximum(jnp.max((span + w - 1) // w, axis=0), 1).astype(I32)
    ee = np.arange(N_EXPERTS)
    expand = jnp.asarray(ee[:, None] == (np.arange(N_EXPERTS * w) // w)[None, :], BF16)
    grid_spec = pltpu.PrefetchScalarGridSpec(
        num_scalar_prefetch=2,
        grid=(n_rows,),
        in_specs=[pl.BlockSpec((tm, D_MODEL), lambda i, lo_, np_: (i, 0)),
                  pl.BlockSpec((tm, N_EXPERTS), lambda i, lo_, np_: (i, 0)),
                  pl.BlockSpec((1, D_MODEL), lambda i, lo_, np_: (0, 0)),
                  pl.BlockSpec((N_EXPERTS, N_EXPERTS * w), lambda i, lo_, np_: (0, 0)),
                  pl.BlockSpec(memory_space=pl.ANY)],
        out_specs=pl.BlockSpec((tm, D_MODEL), lambda i, lo_, np_: (i, 0)),
        scratch_shapes=[pltpu.VMEM((2, N_EXPERTS * w, D_MODEL), BF16),
                        pltpu.SemaphoreType.DMA((2, N_EXPERTS))],
    )
    return pl.pallas_call(
        functools.partial(_combine_body, cap=cap, rows_total=rows_total),
        grid_spec=grid_spec,
        out_shape=jax.ShapeDtypeStruct((n_tok, D_MODEL), F32),
        compiler_params=_cparams(("arbitrary",)),
        name="moe_combine_norm",
    )(lo.reshape(-1), n_pass, x1, pos_t, nfg, expand, ye)


def _trunk(x, w, fft_passes=1, filter_passes=1):
    bsz, seq, d = x.shape
    n_tok = bsz * seq
    x2 = x.reshape(n_tok, d)
    tabs = _fft_tables(seq)

    zhy, zret, zg = _inproj(x2, w["norm1_g"], w["w_in"])
    zc = _shortconv(zhy.reshape(bsz, seq, -1), w["hy_conv_w"], w["hy_conv_b"])

    h_time, h_abs = _hyena_filter_time(seq, w["hy_w1"], w["hy_b1"], w["hy_freq1"], w["hy_w2"], w["hy_b2"],
                                       w["hy_freq2"], w["hy_w3"])
    kr, ki = _stage_b_filter(_stage_a_real(h_time, tabs, filter_passes), h_abs, tabs, filter_passes)
    yhy = _hyena(zc, kr, ki, w["hy_skip"], tabs, fft_passes)

    o_f, o_b = _retention(zret.reshape(bsz, seq, -1), w["ret_decay_logit"])
    x1, h2, pt = _outproj(x2, yhy, o_f.reshape(n_tok, -1), o_b.reshape(n_tok, -1), zret, zg,
                          w["w_hy_out"], w["w_ret_out"], w["w_o"], w["norm2_g"], w["w_router_t"])

    cap = CAPACITY_FACTOR * n_tok // N_EXPERTS
    rows = n_tok // LANES
    pos, idx, gates, lo = _select(pt.reshape(N_EXPERTS, rows, LANES), cap)
    ye = _expert_ffn(idx.reshape(-1), h2, gates, w["w_gate"], w["w_up"], w["w_down"], cap)
    pos_t = pos.reshape(N_EXPERTS, n_tok).T
    y = _combine(lo[:, :, 0], x1, pos_t, w["norm_f_g"], ye, cap)
    return y.reshape(bsz, seq, d)


def kernel(x_prompt, x_sample, norm1_g, w_in, hy_conv_w, hy_conv_b, hy_w1, hy_b1, hy_freq1, hy_w2, hy_b2, hy_freq2,
           hy_w3, hy_skip, ret_decay_logit, w_hy_out, w_ret_out, w_o, norm2_g, w_router, w_gate, w_up, w_down,
           norm_f_g):
    layer = 0
    w = dict(
        norm1_g=norm1_g[layer].astype(F32)[None], w_in=w_in[layer].astype(BF16),
        hy_conv_w=hy_conv_w[layer].astype(F32), hy_conv_b=hy_conv_b[layer].astype(F32)[None],
        hy_w1=hy_w1[layer], hy_b1=hy_b1[layer], hy_freq1=hy_freq1[layer], hy_w2=hy_w2[layer], hy_b2=hy_b2[layer],
        hy_freq2=hy_freq2[layer], hy_w3=hy_w3[layer], hy_skip=hy_skip[layer],
        ret_decay_logit=ret_decay_logit[layer],
        w_hy_out=w_hy_out[layer].astype(BF16), w_ret_out=w_ret_out[layer].astype(BF16), w_o=w_o[layer].astype(BF16),
        norm2_g=norm2_g[layer].astype(F32)[None], w_router_t=w_router[layer].T.astype(BF16),
        w_gate=w_gate[layer].astype(BF16), w_up=w_up[layer].astype(BF16), w_down=w_down[layer].astype(BF16),
        norm_f_g=norm_f_g.astype(F32)[None],
    )
    return _trunk(x_prompt, w), _trunk(x_sample, w)
```

```python
import functools
import math

import numpy as np
import jax
import jax.numpy as jnp
from jax import lax
from jax.experimental import pallas as pl
from jax.experimental.pallas import tpu as pltpu

F32 = jnp.float32
BF16 = jnp.bfloat16
I32 = jnp.int32

D_MODEL = 1024
HY_WIDTH = 512
HY_ORDER = 2
HY_BANDS = 16
HY_FILTER_WIDTH = 64
HY_FAST_DECAY = 0.3
HY_SLOW_DECAY = 1.5
HY_DECAY_TARGET = 1e-2
RET_WIDTH = 512
RET_HEADS = 4
RET_HEAD_DIM = 128
RET_CHUNK = 128
ROPE_BASE = 10000.0
N_EXPERTS = 16
EXPERT_FF = 1408
CAPACITY_FACTOR = 2
EPS = 1e-6

LANES = 128
BF16_SUBLANES = 16
EMB_PAD = 128
DFT_N1_MAX = 128
FF_CHUNK = 256
COMBINE_ALIGN = 16
COMBINE_WIN = 64


def _cparams(sem, vmem_mb=48):
    return pltpu.CompilerParams(dimension_semantics=sem, vmem_limit_bytes=vmem_mb * 1024 * 1024)


def _dot(a, b):
    return jnp.dot(a, b, preferred_element_type=F32)


def _dot_nt(a, b):
    return lax.dot_general(a, b, (((1,), (1,)), ((), ())), preferred_element_type=F32)


def _dot_tn(a, b):
    return lax.dot_general(a, b, (((0,), (0,)), ((), ())), preferred_element_type=F32)


def _split(a):
    hi = a.astype(BF16)
    lo = (a - hi.astype(F32)).astype(BF16)
    return hi, lo


def _dot_const(m_hi, m_lo, x, passes):
    xh = x.astype(BF16)
    r = _dot(m_hi, xh)
    if passes >= 3:
        xl = (x - xh.astype(F32)).astype(BF16)
        r = r + _dot(m_lo, xh) + _dot(m_hi, xl)
    return r


def _dot3(a, b):
    ah, al = _split(a)
    bh, bl = _split(b)
    return _dot(ah, bh) + _dot(al, bh) + _dot(ah, bl)


def _inproj_body(x_ref, g_ref, w_ref, zhy_ref, zret_ref, zg_ref):
    x = x_ref[...]
    ms = jnp.mean(x * x, axis=-1, keepdims=True)
    h = (x * lax.rsqrt(ms + EPS) * g_ref[...]).astype(BF16)
    n_hy = zhy_ref.shape[1]
    n_ret = zret_ref.shape[1]
    zhy_ref[...] = _dot(h, w_ref[:, :n_hy]).astype(BF16)
    zret_ref[...] = _dot(h, w_ref[:, n_hy:n_hy + n_ret]).astype(BF16)
    zg_ref[...] = _dot(h, w_ref[:, n_hy + n_ret:]).astype(BF16)


def _inproj(x2, g, w_bf, tm=256):
    n_tok = x2.shape[0]
    n_hy, n_ret, n_g = 3 * HY_WIDTH, 4 * RET_WIDTH, 2 * D_MODEL
    return pl.pallas_call(
        _inproj_body,
        grid=(n_tok // tm,),
        in_specs=[pl.BlockSpec((tm, D_MODEL), lambda i: (i, 0)),
                  pl.BlockSpec((1, D_MODEL), lambda i: (0, 0)),
                  pl.BlockSpec((D_MODEL, n_hy + n_ret + n_g), lambda i: (0, 0))],
        out_specs=[pl.BlockSpec((tm, n_hy), lambda i: (i, 0)),
                   pl.BlockSpec((tm, n_ret), lambda i: (i, 0)),
                   pl.BlockSpec((tm, n_g), lambda i: (i, 0))],
        out_shape=[jax.ShapeDtypeStruct((n_tok, n_hy), BF16),
                   jax.ShapeDtypeStruct((n_tok, n_ret), BF16),
                   jax.ShapeDtypeStruct((n_tok, n_g), BF16)],
        compiler_params=_cparams(("parallel",)),
        name="inproj",
    )(x2, g, w_bf)


def _shortconv_body(z_ref, zp_ref, zn_ref, w_ref, b_ref, o_ref):
    tl = z_ref.shape[0]
    i = pl.program_id(1)
    z = z_ref[...].astype(F32)
    halo = zp_ref.shape[0]
    prev_row = jnp.where(i > 0, zp_ref[halo - 1:halo, :].astype(F32), 0.0)
    next_row = jnp.where(i < pl.num_programs(1) - 1, zn_ref[0:1, :].astype(F32), 0.0)
    row = lax.broadcasted_iota(I32, (tl, 1), 0)
    zm1 = jnp.where(row == 0, prev_row, pltpu.roll(z, 1, axis=0))
    zp1 = jnp.where(row == tl - 1, next_row, pltpu.roll(z, tl - 1, axis=0))
    o_ref[...] = zm1 * w_ref[0:1, :] + z * w_ref[1:2, :] + zp1 * w_ref[2:3, :] + b_ref[...]


def _shortconv(z3, w, b, tl=256):
    bsz, seq, width = z3.shape
    tl = min(tl, seq)
    halo = BF16_SUBLANES
    nb8 = seq // halo
    r8 = tl // halo
    return pl.pallas_call(
        _shortconv_body,
        grid=(bsz, seq // tl),
        in_specs=[pl.BlockSpec((None, tl, width), lambda b_, i: (b_, i, 0)),
                  pl.BlockSpec((None, halo, width), lambda b_, i: (b_, jnp.maximum(i * r8 - 1, 0), 0)),
                  pl.BlockSpec((None, halo, width), lambda b_, i: (b_, jnp.minimum((i + 1) * r8, nb8 - 1), 0)),
                  pl.BlockSpec((3, width), lambda b_, i: (0, 0)),
                  pl.BlockSpec((1, width), lambda b_, i: (0, 0))],
        out_specs=pl.BlockSpec((None, tl, width), lambda b_, i: (b_, i, 0)),
        out_shape=jax.ShapeDtypeStruct((bsz, seq, width), F32),
        compiler_params=_cparams(("parallel", "parallel")),
        name="shortconv",
    )(z3, z3, z3, w, b)


def _fft_dims(seq):
    n = 2 * seq
    n1 = min(DFT_N1_MAX, 1 << (n.bit_length() // 2))
    n2 = n // n1
    assert n1 * n2 == n and n1 % 2 == 0
    return n, n1, n2


def _hi_lo_const(m):
    m = np.asarray(m, np.float64)
    hi = jnp.asarray(m, F32).astype(BF16)
    lo = (jnp.asarray(m, F32) - hi.astype(F32)).astype(BF16)
    return hi, lo


def _fft_tables(seq):
    n, n1, n2 = _fft_dims(seq)
    r1 = n1 // 2
    k1 = jnp.arange(n1, dtype=I32)
    j = jnp.arange(n2, dtype=I32)

    def cos_sin(n1_count, k1_first):
        t = n2 * jnp.arange(n1_count, dtype=I32)[None, :] + j[:, None]
        m = (k1[None, :, None] * t[:, None, :]) if k1_first else (t[:, :, None] * k1[None, None, :])
        ang = (m % n).astype(F32) * (2.0 * math.pi / n)
        return jnp.cos(ang), jnp.sin(ang)

    cat = jnp.concatenate
    c, s = cos_sin(r1, True)
    mat_a = cat([cat([c, s], axis=2), cat([-s, c], axis=2)], axis=1).astype(BF16)
    c, s = cos_sin(n1, True)
    mat_a_real = cat([c, -s], axis=1).astype(BF16)
    c, s = cos_sin(r1, False)
    mat_c = cat([cat([c, -s], axis=2), cat([s, c], axis=2)], axis=1).astype(BF16)
    a2 = 2.0 * np.pi * np.outer(np.arange(n2), np.arange(n2)) / n2
    gr, gi = np.cos(a2), -np.sin(a2)
    mat_f = np.block([[gr, -gi], [gi, gr]])
    mat_i = np.block([[gr, gi], [-gi, gr]])
    return dict(n=n, n1=n1, n2=n2, mat_a=mat_a, mat_a_real=mat_a_real, mat_c=mat_c,
                mat_f=_hi_lo_const(mat_f), mat_i=_hi_lo_const(mat_i))


def _filter_body(z_ref, w1_ref, b1_ref, f1_ref, w2_ref, b2_ref, f2_ref, w3_ref, dl_ref, h_ref, s_ref, *, seq):
    i = pl.program_id(0)
    tl = z_ref.shape[0]
    z = z_ref[...]
    a = jnp.sin(f1_ref[...] * (_dot3(z, w1_ref[...]) + b1_ref[...]))
    a = jnp.sin(f2_ref[...] * (_dot3(a, w2_ref[...]) + b2_ref[...]))
    h = _dot3(a, w3_ref[...])
    win = jnp.exp(-z[:, 0:1] * dl_ref[...])
    h = h * jnp.concatenate([win] * HY_ORDER, axis=1)
    row = i * tl + lax.broadcasted_iota(I32, (tl, 1), 0)
    h = jnp.where(row == seq, 0.0, h)
    h_ref[...] = h

    @pl.when(i == 0)
    def _():
        s_ref[...] = jnp.zeros_like(s_ref)

    s_ref[...] += jnp.broadcast_to(jnp.sum(jnp.abs(h), axis=0, keepdims=True), s_ref.shape)


def _hyena_filter_time(seq, w1, b1, f1, w2, b2, f2, w3, tl=512):
    n = 2 * seq
    tl = min(tl, seq)
    idx = jnp.arange(n, dtype=I32)
    p = jnp.minimum(jnp.where(idx < seq, idx, n - idx), seq - 1).astype(F32)
    t = p / float(seq - 1)
    ang = 2.0 * math.pi * p / seq
    bands = jnp.linspace(1e-4, HY_BANDS - 1, HY_BANDS, dtype=F32)
    phase = ang[:, None] * bands[None, :]
    emb = jnp.concatenate([t[:, None], jnp.cos(phase), -jnp.sin(phase)], axis=-1)
    emb = jnp.pad(emb, ((0, 0), (0, EMB_PAD - emb.shape[1])))
    w1p = jnp.pad(w1.astype(F32), ((0, EMB_PAD - w1.shape[0]), (0, 0)))
    w3d = w3.astype(F32).reshape(HY_FILTER_WIDTH, HY_ORDER, 2, HY_WIDTH).transpose(2, 0, 1, 3)
    w3d = w3d.reshape(2, HY_FILTER_WIDTH, HY_ORDER * HY_WIDTH)
    deltas = jnp.abs(jnp.linspace(math.log(HY_DECAY_TARGET) / HY_SLOW_DECAY,
                                  math.log(HY_DECAY_TARGET) / HY_FAST_DECAY, HY_WIDTH, dtype=F32))[None, :]
    fw = HY_FILTER_WIDTH
    nblk_half = seq // tl
    const = lambda i: (0, 0)
    return pl.pallas_call(
        functools.partial(_filter_body, seq=seq),
        grid=(n // tl,),
        in_specs=[pl.BlockSpec((tl, EMB_PAD), lambda i: (i, 0)),
                  pl.BlockSpec((EMB_PAD, fw), const), pl.BlockSpec((1, fw), const), pl.BlockSpec((1, fw), const),
                  pl.BlockSpec((fw, fw), const), pl.BlockSpec((1, fw), const), pl.BlockSpec((1, fw), const),
                  pl.BlockSpec((None, fw, HY_ORDER * HY_WIDTH), lambda i: (i // nblk_half, 0, 0)),
                  pl.BlockSpec((1, HY_WIDTH), const)],
        out_specs=[pl.BlockSpec((tl, HY_ORDER * HY_WIDTH), lambda i: (i, 0)),
                   pl.BlockSpec((8, HY_ORDER * HY_WIDTH), const)],
        out_shape=[jax.ShapeDtypeStruct((n, HY_ORDER * HY_WIDTH), F32),
                   jax.ShapeDtypeStruct((8, HY_ORDER * HY_WIDTH), F32)],
        compiler_params=_cparams(("arbitrary",)),
        name="hyena_filter",
    )(emb, w1p, b1.astype(F32)[None], f1.astype(F32)[None], w2.astype(F32), b2.astype(F32)[None],
      f2.astype(F32)[None], w3d, deltas)


def _stage_a_real_body(h_ref, m_ref, a_ref):
    n1 = h_ref.shape[0]
    h_t = jnp.swapaxes(h_ref[...], 0, 1)
    rs = [_dot(m_ref[j], h_t[j].astype(BF16)) for j in range(h_ref.shape[1])]
    r = jnp.swapaxes(jnp.stack(rs, axis=0), 0, 1)
    a_ref[0] = r[:n1]
    a_ref[1] = r[n1:]


def _stage_a_real(h, tabs, n2c=8, cw=512):
    n1, n2 = tabs["n1"], tabs["n2"]
    c = h.shape[1]
    h3 = h.reshape(n1, n2, c)
    mat = tabs["mat_a_real"]
    return pl.pallas_call(
        _stage_a_real_body,
        grid=(n2 // n2c, c // cw),
        in_specs=[pl.BlockSpec((n1, n2c, cw), lambda j, k: (0, j, k)),
                  pl.BlockSpec((n2c,) + mat.shape[1:], lambda j, k: (j, 0, 0))],
        out_specs=pl.BlockSpec((2, n1, n2c, cw), lambda j, k: (0, 0, j, k)),
        out_shape=jax.ShapeDtypeStruct((2, n1, n2, c), F32),
        compiler_params=_cparams(("parallel", "parallel")),
        name="dft_a_filter",
    )(h3, mat)


def _pack_pair(re, im):
    hi = lax.bitcast_convert_type(re.astype(BF16).astype(F32), I32)
    lo = lax.bitcast_convert_type(im.astype(BF16).astype(F32), I32)
    return lax.bitcast_convert_type(hi | lax.shift_right_logical(lo, 16), F32)


def _unpack_pair(packed):
    word = lax.bitcast_convert_type(packed, I32)
    re = lax.bitcast_convert_type(word & jnp.int32(-65536), F32)
    im = lax.bitcast_convert_type(lax.shift_left(word, 16), F32)
    return re, im


def _stage_a_body(u_ref, m_ref, a_ref):
    _, r1, n2c, cw = u_ref.shape
    n1 = 2 * r1
    u_t = jnp.swapaxes(u_ref[...].reshape(n1, n2c, cw), 0, 1)
    packed = []
    for j in range(n2c):
        r = _dot(m_ref[j], u_t[j].astype(BF16))
        packed.append(_pack_pair(r[:n1], r[n1:]))
    a_ref[...] = jnp.swapaxes(jnp.stack(packed, axis=0), 0, 1)


def _stage_a(u5, col, tabs, n2c=8):
    n1, n2 = tabs["n1"], tabs["n2"]
    npair, _, r1, _, _ = u5.shape
    cw = HY_WIDTH
    mat = tabs["mat_a"]
    return pl.pallas_call(
        _stage_a_body,
        grid=(npair, n2 // n2c),
        in_specs=[pl.BlockSpec((None, 2, r1, n2c, cw), lambda p, j: (p, 0, 0, j, col)),
                  pl.BlockSpec((n2c,) + mat.shape[1:], lambda p, j: (j, 0, 0))],
        out_specs=pl.BlockSpec((None, n1, n2c, cw), lambda p, j: (p, 0, j, 0)),
        out_shape=jax.ShapeDtypeStruct((npair, n1, n2, cw), F32),
        compiler_params=_cparams(("parallel", "parallel")),
        name="dft_a",
    )(u5, mat)


def _stage_b_filter_body(a_ref, g_ref, s_ref, kr_ref, ki_ref, *, n):
    n2 = a_ref.shape[2]
    scale = 1.0 / (s_ref[0:1, :] * float(n))
    for kk in range(a_ref.shape[1]):
        d = _dot(g_ref[...], jnp.concatenate([a_ref[0, kk], a_ref[1, kk]], axis=0).astype(BF16))
        kr_ref[kk] = d[:n2] * scale
        ki_ref[kk] = d[n2:] * scale


def _stage_b_filter(a, s, tabs):
    n, n1, n2 = tabs["n"], tabs["n1"], tabs["n2"]
    c = a.shape[-1]
    kb = max(1, min(n1, 512 // n2))
    g = tabs["mat_f"][0]
    return pl.pallas_call(
        functools.partial(_stage_b_filter_body, n=n),
        grid=(n1 // kb,),
        in_specs=[pl.BlockSpec((2, kb, n2, c), lambda k: (0, k, 0, 0)),
                  pl.BlockSpec(g.shape, lambda k: (0, 0)),
                  pl.BlockSpec(s.shape, lambda k: (0, 0))],
        out_specs=[pl.BlockSpec((kb, n2, c), lambda k: (k, 0, 0)),
                   pl.BlockSpec((kb, n2, c), lambda k: (k, 0, 0))],
        out_shape=[jax.ShapeDtypeStruct((n1, n2, c), F32), jax.ShapeDtypeStruct((n1, n2, c), F32)],
        compiler_params=_cparams(("parallel",)),
        name="dft_b_filter",
    )(a, g, s)


def _stage_b_body(a_ref, kr_ref, ki_ref, gf_ref, gi_ref, c_ref):
    n2 = a_ref.shape[1]
    for kk in range(a_ref.shape[0]):
        d = _dot(gf_ref[...], jnp.concatenate(_unpack_pair(a_ref[kk]), axis=0).astype(BF16))
        dr, di = d[:n2], d[n2:]
        kr, ki = kr_ref[kk], ki_ref[kk]
        yr = dr * kr - di * ki
        yi = dr * ki + di * kr
        e = _dot(gi_ref[...], jnp.concatenate([yr, yi], axis=0).astype(BF16))
        c_ref[kk] = _pack_pair(e[:n2], e[n2:])


def _stage_b(a, kr, ki, order, tabs):
    n1, n2 = tabs["n1"], tabs["n2"]
    npair = a.shape[0]
    cw = HY_WIDTH
    kb = max(1, min(n1, 1024 // n2))
    gf = tabs["mat_f"][0]
    gi = tabs["mat_i"][0]
    const = lambda k, p: (0, 0)
    return pl.pallas_call(
        _stage_b_body,
        grid=(n1 // kb, npair),
        in_specs=[pl.BlockSpec((None, kb, n2, cw), lambda k, p: (p, k, 0, 0)),
                  pl.BlockSpec((kb, n2, cw), lambda k, p: (k, 0, order)),
                  pl.BlockSpec((kb, n2, cw), lambda k, p: (k, 0, order)),
                  pl.BlockSpec(gf.shape, const), pl.BlockSpec(gi.shape, const)],
        out_specs=pl.BlockSpec((None, kb, n2, cw), lambda k, p: (p, k, 0, 0)),
        out_shape=jax.ShapeDtypeStruct(a.shape, F32),
        compiler_params=_cparams(("parallel", "parallel")),
        name="dft_b",
    )(a, kr, ki, gf, gi)


def _stage_c_body(c_ref, u_ref, g_ref, skip_ref, mc_ref, ma_ref, y_ref, *rest, fuse_a):
    _, r1, n2c, cw = u_ref.shape
    n1 = 2 * r1
    ys = []
    c_t = jnp.swapaxes(c_ref[...], 0, 1)
    for j in range(n2c):
        cc = jnp.concatenate(_unpack_pair(c_t[j]), axis=0)
        ys.append(_dot(mc_ref[j], cc.astype(BF16)))
    y = jnp.swapaxes(jnp.stack(ys, axis=0), 0, 1)
    u = u_ref[...].reshape(n1, n2c, cw)
    g = g_ref[...].reshape(n1, n2c, cw)
    yo = g * (y + u * skip_ref[...].reshape(1, 1, cw))
    y_ref[...] = yo.reshape(2, r1, n2c, cw).astype(y_ref.dtype)
    if fuse_a:
        a_ref = rest[0]
        yo_t = jnp.swapaxes(yo, 0, 1)
        packed = []
        for j in range(n2c):
            r = _dot(ma_ref[j], yo_t[j].astype(BF16))
            packed.append(_pack_pair(r[:n1], r[n1:]))
        a_ref[...] = jnp.swapaxes(jnp.stack(packed, axis=0), 0, 1)


def _stage_c(c, u5, u_col, g5, g_col, skip, tabs, fuse_a, out_dtype, n2c=8):
    n1, n2 = tabs["n1"], tabs["n2"]
    npair, _, r1, _, _ = u5.shape
    cw = HY_WIDTH
    mc = tabs["mat_c"]
    ma = tabs["mat_a"]
    const = lambda p, j: (0, 0)
    per_j = lambda m: pl.BlockSpec((n2c,) + m.shape[1:], lambda p, j: (j, 0, 0))
    out_shape = [jax.ShapeDtypeStruct((npair, 2, r1, n2, cw), out_dtype)]
    out_specs = [pl.BlockSpec((None, 2, r1, n2c, cw), lambda p, j: (p, 0, 0, j, 0))]
    if fuse_a:
        out_shape.append(jax.ShapeDtypeStruct((npair, n1, n2, cw), F32))
        out_specs.append(pl.BlockSpec((None, n1, n2c, cw), lambda p, j: (p, 0, j, 0)))
    return pl.pallas_call(
        functools.partial(_stage_c_body, fuse_a=fuse_a),
        grid=(npair, n2 // n2c),
        in_specs=[pl.BlockSpec((None, n1, n2c, cw), lambda p, j: (p, 0, j, 0)),
                  pl.BlockSpec((None, 2, r1, n2c, cw), lambda p, j: (p, 0, 0, j, u_col)),
                  pl.BlockSpec((None, 2, r1, n2c, cw), lambda p, j: (p, 0, 0, j, g_col)),
                  pl.BlockSpec((1, cw), const), per_j(mc), per_j(ma)],
        out_specs=out_specs,
        out_shape=out_shape,
        compiler_params=_cparams(("parallel", "parallel")),
        name="dft_c",
    )(c, u5, g5, skip, mc, ma)


def _hyena(zc3, kr, ki, skip, tabs):
    bsz, seq, _ = zc3.shape
    n1, n2 = tabs["n1"], tabs["n2"]
    r1 = n1 // 2
    z5 = zc3.reshape(bsz // 2, 2, r1, n2, 3 * HY_WIDTH)
    a = _stage_a(z5, 2, tabs)
    c = _stage_b(a, kr, ki, 0, tabs)
    y1, a2 = _stage_c(c, z5, 2, z5, 0, skip[0:1].astype(F32), tabs, True, F32)
    c2 = _stage_b(a2, kr, ki, 1, tabs)
    (y2,) = _stage_c(c2, y1, 0, z5, 1, skip[1:2].astype(F32), tabs, False, BF16)
    return y2.reshape(bsz * seq, HY_WIDTH)


def _retention_body(sc_ref, qf_ref, kf_ref, vf_ref, cf_ref, sf_ref, qb_ref, kb_ref, vb_ref, cb_ref, sb_ref,
                    of_ref, ob_ref, state, dmat, qwt, kwt):
    ch = RET_CHUNK
    dh = RET_HEAD_DIM

    @pl.when(pl.program_id(1) == 0)
    def _():
        state[...] = jnp.zeros_like(state)
        ci = lax.broadcasted_iota(I32, (ch, ch), 0).astype(F32)
        mi = lax.broadcasted_iota(I32, (ch, ch), 1).astype(F32)
        lag = ci - mi
        for d in range(2):
            for h in range(RET_HEADS):
                lg = sc_ref[d * RET_HEADS + h]
                if d == 0:
                    dmat[d, h] = jnp.where(lag >= 0, jnp.exp(lg * jnp.maximum(lag, 0.0)), 0.0)
                    qwt[d, h] = jnp.exp(lg * (ci + 1.0))
                    kwt[d, h] = jnp.exp(lg * (ch - 1.0 - ci))
                else:
                    dmat[d, h] = jnp.where(lag < 0, jnp.exp(lg * jnp.maximum(-lag, 0.0)), 0.0)
                    qwt[d, h] = jnp.exp(lg * (ch - ci))
                    kwt[d, h] = jnp.exp(lg * ci)

    scale = RET_HEAD_DIM ** -0.5
    refs = ((qf_ref, kf_ref, vf_ref, cf_ref, sf_ref, of_ref), (qb_ref, kb_ref, vb_ref, cb_ref, sb_ref, ob_ref))
    work = [(d, h) for d in range(2) for h in range(RET_HEADS)]
    qs, ks, qws, kws, vs = {}, {}, {}, {}, {}
    for d, h in work:
        q_ref, k_ref, v_ref, c_ref, s_ref, _ = refs[d]
        sl = slice(h * dh, (h + 1) * dh)
        cosf = c_ref[...]
        sinf = s_ref[...]
        q = q_ref[:, sl].astype(F32)
        k = k_ref[:, sl].astype(F32)
        q = q * cosf + pltpu.roll(q, dh // 2, axis=1) * sinf
        k = (k * cosf + pltpu.roll(k, dh // 2, axis=1) * sinf) * scale
        qs[d, h], ks[d, h] = q.astype(BF16), k.astype(BF16)
        qws[d, h], kws[d, h] = (q * qwt[d, h]).astype(BF16), (k * kwt[d, h]).astype(BF16)
        vs[d, h] = v_ref[:, sl]
    scores = {dh_: _dot_nt(qs[dh_], ks[dh_]) for dh_ in work}
    for d, h in work:
        lhs = jnp.concatenate([(scores[d, h] * dmat[d, h]).astype(BF16), qws[d, h]], axis=1)
        rhs = jnp.concatenate([vs[d, h], state[d, h].astype(BF16)], axis=0)
        refs[d][5][:, h * dh:(h + 1) * dh] = _dot(lhs, rhs).astype(BF16)
    for d, h in work:
        cdec = sc_ref[2 * RET_HEADS + d * RET_HEADS + h]
        state[d, h] = state[d, h] * cdec + _dot_tn(kws[d, h], vs[d, h])


def _retention(zret3, decay_logit):
    bsz, seq, _ = zret3.shape
    ch = RET_CHUNK
    nc = seq // ch
    half = RET_HEAD_DIM // 2
    inv_freq = ROPE_BASE ** (-jnp.arange(half, dtype=F32) / half)
    ang = jnp.arange(seq, dtype=F32)[:, None] * inv_freq[None, :]
    cosf = jnp.concatenate([jnp.cos(ang), jnp.cos(ang)], axis=1)
    sinf = jnp.concatenate([-jnp.sin(ang), jnp.sin(ang)], axis=1)
    log_g = jax.nn.log_sigmoid(decay_logit.astype(F32)).reshape(-1)
    scal = jnp.concatenate([log_g, jnp.exp(log_g * ch)])
    w = RET_WIDTH
    fwd = lambda col: pl.BlockSpec((None, ch, w), lambda b, n, sc: (b, n, col))
    bwd = lambda col: pl.BlockSpec((None, ch, w), lambda b, n, sc: (b, nc - 1 - n, col))
    rope_f = pl.BlockSpec((ch, RET_HEAD_DIM), lambda b, n, sc: (n, 0))
    rope_b = pl.BlockSpec((ch, RET_HEAD_DIM), lambda b, n, sc: (nc - 1 - n, 0))
    grid_spec = pltpu.PrefetchScalarGridSpec(
        num_scalar_prefetch=1,
        grid=(bsz, nc),
        in_specs=[fwd(0), fwd(1), fwd(2), rope_f, rope_f, bwd(0), bwd(1), bwd(2), rope_b, rope_b],
        out_specs=[pl.BlockSpec((None, ch, w), lambda b, n, sc: (b, n, 0)),
                   pl.BlockSpec((None, ch, w), lambda b, n, sc: (b, nc - 1 - n, 0))],
        scratch_shapes=[pltpu.VMEM((2, RET_HEADS, RET_HEAD_DIM, RET_HEAD_DIM), F32),
                        pltpu.VMEM((2, RET_HEADS, ch, ch), F32),
                        pltpu.VMEM((2, RET_HEADS, ch, RET_HEAD_DIM), F32),
                        pltpu.VMEM((2, RET_HEADS, ch, RET_HEAD_DIM), F32)],
    )
    return pl.pallas_call(
        _retention_body,
        grid_spec=grid_spec,
        out_shape=[jax.ShapeDtypeStruct((bsz, seq, w), BF16), jax.ShapeDtypeStruct((bsz, seq, w), BF16)],
        compiler_params=_cparams(("arbitrary", "arbitrary")),
        name="retention",
    )(scal, zret3, zret3, zret3, cosf, sinf, zret3, zret3, zret3, cosf, sinf)


def _outproj_body(x_ref, yhy_ref, of_ref, ob_ref, gr_ref, ghy_ref, gret_ref, whyo_ref, wreto_ref, wo_ref,
                  n2g_ref, wrt_ref, x1_ref, pt_ref):
    dh = RET_HEAD_DIM
    o = of_ref[...].astype(F32) + ob_ref[...].astype(F32)
    parts = []
    for h in range(RET_HEADS):
        oh = o[:, h * dh:(h + 1) * dh]
        parts.append(oh * lax.rsqrt(jnp.mean(oh * oh, axis=-1, keepdims=True) + EPS))
    on = jnp.concatenate(parts, axis=1)
    gr = gr_ref[...].astype(F32)
    ret = (gr * jax.nn.sigmoid(gr)) * on
    y_ret = _dot(ret.astype(BF16), wreto_ref[...])
    y_hy = _dot(yhy_ref[...], whyo_ref[...])
    merged = jax.nn.sigmoid(ghy_ref[...].astype(F32)) * y_hy + jax.nn.sigmoid(gret_ref[...].astype(F32)) * y_ret
    x1 = x_ref[...] + _dot(merged.astype(BF16), wo_ref[...])
    x1_ref[...] = x1
    h2 = x1 * lax.rsqrt(jnp.mean(x1 * x1, axis=-1, keepdims=True) + EPS) * n2g_ref[...]
    logits = _dot_nt(wrt_ref[...], h2.astype(BF16))
    m = jnp.max(logits, axis=0, keepdims=True)
    e = jnp.exp(logits - m)
    pt_ref[...] = e / jnp.sum(e, axis=0, keepdims=True)


def _outproj(x2, yhy, o_f, o_b, zret, zg, whyo, wreto, wo, n2g, wrt, tm=256):
    n_tok = x2.shape[0]
    d = D_MODEL
    row = lambda w, col=0: pl.BlockSpec((tm, w), lambda i: (i, col))
    const = lambda shape: pl.BlockSpec(shape, lambda i: (0, 0))
    return pl.pallas_call(
        _outproj_body,
        grid=(n_tok // tm,),
        in_specs=[row(d), row(HY_WIDTH), row(RET_WIDTH), row(RET_WIDTH), row(RET_WIDTH, 3), row(d, 0), row(d, 1),
                  const((HY_WIDTH, d)), const((RET_WIDTH, d)), const((d, d)), const((1, d)), const((N_EXPERTS, d))],
        out_specs=[row(d), pl.BlockSpec((N_EXPERTS, tm), lambda i: (0, i))],
        out_shape=[jax.ShapeDtypeStruct((n_tok, d), F32), jax.ShapeDtypeStruct((N_EXPERTS, n_tok), F32)],
        compiler_params=_cparams(("parallel",)),
        name="outproj_router",
    )(x2, yhy, o_f, o_b, zret, zg, zg, whyo, wreto, wo, n2g, wrt)


def _select_body(p_ref, upper_ref, lower_ref, lowinc_ref, eye_ref, pos_ref, idx_ref, gate_ref, lo_ref, *, cap):
    rows = p_ref.shape[0]
    p = p_ref[...]
    bits = lax.bitcast_convert_type(p, I32)

    def count(mask):
        return jnp.sum(jnp.sum(mask.astype(F32), axis=1, keepdims=True), axis=0, keepdims=True)

    def bit_step(i, thr):
        cand = thr | jnp.left_shift(jnp.int32(1), 30 - i)
        return jnp.where(count(bits >= cand) >= cap, cand, thr)

    thr = lax.fori_loop(0, 31, bit_step, jnp.zeros((1, 1), I32))
    gt = bits > thr
    eq = bits == thr
    need = cap - count(gt)

    def prefix(mask_f):
        incl = _dot(mask_f.astype(BF16), upper_ref[...])
        tot = jnp.broadcast_to(incl[:, LANES - 1:LANES], incl.shape)
        base = _dot(lower_ref[...], tot.astype(BF16))
        return incl, base, tot

    eq_f = eq.astype(F32)
    incl_e, base_e, _ = prefix(eq_f)
    sel = gt | (eq & (base_e + incl_e - eq_f < need))
    sel_f = sel.astype(F32)
    incl, base, tot = prefix(sel_f)
    pos_ref[...] = jnp.where(sel, (base + incl - 1.0).astype(I32), -1)
    lo_ref[...] = base.astype(I32)

    rowend = base + tot
    incl_t = _dot_nt(lowinc_ref[...], sel_f.astype(BF16)).astype(BF16)
    p_t = []
    rem = p
    for _ in range(3):
        part = rem.astype(BF16)
        rem = rem - part.astype(F32)
        p_t.append(_dot_nt(eye_ref[...], part).astype(BF16))
    table = jnp.concatenate([incl_t] + p_t, axis=0)
    r_iota = lax.broadcasted_iota(I32, (rows, LANES), 0).astype(F32)
    lane_iota = lax.broadcasted_iota(I32, (LANES, LANES), 0).astype(F32)

    def slot_tile(ts, carry):
        s = (ts * LANES + lax.broadcasted_iota(I32, (1, LANES), 1)).astype(F32)
        done = rowend <= s
        row = jnp.sum(done.astype(F32), axis=0, keepdims=True)
        before = jnp.sum(jnp.where(done, tot, 0.0), axis=0, keepdims=True)
        onehot_t = (r_iota == row).astype(BF16)
        got = _dot(table, onehot_t)
        g_t = got[:LANES]
        lane = jnp.sum((g_t <= s - before).astype(F32), axis=0, keepdims=True)
        idx_ref[pl.ds(ts, 1), :] = (row * LANES + lane).astype(I32)
        p_row = got[LANES:2 * LANES] + got[2 * LANES:3 * LANES] + got[3 * LANES:]
        gate_ref[pl.ds(ts, 1), :] = jnp.sum(jnp.where(lane_iota == lane, p_row, 0.0), axis=0, keepdims=True)
        return carry

    lax.fori_loop(0, cap // LANES, slot_tile, 0)


def _select(pt3, cap):
    n_e, rows, _ = pt3.shape
    ii = np.arange(LANES)
    upper = jnp.asarray(ii[:, None] <= ii[None, :], BF16)
    lowinc = jnp.asarray(ii[None, :] <= ii[:, None], BF16)
    eye = jnp.asarray(ii[None, :] == ii[:, None], BF16)
    rr = np.arange(rows)
    lower = jnp.asarray(rr[None, :] < rr[:, None], BF16)
    const = lambda shape: pl.BlockSpec(shape, lambda e: (0, 0))
    tok_spec = pl.BlockSpec((None, rows, LANES), lambda e: (e, 0, 0))
    slot_spec = pl.BlockSpec((None, cap // LANES, LANES), lambda e: (e, 0, 0))
    return pl.pallas_call(
        functools.partial(_select_body, cap=cap),
        grid=(n_e,),
        in_specs=[tok_spec, const((LANES, LANES)), const((rows, rows)), const((LANES, LANES)), const((LANES, LANES))],
        out_specs=[tok_spec, slot_spec, slot_spec, tok_spec],
        out_shape=[jax.ShapeDtypeStruct((n_e, rows, LANES), I32),
                   jax.ShapeDtypeStruct((n_e, cap // LANES, LANES), I32),
                   jax.ShapeDtypeStruct((n_e, cap // LANES, LANES), F32),
                   jax.ShapeDtypeStruct((n_e, rows, LANES), I32)],
        compiler_params=_cparams(("parallel",)),
        name="expert_select",
    )(pt3, upper, lower, lowinc, eye)


def _ffn_body(idx_ref, h_hbm, n2g_ref, gate_ref, wg_ref, wu_ref, wd_ref, ye_ref, xbuf, xb, sems, *,
              tiles_per_expert):
    s = xbuf.shape[1]
    step = pl.program_id(0) * tiles_per_expert + pl.program_id(1)
    n_steps = pl.num_programs(0) * tiles_per_expert
    cur = step % 2

    def row_copy(st, i, buf):
        tok = idx_ref[st * s + i]
        return pltpu.make_async_copy(h_hbm.at[pl.ds(tok, 1), :], xbuf.at[buf, pl.ds(i, 1), :], sems.at[buf])

    def wait_rows(buf):
        pltpu.make_async_copy(h_hbm.at[pl.ds(0, s), :], xbuf.at[buf], sems.at[buf]).wait()

    @pl.when(step == 0)
    def _():
        def one(i, carry):
            row_copy(0, i, 0).start()
            return carry
        lax.fori_loop(0, s, one, 0, unroll=8)

    @pl.when(step + 1 < n_steps)
    def _():
        for i in range(s):
            row_copy(step + 1, i, 1 - cur).start()

    wait_rows(cur)
    x = xbuf[cur]
    xb[...] = (x * lax.rsqrt(jnp.mean(x * x, axis=-1, keepdims=True) + EPS) * n2g_ref[...]).astype(BF16)
    y = None
    for f0 in range(0, EXPERT_FF, FF_CHUNK):
        f1 = min(f0 + FF_CHUNK, EXPERT_FF)
        a = _dot(xb[...], wg_ref[:, f0:f1])
        b = _dot(xb[...], wu_ref[:, f0:f1])
        hid = ((a * jax.nn.sigmoid(a)) * b).astype(BF16)
        part = _dot(hid, wd_ref[f0:f1, :])
        y = part if y is None else y + part
    for k in range(s // LANES):
        col = jnp.transpose(jnp.broadcast_to(gate_ref[k:k + 1, :], (LANES, LANES)))[:, 0:1]
        ye_ref[k * LANES:(k + 1) * LANES, :] = (y[k * LANES:(k + 1) * LANES] * col).astype(ye_ref.dtype)


def _expert_ffn(idx_flat, x1, n2g, gates, wg, wu, wd, cap, s=512):
    s = min(s, cap)
    tiles = cap // s
    g3 = gates.reshape(N_EXPERTS * tiles, s // LANES, LANES)
    grid_spec = pltpu.PrefetchScalarGridSpec(
        num_scalar_prefetch=1,
        grid=(N_EXPERTS, tiles),
        in_specs=[pl.BlockSpec(memory_space=pl.ANY),
                  pl.BlockSpec((1, D_MODEL), lambda e, j, idx: (0, 0)),
                  pl.BlockSpec((None, s // LANES, LANES), lambda e, j, idx: (e * tiles + j, 0, 0)),
                  pl.BlockSpec((None, D_MODEL, EXPERT_FF), lambda e, j, idx: (e, 0, 0)),
                  pl.BlockSpec((None, D_MODEL, EXPERT_FF), lambda e, j, idx: (e, 0, 0)),
                  pl.BlockSpec((None, EXPERT_FF, D_MODEL), lambda e, j, idx: (e, 0, 0))],
        out_specs=pl.BlockSpec((s, D_MODEL), lambda e, j, idx: (e * tiles + j, 0)),
        scratch_shapes=[pltpu.VMEM((2, s, D_MODEL), F32), pltpu.VMEM((s, D_MODEL), BF16),
                        pltpu.SemaphoreType.DMA((2,))],
    )
    return pl.pallas_call(
        functools.partial(_ffn_body, tiles_per_expert=tiles),
        grid_spec=grid_spec,
        out_shape=jax.ShapeDtypeStruct((N_EXPERTS * cap, D_MODEL), BF16),
        compiler_params=_cparams(("arbitrary", "arbitrary")),
        name="expert_ffn",
    )(idx_flat, x1, n2g, g3, wg, wu, wd)


def _combine_body(lo_ref, np_ref, x1_ref, pos_ref, nfg_ref, expand_ref, ye_hbm, o_ref, win, sems, *, cap, rows_total):
    r = pl.program_id(0)
    n_rows = pl.num_programs(0)
    w = COMBINE_WIN
    cur = r % 2

    def starts(rr, m):
        out = []
        for e in range(N_EXPERTS):
            first = e * cap + lo_ref[e * n_rows + rr]
            intended = (first // COMBINE_ALIGN) * COMBINE_ALIGN + m * w
            actual = pl.multiple_of(jnp.minimum(intended, rows_total - w), COMBINE_ALIGN)
            out.append((intended, actual))
        return out

    def copies(rr, m, buf):
        return [pltpu.make_async_copy(ye_hbm.at[pl.ds(actual, w), :], win.at[buf, pl.ds(e * w, w), :], sems.at[buf, e])
                for e, (_, actual) in enumerate(starts(rr, m))]

    def contribution(rr, m, buf):
        lane = lax.broadcasted_iota(I32, (1, N_EXPERTS), 1)
        intended = jnp.zeros((1, N_EXPERTS), I32)
        actual = jnp.zeros((1, N_EXPERTS), I32)
        for e, (i_s, a_s) in enumerate(starts(rr, m)):
            intended = jnp.where(lane == e, i_s, intended)
            actual = jnp.where(lane == e, a_s, actual)
        pos = pos_ref[...]
        glob = pos + lane * cap
        rel = glob - intended
        valid = (pos >= 0) & (rel >= 0) & (rel < w)
        local = jnp.where(valid, glob - actual, -1).astype(F32).astype(BF16)
        spread = _dot(local, expand_ref[...])
        col = (lax.broadcasted_iota(I32, (1, N_EXPERTS * w), 1) & (w - 1)).astype(F32)
        onehot = (spread == col).astype(BF16)
        return _dot(onehot, win[buf])

    @pl.when(r == 0)
    def _():
        for cp in copies(0, 0, 0):
            cp.start()

    @pl.when(r + 1 < n_rows)
    def _():
        for cp in copies(r + 1, 0, 1 - cur):
            cp.start()

    for cp in copies(r, 0, cur):
        cp.wait()
    acc = x1_ref[...] + contribution(r, 0, cur)

    def extra_pass(m, acc):
        for cp in copies(r, m, cur):
            cp.start()
        for cp in copies(r, m, cur):
            cp.wait()
        return acc + contribution(r, m, cur)

    acc = lax.fori_loop(1, np_ref[r], extra_pass, acc)
    o_ref[...] = acc * lax.rsqrt(jnp.mean(acc * acc, axis=-1, keepdims=True) + EPS) * nfg_ref[...]


def _combine(lo, x1, pos_t, nfg, ye, cap):
    n_tok = x1.shape[0]
    tm = LANES
    w = COMBINE_WIN
    rows_total = ye.shape[0]
    n_rows = n_tok // tm
    nxt = jnp.concatenate([lo[:, 1:], jnp.full((N_EXPERTS, 1), cap, I32)], axis=1)
    span = lo % COMBINE_ALIGN + (nxt - lo)
    n_pass = jnp.maximum(jnp.max((span + w - 1) // w, axis=0), 1).astype(I32)
    ee = np.arange(N_EXPERTS)
    expand = jnp.asarray(ee[:, None] == (np.arange(N_EXPERTS * w) // w)[None, :], BF16)
    grid_spec = pltpu.PrefetchScalarGridSpec(
        num_scalar_prefetch=2,
        grid=(n_rows,),
        in_specs=[pl.BlockSpec((tm, D_MODEL), lambda i, lo_, np_: (i, 0)),
                  pl.BlockSpec((tm, N_EXPERTS), lambda i, lo_, np_: (i, 0)),
                  pl.BlockSpec((1, D_MODEL), lambda i, lo_, np_: (0, 0)),
                  pl.BlockSpec((N_EXPERTS, N_EXPERTS * w), lambda i, lo_, np_: (0, 0)),
                  pl.BlockSpec(memory_space=pl.ANY)],
        out_specs=pl.BlockSpec((tm, D_MODEL), lambda i, lo_, np_: (i, 0)),
        scratch_shapes=[pltpu.VMEM((2, N_EXPERTS * w, D_MODEL), BF16),
                        pltpu.SemaphoreType.DMA((2, N_EXPERTS))],
    )
    return pl.pallas_call(
        functools.partial(_combine_body, cap=cap, rows_total=rows_total),
        grid_spec=grid_spec,
        out_shape=jax.ShapeDtypeStruct((n_tok, D_MODEL), F32),
        compiler_params=_cparams(("arbitrary",)),
        name="moe_combine_norm",
    )(lo.reshape(-1), n_pass, x1, pos_t, nfg, expand, ye)


def _trunk(x, w):
    bsz, seq, d = x.shape
    n_tok = bsz * seq
    x2 = x.reshape(n_tok, d)
    tabs = _fft_tables(seq)

    zhy, zret, zg = _inproj(x2, w["norm1_g"], w["w_in"])
    zc = _shortconv(zhy.reshape(bsz, seq, -1), w["hy_conv_w"], w["hy_conv_b"])

    h_time, h_abs = _hyena_filter_time(seq, w["hy_w1"], w["hy_b1"], w["hy_freq1"], w["hy_w2"], w["hy_b2"],
                                       w["hy_freq2"], w["hy_w3"])
    kr, ki = _stage_b_filter(_stage_a_real(h_time, tabs), h_abs, tabs)
    yhy = _hyena(zc, kr, ki, w["hy_skip"], tabs)

    o_f, o_b = _retention(zret.reshape(bsz, seq, -1), w["ret_decay_logit"])
    x1, pt = _outproj(x2, yhy, o_f.reshape(n_tok, -1), o_b.reshape(n_tok, -1), zret, zg,
                      w["w_hy_out"], w["w_ret_out"], w["w_o"], w["norm2_g"], w["w_router_t"])

    cap = CAPACITY_FACTOR * n_tok // N_EXPERTS
    rows = n_tok // LANES
    pos, idx, gates, lo = _select(pt.reshape(N_EXPERTS, rows, LANES), cap)
    ye = _expert_ffn(idx.reshape(-1), x1, w["norm2_g"], gates, w["w_gate"], w["w_up"], w["w_down"], cap)
    pos_t = pos.reshape(N_EXPERTS, n_tok).T
    y = _combine(lo[:, :, 0], x1, pos_t, w["norm_f_g"], ye, cap)
    return y.reshape(bsz, seq, d)


def kernel(x_prompt, x_sample, norm1_g, w_in, hy_conv_w, hy_conv_b, hy_w1, hy_b1, hy_freq1, hy_w2, hy_b2, hy_freq2,
           hy_w3, hy_skip, ret_decay_logit, w_hy_out, w_ret_out, w_o, norm2_g, w_router, w_gate, w_up, w_down,
           norm_f_g):
    layer = 0
    w = dict(
        norm1_g=norm1_g[layer].astype(F32)[None], w_in=w_in[layer].astype(BF16),
        hy_conv_w=hy_conv_w[layer].astype(F32), hy_conv_b=hy_conv_b[layer].astype(F32)[None],
        hy_w1=hy_w1[layer], hy_b1=hy_b1[layer], hy_freq1=hy_freq1[layer], hy_w2=hy_w2[layer], hy_b2=hy_b2[layer],
        hy_freq2=hy_freq2[layer], hy_w3=hy_w3[layer], hy_skip=hy_skip[layer],
        ret_decay_logit=ret_decay_logit[layer],
        w_hy_out=w_hy_out[layer].astype(BF16), w_ret_out=w_ret_out[layer].astype(BF16), w_o=w_o[layer].astype(BF16),
        norm2_g=norm2_g[layer].astype(F32)[None], w_router_t=w_router[layer].T.astype(BF16),
        w_gate=w_gate[layer].astype(BF16), w_up=w_up[layer].astype(BF16), w_down=w_down[layer].astype(BF16),
        norm_f_g=norm_f_g.astype(F32)[None],
    )
    return _trunk(x_prompt, w), _trunk(x_sample, w)
```

```python
import functools
import math

import numpy as np
import jax
import jax.numpy as jnp
from jax import lax
from jax.experimental import pallas as pl
from jax.experimental.pallas import tpu as pltpu

F32 = jnp.float32
BF16 = jnp.bfloat16
I32 = jnp.int32

D_MODEL = 1024
HY_WIDTH = 512
HY_ORDER = 2
HY_BANDS = 16
HY_FILTER_WIDTH = 64
HY_FAST_DECAY = 0.3
HY_SLOW_DECAY = 1.5
HY_DECAY_TARGET = 1e-2
RET_WIDTH = 512
RET_HEADS = 4
RET_HEAD_DIM = 128
RET_CHUNK = 128
ROPE_BASE = 10000.0
N_EXPERTS = 16
EXPERT_FF = 1408
CAPACITY_FACTOR = 2
EPS = 1e-6

LANES = 128
BF16_SUBLANES = 16
EMB_PAD = 128
DFT_N1_MAX = 128
FF_CHUNK = 256
COMBINE_ALIGN = 16
COMBINE_WIN = 64


def _cparams(sem, vmem_mb=48):
    return pltpu.CompilerParams(dimension_semantics=sem, vmem_limit_bytes=vmem_mb * 1024 * 1024)


def _dot(a, b):
    return jnp.dot(a, b, preferred_element_type=F32)


def _dot_nt(a, b):
    return lax.dot_general(a, b, (((1,), (1,)), ((), ())), preferred_element_type=F32)


def _dot_tn(a, b):
    return lax.dot_general(a, b, (((0,), (0,)), ((), ())), preferred_element_type=F32)


def _split(a):
    hi = a.astype(BF16)
    lo = (a - hi.astype(F32)).astype(BF16)
    return hi, lo


def _dot_const(m_hi, m_lo, x, passes):
    xh = x.astype(BF16)
    r = _dot(m_hi, xh)
    if passes >= 3:
        xl = (x - xh.astype(F32)).astype(BF16)
        r = r + _dot(m_lo, xh) + _dot(m_hi, xl)
    return r


def _dot3(a, b):
    ah, al = _split(a)
    bh, bl = _split(b)
    return _dot(ah, bh) + _dot(al, bh) + _dot(ah, bl)


def _inproj_body(x_ref, g_ref, w_ref, zhy_ref, zret_ref, zg_ref):
    x = x_ref[...]
    ms = jnp.mean(x * x, axis=-1, keepdims=True)
    h = (x * lax.rsqrt(ms + EPS) * g_ref[...]).astype(BF16)
    n_hy = zhy_ref.shape[1]
    n_ret = zret_ref.shape[1]
    zhy_ref[...] = _dot(h, w_ref[:, :n_hy]).astype(BF16)
    zret_ref[...] = _dot(h, w_ref[:, n_hy:n_hy + n_ret]).astype(BF16)
    zg_ref[...] = _dot(h, w_ref[:, n_hy + n_ret:]).astype(BF16)


def _inproj(x2, g, w_bf, tm=512):
    n_tok = x2.shape[0]
    n_hy, n_ret, n_g = 3 * HY_WIDTH, 4 * RET_WIDTH, 2 * D_MODEL
    return pl.pallas_call(
        _inproj_body,
        grid=(n_tok // tm,),
        in_specs=[pl.BlockSpec((tm, D_MODEL), lambda i: (i, 0)),
                  pl.BlockSpec((1, D_MODEL), lambda i: (0, 0)),
                  pl.BlockSpec((D_MODEL, n_hy + n_ret + n_g), lambda i: (0, 0))],
        out_specs=[pl.BlockSpec((tm, n_hy), lambda i: (i, 0)),
                   pl.BlockSpec((tm, n_ret), lambda i: (i, 0)),
                   pl.BlockSpec((tm, n_g), lambda i: (i, 0))],
        out_shape=[jax.ShapeDtypeStruct((n_tok, n_hy), BF16),
                   jax.ShapeDtypeStruct((n_tok, n_ret), BF16),
                   jax.ShapeDtypeStruct((n_tok, n_g), BF16)],
        compiler_params=_cparams(("parallel",)),
        name="inproj",
    )(x2, g, w_bf)


def _shortconv_body(z_ref, zp_ref, zn_ref, w_ref, b_ref, o_ref):
    tl = z_ref.shape[0]
    i = pl.program_id(1)
    z = z_ref[...].astype(F32)
    halo = zp_ref.shape[0]
    prev_row = jnp.where(i > 0, zp_ref[halo - 1:halo, :].astype(F32), 0.0)
    next_row = jnp.where(i < pl.num_programs(1) - 1, zn_ref[0:1, :].astype(F32), 0.0)
    row = lax.broadcasted_iota(I32, (tl, 1), 0)
    zm1 = jnp.where(row == 0, prev_row, pltpu.roll(z, 1, axis=0))
    zp1 = jnp.where(row == tl - 1, next_row, pltpu.roll(z, tl - 1, axis=0))
    o_ref[...] = zm1 * w_ref[0:1, :] + z * w_ref[1:2, :] + zp1 * w_ref[2:3, :] + b_ref[...]


def _shortconv(z3, w, b, tl=256):
    bsz, seq, width = z3.shape
    tl = min(tl, seq)
    halo = BF16_SUBLANES
    nb8 = seq // halo
    r8 = tl // halo
    return pl.pallas_call(
        _shortconv_body,
        grid=(bsz, seq // tl),
        in_specs=[pl.BlockSpec((None, tl, width), lambda b_, i: (b_, i, 0)),
                  pl.BlockSpec((None, halo, width), lambda b_, i: (b_, jnp.maximum(i * r8 - 1, 0), 0)),
                  pl.BlockSpec((None, halo, width), lambda b_, i: (b_, jnp.minimum((i + 1) * r8, nb8 - 1), 0)),
                  pl.BlockSpec((3, width), lambda b_, i: (0, 0)),
                  pl.BlockSpec((1, width), lambda b_, i: (0, 0))],
        out_specs=pl.BlockSpec((None, tl, width), lambda b_, i: (b_, i, 0)),
        out_shape=jax.ShapeDtypeStruct((bsz, seq, width), F32),
        compiler_params=_cparams(("parallel", "parallel")),
        name="shortconv",
    )(z3, z3, z3, w, b)


def _fft_dims(seq):
    n = 2 * seq
    n1 = min(DFT_N1_MAX, 1 << (n.bit_length() // 2))
    n2 = n // n1
    assert n1 * n2 == n and n1 % 2 == 0
    return n, n1, n2


def _hi_lo_const(m):
    m = np.asarray(m, np.float64)
    hi = jnp.asarray(m, F32).astype(BF16)
    lo = (jnp.asarray(m, F32) - hi.astype(F32)).astype(BF16)
    return hi, lo


def _fft_tables(seq):
    n, n1, n2 = _fft_dims(seq)
    r1 = n1 // 2
    k1 = jnp.arange(n1, dtype=I32)
    j = jnp.arange(n2, dtype=I32)

    at = ((j[:, None] * k1[None, :]) % n).astype(F32) * (2.0 * math.pi / n)
    ct, st = jnp.cos(at), jnp.sin(at)

    def cos_sin(n1_count, k1_first):
        a1 = ((k1[:, None] * jnp.arange(n1_count, dtype=I32)[None, :]) % n1).astype(F32) * (2.0 * math.pi / n1)
        cf, sf = jnp.cos(a1), jnp.sin(a1)
        if k1_first:
            cf, sf, ctb, stb = cf[None], sf[None], ct[:, :, None], st[:, :, None]
        else:
            cf, sf, ctb, stb = cf.T[None], sf.T[None], ct[:, None, :], st[:, None, :]
        return cf * ctb - sf * stb, sf * ctb + cf * stb

    cat = jnp.concatenate
    c, s = cos_sin(r1, True)
    mat_a = cat([cat([c, s], axis=2), cat([-s, c], axis=2)], axis=1).astype(BF16)
    c, s = cos_sin(n1, True)
    mat_a_real = cat([c, -s], axis=1).astype(BF16)
    c, s = cos_sin(r1, False)
    mat_c = cat([cat([c, -s], axis=2), cat([s, c], axis=2)], axis=1).astype(BF16)
    a2 = 2.0 * np.pi * np.outer(np.arange(n2), np.arange(n2)) / n2
    gr, gi = np.cos(a2), -np.sin(a2)
    mat_f = np.block([[gr, -gi], [gi, gr]])
    mat_i = np.block([[gr, gi], [-gi, gr]])
    return dict(n=n, n1=n1, n2=n2, mat_a=mat_a, mat_a_real=mat_a_real, mat_c=mat_c,
                mat_f=_hi_lo_const(mat_f), mat_i=_hi_lo_const(mat_i))


def _filter_body(z_ref, w1_ref, b1_ref, f1_ref, w2_ref, b2_ref, f2_ref, w3_ref, dl_ref, h_ref, s_ref, *, seq):
    i = pl.program_id(0)
    tl = z_ref.shape[0]
    z = z_ref[...]
    a = jnp.sin(f1_ref[...] * (_dot3(z, w1_ref[...]) + b1_ref[...]))
    a = jnp.sin(f2_ref[...] * (_dot3(a, w2_ref[...]) + b2_ref[...]))
    h = _dot3(a, w3_ref[...])
    win = jnp.exp(-z[:, 0:1] * dl_ref[...])
    h = h * jnp.concatenate([win] * HY_ORDER, axis=1)
    row = i * tl + lax.broadcasted_iota(I32, (tl, 1), 0)
    h = jnp.where(row == seq, 0.0, h)
    h_ref[...] = h

    @pl.when(i == 0)
    def _():
        s_ref[...] = jnp.zeros_like(s_ref)

    s_ref[...] += jnp.broadcast_to(jnp.sum(jnp.abs(h), axis=0, keepdims=True), s_ref.shape)


def _hyena_filter_time(seq, w1, b1, f1, w2, b2, f2, w3, tl=512):
    n = 2 * seq
    tl = min(tl, seq)
    idx = jnp.arange(n, dtype=I32)
    p = jnp.minimum(jnp.where(idx < seq, idx, n - idx), seq - 1).astype(F32)
    t = p / float(seq - 1)
    ang = 2.0 * math.pi * p / seq
    bands = jnp.linspace(1e-4, HY_BANDS - 1, HY_BANDS, dtype=F32)
    phase = ang[:, None] * bands[None, :]
    emb = jnp.concatenate([t[:, None], jnp.cos(phase), -jnp.sin(phase)], axis=-1)
    emb = jnp.pad(emb, ((0, 0), (0, EMB_PAD - emb.shape[1])))
    w1p = jnp.pad(w1.astype(F32), ((0, EMB_PAD - w1.shape[0]), (0, 0)))
    w3d = w3.astype(F32).reshape(HY_FILTER_WIDTH, HY_ORDER, 2, HY_WIDTH).transpose(2, 0, 1, 3)
    w3d = w3d.reshape(2, HY_FILTER_WIDTH, HY_ORDER * HY_WIDTH)
    deltas = jnp.abs(jnp.linspace(math.log(HY_DECAY_TARGET) / HY_SLOW_DECAY,
                                  math.log(HY_DECAY_TARGET) / HY_FAST_DECAY, HY_WIDTH, dtype=F32))[None, :]
    fw = HY_FILTER_WIDTH
    nblk_half = seq // tl
    const = lambda i: (0, 0)
    return pl.pallas_call(
        functools.partial(_filter_body, seq=seq),
        grid=(n // tl,),
        in_specs=[pl.BlockSpec((tl, EMB_PAD), lambda i: (i, 0)),
                  pl.BlockSpec((EMB_PAD, fw), const), pl.BlockSpec((1, fw), const), pl.BlockSpec((1, fw), const),
                  pl.BlockSpec((fw, fw), const), pl.BlockSpec((1, fw), const), pl.BlockSpec((1, fw), const),
                  pl.BlockSpec((None, fw, HY_ORDER * HY_WIDTH), lambda i: (i // nblk_half, 0, 0)),
                  pl.BlockSpec((1, HY_WIDTH), const)],
        out_specs=[pl.BlockSpec((tl, HY_ORDER * HY_WIDTH), lambda i: (i, 0)),
                   pl.BlockSpec((8, HY_ORDER * HY_WIDTH), const)],
        out_shape=[jax.ShapeDtypeStruct((n, HY_ORDER * HY_WIDTH), F32),
                   jax.ShapeDtypeStruct((8, HY_ORDER * HY_WIDTH), F32)],
        compiler_params=_cparams(("arbitrary",)),
        name="hyena_filter",
    )(emb, w1p, b1.astype(F32)[None], f1.astype(F32)[None], w2.astype(F32), b2.astype(F32)[None],
      f2.astype(F32)[None], w3d, deltas)


def _stage_a_real_body(h_ref, m_ref, a_ref):
    n1 = h_ref.shape[0]
    h_t = jnp.swapaxes(h_ref[...], 0, 1)
    rs = [_dot(m_ref[j], h_t[j].astype(BF16)) for j in range(h_ref.shape[1])]
    r = jnp.swapaxes(jnp.stack(rs, axis=0), 0, 1)
    a_ref[0] = r[:n1]
    a_ref[1] = r[n1:]


def _stage_a_real(h, tabs, n2c=8, cw=512):
    n1, n2 = tabs["n1"], tabs["n2"]
    c = h.shape[1]
    h3 = h.reshape(n1, n2, c)
    mat = tabs["mat_a_real"]
    return pl.pallas_call(
        _stage_a_real_body,
        grid=(n2 // n2c, c // cw),
        in_specs=[pl.BlockSpec((n1, n2c, cw), lambda j, k: (0, j, k)),
                  pl.BlockSpec((n2c,) + mat.shape[1:], lambda j, k: (j, 0, 0))],
        out_specs=pl.BlockSpec((2, n1, n2c, cw), lambda j, k: (0, 0, j, k)),
        out_shape=jax.ShapeDtypeStruct((2, n1, n2, c), F32),
        compiler_params=_cparams(("parallel", "parallel")),
        name="dft_a_filter",
    )(h3, mat)


def _pack_pair(re, im):
    hi = lax.bitcast_convert_type(re.astype(BF16).astype(F32), I32)
    lo = lax.bitcast_convert_type(im.astype(BF16).astype(F32), I32)
    return lax.bitcast_convert_type(hi | lax.shift_right_logical(lo, 16), F32)


def _unpack_pair(packed):
    word = lax.bitcast_convert_type(packed, I32)
    re = lax.bitcast_convert_type(word & jnp.int32(-65536), F32)
    im = lax.bitcast_convert_type(lax.shift_left(word, 16), F32)
    return re, im


def _stage_a_body(u_ref, m_ref, a_ref):
    _, r1, n2c, cw = u_ref.shape
    n1 = 2 * r1
    u_t = jnp.swapaxes(u_ref[...].reshape(n1, n2c, cw), 0, 1)
    packed = []
    for j in range(n2c):
        r = _dot(m_ref[j], u_t[j].astype(BF16))
        packed.append(_pack_pair(r[:n1], r[n1:]))
    a_ref[...] = jnp.swapaxes(jnp.stack(packed, axis=0), 0, 1)


def _stage_a(u5, col, tabs, n2c=8):
    n1, n2 = tabs["n1"], tabs["n2"]
    npair, _, r1, _, _ = u5.shape
    cw = HY_WIDTH
    mat = tabs["mat_a"]
    return pl.pallas_call(
        _stage_a_body,
        grid=(npair, n2 // n2c),
        in_specs=[pl.BlockSpec((None, 2, r1, n2c, cw), lambda p, j: (p, 0, 0, j, col)),
                  pl.BlockSpec((n2c,) + mat.shape[1:], lambda p, j: (j, 0, 0))],
        out_specs=pl.BlockSpec((None, n1, n2c, cw), lambda p, j: (p, 0, j, 0)),
        out_shape=jax.ShapeDtypeStruct((npair, n1, n2, cw), F32),
        compiler_params=_cparams(("parallel", "parallel")),
        name="dft_a",
    )(u5, mat)


def _stage_b_filter_body(a_ref, g_ref, s_ref, kr_ref, ki_ref, *, n):
    n2 = a_ref.shape[2]
    scale = 1.0 / (s_ref[0:1, :] * float(n))
    for kk in range(a_ref.shape[1]):
        d = _dot(g_ref[...], jnp.concatenate([a_ref[0, kk], a_ref[1, kk]], axis=0).astype(BF16))
        kr_ref[kk] = d[:n2] * scale
        ki_ref[kk] = d[n2:] * scale


def _stage_b_filter(a, s, tabs):
    n, n1, n2 = tabs["n"], tabs["n1"], tabs["n2"]
    c = a.shape[-1]
    kb = max(1, min(n1, 512 // n2))
    g = tabs["mat_f"][0]
    return pl.pallas_call(
        functools.partial(_stage_b_filter_body, n=n),
        grid=(n1 // kb,),
        in_specs=[pl.BlockSpec((2, kb, n2, c), lambda k: (0, k, 0, 0)),
                  pl.BlockSpec(g.shape, lambda k: (0, 0)),
                  pl.BlockSpec(s.shape, lambda k: (0, 0))],
        out_specs=[pl.BlockSpec((kb, n2, c), lambda k: (k, 0, 0)),
                   pl.BlockSpec((kb, n2, c), lambda k: (k, 0, 0))],
        out_shape=[jax.ShapeDtypeStruct((n1, n2, c), F32), jax.ShapeDtypeStruct((n1, n2, c), F32)],
        compiler_params=_cparams(("parallel",)),
        name="dft_b_filter",
    )(a, g, s)


def _stage_b_body(a_ref, kr_ref, ki_ref, gf_ref, gi_ref, c_ref):
    n2 = a_ref.shape[1]
    for kk in range(a_ref.shape[0]):
        d = _dot(gf_ref[...], jnp.concatenate(_unpack_pair(a_ref[kk]), axis=0).astype(BF16))
        dr, di = d[:n2], d[n2:]
        kr, ki = kr_ref[kk], ki_ref[kk]
        yr = dr * kr - di * ki
        yi = dr * ki + di * kr
        e = _dot(gi_ref[...], jnp.concatenate([yr, yi], axis=0).astype(BF16))
        c_ref[kk] = _pack_pair(e[:n2], e[n2:])


def _stage_b(a, kr, ki, order, tabs):
    n1, n2 = tabs["n1"], tabs["n2"]
    npair = a.shape[0]
    cw = HY_WIDTH
    kb = max(1, min(n1, 1024 // n2))
    gf = tabs["mat_f"][0]
    gi = tabs["mat_i"][0]
    const = lambda k, p: (0, 0)
    return pl.pallas_call(
        _stage_b_body,
        grid=(n1 // kb, npair),
        in_specs=[pl.BlockSpec((None, kb, n2, cw), lambda k, p: (p, k, 0, 0)),
                  pl.BlockSpec((kb, n2, cw), lambda k, p: (k, 0, order)),
                  pl.BlockSpec((kb, n2, cw), lambda k, p: (k, 0, order)),
                  pl.BlockSpec(gf.shape, const), pl.BlockSpec(gi.shape, const)],
        out_specs=pl.BlockSpec((None, kb, n2, cw), lambda k, p: (p, k, 0, 0)),
        out_shape=jax.ShapeDtypeStruct(a.shape, F32),
        compiler_params=_cparams(("parallel", "parallel")),
        name="dft_b",
    )(a, kr, ki, gf, gi)


def _stage_c_body(c_ref, u_ref, g_ref, skip_ref, mc_ref, ma_ref, y_ref, *rest, fuse_a):
    _, r1, n2c, cw = u_ref.shape
    n1 = 2 * r1
    ys = []
    c_t = jnp.swapaxes(c_ref[...], 0, 1)
    for j in range(n2c):
        cc = jnp.concatenate(_unpack_pair(c_t[j]), axis=0)
        ys.append(_dot(mc_ref[j], cc.astype(BF16)))
    y = jnp.swapaxes(jnp.stack(ys, axis=0), 0, 1)
    u = u_ref[...].reshape(n1, n2c, cw)
    g = g_ref[...].reshape(n1, n2c, cw)
    yo = g * (y + u * skip_ref[...].reshape(1, 1, cw))
    y_ref[...] = yo.reshape(2, r1, n2c, cw).astype(y_ref.dtype)
    if fuse_a:
        a_ref = rest[0]
        yo_t = jnp.swapaxes(yo, 0, 1)
        packed = []
        for j in range(n2c):
            r = _dot(ma_ref[j], yo_t[j].astype(BF16))
            packed.append(_pack_pair(r[:n1], r[n1:]))
        a_ref[...] = jnp.swapaxes(jnp.stack(packed, axis=0), 0, 1)


def _stage_c(c, u5, u_col, g5, g_col, skip, tabs, fuse_a, out_dtype, n2c=8):
    n1, n2 = tabs["n1"], tabs["n2"]
    npair, _, r1, _, _ = u5.shape
    cw = HY_WIDTH
    mc = tabs["mat_c"]
    ma = tabs["mat_a"]
    const = lambda p, j: (0, 0)
    per_j = lambda m: pl.BlockSpec((n2c,) + m.shape[1:], lambda p, j: (j, 0, 0))
    out_shape = [jax.ShapeDtypeStruct((npair, 2, r1, n2, cw), out_dtype)]
    out_specs = [pl.BlockSpec((None, 2, r1, n2c, cw), lambda p, j: (p, 0, 0, j, 0))]
    if fuse_a:
        out_shape.append(jax.ShapeDtypeStruct((npair, n1, n2, cw), F32))
        out_specs.append(pl.BlockSpec((None, n1, n2c, cw), lambda p, j: (p, 0, j, 0)))
    return pl.pallas_call(
        functools.partial(_stage_c_body, fuse_a=fuse_a),
        grid=(npair, n2 // n2c),
        in_specs=[pl.BlockSpec((None, n1, n2c, cw), lambda p, j: (p, 0, j, 0)),
                  pl.BlockSpec((None, 2, r1, n2c, cw), lambda p, j: (p, 0, 0, j, u_col)),
                  pl.BlockSpec((None, 2, r1, n2c, cw), lambda p, j: (p, 0, 0, j, g_col)),
                  pl.BlockSpec((1, cw), const), per_j(mc), per_j(ma)],
        out_specs=out_specs,
        out_shape=out_shape,
        compiler_params=_cparams(("parallel", "parallel")),
        name="dft_c",
    )(c, u5, g5, skip, mc, ma)


def _hyena(zc3, kr, ki, skip, tabs):
    bsz, seq, _ = zc3.shape
    n1, n2 = tabs["n1"], tabs["n2"]
    r1 = n1 // 2
    z5 = zc3.reshape(bsz // 2, 2, r1, n2, 3 * HY_WIDTH)
    a = _stage_a(z5, 2, tabs)
    c = _stage_b(a, kr, ki, 0, tabs)
    y1, a2 = _stage_c(c, z5, 2, z5, 0, skip[0:1].astype(F32), tabs, True, F32)
    c2 = _stage_b(a2, kr, ki, 1, tabs)
    (y2,) = _stage_c(c2, y1, 0, z5, 1, skip[1:2].astype(F32), tabs, False, BF16)
    return y2.reshape(bsz * seq, HY_WIDTH)


def _retention_body(sc_ref, qf_ref, kf_ref, vf_ref, cf_ref, sf_ref, qb_ref, kb_ref, vb_ref, cb_ref, sb_ref,
                    of_ref, ob_ref, state, dmat, qwt, kwt):
    ch = RET_CHUNK
    dh = RET_HEAD_DIM

    @pl.when(pl.program_id(1) == 0)
    def _():
        state[...] = jnp.zeros_like(state)
        ci = lax.broadcasted_iota(I32, (ch, ch), 0).astype(F32)
        mi = lax.broadcasted_iota(I32, (ch, ch), 1).astype(F32)
        lag = ci - mi
        for d in range(2):
            for h in range(RET_HEADS):
                lg = sc_ref[d * RET_HEADS + h]
                if d == 0:
                    dmat[d, h] = jnp.where(lag >= 0, jnp.exp(lg * jnp.maximum(lag, 0.0)), 0.0)
                    qwt[d, h] = jnp.exp(lg * (ci + 1.0))
                    kwt[d, h] = jnp.exp(lg * (ch - 1.0 - ci))
                else:
                    dmat[d, h] = jnp.where(lag < 0, jnp.exp(lg * jnp.maximum(-lag, 0.0)), 0.0)
                    qwt[d, h] = jnp.exp(lg * (ch - ci))
                    kwt[d, h] = jnp.exp(lg * ci)

    scale = RET_HEAD_DIM ** -0.5
    refs = ((qf_ref, kf_ref, vf_ref, cf_ref, sf_ref, of_ref), (qb_ref, kb_ref, vb_ref, cb_ref, sb_ref, ob_ref))
    work = [(d, h) for d in range(2) for h in range(RET_HEADS)]
    qs, ks, qws, kws, vs = {}, {}, {}, {}, {}
    for d, h in work:
        q_ref, k_ref, v_ref, c_ref, s_ref, _ = refs[d]
        sl = slice(h * dh, (h + 1) * dh)
        cosf = c_ref[...]
        sinf = s_ref[...]
        q = q_ref[:, sl].astype(F32)
        k = k_ref[:, sl].astype(F32)
        q = q * cosf + pltpu.roll(q, dh // 2, axis=1) * sinf
        k = (k * cosf + pltpu.roll(k, dh // 2, axis=1) * sinf) * scale
        qs[d, h], ks[d, h] = q.astype(BF16), k.astype(BF16)
        qws[d, h], kws[d, h] = (q * qwt[d, h]).astype(BF16), (k * kwt[d, h]).astype(BF16)
        vs[d, h] = v_ref[:, sl]
    scores = {dh_: _dot_nt(qs[dh_], ks[dh_]) for dh_ in work}
    for d, h in work:
        lhs = jnp.concatenate([(scores[d, h] * dmat[d, h]).astype(BF16), qws[d, h]], axis=1)
        rhs = jnp.concatenate([vs[d, h], state[d, h].astype(BF16)], axis=0)
        refs[d][5][:, h * dh:(h + 1) * dh] = _dot(lhs, rhs).astype(BF16)
    for d, h in work:
        cdec = sc_ref[2 * RET_HEADS + d * RET_HEADS + h]
        state[d, h] = state[d, h] * cdec + _dot_tn(kws[d, h], vs[d, h])


def _retention(zret3, decay_logit):
    bsz, seq, _ = zret3.shape
    ch = RET_CHUNK
    nc = seq // ch
    half = RET_HEAD_DIM // 2
    inv_freq = ROPE_BASE ** (-jnp.arange(half, dtype=F32) / half)
    ang = jnp.arange(seq, dtype=F32)[:, None] * inv_freq[None, :]
    cosf = jnp.concatenate([jnp.cos(ang), jnp.cos(ang)], axis=1)
    sinf = jnp.concatenate([-jnp.sin(ang), jnp.sin(ang)], axis=1)
    log_g = jax.nn.log_sigmoid(decay_logit.astype(F32)).reshape(-1)
    scal = jnp.concatenate([log_g, jnp.exp(log_g * ch)])
    w = RET_WIDTH
    fwd = lambda col: pl.BlockSpec((None, ch, w), lambda b, n, sc: (b, n, col))
    bwd = lambda col: pl.BlockSpec((None, ch, w), lambda b, n, sc: (b, nc - 1 - n, col))
    rope_f = pl.BlockSpec((ch, RET_HEAD_DIM), lambda b, n, sc: (n, 0))
    rope_b = pl.BlockSpec((ch, RET_HEAD_DIM), lambda b, n, sc: (nc - 1 - n, 0))
    grid_spec = pltpu.PrefetchScalarGridSpec(
        num_scalar_prefetch=1,
        grid=(bsz, nc),
        in_specs=[fwd(0), fwd(1), fwd(2), rope_f, rope_f, bwd(0), bwd(1), bwd(2), rope_b, rope_b],
        out_specs=[pl.BlockSpec((None, ch, w), lambda b, n, sc: (b, n, 0)),
                   pl.BlockSpec((None, ch, w), lambda b, n, sc: (b, nc - 1 - n, 0))],
        scratch_shapes=[pltpu.VMEM((2, RET_HEADS, RET_HEAD_DIM, RET_HEAD_DIM), F32),
                        pltpu.VMEM((2, RET_HEADS, ch, ch), F32),
                        pltpu.VMEM((2, RET_HEADS, ch, RET_HEAD_DIM), F32),
                        pltpu.VMEM((2, RET_HEADS, ch, RET_HEAD_DIM), F32)],
    )
    return pl.pallas_call(
        _retention_body,
        grid_spec=grid_spec,
        out_shape=[jax.ShapeDtypeStruct((bsz, seq, w), BF16), jax.ShapeDtypeStruct((bsz, seq, w), BF16)],
        compiler_params=_cparams(("arbitrary", "arbitrary")),
        name="retention",
    )(scal, zret3, zret3, zret3, cosf, sinf, zret3, zret3, zret3, cosf, sinf)


def _outproj_body(x_ref, yhy_ref, of_ref, ob_ref, gr_ref, ghy_ref, gret_ref, whyo_ref, wreto_ref, wo_ref,
                  n2g_ref, wrt_ref, x1_ref, pt_ref):
    dh = RET_HEAD_DIM
    o = of_ref[...].astype(F32) + ob_ref[...].astype(F32)
    parts = []
    for h in range(RET_HEADS):
        oh = o[:, h * dh:(h + 1) * dh]
        parts.append(oh * lax.rsqrt(jnp.mean(oh * oh, axis=-1, keepdims=True) + EPS))
    on = jnp.concatenate(parts, axis=1)
    gr = gr_ref[...].astype(F32)
    ret = (gr * jax.nn.sigmoid(gr)) * on
    y_ret = _dot(ret.astype(BF16), wreto_ref[...])
    y_hy = _dot(yhy_ref[...], whyo_ref[...])
    merged = jax.nn.sigmoid(ghy_ref[...].astype(F32)) * y_hy + jax.nn.sigmoid(gret_ref[...].astype(F32)) * y_ret
    x1 = x_ref[...] + _dot(merged.astype(BF16), wo_ref[...])
    x1_ref[...] = x1
    h2 = x1 * lax.rsqrt(jnp.mean(x1 * x1, axis=-1, keepdims=True) + EPS) * n2g_ref[...]
    logits = _dot_nt(wrt_ref[...], h2.astype(BF16))
    m = jnp.max(logits, axis=0, keepdims=True)
    e = jnp.exp(logits - m)
    pt_ref[...] = e / jnp.sum(e, axis=0, keepdims=True)


def _outproj(x2, yhy, o_f, o_b, zret, zg, whyo, wreto, wo, n2g, wrt, tm=512):
    n_tok = x2.shape[0]
    d = D_MODEL
    row = lambda w, col=0: pl.BlockSpec((tm, w), lambda i: (i, col))
    const = lambda shape: pl.BlockSpec(shape, lambda i: (0, 0))
    return pl.pallas_call(
        _outproj_body,
        grid=(n_tok // tm,),
        in_specs=[row(d), row(HY_WIDTH), row(RET_WIDTH), row(RET_WIDTH), row(RET_WIDTH, 3), row(d, 0), row(d, 1),
                  const((HY_WIDTH, d)), const((RET_WIDTH, d)), const((d, d)), const((1, d)), const((N_EXPERTS, d))],
        out_specs=[row(d), pl.BlockSpec((N_EXPERTS, tm), lambda i: (0, i))],
        out_shape=[jax.ShapeDtypeStruct((n_tok, d), F32), jax.ShapeDtypeStruct((N_EXPERTS, n_tok), F32)],
        compiler_params=_cparams(("parallel",)),
        name="outproj_router",
    )(x2, yhy, o_f, o_b, zret, zg, zg, whyo, wreto, wo, n2g, wrt)


def _select_body(p_ref, upper_ref, lower_ref, lowinc_ref, eye_ref, pos_ref, idx_ref, gate_ref, lo_ref, *, cap):
    rows = p_ref.shape[0]
    p = p_ref[...]
    bits = lax.bitcast_convert_type(p, I32)

    def count(mask):
        return jnp.sum(jnp.sum(mask.astype(F32), axis=1, keepdims=True), axis=0, keepdims=True)

    def bit_step(i, thr):
        cand = thr | jnp.left_shift(jnp.int32(1), 30 - i)
        return jnp.where(count(bits >= cand) >= cap, cand, thr)

    thr = lax.fori_loop(0, 31, bit_step, jnp.zeros((1, 1), I32))
    gt = bits > thr
    eq = bits == thr
    need = cap - count(gt)

    def prefix(mask_f):
        incl = _dot(mask_f.astype(BF16), upper_ref[...])
        tot = jnp.broadcast_to(incl[:, LANES - 1:LANES], incl.shape)
        base = _dot(lower_ref[...], tot.astype(BF16))
        return incl, base, tot

    eq_f = eq.astype(F32)
    incl_e, base_e, _ = prefix(eq_f)
    sel = gt | (eq & (base_e + incl_e - eq_f < need))
    sel_f = sel.astype(F32)
    incl, base, tot = prefix(sel_f)
    pos_ref[...] = jnp.where(sel, (base + incl - 1.0).astype(I32), -1)
    lo_ref[...] = base.astype(I32)

    rowend = base + tot
    incl_t = _dot_nt(lowinc_ref[...], sel_f.astype(BF16)).astype(BF16)
    p_t = []
    rem = p
    for _ in range(3):
        part = rem.astype(BF16)
        rem = rem - part.astype(F32)
        p_t.append(_dot_nt(eye_ref[...], part).astype(BF16))
    table = jnp.concatenate([incl_t] + p_t, axis=0)
    groups = 2 if (cap // LANES) % 2 == 0 else 1
    sw = groups * LANES
    wide = lambda a: jnp.concatenate([a] * groups, axis=1)
    rowend_w, tot_w = wide(rowend), wide(tot)
    r_iota = lax.broadcasted_iota(I32, (rows, sw), 0).astype(F32)
    lane_iota = lax.broadcasted_iota(I32, (LANES, sw), 0).astype(F32)

    def slot_tile(ts, carry):
        s = (ts * sw + lax.broadcasted_iota(I32, (1, sw), 1)).astype(F32)
        done = rowend_w <= s
        row = jnp.sum(done.astype(F32), axis=0, keepdims=True)
        before = jnp.sum(jnp.where(done, tot_w, 0.0), axis=0, keepdims=True)
        onehot_t = (r_iota == row).astype(BF16)
        got = _dot(table, onehot_t)
        g_t = got[:LANES]
        lane = jnp.sum((g_t <= s - before).astype(F32), axis=0, keepdims=True)
        tok = (row * LANES + lane).astype(I32)
        p_row = got[LANES:2 * LANES] + got[2 * LANES:3 * LANES] + got[3 * LANES:]
        gate = jnp.sum(jnp.where(lane_iota == lane, p_row, 0.0), axis=0, keepdims=True)
        for k in range(groups):
            idx_ref[pl.ds(ts * groups + k, 1), :] = tok[:, k * LANES:(k + 1) * LANES]
            gate_ref[pl.ds(ts * groups + k, 1), :] = gate[:, k * LANES:(k + 1) * LANES]
        return carry

    lax.fori_loop(0, cap // sw, slot_tile, 0)


def _select(pt3, cap):
    n_e, rows, _ = pt3.shape
    ii = np.arange(LANES)
    upper = jnp.asarray(ii[:, None] <= ii[None, :], BF16)
    lowinc = jnp.asarray(ii[None, :] <= ii[:, None], BF16)
    eye = jnp.asarray(ii[None, :] == ii[:, None], BF16)
    rr = np.arange(rows)
    lower = jnp.asarray(rr[None, :] < rr[:, None], BF16)
    const = lambda shape: pl.BlockSpec(shape, lambda e: (0, 0))
    tok_spec = pl.BlockSpec((None, rows, LANES), lambda e: (e, 0, 0))
    slot_spec = pl.BlockSpec((None, cap // LANES, LANES), lambda e: (e, 0, 0))
    return pl.pallas_call(
        functools.partial(_select_body, cap=cap),
        grid=(n_e,),
        in_specs=[tok_spec, const((LANES, LANES)), const((rows, rows)), const((LANES, LANES)), const((LANES, LANES))],
        out_specs=[tok_spec, slot_spec, slot_spec, tok_spec],
        out_shape=[jax.ShapeDtypeStruct((n_e, rows, LANES), I32),
                   jax.ShapeDtypeStruct((n_e, cap // LANES, LANES), I32),
                   jax.ShapeDtypeStruct((n_e, cap // LANES, LANES), F32),
                   jax.ShapeDtypeStruct((n_e, rows, LANES), I32)],
        compiler_params=_cparams(("parallel",)),
        name="expert_select",
    )(pt3, upper, lower, lowinc, eye)


def _ffn_body(idx_ref, h_hbm, n2g_ref, gate_ref, wg_ref, wu_ref, wd_ref, ye_ref, xbuf, xb, sems, *,
              tiles_per_expert):
    s = xbuf.shape[1]
    step = pl.program_id(0) * tiles_per_expert + pl.program_id(1)
    n_steps = pl.num_programs(0) * tiles_per_expert
    cur = step % 2

    def row_copy(st, i, buf):
        tok = idx_ref[st * s + i]
        return pltpu.make_async_copy(h_hbm.at[pl.ds(tok, 1), :], xbuf.at[buf, pl.ds(i, 1), :], sems.at[buf])

    def wait_rows(buf):
        pltpu.make_async_copy(h_hbm.at[pl.ds(0, s), :], xbuf.at[buf], sems.at[buf]).wait()

    @pl.when(step == 0)
    def _():
        def one(i, carry):
            row_copy(0, i, 0).start()
            return carry
        lax.fori_loop(0, s, one, 0, unroll=8)

    @pl.when(step + 1 < n_steps)
    def _():
        for i in range(s):
            row_copy(step + 1, i, 1 - cur).start()

    wait_rows(cur)
    x = xbuf[cur]
    xb[...] = (x * lax.rsqrt(jnp.mean(x * x, axis=-1, keepdims=True) + EPS) * n2g_ref[...]).astype(BF16)
    y = None
    for f0 in range(0, EXPERT_FF, FF_CHUNK):
        f1 = min(f0 + FF_CHUNK, EXPERT_FF)
        a = _dot(xb[...], wg_ref[:, f0:f1])
        b = _dot(xb[...], wu_ref[:, f0:f1])
        hid = ((a * jax.nn.sigmoid(a)) * b).astype(BF16)
        part = _dot(hid, wd_ref[f0:f1, :])
        y = part if y is None else y + part
    for k in range(s // LANES):
        col = jnp.transpose(jnp.broadcast_to(gate_ref[k:k + 1, :], (LANES, LANES)))[:, 0:1]
        ye_ref[k * LANES:(k + 1) * LANES, :] = (y[k * LANES:(k + 1) * LANES] * col).astype(ye_ref.dtype)


def _expert_ffn(idx_flat, x1, n2g, gates, wg, wu, wd, cap, s=512):
    s = min(s, cap)
    tiles = cap // s
    g3 = gates.reshape(N_EXPERTS * tiles, s // LANES, LANES)
    grid_spec = pltpu.PrefetchScalarGridSpec(
        num_scalar_prefetch=1,
        grid=(N_EXPERTS, tiles),
        in_specs=[pl.BlockSpec(memory_space=pl.ANY),
                  pl.BlockSpec((1, D_MODEL), lambda e, j, idx: (0, 0)),
                  pl.BlockSpec((None, s // LANES, LANES), lambda e, j, idx: (e * tiles + j, 0, 0)),
                  pl.BlockSpec((None, D_MODEL, EXPERT_FF), lambda e, j, idx: (e, 0, 0)),
                  pl.BlockSpec((None, D_MODEL, EXPERT_FF), lambda e, j, idx: (e, 0, 0)),
                  pl.BlockSpec((None, EXPERT_FF, D_MODEL), lambda e, j, idx: (e, 0, 0))],
        out_specs=pl.BlockSpec((s, D_MODEL), lambda e, j, idx: (e * tiles + j, 0)),
        scratch_shapes=[pltpu.VMEM((2, s, D_MODEL), F32), pltpu.VMEM((s, D_MODEL), BF16),
                        pltpu.SemaphoreType.DMA((2,))],
    )
    return pl.pallas_call(
        functools.partial(_ffn_body, tiles_per_expert=tiles),
        grid_spec=grid_spec,
        out_shape=jax.ShapeDtypeStruct((N_EXPERTS * cap, D_MODEL), BF16),
        compiler_params=_cparams(("arbitrary", "arbitrary")),
        name="expert_ffn",
    )(idx_flat, x1, n2g, g3, wg, wu, wd)


def _combine_body(lo_ref, np_ref, x1_ref, pos_ref, nfg_ref, expand_ref, ye_hbm, o_ref, win, sems, *, cap, rows_total):
    r = pl.program_id(0)
    n_rows = pl.num_programs(0)
    w = COMBINE_WIN
    cur = r % 2

    def starts(rr, m):
        out = []
        for e in range(N_EXPERTS):
            first = e * cap + lo_ref[e * n_rows + rr]
            intended = (first // COMBINE_ALIGN) * COMBINE_ALIGN + m * w
            actual = pl.multiple_of(jnp.minimum(intended, rows_total - w), COMBINE_ALIGN)
            out.append((intended, actual))
        return out

    def copies(rr, m, buf):
        return [pltpu.make_async_copy(ye_hbm.at[pl.ds(actual, w), :], win.at[buf, pl.ds(e * w, w), :], sems.at[buf, e])
                for e, (_, actual) in enumerate(starts(rr, m))]

    def contribution(rr, m, buf):
        lane = lax.broadcasted_iota(I32, (1, N_EXPERTS), 1)
        intended = jnp.zeros((1, N_EXPERTS), I32)
        actual = jnp.zeros((1, N_EXPERTS), I32)
        for e, (i_s, a_s) in enumerate(starts(rr, m)):
            intended = jnp.where(lane == e, i_s, intended)
            actual = jnp.where(lane == e, a_s, actual)
        pos = pos_ref[...]
        glob = pos + lane * cap
        rel = glob - intended
        valid = (pos >= 0) & (rel >= 0) & (rel < w)
        local = jnp.where(valid, glob - actual, -1).astype(F32).astype(BF16)
        spread = _dot(local, expand_ref[...])
        col = (lax.broadcasted_iota(I32, (1, N_EXPERTS * w), 1) & (w - 1)).astype(F32)
        onehot = (spread == col).astype(BF16)
        return _dot(onehot, win[buf])

    @pl.when(r == 0)
    def _():
        for cp in copies(0, 0, 0):
            cp.start()

    @pl.when(r + 1 < n_rows)
    def _():
        for cp in copies(r + 1, 0, 1 - cur):
            cp.start()

    for cp in copies(r, 0, cur):
        cp.wait()
    acc = x1_ref[...] + contribution(r, 0, cur)

    def extra_pass(m, acc):
        for cp in copies(r, m, cur):
            cp.start()
        for cp in copies(r, m, cur):
            cp.wait()
        return acc + contribution(r, m, cur)

    acc = lax.fori_loop(1, np_ref[r], extra_pass, acc)
    o_ref[...] = acc * lax.rsqrt(jnp.mean(acc * acc, axis=-1, keepdims=True) + EPS) * nfg_ref[...]


def _combine(lo, x1, pos_t, nfg, ye, cap):
    n_tok = x1.shape[0]
    tm = LANES
    w = COMBINE_WIN
    rows_total = ye.shape[0]
    n_rows = n_tok // tm
    nxt = jnp.concatenate([lo[:, 1:], jnp.full((N_EXPERTS, 1), cap, I32)], axis=1)
    span = lo % COMBINE_ALIGN + (nxt - lo)
    n_pass = jnp.maximum(jnp.max((span + w - 1) // w, axis=0), 1).astype(I32)
    ee = np.arange(N_EXPERTS)
    expand = jnp.asarray(ee[:, None] == (np.arange(N_EXPERTS * w) // w)[None, :], BF16)
    grid_spec = pltpu.PrefetchScalarGridSpec(
        num_scalar_prefetch=2,
        grid=(n_rows,),
        in_specs=[pl.BlockSpec((tm, D_MODEL), lambda i, lo_, np_: (i, 0)),
                  pl.BlockSpec((tm, N_EXPERTS), lambda i, lo_, np_: (i, 0)),
                  pl.BlockSpec((1, D_MODEL), lambda i, lo_, np_: (0, 0)),
                  pl.BlockSpec((N_EXPERTS, N_EXPERTS * w), lambda i, lo_, np_: (0, 0)),
                  pl.BlockSpec(memory_space=pl.ANY)],
        out_specs=pl.BlockSpec((tm, D_MODEL), lambda i, lo_, np_: (i, 0)),
        scratch_shapes=[pltpu.VMEM((2, N_EXPERTS * w, D_MODEL), BF16),
                        pltpu.SemaphoreType.DMA((2, N_EXPERTS))],
    )
    return pl.pallas_call(
        functools.partial(_combine_body, cap=cap, rows_total=rows_total),
        grid_spec=grid_spec,
        out_shape=jax.ShapeDtypeStruct((n_tok, D_MODEL), F32),
        compiler_params=_cparams(("arbitrary",)),
        name="moe_combine_norm",
    )(lo.reshape(-1), n_pass, x1, pos_t, nfg, expand, ye)


def _trunk(x, w):
    bsz, seq, d = x.shape
    n_tok = bsz * seq
    x2 = x.reshape(n_tok, d)
    tabs = _fft_tables(seq)

    zhy, zret, zg = _inproj(x2, w["norm1_g"], w["w_in"])
    zc = _shortconv(zhy.reshape(bsz, seq, -1), w["hy_conv_w"], w["hy_conv_b"])

    h_time, h_abs = _hyena_filter_time(seq, w["hy_w1"], w["hy_b1"], w["hy_freq1"], w["hy_w2"], w["hy_b2"],
                                       w["hy_freq2"], w["hy_w3"])
    kr, ki = _stage_b_filter(_stage_a_real(h_time, tabs), h_abs, tabs)
    yhy = _hyena(zc, kr, ki, w["hy_skip"], tabs)

    o_f, o_b = _retention(zret.reshape(bsz, seq, -1), w["ret_decay_logit"])
    x1, pt = _outproj(x2, yhy, o_f.reshape(n_tok, -1), o_b.reshape(n_tok, -1), zret, zg,
                      w["w_hy_out"], w["w_ret_out"], w["w_o"], w["norm2_g"], w["w_router_t"])

    cap = CAPACITY_FACTOR * n_tok // N_EXPERTS
    rows = n_tok // LANES
    pos, idx, gates, lo = _select(pt.reshape(N_EXPERTS, rows, LANES), cap)
    ye = _expert_ffn(idx.reshape(-1), x1, w["norm2_g"], gates, w["w_gate"], w["w_up"], w["w_down"], cap)
    pos_t = pos.reshape(N_EXPERTS, n_tok).T
    y = _combine(lo[:, :, 0], x1, pos_t, w["norm_f_g"], ye, cap)
    return y.reshape(bsz, seq, d)


def kernel(x_prompt, x_sample, norm1_g, w_in, hy_conv_w, hy_conv_b, hy_w1, hy_b1, hy_freq1, hy_w2, hy_b2, hy_freq2,
           hy_w3, hy_skip, ret_decay_logit, w_hy_out, w_ret_out, w_o, norm2_g, w_router, w_gate, w_up, w_down,
           norm_f_g):
    layer = 0
    w = dict(
        norm1_g=norm1_g[layer].astype(F32)[None], w_in=w_in[layer].astype(BF16),
        hy_conv_w=hy_conv_w[layer].astype(F32), hy_conv_b=hy_conv_b[layer].astype(F32)[None],
        hy_w1=hy_w1[layer], hy_b1=hy_b1[layer], hy_freq1=hy_freq1[layer], hy_w2=hy_w2[layer], hy_b2=hy_b2[layer],
        hy_freq2=hy_freq2[layer], hy_w3=hy_w3[layer], hy_skip=hy_skip[layer],
        ret_decay_logit=ret_decay_logit[layer],
        w_hy_out=w_hy_out[layer].astype(BF16), w_ret_out=w_ret_out[layer].astype(BF16), w_o=w_o[layer].astype(BF16),
        norm2_g=norm2_g[layer].astype(F32)[None], w_router_t=w_router[layer].T.astype(BF16),
        w_gate=w_gate[layer].astype(BF16), w_up=w_up[layer].astype(BF16), w_down=w_down[layer].astype(BF16),
        norm_f_g=norm_f_g.astype(F32)[None],
    )
    return _trunk(x_prompt, w), _trunk(x_sample, w)
```

```python
import functools
import math

import numpy as np
import jax
import jax.numpy as jnp
from jax import lax
from jax.experimental import pallas as pl
from jax.experimental.pallas import tpu as pltpu

F32 = jnp.float32
BF16 = jnp.bfloat16
I32 = jnp.int32

D_MODEL = 1024
HY_WIDTH = 512
HY_ORDER = 2
HY_BANDS = 16
HY_FILTER_WIDTH = 64
HY_FAST_DECAY = 0.3
HY_SLOW_DECAY = 1.5
HY_DECAY_TARGET = 1e-2
RET_WIDTH = 512
RET_HEADS = 4
RET_HEAD_DIM = 128
RET_CHUNK = 128
ROPE_BASE = 10000.0
N_EXPERTS = 16
EXPERT_FF = 1408
CAPACITY_FACTOR = 2
EPS = 1e-6

LANES = 128
SUBLANES = 8
EMB_PAD = 128
DFT_N1_MAX = 128
FF_CHUNK = 256
COMBINE_ALIGN = SUBLANES
COMBINE_WIN = 32


def _cparams(sem, vmem_mb=48):
    return pltpu.CompilerParams(dimension_semantics=sem, vmem_limit_bytes=vmem_mb * 1024 * 1024)


def _dot(a, b):
    return jnp.dot(a, b, preferred_element_type=F32)


def _dot_nt(a, b):
    return lax.dot_general(a, b, (((1,), (1,)), ((), ())), preferred_element_type=F32)


def _dot_tn(a, b):
    return lax.dot_general(a, b, (((0,), (0,)), ((), ())), preferred_element_type=F32)


def _split(a):
    hi = a.astype(BF16)
    lo = (a - hi.astype(F32)).astype(BF16)
    return hi, lo


def _dot_const(m_hi, m_lo, x, passes):
    xh = x.astype(BF16)
    r = _dot(m_hi, xh)
    if passes >= 3:
        xl = (x - xh.astype(F32)).astype(BF16)
        r = r + _dot(m_lo, xh) + _dot(m_hi, xl)
    return r


def _dot3(a, b):
    ah, al = _split(a)
    bh, bl = _split(b)
    return _dot(ah, bh) + _dot(al, bh) + _dot(ah, bl)


def _inproj_body(x_ref, g_ref, w_ref, cw_ref, cb_ref, zc_ref, zret_ref, zg_ref, z_prev, row_prev, *, tiles_per_seq):
    i = pl.program_id(0)
    tm = x_ref.shape[0]
    n_hy = zc_ref.shape[1]
    n_ret = zret_ref.shape[1]

    @pl.when(i == 0)
    def _():
        z_prev[...] = jnp.zeros_like(z_prev)
        row_prev[...] = jnp.zeros_like(row_prev)

    x = x_ref[...]
    ms = jnp.mean(x * x, axis=-1, keepdims=True)
    h = (x * lax.rsqrt(ms + EPS) * g_ref[...]).astype(BF16)
    z_hy = _dot(h, w_ref[:, :n_hy])
    zret_ref[...] = _dot(h, w_ref[:, n_hy:n_hy + n_ret]).astype(BF16)
    zg_ref[...] = _dot(h, w_ref[:, n_hy + n_ret:]).astype(BF16)

    t_prev = (i - 1) % tiles_per_seq
    zp = z_prev[...]
    before = jnp.where(t_prev == 0, 0.0, row_prev[0:1, :])
    after = jnp.where(t_prev == tiles_per_seq - 1, 0.0, z_hy[0:1, :])
    row = lax.broadcasted_iota(I32, (tm, 1), 0)
    zm1 = jnp.where(row == 0, before, pltpu.roll(zp, 1, axis=0))
    zp1 = jnp.where(row == tm - 1, after, pltpu.roll(zp, tm - 1, axis=0))
    zc_ref[...] = zm1 * cw_ref[0:1, :] + zp * cw_ref[1:2, :] + zp1 * cw_ref[2:3, :] + cb_ref[...]
    row_prev[...] = jnp.broadcast_to(zp[tm - 1:tm, :], row_prev.shape)
    z_prev[...] = z_hy


def _inproj(x2, g, w_bf, conv_w, conv_b, seq, tm=256):
    n_tok = x2.shape[0]
    n_hy, n_ret, n_g = 3 * HY_WIDTH, 4 * RET_WIDTH, 2 * D_MODEL
    tm = min(tm, seq)
    n_tiles = n_tok // tm
    cur = lambda i: (jnp.minimum(i, n_tiles - 1), 0)
    const = lambda i: (0, 0)
    return pl.pallas_call(
        functools.partial(_inproj_body, tiles_per_seq=seq // tm),
        grid=(n_tiles + 1,),
        in_specs=[pl.BlockSpec((tm, D_MODEL), cur),
                  pl.BlockSpec((1, D_MODEL), const),
                  pl.BlockSpec((D_MODEL, n_hy + n_ret + n_g), const),
                  pl.BlockSpec((3, n_hy), const),
                  pl.BlockSpec((1, n_hy), const)],
        out_specs=[pl.BlockSpec((tm, n_hy), lambda i: (jnp.maximum(i - 1, 0), 0)),
                   pl.BlockSpec((tm, n_ret), cur),
                   pl.BlockSpec((tm, n_g), cur)],
        out_shape=[jax.ShapeDtypeStruct((n_tok, n_hy), F32),
                   jax.ShapeDtypeStruct((n_tok, n_ret), BF16),
                   jax.ShapeDtypeStruct((n_tok, n_g), BF16)],
        scratch_shapes=[pltpu.VMEM((tm, n_hy), F32), pltpu.VMEM((8, n_hy), F32)],
        compiler_params=_cparams(("arbitrary",)),
        name="inproj_shortconv",
    )(x2, g, w_bf, conv_w, conv_b)


def _fft_dims(seq):
    n = 2 * seq
    n1 = min(DFT_N1_MAX, 1 << (n.bit_length() // 2))
    n2 = n // n1
    assert n1 * n2 == n and n1 % 2 == 0
    return n, n1, n2


def _hi_lo_const(m):
    m = np.asarray(m, np.float64)
    hi = jnp.asarray(m, F32).astype(BF16)
    lo = (jnp.asarray(m, F32) - hi.astype(F32)).astype(BF16)
    return hi, lo


def _fft_tables(seq):
    n, n1, n2 = _fft_dims(seq)
    r1 = n1 // 2
    k1 = jnp.arange(n1, dtype=I32)
    j = jnp.arange(n2, dtype=I32)

    at = ((j[:, None] * k1[None, :]) % n).astype(F32) * (2.0 * math.pi / n)
    ct, st = jnp.cos(at), jnp.sin(at)

    def cos_sin(n1_count, k1_first):
        a1 = ((k1[:, None] * jnp.arange(n1_count, dtype=I32)[None, :]) % n1).astype(F32) * (2.0 * math.pi / n1)
        cf, sf = jnp.cos(a1), jnp.sin(a1)
        if k1_first:
            cf, sf, ctb, stb = cf[None], sf[None], ct[:, :, None], st[:, :, None]
        else:
            cf, sf, ctb, stb = cf.T[None], sf.T[None], ct[:, None, :], st[:, None, :]
        return cf * ctb - sf * stb, sf * ctb + cf * stb

    cat = jnp.concatenate
    c, s = cos_sin(r1, True)
    mat_a = cat([cat([c, s], axis=2), cat([-s, c], axis=2)], axis=1).astype(BF16)
    c, s = cos_sin(n1, True)
    mat_a_real = cat([c, -s], axis=1).astype(BF16)
    c, s = cos_sin(r1, False)
    mat_c = cat([cat([c, -s], axis=2), cat([s, c], axis=2)], axis=1).astype(BF16)
    a2 = 2.0 * np.pi * np.outer(np.arange(n2), np.arange(n2)) / n2
    gr, gi = np.cos(a2), -np.sin(a2)
    mat_f = np.block([[gr, -gi], [gi, gr]])
    mat_i = np.block([[gr, gi], [-gi, gr]])
    return dict(n=n, n1=n1, n2=n2, mat_a=mat_a, mat_a_real=mat_a_real, mat_c=mat_c,
                mat_f=_hi_lo_const(mat_f), mat_i=_hi_lo_const(mat_i))


def _filter_body(z_ref, w1_ref, b1_ref, f1_ref, w2_ref, b2_ref, f2_ref, w3_ref, dl_ref, h_ref, s_ref, *, seq):
    i = pl.program_id(0)
    tl = z_ref.shape[0]
    z = z_ref[...]
    a = jnp.sin(f1_ref[...] * (_dot3(z, w1_ref[...]) + b1_ref[...]))
    a = jnp.sin(f2_ref[...] * (_dot3(a, w2_ref[...]) + b2_ref[...]))
    h = _dot3(a, w3_ref[...])
    win = jnp.exp(-z[:, 0:1] * dl_ref[...])
    h = h * jnp.concatenate([win] * HY_ORDER, axis=1)
    row = i * tl + lax.broadcasted_iota(I32, (tl, 1), 0)
    h = jnp.where(row == seq, 0.0, h)
    h_ref[...] = h

    @pl.when(i == 0)
    def _():
        s_ref[...] = jnp.zeros_like(s_ref)

    s_ref[...] += jnp.broadcast_to(jnp.sum(jnp.abs(h), axis=0, keepdims=True), s_ref.shape)


def _hyena_filter_time(seq, w1, b1, f1, w2, b2, f2, w3, tl=512):
    n = 2 * seq
    tl = min(tl, seq)
    idx = jnp.arange(n, dtype=I32)
    p = jnp.minimum(jnp.where(idx < seq, idx, n - idx), seq - 1).astype(F32)
    t = p / float(seq - 1)
    ang = 2.0 * math.pi * p / seq
    bands = jnp.linspace(1e-4, HY_BANDS - 1, HY_BANDS, dtype=F32)
    phase = ang[:, None] * bands[None, :]
    emb = jnp.concatenate([t[:, None], jnp.cos(phase), -jnp.sin(phase)], axis=-1)
    emb = jnp.pad(emb, ((0, 0), (0, EMB_PAD - emb.shape[1])))
    w1p = jnp.pad(w1.astype(F32), ((0, EMB_PAD - w1.shape[0]), (0, 0)))
    w3d = w3.astype(F32).reshape(HY_FILTER_WIDTH, HY_ORDER, 2, HY_WIDTH).transpose(2, 0, 1, 3)
    w3d = w3d.reshape(2, HY_FILTER_WIDTH, HY_ORDER * HY_WIDTH)
    deltas = jnp.abs(jnp.linspace(math.log(HY_DECAY_TARGET) / HY_SLOW_DECAY,
                                  math.log(HY_DECAY_TARGET) / HY_FAST_DECAY, HY_WIDTH, dtype=F32))[None, :]
    fw = HY_FILTER_WIDTH
    nblk_half = seq // tl
    const = lambda i: (0, 0)
    return pl.pallas_call(
        functools.partial(_filter_body, seq=seq),
        grid=(n // tl,),
        in_specs=[pl.BlockSpec((tl, EMB_PAD), lambda i: (i, 0)),
                  pl.BlockSpec((EMB_PAD, fw), const), pl.BlockSpec((1, fw), const), pl.BlockSpec((1, fw), const),
                  pl.BlockSpec((fw, fw), const), pl.BlockSpec((1, fw), const), pl.BlockSpec((1, fw), const),
                  pl.BlockSpec((None, fw, HY_ORDER * HY_WIDTH), lambda i: (i // nblk_half, 0, 0)),
                  pl.BlockSpec((1, HY_WIDTH), const)],
        out_specs=[pl.BlockSpec((tl, HY_ORDER * HY_WIDTH), lambda i: (i, 0)),
                   pl.BlockSpec((8, HY_ORDER * HY_WIDTH), const)],
        out_shape=[jax.ShapeDtypeStruct((n, HY_ORDER * HY_WIDTH), F32),
                   jax.ShapeDtypeStruct((8, HY_ORDER * HY_WIDTH), F32)],
        compiler_params=_cparams(("arbitrary",)),
        name="hyena_filter",
    )(emb, w1p, b1.astype(F32)[None], f1.astype(F32)[None], w2.astype(F32), b2.astype(F32)[None],
      f2.astype(F32)[None], w3d, deltas)


def _stage_a_real_body(h_ref, m_ref, a_ref):
    n1 = h_ref.shape[0]
    h_t = jnp.swapaxes(h_ref[...], 0, 1)
    rs = [_dot(m_ref[j], h_t[j].astype(BF16)) for j in range(h_ref.shape[1])]
    r = jnp.swapaxes(jnp.stack(rs, axis=0), 0, 1)
    a_ref[0] = r[:n1]
    a_ref[1] = r[n1:]


def _stage_a_real(h, tabs, n2c=8, cw=512):
    n1, n2 = tabs["n1"], tabs["n2"]
    c = h.shape[1]
    h3 = h.reshape(n1, n2, c)
    mat = tabs["mat_a_real"]
    return pl.pallas_call(
        _stage_a_real_body,
        grid=(n2 // n2c, c // cw),
        in_specs=[pl.BlockSpec((n1, n2c, cw), lambda j, k: (0, j, k)),
                  pl.BlockSpec((n2c,) + mat.shape[1:], lambda j, k: (j, 0, 0))],
        out_specs=pl.BlockSpec((2, n1, n2c, cw), lambda j, k: (0, 0, j, k)),
        out_shape=jax.ShapeDtypeStruct((2, n1, n2, c), F32),
        compiler_params=_cparams(("parallel", "parallel")),
        name="dft_a_filter",
    )(h3, mat)


def _pack_pair(re, im):
    hi = lax.bitcast_convert_type(re.astype(BF16).astype(F32), I32)
    lo = lax.bitcast_convert_type(im.astype(BF16).astype(F32), I32)
    return lax.bitcast_convert_type(hi | lax.shift_right_logical(lo, 16), F32)


def _unpack_pair(packed):
    word = lax.bitcast_convert_type(packed, I32)
    re = lax.bitcast_convert_type(word & jnp.int32(-65536), F32)
    im = lax.bitcast_convert_type(lax.shift_left(word, 16), F32)
    return re, im


def _stage_a_body(u_ref, m_ref, a_ref):
    _, r1, n2c, cw = u_ref.shape
    n1 = 2 * r1
    u_t = jnp.swapaxes(u_ref[...].reshape(n1, n2c, cw), 0, 1)
    packed = []
    for j in range(n2c):
        r = _dot(m_ref[j], u_t[j].astype(BF16))
        packed.append(_pack_pair(r[:n1], r[n1:]))
    a_ref[...] = jnp.swapaxes(jnp.stack(packed, axis=0), 0, 1)


def _stage_a(u5, col, tabs, n2c=8):
    n1, n2 = tabs["n1"], tabs["n2"]
    npair, _, r1, _, _ = u5.shape
    cw = HY_WIDTH
    mat = tabs["mat_a"]
    return pl.pallas_call(
        _stage_a_body,
        grid=(npair, n2 // n2c),
        in_specs=[pl.BlockSpec((None, 2, r1, n2c, cw), lambda p, j: (p, 0, 0, j, col)),
                  pl.BlockSpec((n2c,) + mat.shape[1:], lambda p, j: (j, 0, 0))],
        out_specs=pl.BlockSpec((None, n1, n2c, cw), lambda p, j: (p, 0, j, 0)),
        out_shape=jax.ShapeDtypeStruct((npair, n1, n2, cw), F32),
        compiler_params=_cparams(("parallel", "parallel")),
        name="dft_a",
    )(u5, mat)


def _stage_b_filter_body(a_ref, g_ref, s_ref, kr_ref, ki_ref, *, n):
    n2 = a_ref.shape[2]
    scale = 1.0 / (s_ref[0:1, :] * float(n))
    for kk in range(a_ref.shape[1]):
        d = _dot(g_ref[...], jnp.concatenate([a_ref[0, kk], a_ref[1, kk]], axis=0).astype(BF16))
        kr_ref[kk] = d[:n2] * scale
        ki_ref[kk] = d[n2:] * scale


def _stage_b_filter(a, s, tabs):
    n, n1, n2 = tabs["n"], tabs["n1"], tabs["n2"]
    c = a.shape[-1]
    kb = max(1, min(n1, 512 // n2))
    g = tabs["mat_f"][0]
    return pl.pallas_call(
        functools.partial(_stage_b_filter_body, n=n),
        grid=(n1 // kb,),
        in_specs=[pl.BlockSpec((2, kb, n2, c), lambda k: (0, k, 0, 0)),
                  pl.BlockSpec(g.shape, lambda k: (0, 0)),
                  pl.BlockSpec(s.shape, lambda k: (0, 0))],
        out_specs=[pl.BlockSpec((kb, n2, c), lambda k: (k, 0, 0)),
                   pl.BlockSpec((kb, n2, c), lambda k: (k, 0, 0))],
        out_shape=[jax.ShapeDtypeStruct((n1, n2, c), F32), jax.ShapeDtypeStruct((n1, n2, c), F32)],
        compiler_params=_cparams(("parallel",)),
        name="dft_b_filter",
    )(a, g, s)


def _stage_b_body(a_ref, kr_ref, ki_ref, gf_ref, gi_ref, c_ref):
    n2 = a_ref.shape[1]
    for kk in range(a_ref.shape[0]):
        d = _dot(gf_ref[...], jnp.concatenate(_unpack_pair(a_ref[kk]), axis=0).astype(BF16))
        dr, di = d[:n2], d[n2:]
        kr, ki = kr_ref[kk], ki_ref[kk]
        yr = dr * kr - di * ki
        yi = dr * ki + di * kr
        e = _dot(gi_ref[...], jnp.concatenate([yr, yi], axis=0).astype(BF16))
        c_ref[kk] = _pack_pair(e[:n2], e[n2:])


def _stage_b(a, kr, ki, order, tabs):
    n1, n2 = tabs["n1"], tabs["n2"]
    npair = a.shape[0]
    cw = HY_WIDTH
    kb = max(1, min(n1, 1024 // n2))
    gf = tabs["mat_f"][0]
    gi = tabs["mat_i"][0]
    const = lambda k, p: (0, 0)
    return pl.pallas_call(
        _stage_b_body,
        grid=(n1 // kb, npair),
        in_specs=[pl.BlockSpec((None, kb, n2, cw), lambda k, p: (p, k, 0, 0)),
                  pl.BlockSpec((kb, n2, cw), lambda k, p: (k, 0, order)),
                  pl.BlockSpec((kb, n2, cw), lambda k, p: (k, 0, order)),
                  pl.BlockSpec(gf.shape, const), pl.BlockSpec(gi.shape, const)],
        out_specs=pl.BlockSpec((None, kb, n2, cw), lambda k, p: (p, k, 0, 0)),
        out_shape=jax.ShapeDtypeStruct(a.shape, F32),
        compiler_params=_cparams(("parallel", "parallel")),
        name="dft_b",
    )(a, kr, ki, gf, gi)


def _stage_c_body(c_ref, u_ref, g_ref, skip_ref, mc_ref, ma_ref, y_ref, *rest, fuse_a):
    _, r1, n2c, cw = u_ref.shape
    n1 = 2 * r1
    ys = []
    c_t = jnp.swapaxes(c_ref[...], 0, 1)
    for j in range(n2c):
        cc = jnp.concatenate(_unpack_pair(c_t[j]), axis=0)
        ys.append(_dot(mc_ref[j], cc.astype(BF16)))
    y = jnp.swapaxes(jnp.stack(ys, axis=0), 0, 1)
    u = u_ref[...].reshape(n1, n2c, cw)
    g = g_ref[...].reshape(n1, n2c, cw)
    yo = g * (y + u * skip_ref[...].reshape(1, 1, cw))
    y_ref[...] = yo.reshape(2, r1, n2c, cw).astype(y_ref.dtype)
    if fuse_a:
        a_ref = rest[0]
        yo_t = jnp.swapaxes(yo, 0, 1)
        packed = []
        for j in range(n2c):
            r = _dot(ma_ref[j], yo_t[j].astype(BF16))
            packed.append(_pack_pair(r[:n1], r[n1:]))
        a_ref[...] = jnp.swapaxes(jnp.stack(packed, axis=0), 0, 1)


def _stage_c(c, u5, u_col, g5, g_col, skip, tabs, fuse_a, out_dtype, n2c=8):
    n1, n2 = tabs["n1"], tabs["n2"]
    npair, _, r1, _, _ = u5.shape
    cw = HY_WIDTH
    mc = tabs["mat_c"]
    ma = tabs["mat_a"]
    const = lambda p, j: (0, 0)
    per_j = lambda m: pl.BlockSpec((n2c,) + m.shape[1:], lambda p, j: (j, 0, 0))
    out_shape = [jax.ShapeDtypeStruct((npair, 2, r1, n2, cw), out_dtype)]
    out_specs = [pl.BlockSpec((None, 2, r1, n2c, cw), lambda p, j: (p, 0, 0, j, 0))]
    if fuse_a:
        out_shape.append(jax.ShapeDtypeStruct((npair, n1, n2, cw), F32))
        out_specs.append(pl.BlockSpec((None, n1, n2c, cw), lambda p, j: (p, 0, j, 0)))
    return pl.pallas_call(
        functools.partial(_stage_c_body, fuse_a=fuse_a),
        grid=(npair, n2 // n2c),
        in_specs=[pl.BlockSpec((None, n1, n2c, cw), lambda p, j: (p, 0, j, 0)),
                  pl.BlockSpec((None, 2, r1, n2c, cw), lambda p, j: (p, 0, 0, j, u_col)),
                  pl.BlockSpec((None, 2, r1, n2c, cw), lambda p, j: (p, 0, 0, j, g_col)),
                  pl.BlockSpec((1, cw), const), per_j(mc), per_j(ma)],
        out_specs=out_specs,
        out_shape=out_shape,
        compiler_params=_cparams(("parallel", "parallel")),
        name="dft_c",
    )(c, u5, g5, skip, mc, ma)


def _hyena(zc3, kr, ki, skip, tabs):
    bsz, seq, _ = zc3.shape
    n1, n2 = tabs["n1"], tabs["n2"]
    r1 = n1 // 2
    z5 = zc3.reshape(bsz // 2, 2, r1, n2, 3 * HY_WIDTH)
    a = _stage_a(z5, 2, tabs)
    c = _stage_b(a, kr, ki, 0, tabs)
    y1, a2 = _stage_c(c, z5, 2, z5, 0, skip[0:1].astype(F32), tabs, True, F32)
    c2 = _stage_b(a2, kr, ki, 1, tabs)
    (y2,) = _stage_c(c2, y1, 0, z5, 1, skip[1:2].astype(F32), tabs, False, BF16)
    return y2.reshape(bsz * seq, HY_WIDTH)


def _retention_body(sc_ref, qf_ref, kf_ref, vf_ref, cf_ref, sf_ref, qb_ref, kb_ref, vb_ref, cb_ref, sb_ref,
                    of_ref, ob_ref, state, dmat, qwt, kwt):
    ch = RET_CHUNK
    dh = RET_HEAD_DIM

    @pl.when(pl.program_id(1) == 0)
    def _():
        state[...] = jnp.zeros_like(state)
        ci = lax.broadcasted_iota(I32, (ch, ch), 0).astype(F32)
        mi = lax.broadcasted_iota(I32, (ch, ch), 1).astype(F32)
        lag = ci - mi
        for d in range(2):
            for h in range(RET_HEADS):
                lg = sc_ref[d * RET_HEADS + h]
                if d == 0:
                    dmat[d, h] = jnp.where(lag >= 0, jnp.exp(lg * jnp.maximum(lag, 0.0)), 0.0)
                    qwt[d, h] = jnp.exp(lg * (ci + 1.0))
                    kwt[d, h] = jnp.exp(lg * (ch - 1.0 - ci))
                else:
                    dmat[d, h] = jnp.where(lag < 0, jnp.exp(lg * jnp.maximum(-lag, 0.0)), 0.0)
                    qwt[d, h] = jnp.exp(lg * (ch - ci))
                    kwt[d, h] = jnp.exp(lg * ci)

    scale = RET_HEAD_DIM ** -0.5
    refs = ((qf_ref, kf_ref, vf_ref, cf_ref, sf_ref, of_ref), (qb_ref, kb_ref, vb_ref, cb_ref, sb_ref, ob_ref))
    work = [(d, h) for d in range(2) for h in range(RET_HEADS)]
    qs, ks, qws, kws, vs = {}, {}, {}, {}, {}
    for d, h in work:
        q_ref, k_ref, v_ref, c_ref, s_ref, _ = refs[d]
        sl = slice(h * dh, (h + 1) * dh)
        cosf = c_ref[...]
        sinf = s_ref[...]
        q = q_ref[:, sl].astype(F32)
        k = k_ref[:, sl].astype(F32)
        q = q * cosf + pltpu.roll(q, dh // 2, axis=1) * sinf
        k = (k * cosf + pltpu.roll(k, dh // 2, axis=1) * sinf) * scale
        qs[d, h], ks[d, h] = q.astype(BF16), k.astype(BF16)
        qws[d, h], kws[d, h] = (q * qwt[d, h]).astype(BF16), (k * kwt[d, h]).astype(BF16)
        vs[d, h] = v_ref[:, sl]
    scores = {dh_: _dot_nt(qs[dh_], ks[dh_]) for dh_ in work}
    for d, h in work:
        lhs = jnp.concatenate([(scores[d, h] * dmat[d, h]).astype(BF16), qws[d, h]], axis=1)
        rhs = jnp.concatenate([vs[d, h], state[d, h].astype(BF16)], axis=0)
        refs[d][5][:, h * dh:(h + 1) * dh] = _dot(lhs, rhs).astype(BF16)
    for d, h in work:
        cdec = sc_ref[2 * RET_HEADS + d * RET_HEADS + h]
        state[d, h] = state[d, h] * cdec + _dot_tn(kws[d, h], vs[d, h])


def _retention(zret3, decay_logit):
    bsz, seq, _ = zret3.shape
    ch = RET_CHUNK
    nc = seq // ch
    half = RET_HEAD_DIM // 2
    inv_freq = ROPE_BASE ** (-jnp.arange(half, dtype=F32) / half)
    ang = jnp.arange(seq, dtype=F32)[:, None] * inv_freq[None, :]
    cosf = jnp.concatenate([jnp.cos(ang), jnp.cos(ang)], axis=1)
    sinf = jnp.concatenate([-jnp.sin(ang), jnp.sin(ang)], axis=1)
    log_g = jax.nn.log_sigmoid(decay_logit.astype(F32)).reshape(-1)
    scal = jnp.concatenate([log_g, jnp.exp(log_g * ch)])
    w = RET_WIDTH
    fwd = lambda col: pl.BlockSpec((None, ch, w), lambda b, n, sc: (b, n, col))
    bwd = lambda col: pl.BlockSpec((None, ch, w), lambda b, n, sc: (b, nc - 1 - n, col))
    rope_f = pl.BlockSpec((ch, RET_HEAD_DIM), lambda b, n, sc: (n, 0))
    rope_b = pl.BlockSpec((ch, RET_HEAD_DIM), lambda b, n, sc: (nc - 1 - n, 0))
    grid_spec = pltpu.PrefetchScalarGridSpec(
        num_scalar_prefetch=1,
        grid=(bsz, nc),
        in_specs=[fwd(0), fwd(1), fwd(2), rope_f, rope_f, bwd(0), bwd(1), bwd(2), rope_b, rope_b],
        out_specs=[pl.BlockSpec((None, ch, w), lambda b, n, sc: (b, n, 0)),
                   pl.BlockSpec((None, ch, w), lambda b, n, sc: (b, nc - 1 - n, 0))],
        scratch_shapes=[pltpu.VMEM((2, RET_HEADS, RET_HEAD_DIM, RET_HEAD_DIM), F32),
                        pltpu.VMEM((2, RET_HEADS, ch, ch), F32),
                        pltpu.VMEM((2, RET_HEADS, ch, RET_HEAD_DIM), F32),
                        pltpu.VMEM((2, RET_HEADS, ch, RET_HEAD_DIM), F32)],
    )
    return pl.pallas_call(
        _retention_body,
        grid_spec=grid_spec,
        out_shape=[jax.ShapeDtypeStruct((bsz, seq, w), BF16), jax.ShapeDtypeStruct((bsz, seq, w), BF16)],
        compiler_params=_cparams(("arbitrary", "arbitrary")),
        name="retention",
    )(scal, zret3, zret3, zret3, cosf, sinf, zret3, zret3, zret3, cosf, sinf)


def _outproj_body(x_ref, yhy_ref, of_ref, ob_ref, gr_ref, ghy_ref, gret_ref, whyo_ref, wreto_ref, wo_ref,
                  n2g_ref, wrt_ref, x1_ref, pt_ref):
    dh = RET_HEAD_DIM
    o = of_ref[...].astype(F32) + ob_ref[...].astype(F32)
    parts = []
    for h in range(RET_HEADS):
        oh = o[:, h * dh:(h + 1) * dh]
        parts.append(oh * lax.rsqrt(jnp.mean(oh * oh, axis=-1, keepdims=True) + EPS))
    on = jnp.concatenate(parts, axis=1)
    gr = gr_ref[...].astype(F32)
    ret = (gr * jax.nn.sigmoid(gr)) * on
    y_ret = _dot(ret.astype(BF16), wreto_ref[...])
    y_hy = _dot(yhy_ref[...], whyo_ref[...])
    merged = jax.nn.sigmoid(ghy_ref[...].astype(F32)) * y_hy + jax.nn.sigmoid(gret_ref[...].astype(F32)) * y_ret
    x1 = x_ref[...] + _dot(merged.astype(BF16), wo_ref[...])
    x1_ref[...] = x1
    h2 = x1 * lax.rsqrt(jnp.mean(x1 * x1, axis=-1, keepdims=True) + EPS) * n2g_ref[...]
    logits = _dot_nt(wrt_ref[...], h2.astype(BF16))
    m = jnp.max(logits, axis=0, keepdims=True)
    e = jnp.exp(logits - m)
    pt_ref[...] = e / jnp.sum(e, axis=0, keepdims=True)


def _outproj(x2, yhy, o_f, o_b, zret, zg, whyo, wreto, wo, n2g, wrt, tm=512):
    n_tok = x2.shape[0]
    d = D_MODEL
    row = lambda w, col=0: pl.BlockSpec((tm, w), lambda i: (i, col))
    const = lambda shape: pl.BlockSpec(shape, lambda i: (0, 0))
    return pl.pallas_call(
        _outproj_body,
        grid=(n_tok // tm,),
        in_specs=[row(d), row(HY_WIDTH), row(RET_WIDTH), row(RET_WIDTH), row(RET_WIDTH, 3), row(d, 0), row(d, 1),
                  const((HY_WIDTH, d)), const((RET_WIDTH, d)), const((d, d)), const((1, d)), const((N_EXPERTS, d))],
        out_specs=[row(d), pl.BlockSpec((N_EXPERTS, tm), lambda i: (0, i))],
        out_shape=[jax.ShapeDtypeStruct((n_tok, d), F32), jax.ShapeDtypeStruct((N_EXPERTS, n_tok), F32)],
        compiler_params=_cparams(("parallel",)),
        name="outproj_router",
    )(x2, yhy, o_f, o_b, zret, zg, zg, whyo, wreto, wo, n2g, wrt)


def _select_body(p_ref, upper_ref, lower_ref, lowinc_ref, eye_ref, pos_ref, idx_ref, gate_ref, lo_ref, *, cap):
    rows = p_ref.shape[0]
    p = p_ref[...]
    bits = lax.bitcast_convert_type(p, I32)

    def count(mask):
        return jnp.sum(jnp.sum(mask.astype(F32), axis=1, keepdims=True), axis=0, keepdims=True)

    def bit_step(i, thr):
        cand = thr | jnp.left_shift(jnp.int32(1), 30 - i)
        return jnp.where(count(bits >= cand) >= cap, cand, thr)

    thr = lax.fori_loop(0, 31, bit_step, jnp.zeros((1, 1), I32))
    gt = bits > thr
    eq = bits == thr
    need = cap - count(gt)

    def prefix(mask_f):
        incl = _dot(mask_f.astype(BF16), upper_ref[...])
        tot = jnp.broadcast_to(incl[:, LANES - 1:LANES], incl.shape)
        base = _dot(lower_ref[...], tot.astype(BF16))
        return incl, base, tot

    eq_f = eq.astype(F32)
    incl_e, base_e, _ = prefix(eq_f)
    sel = gt | (eq & (base_e + incl_e - eq_f < need))
    sel_f = sel.astype(F32)
    incl, base, tot = prefix(sel_f)
    pos_ref[...] = jnp.where(sel, (base + incl - 1.0).astype(I32), -1)
    lo_ref[...] = base.astype(I32)

    rowend = base + tot
    incl_t = _dot_nt(lowinc_ref[...], sel_f.astype(BF16)).astype(BF16)
    p_t = []
    rem = p
    for _ in range(3):
        part = rem.astype(BF16)
        rem = rem - part.astype(F32)
        p_t.append(_dot_nt(eye_ref[...], part).astype(BF16))
    table = jnp.concatenate([incl_t] + p_t, axis=0)
    groups = 2 if (cap // LANES) % 2 == 0 else 1
    sw = groups * LANES
    wide = lambda a: jnp.concatenate([a] * groups, axis=1)
    rowend_w, tot_w = wide(rowend), wide(tot)
    r_iota = lax.broadcasted_iota(I32, (rows, sw), 0).astype(F32)
    lane_iota = lax.broadcasted_iota(I32, (LANES, sw), 0).astype(F32)

    def slot_tile(ts, carry):
        s = (ts * sw + lax.broadcasted_iota(I32, (1, sw), 1)).astype(F32)
        done = rowend_w <= s
        row = jnp.sum(done.astype(F32), axis=0, keepdims=True)
        before = jnp.sum(jnp.where(done, tot_w, 0.0), axis=0, keepdims=True)
        onehot_t = (r_iota == row).astype(BF16)
        got = _dot(table, onehot_t)
        g_t = got[:LANES]
        lane = jnp.sum((g_t <= s - before).astype(F32), axis=0, keepdims=True)
        tok = (row * LANES + lane).astype(I32)
        p_row = got[LANES:2 * LANES] + got[2 * LANES:3 * LANES] + got[3 * LANES:]
        gate = jnp.sum(jnp.where(lane_iota == lane, p_row, 0.0), axis=0, keepdims=True)
        for k in range(groups):
            idx_ref[pl.ds(ts * groups + k, 1), :] = tok[:, k * LANES:(k + 1) * LANES]
            gate_ref[pl.ds(ts * groups + k, 1), :] = gate[:, k * LANES:(k + 1) * LANES]
        return carry

    lax.fori_loop(0, cap // sw, slot_tile, 0)


def _select(pt3, cap):
    n_e, rows, _ = pt3.shape
    ii = np.arange(LANES)
    upper = jnp.asarray(ii[:, None] <= ii[None, :], BF16)
    lowinc = jnp.asarray(ii[None, :] <= ii[:, None], BF16)
    eye = jnp.asarray(ii[None, :] == ii[:, None], BF16)
    rr = np.arange(rows)
    lower = jnp.asarray(rr[None, :] < rr[:, None], BF16)
    const = lambda shape: pl.BlockSpec(shape, lambda e: (0, 0))
    tok_spec = pl.BlockSpec((None, rows, LANES), lambda e: (e, 0, 0))
    slot_spec = pl.BlockSpec((None, cap // LANES, LANES), lambda e: (e, 0, 0))
    return pl.pallas_call(
        functools.partial(_select_body, cap=cap),
        grid=(n_e,),
        in_specs=[tok_spec, const((LANES, LANES)), const((rows, rows)), const((LANES, LANES)), const((LANES, LANES))],
        out_specs=[tok_spec, slot_spec, slot_spec, tok_spec],
        out_shape=[jax.ShapeDtypeStruct((n_e, rows, LANES), I32),
                   jax.ShapeDtypeStruct((n_e, cap // LANES, LANES), I32),
                   jax.ShapeDtypeStruct((n_e, cap // LANES, LANES), F32),
                   jax.ShapeDtypeStruct((n_e, rows, LANES), I32)],
        compiler_params=_cparams(("parallel",)),
        name="expert_select",
    )(pt3, upper, lower, lowinc, eye)


def _ffn_body(idx_ref, h_hbm, n2g_ref, gate_ref, wg_ref, wu_ref, wd_ref, ye_ref, xbuf, xb, sems, *,
              tiles_per_expert):
    s = xbuf.shape[1]
    step = pl.program_id(0) * tiles_per_expert + pl.program_id(1)
    n_steps = pl.num_programs(0) * tiles_per_expert
    cur = step % 2

    def row_copy(st, i, buf):
        tok = idx_ref[st * s + i]
        return pltpu.make_async_copy(h_hbm.at[pl.ds(tok, 1), :], xbuf.at[buf, pl.ds(i, 1), :], sems.at[buf])

    def wait_rows(buf):
        pltpu.make_async_copy(h_hbm.at[pl.ds(0, s), :], xbuf.at[buf], sems.at[buf]).wait()

    @pl.when(step == 0)
    def _():
        def one(i, carry):
            row_copy(0, i, 0).start()
            return carry
        lax.fori_loop(0, s, one, 0, unroll=8)

    @pl.when(step + 1 < n_steps)
    def _():
        for i in range(s):
            row_copy(step + 1, i, 1 - cur).start()

    wait_rows(cur)
    x = xbuf[cur]
    xb[...] = (x * lax.rsqrt(jnp.mean(x * x, axis=-1, keepdims=True) + EPS) * n2g_ref[...]).astype(BF16)
    y = None
    for f0 in range(0, EXPERT_FF, FF_CHUNK):
        f1 = min(f0 + FF_CHUNK, EXPERT_FF)
        a = _dot(xb[...], wg_ref[:, f0:f1])
        b = _dot(xb[...], wu_ref[:, f0:f1])
        hid = ((a * jax.nn.sigmoid(a)) * b).astype(BF16)
        part = _dot(hid, wd_ref[f0:f1, :])
        y = part if y is None else y + part
    for k in range(s // LANES):
        col = jnp.transpose(jnp.broadcast_to(gate_ref[k:k + 1, :], (LANES, LANES)))[:, 0:1]
        yk = y[k * LANES:(k + 1) * LANES] * col
        ye_ref[k * LANES:(k + 1) * LANES, :] = _pack_pair(yk[:, :D_MODEL // 2], yk[:, D_MODEL // 2:])


def _expert_ffn(idx_flat, x1, n2g, gates, wg, wu, wd, cap, s=512):
    s = min(s, cap)
    tiles = cap // s
    g3 = gates.reshape(N_EXPERTS * tiles, s // LANES, LANES)
    grid_spec = pltpu.PrefetchScalarGridSpec(
        num_scalar_prefetch=1,
        grid=(N_EXPERTS, tiles),
        in_specs=[pl.BlockSpec(memory_space=pl.ANY),
                  pl.BlockSpec((1, D_MODEL), lambda e, j, idx: (0, 0)),
                  pl.BlockSpec((None, s // LANES, LANES), lambda e, j, idx: (e * tiles + j, 0, 0)),
                  pl.BlockSpec((None, D_MODEL, EXPERT_FF), lambda e, j, idx: (e, 0, 0)),
                  pl.BlockSpec((None, D_MODEL, EXPERT_FF), lambda e, j, idx: (e, 0, 0)),
                  pl.BlockSpec((None, EXPERT_FF, D_MODEL), lambda e, j, idx: (e, 0, 0))],
        out_specs=pl.BlockSpec((s, D_MODEL // 2), lambda e, j, idx: (e * tiles + j, 0)),
        scratch_shapes=[pltpu.VMEM((2, s, D_MODEL), F32), pltpu.VMEM((s, D_MODEL), BF16),
                        pltpu.SemaphoreType.DMA((2,))],
    )
    return pl.pallas_call(
        functools.partial(_ffn_body, tiles_per_expert=tiles),
        grid_spec=grid_spec,
        out_shape=jax.ShapeDtypeStruct((N_EXPERTS * cap, D_MODEL // 2), F32),
        compiler_params=_cparams(("arbitrary", "arbitrary")),
        name="expert_ffn",
    )(idx_flat, x1, n2g, g3, wg, wu, wd)


def _combine_body(lo_ref, np_ref, x1_ref, pos_ref, nfg_ref, expand_ref, ye_hbm, o_ref, win, sems, *, cap, rows_total):
    r = pl.program_id(0)
    n_rows = pl.num_programs(0)
    w = COMBINE_WIN
    cur = r % 2

    def starts(rr, m):
        out = []
        for e in range(N_EXPERTS):
            first = e * cap + lo_ref[e * n_rows + rr]
            intended = (first // COMBINE_ALIGN) * COMBINE_ALIGN + m * w
            actual = pl.multiple_of(jnp.minimum(intended, rows_total - w), COMBINE_ALIGN)
            out.append((intended, actual))
        return out

    def copies(rr, m, buf):
        return [pltpu.make_async_copy(ye_hbm.at[pl.ds(actual, w), :], win.at[buf, pl.ds(e * w, w), :], sems.at[buf, e])
                for e, (_, actual) in enumerate(starts(rr, m))]

    def contribution(rr, m, buf):
        lane = lax.broadcasted_iota(I32, (1, N_EXPERTS), 1)
        intended = jnp.zeros((1, N_EXPERTS), I32)
        actual = jnp.zeros((1, N_EXPERTS), I32)
        for e, (i_s, a_s) in enumerate(starts(rr, m)):
            intended = jnp.where(lane == e, i_s, intended)
            actual = jnp.where(lane == e, a_s, actual)
        pos = pos_ref[...]
        glob = pos + lane * cap
        rel = glob - intended
        valid = (pos >= 0) & (rel >= 0) & (rel < w)
        local = jnp.where(valid, glob - actual, -1).astype(F32).astype(BF16)
        spread = _dot(local, expand_ref[...])
        col = (lax.broadcasted_iota(I32, (1, N_EXPERTS * w), 1) & (w - 1)).astype(F32)
        onehot = (spread == col).astype(BF16)
        lo_cols, hi_cols = _unpack_pair(win[buf])
        return jnp.concatenate([_dot(onehot, lo_cols.astype(BF16)), _dot(onehot, hi_cols.astype(BF16))], axis=1)

    @pl.when(r == 0)
    def _():
        for cp in copies(0, 0, 0):
            cp.start()

    @pl.when(r + 1 < n_rows)
    def _():
        for cp in copies(r + 1, 0, 1 - cur):
            cp.start()

    for cp in copies(r, 0, cur):
        cp.wait()
    acc = x1_ref[...] + contribution(r, 0, cur)

    def extra_pass(m, acc):
        for cp in copies(r, m, cur):
            cp.start()
        for cp in copies(r, m, cur):
            cp.wait()
        return acc + contribution(r, m, cur)

    acc = lax.fori_loop(1, np_ref[r], extra_pass, acc)
    o_ref[...] = acc * lax.rsqrt(jnp.mean(acc * acc, axis=-1, keepdims=True) + EPS) * nfg_ref[...]


def _combine(lo, x1, pos_t, nfg, ye, cap):
    n_tok = x1.shape[0]
    tm = LANES
    w = COMBINE_WIN
    rows_total = ye.shape[0]
    n_rows = n_tok // tm
    nxt = jnp.concatenate([lo[:, 1:], jnp.full((N_EXPERTS, 1), cap, I32)], axis=1)
    span = lo % COMBINE_ALIGN + (nxt - lo)
    n_pass = jnp.maximum(jnp.max((span + w - 1) // w, axis=0), 1).astype(I32)
    ee = np.arange(N_EXPERTS)
    expand = jnp.asarray(ee[:, None] == (np.arange(N_EXPERTS * w) // w)[None, :], BF16)
    grid_spec = pltpu.PrefetchScalarGridSpec(
        num_scalar_prefetch=2,
        grid=(n_rows,),
        in_specs=[pl.BlockSpec((tm, D_MODEL), lambda i, lo_, np_: (i, 0)),
                  pl.BlockSpec((tm, N_EXPERTS), lambda i, lo_, np_: (i, 0)),
                  pl.BlockSpec((1, D_MODEL), lambda i, lo_, np_: (0, 0)),
                  pl.BlockSpec((N_EXPERTS, N_EXPERTS * w), lambda i, lo_, np_: (0, 0)),
                  pl.BlockSpec(memory_space=pl.ANY)],
        out_specs=pl.BlockSpec((tm, D_MODEL), lambda i, lo_, np_: (i, 0)),
        scratch_shapes=[pltpu.VMEM((2, N_EXPERTS * w, D_MODEL // 2), F32),
                        pltpu.SemaphoreType.DMA((2, N_EXPERTS))],
    )
    return pl.pallas_call(
        functools.partial(_combine_body, cap=cap, rows_total=rows_total),
        grid_spec=grid_spec,
        out_shape=jax.ShapeDtypeStruct((n_tok, D_MODEL), F32),
        compiler_params=_cparams(("arbitrary",)),
        name="moe_combine_norm",
    )(lo.reshape(-1), n_pass, x1, pos_t, nfg, expand, ye)


def _trunk(x, w):
    bsz, seq, d = x.shape
    n_tok = bsz * seq
    x2 = x.reshape(n_tok, d)
    tabs = _fft_tables(seq)

    zc, zret, zg = _inproj(x2, w["norm1_g"], w["w_in"], w["hy_conv_w"], w["hy_conv_b"], seq)
    zc = zc.reshape(bsz, seq, -1)

    h_time, h_abs = _hyena_filter_time(seq, w["hy_w1"], w["hy_b1"], w["hy_freq1"], w["hy_w2"], w["hy_b2"],
                                       w["hy_freq2"], w["hy_w3"])
    kr, ki = _stage_b_filter(_stage_a_real(h_time, tabs), h_abs, tabs)
    yhy = _hyena(zc, kr, ki, w["hy_skip"], tabs)

    o_f, o_b = _retention(zret.reshape(bsz, seq, -1), w["ret_decay_logit"])
    x1, pt = _outproj(x2, yhy, o_f.reshape(n_tok, -1), o_b.reshape(n_tok, -1), zret, zg,
                      w["w_hy_out"], w["w_ret_out"], w["w_o"], w["norm2_g"], w["w_router_t"])

    cap = CAPACITY_FACTOR * n_tok // N_EXPERTS
    rows = n_tok // LANES
    pos, idx, gates, lo = _select(pt.reshape(N_EXPERTS, rows, LANES), cap)
    ye = _expert_ffn(idx.reshape(-1), x1, w["norm2_g"], gates, w["w_gate"], w["w_up"], w["w_down"], cap)
    pos_t = pos.reshape(N_EXPERTS, n_tok).T
    y = _combine(lo[:, :, 0], x1, pos_t, w["norm_f_g"], ye, cap)
    return y.reshape(bsz, seq, d)


def kernel(x_prompt, x_sample, norm1_g, w_in, hy_conv_w, hy_conv_b, hy_w1, hy_b1, hy_freq1, hy_w2, hy_b2, hy_freq2,
           hy_w3, hy_skip, ret_decay_logit, w_hy_out, w_ret_out, w_o, norm2_g, w_router, w_gate, w_up, w_down,
           norm_f_g):
    layer = 0
    w = dict(
        norm1_g=norm1_g[layer].astype(F32)[None], w_in=w_in[layer].astype(BF16),
        hy_conv_w=hy_conv_w[layer].astype(F32), hy_conv_b=hy_conv_b[layer].astype(F32)[None],
        hy_w1=hy_w1[layer], hy_b1=hy_b1[layer], hy_freq1=hy_freq1[layer], hy_w2=hy_w2[layer], hy_b2=hy_b2[layer],
        hy_freq2=hy_freq2[layer], hy_w3=hy_w3[layer], hy_skip=hy_skip[layer],
        ret_decay_logit=ret_decay_logit[layer],
        w_hy_out=w_hy_out[layer].astype(BF16), w_ret_out=w_ret_out[layer].astype(BF16), w_o=w_o[layer].astype(BF16),
        norm2_g=norm2_g[layer].astype(F32)[None], w_router_t=w_router[layer].T.astype(BF16),
        w_gate=w_gate[layer].astype(BF16), w_up=w_up[layer].astype(BF16), w_down=w_down[layer].astype(BF16),
        norm_f_g=norm_f_g.astype(F32)[None],
    )
    return _trunk(x_prompt, w), _trunk(x_sample, w)
```

```python
import functools
import math

import numpy as np
import jax
import jax.numpy as jnp
from jax import lax
from jax.experimental import pallas as pl
from jax.experimental.pallas import tpu as pltpu

F32 = jnp.float32
BF16 = jnp.bfloat16
I32 = jnp.int32

D_MODEL = 1024
HY_WIDTH = 512
HY_ORDER = 2
HY_BANDS = 16
HY_FILTER_WIDTH = 64
HY_FAST_DECAY = 0.3
HY_SLOW_DECAY = 1.5
HY_DECAY_TARGET = 1e-2
RET_WIDTH = 512
RET_HEADS = 4
RET_HEAD_DIM = 128
RET_CHUNK = 128
ROPE_BASE = 10000.0
N_EXPERTS = 16
EXPERT_FF = 1408
CAPACITY_FACTOR = 2
EPS = 1e-6

LANES = 128
SUBLANES = 8
EMB_PAD = 128
DFT_N1_MAX = 128
FF_CHUNK = 256
COMBINE_ALIGN = SUBLANES
COMBINE_WIN = 32


def _cparams(sem, vmem_mb=48):
    return pltpu.CompilerParams(dimension_semantics=sem, vmem_limit_bytes=vmem_mb * 1024 * 1024)


def _dot(a, b):
    return jnp.dot(a, b, preferred_element_type=F32)


def _dot_nt(a, b):
    return lax.dot_general(a, b, (((1,), (1,)), ((), ())), preferred_element_type=F32)


def _dot_tn(a, b):
    return lax.dot_general(a, b, (((0,), (0,)), ((), ())), preferred_element_type=F32)


def _split(a):
    hi = a.astype(BF16)
    lo = (a - hi.astype(F32)).astype(BF16)
    return hi, lo


def _dot_const(m_hi, m_lo, x, passes):
    xh = x.astype(BF16)
    r = _dot(m_hi, xh)
    if passes >= 3:
        xl = (x - xh.astype(F32)).astype(BF16)
        r = r + _dot(m_lo, xh) + _dot(m_hi, xl)
    return r


def _dot3(a, b):
    ah, al = _split(a)
    bh, bl = _split(b)
    return _dot(ah, bh) + _dot(al, bh) + _dot(ah, bl)


def _inproj_body(x_ref, g_ref, w_ref, cw_ref, cb_ref, zc_ref, zret_ref, zg_ref, z_prev, row_prev, *, tiles_per_seq):
    i = pl.program_id(0)
    tm = x_ref.shape[0]
    n_hy = zc_ref.shape[1]
    n_ret = zret_ref.shape[1]

    @pl.when(i == 0)
    def _():
        z_prev[...] = jnp.zeros_like(z_prev)
        row_prev[...] = jnp.zeros_like(row_prev)

    x = x_ref[...]
    ms = jnp.mean(x * x, axis=-1, keepdims=True)
    h = (x * lax.rsqrt(ms + EPS) * g_ref[...]).astype(BF16)
    z_hy = _dot(h, w_ref[:, :n_hy])
    zret_ref[...] = _dot(h, w_ref[:, n_hy:n_hy + n_ret]).astype(BF16)
    zg_ref[...] = _dot(h, w_ref[:, n_hy + n_ret:]).astype(BF16)

    t_prev = (i - 1) % tiles_per_seq
    zp = z_prev[...]
    before = jnp.where(t_prev == 0, 0.0, row_prev[0:1, :])
    after = jnp.where(t_prev == tiles_per_seq - 1, 0.0, z_hy[0:1, :])
    row = lax.broadcasted_iota(I32, (tm, 1), 0)
    zm1 = jnp.where(row == 0, before, pltpu.roll(zp, 1, axis=0))
    zp1 = jnp.where(row == tm - 1, after, pltpu.roll(zp, tm - 1, axis=0))
    zc_ref[...] = zm1 * cw_ref[0:1, :] + zp * cw_ref[1:2, :] + zp1 * cw_ref[2:3, :] + cb_ref[...]
    row_prev[...] = jnp.broadcast_to(zp[tm - 1:tm, :], row_prev.shape)
    z_prev[...] = z_hy


def _inproj(x2, g, w_bf, conv_w, conv_b, seq, tm=256):
    n_tok = x2.shape[0]
    n_hy, n_ret, n_g = 3 * HY_WIDTH, 4 * RET_WIDTH, 2 * D_MODEL
    tm = min(tm, seq)
    n_tiles = n_tok // tm
    cur = lambda i: (jnp.minimum(i, n_tiles - 1), 0)
    const = lambda i: (0, 0)
    return pl.pallas_call(
        functools.partial(_inproj_body, tiles_per_seq=seq // tm),
        grid=(n_tiles + 1,),
        in_specs=[pl.BlockSpec((tm, D_MODEL), cur),
                  pl.BlockSpec((1, D_MODEL), const),
                  pl.BlockSpec((D_MODEL, n_hy + n_ret + n_g), const),
                  pl.BlockSpec((3, n_hy), const),
                  pl.BlockSpec((1, n_hy), const)],
        out_specs=[pl.BlockSpec((tm, n_hy), lambda i: (jnp.maximum(i - 1, 0), 0)),
                   pl.BlockSpec((tm, n_ret), cur),
                   pl.BlockSpec((tm, n_g), cur)],
        out_shape=[jax.ShapeDtypeStruct((n_tok, n_hy), F32),
                   jax.ShapeDtypeStruct((n_tok, n_ret), BF16),
                   jax.ShapeDtypeStruct((n_tok, n_g), BF16)],
        scratch_shapes=[pltpu.VMEM((tm, n_hy), F32), pltpu.VMEM((8, n_hy), F32)],
        compiler_params=_cparams(("arbitrary",)),
        name="inproj_shortconv",
    )(x2, g, w_bf, conv_w, conv_b)


def _fft_dims(seq):
    n = 2 * seq
    n1 = min(DFT_N1_MAX, 1 << (n.bit_length() // 2))
    n2 = n // n1
    assert n1 * n2 == n and n1 % 2 == 0
    return n, n1, n2


def _hi_lo_const(m):
    m = np.asarray(m, np.float64)
    hi = jnp.asarray(m, F32).astype(BF16)
    lo = (jnp.asarray(m, F32) - hi.astype(F32)).astype(BF16)
    return hi, lo


def _fft_tables(seq):
    n, n1, n2 = _fft_dims(seq)
    r1 = n1 // 2
    k1 = jnp.arange(n1, dtype=I32)
    j = jnp.arange(n2, dtype=I32)

    at = ((j[:, None] * k1[None, :]) % n).astype(F32) * (2.0 * math.pi / n)
    ct, st = jnp.cos(at), jnp.sin(at)

    def cos_sin(n1_count, k1_first):
        a1 = ((k1[:, None] * jnp.arange(n1_count, dtype=I32)[None, :]) % n1).astype(F32) * (2.0 * math.pi / n1)
        cf, sf = jnp.cos(a1), jnp.sin(a1)
        if k1_first:
            cf, sf, ctb, stb = cf[None], sf[None], ct[:, :, None], st[:, :, None]
        else:
            cf, sf, ctb, stb = cf.T[None], sf.T[None], ct[:, None, :], st[:, None, :]
        return cf * ctb - sf * stb, sf * ctb + cf * stb

    cat = jnp.concatenate
    c, s = cos_sin(r1, True)
    mat_a = cat([cat([c, s], axis=2), cat([-s, c], axis=2)], axis=1).astype(BF16)
    c, s = cos_sin(n1, True)
    mat_a_real = cat([c, -s], axis=1).astype(BF16)
    c, s = cos_sin(r1, False)
    mat_c = cat([cat([c, -s], axis=2), cat([s, c], axis=2)], axis=1).astype(BF16)
    a2 = 2.0 * np.pi * np.outer(np.arange(n2), np.arange(n2)) / n2
    gr, gi = np.cos(a2), -np.sin(a2)
    mat_f = np.block([[gr, -gi], [gi, gr]])
    mat_i = np.block([[gr, gi], [-gi, gr]])
    return dict(n=n, n1=n1, n2=n2, mat_a=mat_a, mat_a_real=mat_a_real, mat_c=mat_c,
                mat_f=_hi_lo_const(mat_f), mat_i=_hi_lo_const(mat_i))


def _filter_body(z_ref, w1_ref, b1_ref, f1_ref, w2_ref, b2_ref, f2_ref, w3_ref, dl_ref, h_ref, s_ref, *, seq):
    i = pl.program_id(0)
    tl = z_ref.shape[0]
    z = z_ref[...]
    a = jnp.sin(f1_ref[...] * (_dot3(z, w1_ref[...]) + b1_ref[...]))
    a = jnp.sin(f2_ref[...] * (_dot3(a, w2_ref[...]) + b2_ref[...]))
    h = _dot(a.astype(BF16), w3_ref[...].astype(BF16))
    win = jnp.exp(-z[:, 0:1] * dl_ref[...])
    h = h * jnp.concatenate([win] * HY_ORDER, axis=1)
    row = i * tl + lax.broadcasted_iota(I32, (tl, 1), 0)
    h = jnp.where(row == seq, 0.0, h)
    h_ref[...] = h

    @pl.when(i == 0)
    def _():
        s_ref[...] = jnp.zeros_like(s_ref)

    s_ref[...] += jnp.broadcast_to(jnp.sum(jnp.abs(h), axis=0, keepdims=True), s_ref.shape)


def _hyena_filter_time(seq, w1, b1, f1, w2, b2, f2, w3, tl=512):
    n = 2 * seq
    tl = min(tl, seq)
    idx = jnp.arange(n, dtype=I32)
    p = jnp.minimum(jnp.where(idx < seq, idx, n - idx), seq - 1).astype(F32)
    t = p / float(seq - 1)
    ang = 2.0 * math.pi * p / seq
    bands = jnp.linspace(1e-4, HY_BANDS - 1, HY_BANDS, dtype=F32)
    phase = ang[:, None] * bands[None, :]
    emb = jnp.concatenate([t[:, None], jnp.cos(phase), -jnp.sin(phase)], axis=-1)
    emb = jnp.pad(emb, ((0, 0), (0, EMB_PAD - emb.shape[1])))
    w1p = jnp.pad(w1.astype(F32), ((0, EMB_PAD - w1.shape[0]), (0, 0)))
    w3d = w3.astype(F32).reshape(HY_FILTER_WIDTH, HY_ORDER, 2, HY_WIDTH).transpose(2, 0, 1, 3)
    w3d = w3d.reshape(2, HY_FILTER_WIDTH, HY_ORDER * HY_WIDTH)
    deltas = jnp.abs(jnp.linspace(math.log(HY_DECAY_TARGET) / HY_SLOW_DECAY,
                                  math.log(HY_DECAY_TARGET) / HY_FAST_DECAY, HY_WIDTH, dtype=F32))[None, :]
    fw = HY_FILTER_WIDTH
    nblk_half = seq // tl
    const = lambda i: (0, 0)
    return pl.pallas_call(
        functools.partial(_filter_body, seq=seq),
        grid=(n // tl,),
        in_specs=[pl.BlockSpec((tl, EMB_PAD), lambda i: (i, 0)),
                  pl.BlockSpec((EMB_PAD, fw), const), pl.BlockSpec((1, fw), const), pl.BlockSpec((1, fw), const),
                  pl.BlockSpec((fw, fw), const), pl.BlockSpec((1, fw), const), pl.BlockSpec((1, fw), const),
                  pl.BlockSpec((None, fw, HY_ORDER * HY_WIDTH), lambda i: (i // nblk_half, 0, 0)),
                  pl.BlockSpec((1, HY_WIDTH), const)],
        out_specs=[pl.BlockSpec((tl, HY_ORDER * HY_WIDTH), lambda i: (i, 0)),
                   pl.BlockSpec((8, HY_ORDER * HY_WIDTH), const)],
        out_shape=[jax.ShapeDtypeStruct((n, HY_ORDER * HY_WIDTH), F32),
                   jax.ShapeDtypeStruct((8, HY_ORDER * HY_WIDTH), F32)],
        compiler_params=_cparams(("arbitrary",)),
        name="hyena_filter",
    )(emb, w1p, b1.astype(F32)[None], f1.astype(F32)[None], w2.astype(F32), b2.astype(F32)[None],
      f2.astype(F32)[None], w3d, deltas)


def _stage_a_real_body(h_ref, m_ref, a_ref):
    n1 = h_ref.shape[0]
    h_t = jnp.swapaxes(h_ref[...], 0, 1)
    rs = [_dot(m_ref[j], h_t[j].astype(BF16)) for j in range(h_ref.shape[1])]
    r = jnp.swapaxes(jnp.stack(rs, axis=0), 0, 1)
    a_ref[0] = r[:n1]
    a_ref[1] = r[n1:]


def _stage_a_real(h, tabs, n2c=8, cw=512):
    n1, n2 = tabs["n1"], tabs["n2"]
    c = h.shape[1]
    h3 = h.reshape(n1, n2, c)
    mat = tabs["mat_a_real"]
    return pl.pallas_call(
        _stage_a_real_body,
        grid=(n2 // n2c, c // cw),
        in_specs=[pl.BlockSpec((n1, n2c, cw), lambda j, k: (0, j, k)),
                  pl.BlockSpec((n2c,) + mat.shape[1:], lambda j, k: (j, 0, 0))],
        out_specs=pl.BlockSpec((2, n1, n2c, cw), lambda j, k: (0, 0, j, k)),
        out_shape=jax.ShapeDtypeStruct((2, n1, n2, c), F32),
        compiler_params=_cparams(("parallel", "parallel")),
        name="dft_a_filter",
    )(h3, mat)


def _pack_pair(re, im):
    hi = lax.bitcast_convert_type(re.astype(BF16).astype(F32), I32)
    lo = lax.bitcast_convert_type(im.astype(BF16).astype(F32), I32)
    return lax.bitcast_convert_type(hi | lax.shift_right_logical(lo, 16), F32)


def _unpack_pair(packed):
    word = lax.bitcast_convert_type(packed, I32)
    re = lax.bitcast_convert_type(word & jnp.int32(-65536), F32)
    im = lax.bitcast_convert_type(lax.shift_left(word, 16), F32)
    return re, im


def _stage_a_body(u_ref, m_ref, a_ref):
    _, r1, n2c, cw = u_ref.shape
    n1 = 2 * r1
    u_t = jnp.swapaxes(u_ref[...].reshape(n1, n2c, cw), 0, 1)
    packed = []
    for j in range(n2c):
        r = _dot(m_ref[j], u_t[j].astype(BF16))
        packed.append(_pack_pair(r[:n1], r[n1:]))
    a_ref[...] = jnp.swapaxes(jnp.stack(packed, axis=0), 0, 1)


def _stage_a(u5, col, tabs, n2c=8):
    n1, n2 = tabs["n1"], tabs["n2"]
    npair, _, r1, _, _ = u5.shape
    cw = HY_WIDTH
    mat = tabs["mat_a"]
    return pl.pallas_call(
        _stage_a_body,
        grid=(npair, n2 // n2c),
        in_specs=[pl.BlockSpec((None, 2, r1, n2c, cw), lambda p, j: (p, 0, 0, j, col)),
                  pl.BlockSpec((n2c,) + mat.shape[1:], lambda p, j: (j, 0, 0))],
        out_specs=pl.BlockSpec((None, n1, n2c, cw), lambda p, j: (p, 0, j, 0)),
        out_shape=jax.ShapeDtypeStruct((npair, n1, n2, cw), F32),
        compiler_params=_cparams(("parallel", "parallel")),
        name="dft_a",
    )(u5, mat)


def _stage_b_filter_body(a_ref, g_ref, s_ref, kr_ref, ki_ref, *, n):
    n2 = a_ref.shape[2]
    scale = 1.0 / (s_ref[0:1, :] * float(n))
    for kk in range(a_ref.shape[1]):
        d = _dot(g_ref[...], jnp.concatenate([a_ref[0, kk], a_ref[1, kk]], axis=0).astype(BF16))
        kr_ref[kk] = d[:n2] * scale
        ki_ref[kk] = d[n2:] * scale


def _stage_b_filter(a, s, tabs):
    n, n1, n2 = tabs["n"], tabs["n1"], tabs["n2"]
    c = a.shape[-1]
    kb = max(1, min(n1, 512 // n2))
    g = tabs["mat_f"][0]
    return pl.pallas_call(
        functools.partial(_stage_b_filter_body, n=n),
        grid=(n1 // kb,),
        in_specs=[pl.BlockSpec((2, kb, n2, c), lambda k: (0, k, 0, 0)),
                  pl.BlockSpec(g.shape, lambda k: (0, 0)),
                  pl.BlockSpec(s.shape, lambda k: (0, 0))],
        out_specs=[pl.BlockSpec((kb, n2, c), lambda k: (k, 0, 0)),
                   pl.BlockSpec((kb, n2, c), lambda k: (k, 0, 0))],
        out_shape=[jax.ShapeDtypeStruct((n1, n2, c), F32), jax.ShapeDtypeStruct((n1, n2, c), F32)],
        compiler_params=_cparams(("parallel",)),
        name="dft_b_filter",
    )(a, g, s)


def _stage_b_body(a_ref, kr_ref, ki_ref, gf_ref, gi_ref, c_ref):
    n2 = a_ref.shape[1]
    for kk in range(a_ref.shape[0]):
        d = _dot(gf_ref[...], jnp.concatenate(_unpack_pair(a_ref[kk]), axis=0).astype(BF16))
        dr, di = d[:n2], d[n2:]
        kr, ki = kr_ref[kk], ki_ref[kk]
        yr = dr * kr - di * ki
        yi = dr * ki + di * kr
        e = _dot(gi_ref[...], jnp.concatenate([yr, yi], axis=0).astype(BF16))
        c_ref[kk] = _pack_pair(e[:n2], e[n2:])


def _stage_b(a, kr, ki, order, tabs):
    n1, n2 = tabs["n1"], tabs["n2"]
    npair = a.shape[0]
    cw = HY_WIDTH
    kb = max(1, min(n1, 1024 // n2))
    gf = tabs["mat_f"][0]
    gi = tabs["mat_i"][0]
    const = lambda k, p: (0, 0)
    return pl.pallas_call(
        _stage_b_body,
        grid=(n1 // kb, npair),
        in_specs=[pl.BlockSpec((None, kb, n2, cw), lambda k, p: (p, k, 0, 0)),
                  pl.BlockSpec((kb, n2, cw), lambda k, p: (k, 0, order)),
                  pl.BlockSpec((kb, n2, cw), lambda k, p: (k, 0, order)),
                  pl.BlockSpec(gf.shape, const), pl.BlockSpec(gi.shape, const)],
        out_specs=pl.BlockSpec((None, kb, n2, cw), lambda k, p: (p, k, 0, 0)),
        out_shape=jax.ShapeDtypeStruct(a.shape, F32),
        compiler_params=_cparams(("parallel", "parallel")),
        name="dft_b",
    )(a, kr, ki, gf, gi)


def _stage_c_body(c_ref, u_ref, g_ref, skip_ref, mc_ref, ma_ref, y_ref, *rest, fuse_a):
    _, r1, n2c, cw = u_ref.shape
    n1 = 2 * r1
    ys = []
    c_t = jnp.swapaxes(c_ref[...], 0, 1)
    for j in range(n2c):
        cc = jnp.concatenate(_unpack_pair(c_t[j]), axis=0)
        ys.append(_dot(mc_ref[j], cc.astype(BF16)))
    y = jnp.swapaxes(jnp.stack(ys, axis=0), 0, 1)
    u = u_ref[...].reshape(n1, n2c, cw)
    g = g_ref[...].reshape(n1, n2c, cw)
    yo = g * (y + u * skip_ref[...].reshape(1, 1, cw))
    y_ref[...] = yo.reshape(2, r1, n2c, cw).astype(y_ref.dtype)
    if fuse_a:
        a_ref = rest[0]
        yo_t = jnp.swapaxes(yo, 0, 1)
        packed = []
        for j in range(n2c):
            r = _dot(ma_ref[j], yo_t[j].astype(BF16))
            packed.append(_pack_pair(r[:n1], r[n1:]))
        a_ref[...] = jnp.swapaxes(jnp.stack(packed, axis=0), 0, 1)


def _stage_c(c, u5, u_col, g5, g_col, skip, tabs, fuse_a, out_dtype, n2c=8):
    n1, n2 = tabs["n1"], tabs["n2"]
    npair, _, r1, _, _ = u5.shape
    cw = HY_WIDTH
    mc = tabs["mat_c"]
    ma = tabs["mat_a"]
    const = lambda p, j: (0, 0)
    per_j = lambda m: pl.BlockSpec((n2c,) + m.shape[1:], lambda p, j: (j, 0, 0))
    out_shape = [jax.ShapeDtypeStruct((npair, 2, r1, n2, cw), out_dtype)]
    out_specs = [pl.BlockSpec((None, 2, r1, n2c, cw), lambda p, j: (p, 0, 0, j, 0))]
    if fuse_a:
        out_shape.append(jax.ShapeDtypeStruct((npair, n1, n2, cw), F32))
        out_specs.append(pl.BlockSpec((None, n1, n2c, cw), lambda p, j: (p, 0, j, 0)))
    return pl.pallas_call(
        functools.partial(_stage_c_body, fuse_a=fuse_a),
        grid=(npair, n2 // n2c),
        in_specs=[pl.BlockSpec((None, n1, n2c, cw), lambda p, j: (p, 0, j, 0)),
                  pl.BlockSpec((None, 2, r1, n2c, cw), lambda p, j: (p, 0, 0, j, u_col)),
                  pl.BlockSpec((None, 2, r1, n2c, cw), lambda p, j: (p, 0, 0, j, g_col)),
                  pl.BlockSpec((1, cw), const), per_j(mc), per_j(ma)],
        out_specs=out_specs,
        out_shape=out_shape,
        compiler_params=_cparams(("parallel", "parallel")),
        name="dft_c",
    )(c, u5, g5, skip, mc, ma)


def _hyena(zc3, kr, ki, skip, tabs):
    bsz, seq, _ = zc3.shape
    n1, n2 = tabs["n1"], tabs["n2"]
    r1 = n1 // 2
    z5 = zc3.reshape(bsz // 2, 2, r1, n2, 3 * HY_WIDTH)
    a = _stage_a(z5, 2, tabs)
    c = _stage_b(a, kr, ki, 0, tabs)
    y1, a2 = _stage_c(c, z5, 2, z5, 0, skip[0:1].astype(F32), tabs, True, F32)
    c2 = _stage_b(a2, kr, ki, 1, tabs)
    (y2,) = _stage_c(c2, y1, 0, z5, 1, skip[1:2].astype(F32), tabs, False, BF16)
    return y2.reshape(bsz * seq, HY_WIDTH)


def _retention_body(sc_ref, qf_ref, kf_ref, vf_ref, cf_ref, sf_ref, qb_ref, kb_ref, vb_ref, cb_ref, sb_ref,
                    of_ref, ob_ref, state, dmat, qwt, kwt):
    ch = RET_CHUNK
    dh = RET_HEAD_DIM

    @pl.when(pl.program_id(1) == 0)
    def _():
        state[...] = jnp.zeros_like(state)
        ci = lax.broadcasted_iota(I32, (ch, ch), 0).astype(F32)
        mi = lax.broadcasted_iota(I32, (ch, ch), 1).astype(F32)
        lag = ci - mi
        for d in range(2):
            for h in range(RET_HEADS):
                lg = sc_ref[d * RET_HEADS + h]
                if d == 0:
                    dmat[d, h] = jnp.where(lag >= 0, jnp.exp(lg * jnp.maximum(lag, 0.0)), 0.0)
                    qwt[d, h] = jnp.exp(lg * (ci + 1.0))
                    kwt[d, h] = jnp.exp(lg * (ch - 1.0 - ci))
                else:
                    dmat[d, h] = jnp.where(lag < 0, jnp.exp(lg * jnp.maximum(-lag, 0.0)), 0.0)
                    qwt[d, h] = jnp.exp(lg * (ch - ci))
                    kwt[d, h] = jnp.exp(lg * ci)

    scale = RET_HEAD_DIM ** -0.5
    refs = ((qf_ref, kf_ref, vf_ref, cf_ref, sf_ref, of_ref), (qb_ref, kb_ref, vb_ref, cb_ref, sb_ref, ob_ref))
    work = [(d, h) for d in range(2) for h in range(RET_HEADS)]
    qs, ks, qws, kws, vs = {}, {}, {}, {}, {}
    for d, h in work:
        q_ref, k_ref, v_ref, c_ref, s_ref, _ = refs[d]
        sl = slice(h * dh, (h + 1) * dh)
        cosf = c_ref[...]
        sinf = s_ref[...]
        q = q_ref[:, sl].astype(F32)
        k = k_ref[:, sl].astype(F32)
        q = q * cosf + pltpu.roll(q, dh // 2, axis=1) * sinf
        k = (k * cosf + pltpu.roll(k, dh // 2, axis=1) * sinf) * scale
        qs[d, h], ks[d, h] = q.astype(BF16), k.astype(BF16)
        qws[d, h], kws[d, h] = (q * qwt[d, h]).astype(BF16), (k * kwt[d, h]).astype(BF16)
        vs[d, h] = v_ref[:, sl]
    scores = {dh_: _dot_nt(qs[dh_], ks[dh_]) for dh_ in work}
    for d, h in work:
        lhs = jnp.concatenate([(scores[d, h] * dmat[d, h]).astype(BF16), qws[d, h]], axis=1)
        rhs = jnp.concatenate([vs[d, h], state[d, h].astype(BF16)], axis=0)
        refs[d][5][:, h * dh:(h + 1) * dh] = _dot(lhs, rhs).astype(BF16)
    for d, h in work:
        cdec = sc_ref[2 * RET_HEADS + d * RET_HEADS + h]
        state[d, h] = state[d, h] * cdec + _dot_tn(kws[d, h], vs[d, h])


def _retention(zret3, decay_logit):
    bsz, seq, _ = zret3.shape
    ch = RET_CHUNK
    nc = seq // ch
    half = RET_HEAD_DIM // 2
    inv_freq = ROPE_BASE ** (-jnp.arange(half, dtype=F32) / half)
    ang = jnp.arange(seq, dtype=F32)[:, None] * inv_freq[None, :]
    cosf = jnp.concatenate([jnp.cos(ang), jnp.cos(ang)], axis=1)
    sinf = jnp.concatenate([-jnp.sin(ang), jnp.sin(ang)], axis=1)
    log_g = jax.nn.log_sigmoid(decay_logit.astype(F32)).reshape(-1)
    scal = jnp.concatenate([log_g, jnp.exp(log_g * ch)])
    w = RET_WIDTH
    fwd = lambda col: pl.BlockSpec((None, ch, w), lambda b, n, sc: (b, n, col))
    bwd = lambda col: pl.BlockSpec((None, ch, w), lambda b, n, sc: (b, nc - 1 - n, col))
    rope_f = pl.BlockSpec((ch, RET_HEAD_DIM), lambda b, n, sc: (n, 0))
    rope_b = pl.BlockSpec((ch, RET_HEAD_DIM), lambda b, n, sc: (nc - 1 - n, 0))
    grid_spec = pltpu.PrefetchScalarGridSpec(
        num_scalar_prefetch=1,
        grid=(bsz, nc),
        in_specs=[fwd(0), fwd(1), fwd(2), rope_f, rope_f, bwd(0), bwd(1), bwd(2), rope_b, rope_b],
        out_specs=[pl.BlockSpec((None, ch, w), lambda b, n, sc: (b, n, 0)),
                   pl.BlockSpec((None, ch, w), lambda b, n, sc: (b, nc - 1 - n, 0))],
        scratch_shapes=[pltpu.VMEM((2, RET_HEADS, RET_HEAD_DIM, RET_HEAD_DIM), F32),
                        pltpu.VMEM((2, RET_HEADS, ch, ch), F32),
                        pltpu.VMEM((2, RET_HEADS, ch, RET_HEAD_DIM), F32),
                        pltpu.VMEM((2, RET_HEADS, ch, RET_HEAD_DIM), F32)],
    )
    return pl.pallas_call(
        _retention_body,
        grid_spec=grid_spec,
        out_shape=[jax.ShapeDtypeStruct((bsz, seq, w), BF16), jax.ShapeDtypeStruct((bsz, seq, w), BF16)],
        compiler_params=_cparams(("arbitrary", "arbitrary")),
        name="retention",
    )(scal, zret3, zret3, zret3, cosf, sinf, zret3, zret3, zret3, cosf, sinf)


def _outproj_body(x_ref, yhy_ref, of_ref, ob_ref, gr_ref, ghy_ref, gret_ref, whyo_ref, wreto_ref, wo_ref,
                  n2g_ref, wrt_ref, x1_ref, pt_ref):
    dh = RET_HEAD_DIM
    o = of_ref[...].astype(F32) + ob_ref[...].astype(F32)
    parts = []
    for h in range(RET_HEADS):
        oh = o[:, h * dh:(h + 1) * dh]
        parts.append(oh * lax.rsqrt(jnp.mean(oh * oh, axis=-1, keepdims=True) + EPS))
    on = jnp.concatenate(parts, axis=1)
    gr = gr_ref[...].astype(F32)
    ret = (gr * jax.nn.sigmoid(gr)) * on
    y_ret = _dot(ret.astype(BF16), wreto_ref[...])
    y_hy = _dot(yhy_ref[...], whyo_ref[...])
    merged = jax.nn.sigmoid(ghy_ref[...].astype(F32)) * y_hy + jax.nn.sigmoid(gret_ref[...].astype(F32)) * y_ret
    x1 = x_ref[...] + _dot(merged.astype(BF16), wo_ref[...])
    x1_ref[...] = x1
    h2 = x1 * lax.rsqrt(jnp.mean(x1 * x1, axis=-1, keepdims=True) + EPS) * n2g_ref[...]
    logits = _dot_nt(wrt_ref[...], h2.astype(BF16))
    m = jnp.max(logits, axis=0, keepdims=True)
    e = jnp.exp(logits - m)
    pt_ref[...] = e / jnp.sum(e, axis=0, keepdims=True)


def _outproj(x2, yhy, o_f, o_b, zret, zg, whyo, wreto, wo, n2g, wrt, tm=512):
    n_tok = x2.shape[0]
    d = D_MODEL
    row = lambda w, col=0: pl.BlockSpec((tm, w), lambda i: (i, col))
    const = lambda shape: pl.BlockSpec(shape, lambda i: (0, 0))
    return pl.pallas_call(
        _outproj_body,
        grid=(n_tok // tm,),
        in_specs=[row(d), row(HY_WIDTH), row(RET_WIDTH), row(RET_WIDTH), row(RET_WIDTH, 3), row(d, 0), row(d, 1),
                  const((HY_WIDTH, d)), const((RET_WIDTH, d)), const((d, d)), const((1, d)), const((N_EXPERTS, d))],
        out_specs=[row(d), pl.BlockSpec((N_EXPERTS, tm), lambda i: (0, i))],
        out_shape=[jax.ShapeDtypeStruct((n_tok, d), F32), jax.ShapeDtypeStruct((N_EXPERTS, n_tok), F32)],
        compiler_params=_cparams(("parallel",)),
        name="outproj_router",
    )(x2, yhy, o_f, o_b, zret, zg, zg, whyo, wreto, wo, n2g, wrt)


def _select_body(p_ref, upper_ref, lower_ref, lowinc_ref, eye_ref, pos_ref, idx_ref, gate_ref, lo_ref, *, cap):
    rows = p_ref.shape[0]
    p = p_ref[...]
    bits = lax.bitcast_convert_type(p, I32)

    def count(mask):
        return jnp.sum(jnp.sum(mask.astype(F32), axis=1, keepdims=True), axis=0, keepdims=True)

    def bit_step(i, thr):
        cand = thr | jnp.left_shift(jnp.int32(1), 30 - i)
        return jnp.where(count(bits >= cand) >= cap, cand, thr)

    thr = lax.fori_loop(0, 31, bit_step, jnp.zeros((1, 1), I32))
    gt = bits > thr
    eq = bits == thr
    need = cap - count(gt)

    def prefix(mask_f):
        incl = _dot(mask_f.astype(BF16), upper_ref[...])
        tot = jnp.broadcast_to(incl[:, LANES - 1:LANES], incl.shape)
        base = _dot(lower_ref[...], tot.astype(BF16))
        return incl, base, tot

    eq_f = eq.astype(F32)
    incl_e, base_e, _ = prefix(eq_f)
    sel = gt | (eq & (base_e + incl_e - eq_f < need))
    sel_f = sel.astype(F32)
    incl, base, tot = prefix(sel_f)
    pos_ref[...] = jnp.where(sel, (base + incl - 1.0).astype(I32), -1)
    lo_ref[...] = base.astype(I32)

    rowend = base + tot
    incl_t = _dot_nt(lowinc_ref[...], sel_f.astype(BF16)).astype(BF16)
    p_t = []
    rem = p
    for _ in range(3):
        part = rem.astype(BF16)
        rem = rem - part.astype(F32)
        p_t.append(_dot_nt(eye_ref[...], part).astype(BF16))
    table = jnp.concatenate([incl_t] + p_t, axis=0)
    groups = 2 if (cap // LANES) % 2 == 0 else 1
    sw = groups * LANES
    wide = lambda a: jnp.concatenate([a] * groups, axis=1)
    rowend_w, tot_w = wide(rowend), wide(tot)
    r_iota = lax.broadcasted_iota(I32, (rows, sw), 0).astype(F32)
    lane_iota = lax.broadcasted_iota(I32, (LANES, sw), 0).astype(F32)

    def slot_tile(ts, carry):
        s = (ts * sw + lax.broadcasted_iota(I32, (1, sw), 1)).astype(F32)
        done = rowend_w <= s
        row = jnp.sum(done.astype(F32), axis=0, keepdims=True)
        before = jnp.sum(jnp.where(done, tot_w, 0.0), axis=0, keepdims=True)
        onehot_t = (r_iota == row).astype(BF16)
        got = _dot(table, onehot_t)
        g_t = got[:LANES]
        lane = jnp.sum((g_t <= s - before).astype(F32), axis=0, keepdims=True)
        tok = (row * LANES + lane).astype(I32)
        p_row = got[LANES:2 * LANES] + got[2 * LANES:3 * LANES] + got[3 * LANES:]
        gate = jnp.sum(jnp.where(lane_iota == lane, p_row, 0.0), axis=0, keepdims=True)
        for k in range(groups):
            idx_ref[pl.ds(ts * groups + k, 1), :] = tok[:, k * LANES:(k + 1) * LANES]
            gate_ref[pl.ds(ts * groups + k, 1), :] = gate[:, k * LANES:(k + 1) * LANES]
        return carry

    lax.fori_loop(0, cap // sw, slot_tile, 0)


def _select(pt3, cap):
    n_e, rows, _ = pt3.shape
    ii = np.arange(LANES)
    upper = jnp.asarray(ii[:, None] <= ii[None, :], BF16)
    lowinc = jnp.asarray(ii[None, :] <= ii[:, None], BF16)
    eye = jnp.asarray(ii[None, :] == ii[:, None], BF16)
    rr = np.arange(rows)
    lower = jnp.asarray(rr[None, :] < rr[:, None], BF16)
    const = lambda shape: pl.BlockSpec(shape, lambda e: (0, 0))
    tok_spec = pl.BlockSpec((None, rows, LANES), lambda e: (e, 0, 0))
    slot_spec = pl.BlockSpec((None, cap // LANES, LANES), lambda e: (e, 0, 0))
    return pl.pallas_call(
        functools.partial(_select_body, cap=cap),
        grid=(n_e,),
        in_specs=[tok_spec, const((LANES, LANES)), const((rows, rows)), const((LANES, LANES)), const((LANES, LANES))],
        out_specs=[tok_spec, slot_spec, slot_spec, tok_spec],
        out_shape=[jax.ShapeDtypeStruct((n_e, rows, LANES), I32),
                   jax.ShapeDtypeStruct((n_e, cap // LANES, LANES), I32),
                   jax.ShapeDtypeStruct((n_e, cap // LANES, LANES), F32),
                   jax.ShapeDtypeStruct((n_e, rows, LANES), I32)],
        compiler_params=_cparams(("parallel",)),
        name="expert_select",
    )(pt3, upper, lower, lowinc, eye)


def _ffn_body(idx_ref, h_hbm, n2g_ref, gate_ref, wg_ref, wu_ref, wd_ref, ye_ref, xbuf, xb, sems, *,
              tiles_per_expert):
    s = xbuf.shape[1]
    step = pl.program_id(0) * tiles_per_expert + pl.program_id(1)
    n_steps = pl.num_programs(0) * tiles_per_expert
    cur = step % 2

    def row_copy(st, i, buf):
        tok = idx_ref[st * s + i]
        return pltpu.make_async_copy(h_hbm.at[pl.ds(tok, 1), :], xbuf.at[buf, pl.ds(i, 1), :], sems.at[buf])

    def wait_rows(buf):
        pltpu.make_async_copy(h_hbm.at[pl.ds(0, s), :], xbuf.at[buf], sems.at[buf]).wait()

    @pl.when(step == 0)
    def _():
        def one(i, carry):
            row_copy(0, i, 0).start()
            return carry
        lax.fori_loop(0, s, one, 0, unroll=8)

    @pl.when(step + 1 < n_steps)
    def _():
        for i in range(s):
            row_copy(step + 1, i, 1 - cur).start()

    wait_rows(cur)
    x = xbuf[cur]
    xb[...] = (x * lax.rsqrt(jnp.mean(x * x, axis=-1, keepdims=True) + EPS) * n2g_ref[...]).astype(BF16)
    y = None
    for f0 in range(0, EXPERT_FF, FF_CHUNK):
        f1 = min(f0 + FF_CHUNK, EXPERT_FF)
        a = _dot(xb[...], wg_ref[:, f0:f1])
        b = _dot(xb[...], wu_ref[:, f0:f1])
        hid = ((a * jax.nn.sigmoid(a)) * b).astype(BF16)
        part = _dot(hid, wd_ref[f0:f1, :])
        y = part if y is None else y + part
    for k in range(s // LANES):
        col = jnp.transpose(jnp.broadcast_to(gate_ref[k:k + 1, :], (LANES, LANES)))[:, 0:1]
        yk = y[k * LANES:(k + 1) * LANES] * col
        ye_ref[k * LANES:(k + 1) * LANES, :] = _pack_pair(yk[:, :D_MODEL // 2], yk[:, D_MODEL // 2:])


def _expert_ffn(idx_flat, x1, n2g, gates, wg, wu, wd, cap, s=512):
    s = min(s, cap)
    tiles = cap // s
    g3 = gates.reshape(N_EXPERTS * tiles, s // LANES, LANES)
    grid_spec = pltpu.PrefetchScalarGridSpec(
        num_scalar_prefetch=1,
        grid=(N_EXPERTS, tiles),
        in_specs=[pl.BlockSpec(memory_space=pl.ANY),
                  pl.BlockSpec((1, D_MODEL), lambda e, j, idx: (0, 0)),
                  pl.BlockSpec((None, s // LANES, LANES), lambda e, j, idx: (e * tiles + j, 0, 0)),
                  pl.BlockSpec((None, D_MODEL, EXPERT_FF), lambda e, j, idx: (e, 0, 0)),
                  pl.BlockSpec((None, D_MODEL, EXPERT_FF), lambda e, j, idx: (e, 0, 0)),
                  pl.BlockSpec((None, EXPERT_FF, D_MODEL), lambda e, j, idx: (e, 0, 0))],
        out_specs=pl.BlockSpec((s, D_MODEL // 2), lambda e, j, idx: (e * tiles + j, 0)),
        scratch_shapes=[pltpu.VMEM((2, s, D_MODEL), F32), pltpu.VMEM((s, D_MODEL), BF16),
                        pltpu.SemaphoreType.DMA((2,))],
    )
    return pl.pallas_call(
        functools.partial(_ffn_body, tiles_per_expert=tiles),
        grid_spec=grid_spec,
        out_shape=jax.ShapeDtypeStruct((N_EXPERTS * cap, D_MODEL // 2), F32),
        compiler_params=_cparams(("arbitrary", "arbitrary")),
        name="expert_ffn",
    )(idx_flat, x1, n2g, g3, wg, wu, wd)


def _combine_body(lo_ref, np_ref, x1_ref, pos_ref, nfg_ref, expand_ref, lov_ref, ye_hbm, o_ref, win, sems, *, cap,
                  rows_total):
    r = pl.program_id(0)
    n_rows = pl.num_programs(0)
    w = COMBINE_WIN
    cur = r % 2

    def starts(rr, m):
        out = []
        for e in range(N_EXPERTS):
            first = e * cap + lo_ref[e * n_rows + rr]
            intended = (first // COMBINE_ALIGN) * COMBINE_ALIGN + m * w
            actual = pl.multiple_of(jnp.minimum(intended, rows_total - w), COMBINE_ALIGN)
            out.append((intended, actual))
        return out

    def copies(rr, m, buf):
        return [pltpu.make_async_copy(ye_hbm.at[pl.ds(actual, w), :], win.at[buf, pl.ds(e * w, w), :], sems.at[buf, e])
                for e, (_, actual) in enumerate(starts(rr, m))]

    def wait_windows(buf):
        for e in range(N_EXPERTS):
            pltpu.make_async_copy(ye_hbm.at[pl.ds(0, w), :], win.at[buf, pl.ds(e * w, w), :], sems.at[buf, e]).wait()

    def contribution(rr, m, buf):
        lane = lax.broadcasted_iota(I32, (1, N_EXPERTS), 1)
        first = lane * cap + lov_ref[pl.ds(rr, 1), :]
        intended = (first & jnp.int32(-COMBINE_ALIGN)) + m * w
        actual = jnp.minimum(intended, rows_total - w)
        pos = pos_ref[...]
        glob = pos + lane * cap
        rel = glob - intended
        valid = (pos >= 0) & (rel >= 0) & (rel < w)
        local = jnp.where(valid, glob - actual, -1).astype(F32).astype(BF16)
        spread = _dot(local, expand_ref[...])
        col = (lax.broadcasted_iota(I32, (1, N_EXPERTS * w), 1) & (w - 1)).astype(F32)
        onehot = (spread == col).astype(BF16)
        lo_cols, hi_cols = _unpack_pair(win[buf])
        return jnp.concatenate([_dot(onehot, lo_cols.astype(BF16)), _dot(onehot, hi_cols.astype(BF16))], axis=1)

    @pl.when(r == 0)
    def _():
        for cp in copies(0, 0, 0):
            cp.start()

    @pl.when(r + 1 < n_rows)
    def _():
        for cp in copies(r + 1, 0, 1 - cur):
            cp.start()

    wait_windows(cur)
    acc = x1_ref[...] + contribution(r, 0, cur)

    def extra_pass(m, acc):
        for cp in copies(r, m, cur):
            cp.start()
        wait_windows(cur)
        return acc + contribution(r, m, cur)

    acc = lax.fori_loop(1, np_ref[r], extra_pass, acc)
    o_ref[...] = acc * lax.rsqrt(jnp.mean(acc * acc, axis=-1, keepdims=True) + EPS) * nfg_ref[...]


def _combine(lo, x1, pos_t, nfg, ye, cap):
    n_tok = x1.shape[0]
    tm = LANES
    w = COMBINE_WIN
    rows_total = ye.shape[0]
    n_rows = n_tok // tm
    nxt = jnp.concatenate([lo[:, 1:], jnp.full((N_EXPERTS, 1), cap, I32)], axis=1)
    span = lo % COMBINE_ALIGN + (nxt - lo)
    n_pass = jnp.maximum(jnp.max((span + w - 1) // w, axis=0), 1).astype(I32)
    ee = np.arange(N_EXPERTS)
    expand = jnp.asarray(ee[:, None] == (np.arange(N_EXPERTS * w) // w)[None, :], BF16)
    grid_spec = pltpu.PrefetchScalarGridSpec(
        num_scalar_prefetch=2,
        grid=(n_rows,),
        in_specs=[pl.BlockSpec((tm, D_MODEL), lambda i, lo_, np_: (i, 0)),
                  pl.BlockSpec((tm, N_EXPERTS), lambda i, lo_, np_: (i, 0)),
                  pl.BlockSpec((1, D_MODEL), lambda i, lo_, np_: (0, 0)),
                  pl.BlockSpec((N_EXPERTS, N_EXPERTS * w), lambda i, lo_, np_: (0, 0)),
                  pl.BlockSpec((n_rows, N_EXPERTS), lambda i, lo_, np_: (0, 0)),
                  pl.BlockSpec(memory_space=pl.ANY)],
        out_specs=pl.BlockSpec((tm, D_MODEL), lambda i, lo_, np_: (i, 0)),
        scratch_shapes=[pltpu.VMEM((2, N_EXPERTS * w, D_MODEL // 2), F32),
                        pltpu.SemaphoreType.DMA((2, N_EXPERTS))],
    )
    return pl.pallas_call(
        functools.partial(_combine_body, cap=cap, rows_total=rows_total),
        grid_spec=grid_spec,
        out_shape=jax.ShapeDtypeStruct((n_tok, D_MODEL), F32),
        compiler_params=_cparams(("arbitrary",)),
        name="moe_combine_norm",
    )(lo.reshape(-1), n_pass, x1, pos_t, nfg, expand, lo.T, ye)


def _trunk(x, w):
    bsz, seq, d = x.shape
    n_tok = bsz * seq
    x2 = x.reshape(n_tok, d)
    tabs = _fft_tables(seq)

    zc, zret, zg = _inproj(x2, w["norm1_g"], w["w_in"], w["hy_conv_w"], w["hy_conv_b"], seq)
    zc = zc.reshape(bsz, seq, -1)

    h_time, h_abs = _hyena_filter_time(seq, w["hy_w1"], w["hy_b1"], w["hy_freq1"], w["hy_w2"], w["hy_b2"],
                                       w["hy_freq2"], w["hy_w3"])
    kr, ki = _stage_b_filter(_stage_a_real(h_time, tabs), h_abs, tabs)
    yhy = _hyena(zc, kr, ki, w["hy_skip"], tabs)

    o_f, o_b = _retention(zret.reshape(bsz, seq, -1), w["ret_decay_logit"])
    x1, pt = _outproj(x2, yhy, o_f.reshape(n_tok, -1), o_b.reshape(n_tok, -1), zret, zg,
                      w["w_hy_out"], w["w_ret_out"], w["w_o"], w["norm2_g"], w["w_router_t"])

    cap = CAPACITY_FACTOR * n_tok // N_EXPERTS
    rows = n_tok // LANES
    pos, idx, gates, lo = _select(pt.reshape(N_EXPERTS, rows, LANES), cap)
    ye = _expert_ffn(idx.reshape(-1), x1, w["norm2_g"], gates, w["w_gate"], w["w_up"], w["w_down"], cap)
    pos_t = pos.reshape(N_EXPERTS, n_tok).T
    y = _combine(lo[:, :, 0], x1, pos_t, w["norm_f_g"], ye, cap)
    return y.reshape(bsz, seq, d)


def kernel(x_prompt, x_sample, norm1_g, w_in, hy_conv_w, hy_conv_b, hy_w1, hy_b1, hy_freq1, hy_w2, hy_b2, hy_freq2,
           hy_w3, hy_skip, ret_decay_logit, w_hy_out, w_ret_out, w_o, norm2_g, w_router, w_gate, w_up, w_down,
           norm_f_g):
    layer = 0
    w = dict(
        norm1_g=norm1_g[layer].astype(F32)[None], w_in=w_in[layer].astype(BF16),
        hy_conv_w=hy_conv_w[layer].astype(F32), hy_conv_b=hy_conv_b[layer].astype(F32)[None],
        hy_w1=hy_w1[layer], hy_b1=hy_b1[layer], hy_freq1=hy_freq1[layer], hy_w2=hy_w2[layer], hy_b2=hy_b2[layer],
        hy_freq2=hy_freq2[layer], hy_w3=hy_w3[layer], hy_skip=hy_skip[layer],
        ret_decay_logit=ret_decay_logit[layer],
        w_hy_out=w_hy_out[layer].astype(BF16), w_ret_out=w_ret_out[layer].astype(BF16), w_o=w_o[layer].astype(BF16),
        norm2_g=norm2_g[layer].astype(F32)[None], w_router_t=w_router[layer].T.astype(BF16),
        w_gate=w_gate[layer].astype(BF16), w_up=w_up[layer].astype(BF16), w_down=w_down[layer].astype(BF16),
        norm_f_g=norm_f_g.astype(F32)[None],
    )
    return _trunk(x_prompt, w), _trunk(x_sample, w)
```

```python
import functools
import math

import numpy as np
import jax
import jax.numpy as jnp
from jax import lax
from jax.experimental import pallas as pl
from jax.experimental.pallas import tpu as pltpu

F32 = jnp.float32
BF16 = jnp.bfloat16
I32 = jnp.int32

D_MODEL = 1024
HY_WIDTH = 512
HY_ORDER = 2
HY_BANDS = 16
HY_FILTER_WIDTH = 64
HY_FAST_DECAY = 0.3
HY_SLOW_DECAY = 1.5
HY_DECAY_TARGET = 1e-2
RET_WIDTH = 512
RET_HEADS = 4
RET_HEAD_DIM = 128
RET_CHUNK = 128
ROPE_BASE = 10000.0
N_EXPERTS = 16
EXPERT_FF = 1408
CAPACITY_FACTOR = 2
EPS = 1e-6

LANES = 128
SUBLANES = 8
EMB_PAD = 128
DFT_N1_MAX = 128
FF_CHUNK = 256
COMBINE_ALIGN = SUBLANES
COMBINE_WIN = 32


def _cparams(sem, vmem_mb=48):
    return pltpu.CompilerParams(dimension_semantics=sem, vmem_limit_bytes=vmem_mb * 1024 * 1024)


def _dot(a, b):
    return jnp.dot(a, b, preferred_element_type=F32)


def _dot_nt(a, b):
    return lax.dot_general(a, b, (((1,), (1,)), ((), ())), preferred_element_type=F32)


def _dot_tn(a, b):
    return lax.dot_general(a, b, (((0,), (0,)), ((), ())), preferred_element_type=F32)


def _split(a):
    hi = a.astype(BF16)
    lo = (a - hi.astype(F32)).astype(BF16)
    return hi, lo


def _dot_const(m_hi, m_lo, x, passes):
    xh = x.astype(BF16)
    r = _dot(m_hi, xh)
    if passes >= 3:
        xl = (x - xh.astype(F32)).astype(BF16)
        r = r + _dot(m_lo, xh) + _dot(m_hi, xl)
    return r


def _dot3(a, b):
    ah, al = _split(a)
    bh, bl = _split(b)
    return _dot(ah, bh) + _dot(al, bh) + _dot(ah, bl)


def _inproj_body(x_ref, g_ref, w_ref, cw_ref, cb_ref, zc_ref, zret_ref, zg_ref, z_prev, row_prev, *, tiles_per_seq):
    i = pl.program_id(0)
    tm = x_ref.shape[0]
    n_hy = zc_ref.shape[1]
    n_ret = zret_ref.shape[1]

    @pl.when(i == 0)
    def _():
        z_prev[...] = jnp.zeros_like(z_prev)
        row_prev[...] = jnp.zeros_like(row_prev)

    x = x_ref[...]
    ms = jnp.mean(x * x, axis=-1, keepdims=True)
    h = (x * lax.rsqrt(ms + EPS) * g_ref[...]).astype(BF16)
    z_hy = _dot(h, w_ref[:, :n_hy])
    zret_ref[...] = _dot(h, w_ref[:, n_hy:n_hy + n_ret]).astype(BF16)
    zg_ref[...] = _dot(h, w_ref[:, n_hy + n_ret:]).astype(BF16)

    t_prev = (i - 1) % tiles_per_seq
    zp = z_prev[...]
    before = jnp.where(t_prev == 0, 0.0, row_prev[0:1, :])
    after = jnp.where(t_prev == tiles_per_seq - 1, 0.0, z_hy[0:1, :])
    row = lax.broadcasted_iota(I32, (tm, 1), 0)
    zm1 = jnp.where(row == 0, before, pltpu.roll(zp, 1, axis=0))
    zp1 = jnp.where(row == tm - 1, after, pltpu.roll(zp, tm - 1, axis=0))
    zc_ref[...] = zm1 * cw_ref[0:1, :] + zp * cw_ref[1:2, :] + zp1 * cw_ref[2:3, :] + cb_ref[...]
    row_prev[...] = jnp.broadcast_to(zp[tm - 1:tm, :], row_prev.shape)
    z_prev[...] = z_hy


def _inproj(x2, g, w_bf, conv_w, conv_b, seq, tm=256):
    n_tok = x2.shape[0]
    n_hy, n_ret, n_g = 3 * HY_WIDTH, 4 * RET_WIDTH, 2 * D_MODEL
    tm = min(tm, seq)
    n_tiles = n_tok // tm
    cur = lambda i: (jnp.minimum(i, n_tiles - 1), 0)
    const = lambda i: (0, 0)
    return pl.pallas_call(
        functools.partial(_inproj_body, tiles_per_seq=seq // tm),
        grid=(n_tiles + 1,),
        in_specs=[pl.BlockSpec((tm, D_MODEL), cur),
                  pl.BlockSpec((1, D_MODEL), const),
                  pl.BlockSpec((D_MODEL, n_hy + n_ret + n_g), const),
                  pl.BlockSpec((3, n_hy), const),
                  pl.BlockSpec((1, n_hy), const)],
        out_specs=[pl.BlockSpec((tm, n_hy), lambda i: (jnp.maximum(i - 1, 0), 0)),
                   pl.BlockSpec((tm, n_ret), cur),
                   pl.BlockSpec((tm, n_g), cur)],
        out_shape=[jax.ShapeDtypeStruct((n_tok, n_hy), F32),
                   jax.ShapeDtypeStruct((n_tok, n_ret), BF16),
                   jax.ShapeDtypeStruct((n_tok, n_g), BF16)],
        scratch_shapes=[pltpu.VMEM((tm, n_hy), F32), pltpu.VMEM((8, n_hy), F32)],
        compiler_params=_cparams(("arbitrary",)),
        name="inproj_shortconv",
    )(x2, g, w_bf, conv_w, conv_b)


def _fft_dims(seq):
    n = 2 * seq
    n1 = min(DFT_N1_MAX, 1 << (n.bit_length() // 2))
    n2 = n // n1
    assert n1 * n2 == n and n1 % 2 == 0
    return n, n1, n2


def _hi_lo_const(m):
    m = np.asarray(m, np.float64)
    hi = jnp.asarray(m, F32).astype(BF16)
    lo = (jnp.asarray(m, F32) - hi.astype(F32)).astype(BF16)
    return hi, lo


def _fft_tables(seq):
    n, n1, n2 = _fft_dims(seq)
    r1 = n1 // 2
    k1 = jnp.arange(n1, dtype=I32)
    j = jnp.arange(n2, dtype=I32)

    at = ((j[:, None] * k1[None, :]) % n).astype(F32) * (2.0 * math.pi / n)
    ct, st = jnp.cos(at), jnp.sin(at)

    def cos_sin(n1_count, k1_first):
        a1 = ((k1[:, None] * jnp.arange(n1_count, dtype=I32)[None, :]) % n1).astype(F32) * (2.0 * math.pi / n1)
        cf, sf = jnp.cos(a1), jnp.sin(a1)
        if k1_first:
            cf, sf, ctb, stb = cf[None], sf[None], ct[:, :, None], st[:, :, None]
        else:
            cf, sf, ctb, stb = cf.T[None], sf.T[None], ct[:, None, :], st[:, None, :]
        return cf * ctb - sf * stb, sf * ctb + cf * stb

    cat = jnp.concatenate
    c, s = cos_sin(r1, True)
    mat_a = cat([cat([c, s], axis=2), cat([-s, c], axis=2)], axis=1).astype(BF16)
    c, s = cos_sin(n1, True)
    mat_a_real = cat([c, -s], axis=1).astype(BF16)
    c, s = cos_sin(r1, False)
    mat_c = cat([cat([c, -s], axis=2), cat([s, c], axis=2)], axis=1).astype(BF16)
    a2 = 2.0 * np.pi * np.outer(np.arange(n2), np.arange(n2)) / n2
    gr, gi = np.cos(a2), -np.sin(a2)
    mat_f = np.block([[gr, -gi], [gi, gr]])
    mat_i = np.block([[gr, gi], [-gi, gr]])
    return dict(n=n, n1=n1, n2=n2, mat_a=mat_a, mat_a_real=mat_a_real, mat_c=mat_c,
                mat_f=_hi_lo_const(mat_f), mat_i=_hi_lo_const(mat_i))


def _filter_body(z_ref, w1_ref, b1_ref, f1_ref, w2_ref, b2_ref, f2_ref, w3_ref, dl_ref, h_ref, s_ref, *, seq):
    i = pl.program_id(0)
    tl = z_ref.shape[0]
    z = z_ref[...]
    a = jnp.sin(f1_ref[...] * (_dot3(z, w1_ref[...]) + b1_ref[...]))
    a = jnp.sin(f2_ref[...] * (_dot3(a, w2_ref[...]) + b2_ref[...]))
    h = _dot(a.astype(BF16), w3_ref[...].astype(BF16))
    win = jnp.exp(-z[:, 0:1] * dl_ref[...])
    h = h * jnp.concatenate([win] * HY_ORDER, axis=1)
    row = i * tl + lax.broadcasted_iota(I32, (tl, 1), 0)
    h = jnp.where(row == seq, 0.0, h)
    h_ref[...] = h

    @pl.when(i == 0)
    def _():
        s_ref[...] = jnp.zeros_like(s_ref)

    s_ref[...] += jnp.broadcast_to(jnp.sum(jnp.abs(h), axis=0, keepdims=True), s_ref.shape)


def _hyena_filter_time(seq, w1, b1, f1, w2, b2, f2, w3, tl=512):
    n = 2 * seq
    tl = min(tl, seq)
    idx = jnp.arange(n, dtype=I32)
    p = jnp.minimum(jnp.where(idx < seq, idx, n - idx), seq - 1).astype(F32)
    t = p / float(seq - 1)
    ang = 2.0 * math.pi * p / seq
    bands = jnp.linspace(1e-4, HY_BANDS - 1, HY_BANDS, dtype=F32)
    phase = ang[:, None] * bands[None, :]
    emb = jnp.concatenate([t[:, None], jnp.cos(phase), -jnp.sin(phase)], axis=-1)
    emb = jnp.pad(emb, ((0, 0), (0, EMB_PAD - emb.shape[1])))
    w1p = jnp.pad(w1.astype(F32), ((0, EMB_PAD - w1.shape[0]), (0, 0)))
    w3d = w3.astype(F32).reshape(HY_FILTER_WIDTH, HY_ORDER, 2, HY_WIDTH).transpose(2, 0, 1, 3)
    w3d = w3d.reshape(2, HY_FILTER_WIDTH, HY_ORDER * HY_WIDTH)
    deltas = jnp.abs(jnp.linspace(math.log(HY_DECAY_TARGET) / HY_SLOW_DECAY,
                                  math.log(HY_DECAY_TARGET) / HY_FAST_DECAY, HY_WIDTH, dtype=F32))[None, :]
    fw = HY_FILTER_WIDTH
    nblk_half = seq // tl
    const = lambda i: (0, 0)
    return pl.pallas_call(
        functools.partial(_filter_body, seq=seq),
        grid=(n // tl,),
        in_specs=[pl.BlockSpec((tl, EMB_PAD), lambda i: (i, 0)),
                  pl.BlockSpec((EMB_PAD, fw), const), pl.BlockSpec((1, fw), const), pl.BlockSpec((1, fw), const),
                  pl.BlockSpec((fw, fw), const), pl.BlockSpec((1, fw), const), pl.BlockSpec((1, fw), const),
                  pl.BlockSpec((None, fw, HY_ORDER * HY_WIDTH), lambda i: (i // nblk_half, 0, 0)),
                  pl.BlockSpec((1, HY_WIDTH), const)],
        out_specs=[pl.BlockSpec((tl, HY_ORDER * HY_WIDTH), lambda i: (i, 0)),
                   pl.BlockSpec((8, HY_ORDER * HY_WIDTH), const)],
        out_shape=[jax.ShapeDtypeStruct((n, HY_ORDER * HY_WIDTH), F32),
                   jax.ShapeDtypeStruct((8, HY_ORDER * HY_WIDTH), F32)],
        compiler_params=_cparams(("arbitrary",)),
        name="hyena_filter",
    )(emb, w1p, b1.astype(F32)[None], f1.astype(F32)[None], w2.astype(F32), b2.astype(F32)[None],
      f2.astype(F32)[None], w3d, deltas)


def _stage_a_real_body(h_ref, m_ref, a_ref):
    n1 = h_ref.shape[0]
    h_t = jnp.swapaxes(h_ref[...], 0, 1)
    rs = [_dot(m_ref[j], h_t[j].astype(BF16)) for j in range(h_ref.shape[1])]
    r = jnp.swapaxes(jnp.stack(rs, axis=0), 0, 1)
    a_ref[0] = r[:n1]
    a_ref[1] = r[n1:]


def _stage_a_real(h, tabs, n2c=8, cw=512):
    n1, n2 = tabs["n1"], tabs["n2"]
    c = h.shape[1]
    h3 = h.reshape(n1, n2, c)
    mat = tabs["mat_a_real"]
    return pl.pallas_call(
        _stage_a_real_body,
        grid=(n2 // n2c, c // cw),
        in_specs=[pl.BlockSpec((n1, n2c, cw), lambda j, k: (0, j, k)),
                  pl.BlockSpec((n2c,) + mat.shape[1:], lambda j, k: (j, 0, 0))],
        out_specs=pl.BlockSpec((2, n1, n2c, cw), lambda j, k: (0, 0, j, k)),
        out_shape=jax.ShapeDtypeStruct((2, n1, n2, c), F32),
        compiler_params=_cparams(("parallel", "parallel")),
        name="dft_a_filter",
    )(h3, mat)


def _pack_pair(re, im):
    hi = lax.bitcast_convert_type(re.astype(BF16).astype(F32), I32)
    lo = lax.bitcast_convert_type(im.astype(BF16).astype(F32), I32)
    return lax.bitcast_convert_type(hi | lax.shift_right_logical(lo, 16), F32)


def _unpack_pair(packed):
    word = lax.bitcast_convert_type(packed, I32)
    re = lax.bitcast_convert_type(word & jnp.int32(-65536), F32)
    im = lax.bitcast_convert_type(lax.shift_left(word, 16), F32)
    return re, im


def _stage_a_body(u_ref, m_ref, a_ref):
    _, r1, n2c, cw = u_ref.shape
    n1 = 2 * r1
    u_t = jnp.swapaxes(u_ref[...].reshape(n1, n2c, cw), 0, 1)
    packed = []
    for j in range(n2c):
        r = _dot(m_ref[j], u_t[j].astype(BF16))
        packed.append(_pack_pair(r[:n1], r[n1:]))
    a_ref[...] = jnp.swapaxes(jnp.stack(packed, axis=0), 0, 1)


def _stage_a(u5, col, tabs, n2c=8):
    n1, n2 = tabs["n1"], tabs["n2"]
    npair, _, r1, _, _ = u5.shape
    cw = HY_WIDTH
    mat = tabs["mat_a"]
    return pl.pallas_call(
        _stage_a_body,
        grid=(npair, n2 // n2c),
        in_specs=[pl.BlockSpec((None, 2, r1, n2c, cw), lambda p, j: (p, 0, 0, j, col)),
                  pl.BlockSpec((n2c,) + mat.shape[1:], lambda p, j: (j, 0, 0))],
        out_specs=pl.BlockSpec((None, n1, n2c, cw), lambda p, j: (p, 0, j, 0)),
        out_shape=jax.ShapeDtypeStruct((npair, n1, n2, cw), F32),
        compiler_params=_cparams(("parallel", "parallel")),
        name="dft_a",
    )(u5, mat)


def _stage_b_filter_body(a_ref, g_ref, s_ref, kr_ref, ki_ref, *, n):
    n2 = a_ref.shape[2]
    scale = 1.0 / (s_ref[0:1, :] * float(n))
    for kk in range(a_ref.shape[1]):
        d = _dot(g_ref[...], jnp.concatenate([a_ref[0, kk], a_ref[1, kk]], axis=0).astype(BF16))
        kr_ref[kk] = d[:n2] * scale
        ki_ref[kk] = d[n2:] * scale


def _stage_b_filter(a, s, tabs):
    n, n1, n2 = tabs["n"], tabs["n1"], tabs["n2"]
    c = a.shape[-1]
    kb = max(1, min(n1, 512 // n2))
    g = tabs["mat_f"][0]
    return pl.pallas_call(
        functools.partial(_stage_b_filter_body, n=n),
        grid=(n1 // kb,),
        in_specs=[pl.BlockSpec((2, kb, n2, c), lambda k: (0, k, 0, 0)),
                  pl.BlockSpec(g.shape, lambda k: (0, 0)),
                  pl.BlockSpec(s.shape, lambda k: (0, 0))],
        out_specs=[pl.BlockSpec((kb, n2, c), lambda k: (k, 0, 0)),
                   pl.BlockSpec((kb, n2, c), lambda k: (k, 0, 0))],
        out_shape=[jax.ShapeDtypeStruct((n1, n2, c), F32), jax.ShapeDtypeStruct((n1, n2, c), F32)],
        compiler_params=_cparams(("parallel",)),
        name="dft_b_filter",
    )(a, g, s)


def _stage_b_body(a_ref, kr_ref, ki_ref, gf_ref, gi_ref, c_ref):
    n2 = a_ref.shape[1]
    for kk in range(a_ref.shape[0]):
        d = _dot(gf_ref[...], jnp.concatenate(_unpack_pair(a_ref[kk]), axis=0).astype(BF16))
        dr, di = d[:n2], d[n2:]
        kr, ki = kr_ref[kk], ki_ref[kk]
        yr = dr * kr - di * ki
        yi = dr * ki + di * kr
        e = _dot(gi_ref[...], jnp.concatenate([yr, yi], axis=0).astype(BF16))
        c_ref[kk] = _pack_pair(e[:n2], e[n2:])


def _stage_b(a, kr, ki, order, tabs):
    n1, n2 = tabs["n1"], tabs["n2"]
    npair = a.shape[0]
    cw = HY_WIDTH
    kb = max(1, min(n1, 1024 // n2))
    gf = tabs["mat_f"][0]
    gi = tabs["mat_i"][0]
    const = lambda k, p: (0, 0)
    return pl.pallas_call(
        _stage_b_body,
        grid=(n1 // kb, npair),
        in_specs=[pl.BlockSpec((None, kb, n2, cw), lambda k, p: (p, k, 0, 0)),
                  pl.BlockSpec((kb, n2, cw), lambda k, p: (k, 0, order)),
                  pl.BlockSpec((kb, n2, cw), lambda k, p: (k, 0, order)),
                  pl.BlockSpec(gf.shape, const), pl.BlockSpec(gi.shape, const)],
        out_specs=pl.BlockSpec((None, kb, n2, cw), lambda k, p: (p, k, 0, 0)),
        out_shape=jax.ShapeDtypeStruct(a.shape, F32),
        compiler_params=_cparams(("parallel", "parallel")),
        name="dft_b",
    )(a, kr, ki, gf, gi)


def _stage_c_body(c_ref, u_ref, g_ref, skip_ref, mc_ref, ma_ref, y_ref, *rest, fuse_a):
    _, r1, n2c, cw = u_ref.shape
    n1 = 2 * r1
    ys = []
    c_t = jnp.swapaxes(c_ref[...], 0, 1)
    for j in range(n2c):
        cc = jnp.concatenate(_unpack_pair(c_t[j]), axis=0)
        ys.append(_dot(mc_ref[j], cc.astype(BF16)))
    y = jnp.swapaxes(jnp.stack(ys, axis=0), 0, 1)
    u = u_ref[...].reshape(n1, n2c, cw)
    g = g_ref[...].reshape(n1, n2c, cw)
    yo = g * (y + u * skip_ref[...].reshape(1, 1, cw))
    y_ref[...] = yo.reshape(2, r1, n2c, cw).astype(y_ref.dtype)
    if fuse_a:
        a_ref = rest[0]
        yo_t = jnp.swapaxes(yo, 0, 1)
        packed = []
        for j in range(n2c):
            r = _dot(ma_ref[j], yo_t[j].astype(BF16))
            packed.append(_pack_pair(r[:n1], r[n1:]))
        a_ref[...] = jnp.swapaxes(jnp.stack(packed, axis=0), 0, 1)


def _stage_c(c, u5, u_col, g5, g_col, skip, tabs, fuse_a, out_dtype, n2c=8):
    n1, n2 = tabs["n1"], tabs["n2"]
    npair, _, r1, _, _ = u5.shape
    cw = HY_WIDTH
    mc = tabs["mat_c"]
    ma = tabs["mat_a"]
    const = lambda p, j: (0, 0)
    per_j = lambda m: pl.BlockSpec((n2c,) + m.shape[1:], lambda p, j: (j, 0, 0))
    out_shape = [jax.ShapeDtypeStruct((npair, 2, r1, n2, cw), out_dtype)]
    out_specs = [pl.BlockSpec((None, 2, r1, n2c, cw), lambda p, j: (p, 0, 0, j, 0))]
    if fuse_a:
        out_shape.append(jax.ShapeDtypeStruct((npair, n1, n2, cw), F32))
        out_specs.append(pl.BlockSpec((None, n1, n2c, cw), lambda p, j: (p, 0, j, 0)))
    return pl.pallas_call(
        functools.partial(_stage_c_body, fuse_a=fuse_a),
        grid=(npair, n2 // n2c),
        in_specs=[pl.BlockSpec((None, n1, n2c, cw), lambda p, j: (p, 0, j, 0)),
                  pl.BlockSpec((None, 2, r1, n2c, cw), lambda p, j: (p, 0, 0, j, u_col)),
                  pl.BlockSpec((None, 2, r1, n2c, cw), lambda p, j: (p, 0, 0, j, g_col)),
                  pl.BlockSpec((1, cw), const), per_j(mc), per_j(ma)],
        out_specs=out_specs,
        out_shape=out_shape,
        compiler_params=_cparams(("parallel", "parallel")),
        name="dft_c",
    )(c, u5, g5, skip, mc, ma)


def _hyena(zc3, kr, ki, skip, tabs):
    bsz, seq, _ = zc3.shape
    n1, n2 = tabs["n1"], tabs["n2"]
    r1 = n1 // 2
    z5 = zc3.reshape(bsz // 2, 2, r1, n2, 3 * HY_WIDTH)
    a = _stage_a(z5, 2, tabs)
    c = _stage_b(a, kr, ki, 0, tabs)
    y1, a2 = _stage_c(c, z5, 2, z5, 0, skip[0:1].astype(F32), tabs, True, F32)
    c2 = _stage_b(a2, kr, ki, 1, tabs)
    (y2,) = _stage_c(c2, y1, 0, z5, 1, skip[1:2].astype(F32), tabs, False, BF16)
    return y2.reshape(bsz * seq, HY_WIDTH)


def _retention_body(sc_ref, qf_ref, kf_ref, vf_ref, cf_ref, sf_ref, qb_ref, kb_ref, vb_ref, cb_ref, sb_ref,
                    of_ref, ob_ref, state, dmat, qwt, kwt):
    ch = RET_CHUNK
    dh = RET_HEAD_DIM

    @pl.when(pl.program_id(1) == 0)
    def _():
        state[...] = jnp.zeros_like(state)
        ci = lax.broadcasted_iota(I32, (ch, ch), 0).astype(F32)
        mi = lax.broadcasted_iota(I32, (ch, ch), 1).astype(F32)
        lag = ci - mi
        for d in range(2):
            for h in range(RET_HEADS):
                lg = sc_ref[d * RET_HEADS + h]
                if d == 0:
                    dmat[d, h] = jnp.where(lag >= 0, jnp.exp(lg * jnp.maximum(lag, 0.0)), 0.0)
                    qwt[d, h] = jnp.exp(lg * (ci + 1.0))
                    kwt[d, h] = jnp.exp(lg * (ch - 1.0 - ci))
                else:
                    dmat[d, h] = jnp.where(lag < 0, jnp.exp(lg * jnp.maximum(-lag, 0.0)), 0.0)
                    qwt[d, h] = jnp.exp(lg * (ch - ci))
                    kwt[d, h] = jnp.exp(lg * ci)

    scale = RET_HEAD_DIM ** -0.5
    refs = ((qf_ref, kf_ref, vf_ref, cf_ref, sf_ref, of_ref), (qb_ref, kb_ref, vb_ref, cb_ref, sb_ref, ob_ref))
    work = [(d, h) for d in range(2) for h in range(RET_HEADS)]
    per_step = qf_ref.shape[0] // ch
    for c in range(per_step):
        rows = (slice(c * ch, (c + 1) * ch), slice((per_step - 1 - c) * ch, (per_step - c) * ch))
        qs, ks, qws, kws, vs = {}, {}, {}, {}, {}
        for d, h in work:
            q_ref, k_ref, v_ref, c_ref, s_ref, _ = refs[d]
            sl = slice(h * dh, (h + 1) * dh)
            cosf = c_ref[rows[d], :]
            sinf = s_ref[rows[d], :]
            q = q_ref[rows[d], sl].astype(F32)
            k = k_ref[rows[d], sl].astype(F32)
            q = q * cosf + pltpu.roll(q, dh // 2, axis=1) * sinf
            k = (k * cosf + pltpu.roll(k, dh // 2, axis=1) * sinf) * scale
            qs[d, h], ks[d, h] = q.astype(BF16), k.astype(BF16)
            qws[d, h], kws[d, h] = (q * qwt[d, h]).astype(BF16), (k * kwt[d, h]).astype(BF16)
            vs[d, h] = v_ref[rows[d], sl]
        scores = {dh_: _dot_nt(qs[dh_], ks[dh_]) for dh_ in work}
        for d, h in work:
            lhs = jnp.concatenate([(scores[d, h] * dmat[d, h]).astype(BF16), qws[d, h]], axis=1)
            rhs = jnp.concatenate([vs[d, h], state[d, h].astype(BF16)], axis=0)
            refs[d][5][rows[d], h * dh:(h + 1) * dh] = _dot(lhs, rhs).astype(BF16)
        for d, h in work:
            cdec = sc_ref[2 * RET_HEADS + d * RET_HEADS + h]
            state[d, h] = state[d, h] * cdec + _dot_tn(kws[d, h], vs[d, h])


def _retention(zret3, decay_logit):
    bsz, seq, _ = zret3.shape
    ch = RET_CHUNK
    nc = seq // ch
    half = RET_HEAD_DIM // 2
    inv_freq = ROPE_BASE ** (-jnp.arange(half, dtype=F32) / half)
    ang = jnp.arange(seq, dtype=F32)[:, None] * inv_freq[None, :]
    cosf = jnp.concatenate([jnp.cos(ang), jnp.cos(ang)], axis=1)
    sinf = jnp.concatenate([-jnp.sin(ang), jnp.sin(ang)], axis=1)
    log_g = jax.nn.log_sigmoid(decay_logit.astype(F32)).reshape(-1)
    scal = jnp.concatenate([log_g, jnp.exp(log_g * ch)])
    w = RET_WIDTH
    per_step = 2 if nc % 2 == 0 else 1
    nb = nc // per_step
    blk = per_step * ch
    fwd = lambda col: pl.BlockSpec((None, blk, w), lambda b, n, sc: (b, n, col))
    bwd = lambda col: pl.BlockSpec((None, blk, w), lambda b, n, sc: (b, nb - 1 - n, col))
    rope_f = pl.BlockSpec((blk, RET_HEAD_DIM), lambda b, n, sc: (n, 0))
    rope_b = pl.BlockSpec((blk, RET_HEAD_DIM), lambda b, n, sc: (nb - 1 - n, 0))
    grid_spec = pltpu.PrefetchScalarGridSpec(
        num_scalar_prefetch=1,
        grid=(bsz, nb),
        in_specs=[fwd(0), fwd(1), fwd(2), rope_f, rope_f, bwd(0), bwd(1), bwd(2), rope_b, rope_b],
        out_specs=[pl.BlockSpec((None, blk, w), lambda b, n, sc: (b, n, 0)),
                   pl.BlockSpec((None, blk, w), lambda b, n, sc: (b, nb - 1 - n, 0))],
        scratch_shapes=[pltpu.VMEM((2, RET_HEADS, RET_HEAD_DIM, RET_HEAD_DIM), F32),
                        pltpu.VMEM((2, RET_HEADS, ch, ch), F32),
                        pltpu.VMEM((2, RET_HEADS, ch, RET_HEAD_DIM), F32),
                        pltpu.VMEM((2, RET_HEADS, ch, RET_HEAD_DIM), F32)],
    )
    return pl.pallas_call(
        _retention_body,
        grid_spec=grid_spec,
        out_shape=[jax.ShapeDtypeStruct((bsz, seq, w), BF16), jax.ShapeDtypeStruct((bsz, seq, w), BF16)],
        compiler_params=_cparams(("arbitrary", "arbitrary")),
        name="retention",
    )(scal, zret3, zret3, zret3, cosf, sinf, zret3, zret3, zret3, cosf, sinf)


def _outproj_body(x_ref, yhy_ref, of_ref, ob_ref, gr_ref, ghy_ref, gret_ref, whyo_ref, wreto_ref, wo_ref,
                  n2g_ref, wrt_ref, x1_ref, pt_ref):
    dh = RET_HEAD_DIM
    o = of_ref[...].astype(F32) + ob_ref[...].astype(F32)
    parts = []
    for h in range(RET_HEADS):
        oh = o[:, h * dh:(h + 1) * dh]
        parts.append(oh * lax.rsqrt(jnp.mean(oh * oh, axis=-1, keepdims=True) + EPS))
    on = jnp.concatenate(parts, axis=1)
    gr = gr_ref[...].astype(F32)
    ret = (gr * jax.nn.sigmoid(gr)) * on
    y_ret = _dot(ret.astype(BF16), wreto_ref[...])
    y_hy = _dot(yhy_ref[...], whyo_ref[...])
    merged = jax.nn.sigmoid(ghy_ref[...].astype(F32)) * y_hy + jax.nn.sigmoid(gret_ref[...].astype(F32)) * y_ret
    x1 = x_ref[...] + _dot(merged.astype(BF16), wo_ref[...])
    x1_ref[...] = x1
    h2 = x1 * lax.rsqrt(jnp.mean(x1 * x1, axis=-1, keepdims=True) + EPS) * n2g_ref[...]
    logits = _dot_nt(wrt_ref[...], h2.astype(BF16))
    m = jnp.max(logits, axis=0, keepdims=True)
    e = jnp.exp(logits - m)
    pt_ref[...] = e / jnp.sum(e, axis=0, keepdims=True)


def _outproj(x2, yhy, o_f, o_b, zret, zg, whyo, wreto, wo, n2g, wrt, tm=512):
    n_tok = x2.shape[0]
    d = D_MODEL
    row = lambda w, col=0: pl.BlockSpec((tm, w), lambda i: (i, col))
    const = lambda shape: pl.BlockSpec(shape, lambda i: (0, 0))
    return pl.pallas_call(
        _outproj_body,
        grid=(n_tok // tm,),
        in_specs=[row(d), row(HY_WIDTH), row(RET_WIDTH), row(RET_WIDTH), row(RET_WIDTH, 3), row(d, 0), row(d, 1),
                  const((HY_WIDTH, d)), const((RET_WIDTH, d)), const((d, d)), const((1, d)), const((N_EXPERTS, d))],
        out_specs=[row(d), pl.BlockSpec((N_EXPERTS, tm), lambda i: (0, i))],
        out_shape=[jax.ShapeDtypeStruct((n_tok, d), F32), jax.ShapeDtypeStruct((N_EXPERTS, n_tok), F32)],
        compiler_params=_cparams(("parallel",)),
        name="outproj_router",
    )(x2, yhy, o_f, o_b, zret, zg, zg, whyo, wreto, wo, n2g, wrt)


def _select_body(p_ref, upper_ref, lower_ref, lowinc_ref, eye_ref, pos_ref, idx_ref, gate_ref, lo_ref, *, cap):
    rows = p_ref.shape[0]
    p = p_ref[...]
    bits = lax.bitcast_convert_type(p, I32)

    def count(mask):
        return jnp.sum(jnp.sum(mask.astype(F32), axis=1, keepdims=True), axis=0, keepdims=True)

    def bit_step(i, thr):
        cand = thr | jnp.left_shift(jnp.int32(1), 30 - i)
        return jnp.where(count(bits >= cand) >= cap, cand, thr)

    thr = lax.fori_loop(0, 31, bit_step, jnp.zeros((1, 1), I32))
    gt = bits > thr
    eq = bits == thr
    need = cap - count(gt)

    def prefix(mask_f):
        incl = _dot(mask_f.astype(BF16), upper_ref[...])
        tot = jnp.broadcast_to(incl[:, LANES - 1:LANES], incl.shape)
        base = _dot(lower_ref[...], tot.astype(BF16))
        return incl, base, tot

    eq_f = eq.astype(F32)
    incl_e, base_e, _ = prefix(eq_f)
    sel = gt | (eq & (base_e + incl_e - eq_f < need))
    sel_f = sel.astype(F32)
    incl, base, tot = prefix(sel_f)
    pos_ref[...] = jnp.where(sel, (base + incl - 1.0).astype(I32), -1)
    lo_ref[...] = base.astype(I32)

    rowend = base + tot
    incl_t = _dot_nt(lowinc_ref[...], sel_f.astype(BF16)).astype(BF16)
    p_t = []
    rem = p
    for _ in range(3):
        part = rem.astype(BF16)
        rem = rem - part.astype(F32)
        p_t.append(_dot_nt(eye_ref[...], part).astype(BF16))
    table = jnp.concatenate([incl_t] + p_t, axis=0)
    groups = 2 if (cap // LANES) % 2 == 0 else 1
    sw = groups * LANES
    wide = lambda a: jnp.concatenate([a] * groups, axis=1)
    rowend_w, tot_w = wide(rowend), wide(tot)
    r_iota = lax.broadcasted_iota(I32, (rows, sw), 0).astype(F32)
    lane_iota = lax.broadcasted_iota(I32, (LANES, sw), 0).astype(F32)

    def slot_tile(ts, carry):
        s = (ts * sw + lax.broadcasted_iota(I32, (1, sw), 1)).astype(F32)
        done = rowend_w <= s
        row = jnp.sum(done.astype(F32), axis=0, keepdims=True)
        before = jnp.sum(jnp.where(done, tot_w, 0.0), axis=0, keepdims=True)
        onehot_t = (r_iota == row).astype(BF16)
        got = _dot(table, onehot_t)
        g_t = got[:LANES]
        lane = jnp.sum((g_t <= s - before).astype(F32), axis=0, keepdims=True)
        tok = (row * LANES + lane).astype(I32)
        p_row = got[LANES:2 * LANES] + got[2 * LANES:3 * LANES] + got[3 * LANES:]
        gate = jnp.sum(jnp.where(lane_iota == lane, p_row, 0.0), axis=0, keepdims=True)
        for k in range(groups):
            idx_ref[pl.ds(ts * groups + k, 1), :] = tok[:, k * LANES:(k + 1) * LANES]
            gate_ref[pl.ds(ts * groups + k, 1), :] = gate[:, k * LANES:(k + 1) * LANES]
        return carry

    lax.fori_loop(0, cap // sw, slot_tile, 0)


def _select(pt3, cap):
    n_e, rows, _ = pt3.shape
    ii = np.arange(LANES)
    upper = jnp.asarray(ii[:, None] <= ii[None, :], BF16)
    lowinc = jnp.asarray(ii[None, :] <= ii[:, None], BF16)
    eye = jnp.asarray(ii[None, :] == ii[:, None], BF16)
    rr = np.arange(rows)
    lower = jnp.asarray(rr[None, :] < rr[:, None], BF16)
    const = lambda shape: pl.BlockSpec(shape, lambda e: (0, 0))
    tok_spec = pl.BlockSpec((None, rows, LANES), lambda e: (e, 0, 0))
    slot_spec = pl.BlockSpec((None, cap // LANES, LANES), lambda e: (e, 0, 0))
    return pl.pallas_call(
        functools.partial(_select_body, cap=cap),
        grid=(n_e,),
        in_specs=[tok_spec, const((LANES, LANES)), const((rows, rows)), const((LANES, LANES)), const((LANES, LANES))],
        out_specs=[tok_spec, slot_spec, slot_spec, tok_spec],
        out_shape=[jax.ShapeDtypeStruct((n_e, rows, LANES), I32),
                   jax.ShapeDtypeStruct((n_e, cap // LANES, LANES), I32),
                   jax.ShapeDtypeStruct((n_e, cap // LANES, LANES), F32),
                   jax.ShapeDtypeStruct((n_e, rows, LANES), I32)],
        compiler_params=_cparams(("parallel",)),
        name="expert_select",
    )(pt3, upper, lower, lowinc, eye)


def _ffn_body(idx_ref, h_hbm, n2g_ref, gate_ref, wg_ref, wu_ref, wd_ref, ye_ref, xbuf, xb, sems, *,
              tiles_per_expert):
    s = xbuf.shape[1]
    step = pl.program_id(0) * tiles_per_expert + pl.program_id(1)
    n_steps = pl.num_programs(0) * tiles_per_expert
    cur = step % 2

    def row_copy(st, i, buf):
        tok = idx_ref[st * s + i]
        return pltpu.make_async_copy(h_hbm.at[pl.ds(tok, 1), :], xbuf.at[buf, pl.ds(i, 1), :], sems.at[buf])

    def wait_rows(buf):
        pltpu.make_async_copy(h_hbm.at[pl.ds(0, s), :], xbuf.at[buf], sems.at[buf]).wait()

    @pl.when(step == 0)
    def _():
        def one(i, carry):
            row_copy(0, i, 0).start()
            return carry
        lax.fori_loop(0, s, one, 0, unroll=8)

    @pl.when(step + 1 < n_steps)
    def _():
        for i in range(s):
            row_copy(step + 1, i, 1 - cur).start()

    wait_rows(cur)
    x = xbuf[cur]
    xb[...] = (x * lax.rsqrt(jnp.mean(x * x, axis=-1, keepdims=True) + EPS) * n2g_ref[...]).astype(BF16)
    y = None
    for f0 in range(0, EXPERT_FF, FF_CHUNK):
        f1 = min(f0 + FF_CHUNK, EXPERT_FF)
        a = _dot(xb[...], wg_ref[:, f0:f1])
        b = _dot(xb[...], wu_ref[:, f0:f1])
        hid = ((a * jax.nn.sigmoid(a)) * b).astype(BF16)
        part = _dot(hid, wd_ref[f0:f1, :])
        y = part if y is None else y + part
    for k in range(s // LANES):
        col = jnp.transpose(jnp.broadcast_to(gate_ref[k:k + 1, :], (LANES, LANES)))[:, 0:1]
        yk = y[k * LANES:(k + 1) * LANES] * col
        ye_ref[k * LANES:(k + 1) * LANES, :] = _pack_pair(yk[:, :D_MODEL // 2], yk[:, D_MODEL // 2:])


def _expert_ffn(idx_flat, x1, n2g, gates, wg, wu, wd, cap, s=512):
    s = min(s, cap)
    tiles = cap // s
    g3 = gates.reshape(N_EXPERTS * tiles, s // LANES, LANES)
    grid_spec = pltpu.PrefetchScalarGridSpec(
        num_scalar_prefetch=1,
        grid=(N_EXPERTS, tiles),
        in_specs=[pl.BlockSpec(memory_space=pl.ANY),
                  pl.BlockSpec((1, D_MODEL), lambda e, j, idx: (0, 0)),
                  pl.BlockSpec((None, s // LANES, LANES), lambda e, j, idx: (e * tiles + j, 0, 0)),
                  pl.BlockSpec((None, D_MODEL, EXPERT_FF), lambda e, j, idx: (e, 0, 0)),
                  pl.BlockSpec((None, D_MODEL, EXPERT_FF), lambda e, j, idx: (e, 0, 0)),
                  pl.BlockSpec((None, EXPERT_FF, D_MODEL), lambda e, j, idx: (e, 0, 0))],
        out_specs=pl.BlockSpec((s, D_MODEL // 2), lambda e, j, idx: (e * tiles + j, 0)),
        scratch_shapes=[pltpu.VMEM((2, s, D_MODEL), F32), pltpu.VMEM((s, D_MODEL), BF16),
                        pltpu.SemaphoreType.DMA((2,))],
    )
    return pl.pallas_call(
        functools.partial(_ffn_body, tiles_per_expert=tiles),
        grid_spec=grid_spec,
        out_shape=jax.ShapeDtypeStruct((N_EXPERTS * cap, D_MODEL // 2), F32),
        compiler_params=_cparams(("arbitrary", "arbitrary")),
        name="expert_ffn",
    )(idx_flat, x1, n2g, g3, wg, wu, wd)


def _combine_body(lo_ref, np_ref, x1_ref, pos_ref, nfg_ref, expand_ref, lov_ref, ye_hbm, o_ref, win, sems, *, cap,
                  rows_total, n_rows):
    step = pl.program_id(0)
    n_steps = pl.num_programs(0)
    groups = x1_ref.shape[0] // LANES
    w = COMBINE_WIN
    cur = step % 2

    def copies(rr, m, buf, g):
        out = []
        for e in range(N_EXPERTS):
            first = e * cap + lo_ref[e * n_rows + rr]
            intended = (first // COMBINE_ALIGN) * COMBINE_ALIGN + m * w
            actual = pl.multiple_of(jnp.minimum(intended, rows_total - w), COMBINE_ALIGN)
            out.append(pltpu.make_async_copy(ye_hbm.at[pl.ds(actual, w), :], win.at[buf, g, pl.ds(e * w, w), :],
                                             sems.at[buf, g, e]))
        return out

    def wait_windows(buf, g):
        for e in range(N_EXPERTS):
            pltpu.make_async_copy(ye_hbm.at[pl.ds(0, w), :], win.at[buf, g, pl.ds(e * w, w), :],
                                  sems.at[buf, g, e]).wait()

    def contribution(rr, m, buf, g):
        lane = lax.broadcasted_iota(I32, (1, N_EXPERTS), 1)
        first = lane * cap + lov_ref[pl.ds(rr, 1), :]
        intended = (first & jnp.int32(-COMBINE_ALIGN)) + m * w
        actual = jnp.minimum(intended, rows_total - w)
        pos = pos_ref[g * LANES:(g + 1) * LANES, :]
        glob = pos + lane * cap
        rel = glob - intended
        valid = (pos >= 0) & (rel >= 0) & (rel < w)
        local = jnp.where(valid, glob - actual, -1).astype(F32).astype(BF16)
        spread = _dot(local, expand_ref[...])
        col = (lax.broadcasted_iota(I32, (1, N_EXPERTS * w), 1) & (w - 1)).astype(F32)
        onehot = (spread == col).astype(BF16)
        lo_cols, hi_cols = _unpack_pair(win[buf, g])
        return jnp.concatenate([_dot(onehot, lo_cols.astype(BF16)), _dot(onehot, hi_cols.astype(BF16))], axis=1)

    @pl.when(step == 0)
    def _():
        for g in range(groups):
            for cp in copies(g, 0, 0, g):
                cp.start()

    @pl.when(step + 1 < n_steps)
    def _():
        for g in range(groups):
            for cp in copies((step + 1) * groups + g, 0, 1 - cur, g):
                cp.start()

    for g in range(groups):
        rr = step * groups + g
        wait_windows(cur, g)
        acc = x1_ref[g * LANES:(g + 1) * LANES, :] + contribution(rr, 0, cur, g)

        def extra_pass(m, acc, rr=rr, g=g):
            for cp in copies(rr, m, cur, g):
                cp.start()
            wait_windows(cur, g)
            return acc + contribution(rr, m, cur, g)

        acc = lax.fori_loop(1, np_ref[rr], extra_pass, acc)
        o_ref[g * LANES:(g + 1) * LANES, :] = (
            acc * lax.rsqrt(jnp.mean(acc * acc, axis=-1, keepdims=True) + EPS) * nfg_ref[...])


def _combine(lo, x1, pos_t, nfg, ye, cap):
    n_tok = x1.shape[0]
    w = COMBINE_WIN
    rows_total = ye.shape[0]
    n_rows = n_tok // LANES
    groups = 2 if n_rows % 2 == 0 else 1
    tm = groups * LANES
    nxt =jnp.concatenate([lo[:, 1:], jnp.full((N_EXPERTS, 1), cap, I32)], axis=1)
    span = lo % COMBINE_ALIGN + (nxt - lo)
    n_pass = jnp.maximum(jnp.max((span + w - 1) // w, axis=0), 1).astype(I32)
    ee = np.arange(N_EXPERTS)
    expand = jnp.asarray(ee[:, None] == (np.arange(N_EXPERTS * w) // w)[None, :], BF16)
    grid_spec = pltpu.PrefetchScalarGridSpec(
        num_scalar_prefetch=2,
        grid=(n_rows // groups,),
        in_specs=[pl.BlockSpec((tm, D_MODEL), lambda i, lo_, np_: (i, 0)),
                  pl.BlockSpec((tm, N_EXPERTS), lambda i, lo_, np_: (i, 0)),
                  pl.BlockSpec((1, D_MODEL), lambda i, lo_, np_: (0, 0)),
                  pl.BlockSpec((N_EXPERTS, N_EXPERTS * w), lambda i, lo_, np_: (0, 0)),
                  pl.BlockSpec((n_rows, N_EXPERTS), lambda i, lo_, np_: (0, 0)),
                  pl.BlockSpec(memory_space=pl.ANY)],
        out_specs=pl.BlockSpec((tm, D_MODEL), lambda i, lo_, np_: (i, 0)),
        scratch_shapes=[pltpu.VMEM((2, groups, N_EXPERTS * w, D_MODEL // 2), F32),
                        pltpu.SemaphoreType.DMA((2, groups, N_EXPERTS))],
    )
    return pl.pallas_call(
        functools.partial(_combine_body, cap=cap, rows_total=rows_total, n_rows=n_rows),
        grid_spec=grid_spec,
        out_shape=jax.ShapeDtypeStruct((n_tok, D_MODEL), F32),
        compiler_params=_cparams(("arbitrary",)),
        name="moe_combine_norm",
    )(lo.reshape(-1), n_pass, x1, pos_t, nfg, expand, lo.T, ye)


def _trunk(x, w):
    bsz, seq, d = x.shape
    n_tok = bsz * seq
    x2 = x.reshape(n_tok, d)
    tabs = _fft_tables(seq)

    zc, zret, zg = _inproj(x2, w["norm1_g"], w["w_in"], w["hy_conv_w"], w["hy_conv_b"], seq)
    zc = zc.reshape(bsz, seq, -1)

    h_time, h_abs = _hyena_filter_time(seq, w["hy_w1"], w["hy_b1"], w["hy_freq1"], w["hy_w2"], w["hy_b2"],
                                       w["hy_freq2"], w["hy_w3"])
    kr, ki = _stage_b_filter(_stage_a_real(h_time, tabs), h_abs, tabs)
    yhy = _hyena(zc, kr, ki, w["hy_skip"], tabs)

    o_f, o_b = _retention(zret.reshape(bsz, seq, -1), w["ret_decay_logit"])
    x1, pt = _outproj(x2, yhy, o_f.reshape(n_tok, -1), o_b.reshape(n_tok, -1), zret, zg,
                      w["w_hy_out"], w["w_ret_out"], w["w_o"], w["norm2_g"], w["w_router_t"])

    cap = CAPACITY_FACTOR * n_tok // N_EXPERTS
    rows = n_tok // LANES
    pos, idx, gates, lo = _select(pt.reshape(N_EXPERTS, rows, LANES), cap)
    ye = _expert_ffn(idx.reshape(-1), x1, w["norm2_g"], gates, w["w_gate"], w["w_up"], w["w_down"], cap)
    pos_t = pos.reshape(N_EXPERTS, n_tok).T
    y = _combine(lo[:, :, 0], x1, pos_t, w["norm_f_g"], ye, cap)
    return y.reshape(bsz, seq, d)


def kernel(x_prompt, x_sample, norm1_g, w_in, hy_conv_w, hy_conv_b, hy_w1, hy_b1, hy_freq1, hy_w2, hy_b2, hy_freq2,
           hy_w3, hy_skip, ret_decay_logit, w_hy_out, w_ret_out, w_o, norm2_g, w_router, w_gate, w_up, w_down,
           norm_f_g):
    layer = 0
    w = dict(
        norm1_g=norm1_g[layer].astype(F32)[None], w_in=w_in[layer].astype(BF16),
        hy_conv_w=hy_conv_w[layer].astype(F32), hy_conv_b=hy_conv_b[layer].astype(F32)[None],
        hy_w1=hy_w1[layer], hy_b1=hy_b1[layer], hy_freq1=hy_freq1[layer], hy_w2=hy_w2[layer], hy_b2=hy_b2[layer],
        hy_freq2=hy_freq2[layer], hy_w3=hy_w3[layer], hy_skip=hy_skip[layer],
        ret_decay_logit=ret_decay_logit[layer],
        w_hy_out=w_hy_out[layer].astype(BF16), w_ret_out=w_ret_out[layer].astype(BF16), w_o=w_o[layer].astype(BF16),
        norm2_g=norm2_g[layer].astype(F32)[None], w_router_t=w_router[layer].T.astype(BF16),
        w_gate=w_gate[layer].astype(BF16), w_up=w_up[layer].astype(BF16), w_down=w_down[layer].astype(BF16),
        norm_f_g=norm_f_g.astype(F32)[None],
    )
    return _trunk(x_prompt, w), _trunk(x_sample, w)
```

```python
import functools
import math

import numpy as np
import jax
import jax.numpy as jnp
from jax import lax
from jax.experimental import pallas as pl
from jax.experimental.pallas import tpu as pltpu

F32 = jnp.float32
BF16 = jnp.bfloat16
I32 = jnp.int32

D_MODEL = 1024
HY_WIDTH = 512
HY_ORDER = 2
HY_BANDS = 16
HY_FILTER_WIDTH = 64
HY_FAST_DECAY = 0.3
HY_SLOW_DECAY = 1.5
HY_DECAY_TARGET = 1e-2
RET_WIDTH = 512
RET_HEADS = 4
RET_HEAD_DIM = 128
RET_CHUNK = 128
ROPE_BASE = 10000.0
N_EXPERTS = 16
EXPERT_FF = 1408
CAPACITY_FACTOR = 2
EPS = 1e-6

LANES = 128
SUBLANES = 8
EMB_PAD = 128
DFT_N1_MAX = 128
FF_CHUNK = 256
COMBINE_ALIGN = SUBLANES
COMBINE_WIN = 32


def _cparams(sem, vmem_mb=48):
    return pltpu.CompilerParams(dimension_semantics=sem, vmem_limit_bytes=vmem_mb * 1024 * 1024)


def _dot(a, b):
    return jnp.dot(a, b, preferred_element_type=F32)


def _dot_nt(a, b):
    return lax.dot_general(a, b, (((1,), (1,)), ((), ())), preferred_element_type=F32)


def _dot_tn(a, b):
    return lax.dot_general(a, b, (((0,), (0,)), ((), ())), preferred_element_type=F32)


def _split(a):
    hi = a.astype(BF16)
    lo = (a - hi.astype(F32)).astype(BF16)
    return hi, lo


def _dot_const(m_hi, m_lo, x, passes):
    xh = x.astype(BF16)
    r = _dot(m_hi, xh)
    if passes >= 3:
        xl = (x - xh.astype(F32)).astype(BF16)
        r = r + _dot(m_lo, xh) + _dot(m_hi, xl)
    return r


def _dot3(a, b):
    ah, al = _split(a)
    bh, bl = _split(b)
    return _dot(ah, bh) + _dot(al, bh) + _dot(ah, bl)


def _inproj_body(x_ref, g_ref, w_ref, cw_ref, cb_ref, zc_ref, zret_ref, zg_ref, z_prev, row_prev, *, tiles_per_seq):
    i = pl.program_id(0)
    tm = x_ref.shape[0]
    n_hy = zc_ref.shape[1]
    n_ret = zret_ref.shape[1]

    @pl.when(i == 0)
    def _():
        z_prev[...] = jnp.zeros_like(z_prev)
        row_prev[...] = jnp.zeros_like(row_prev)

    x = x_ref[...]
    ms = jnp.mean(x * x, axis=-1, keepdims=True)
    h = (x * lax.rsqrt(ms + EPS) * g_ref[...]).astype(BF16)
    z_hy = _dot(h, w_ref[:, :n_hy])
    zret_ref[...] = _dot(h, w_ref[:, n_hy:n_hy + n_ret]).astype(BF16)
    zg_ref[...] = _dot(h, w_ref[:, n_hy + n_ret:]).astype(BF16)

    t_prev = (i - 1) % tiles_per_seq
    zp = z_prev[...]
    before = jnp.where(t_prev == 0, 0.0, row_prev[0:1, :])
    after = jnp.where(t_prev == tiles_per_seq - 1, 0.0, z_hy[0:1, :])
    row = lax.broadcasted_iota(I32, (tm, 1), 0)
    zm1 = jnp.where(row == 0, before, pltpu.roll(zp, 1, axis=0))
    zp1 = jnp.where(row == tm - 1, after, pltpu.roll(zp, tm - 1, axis=0))
    zc_ref[...] = zm1 * cw_ref[0:1, :] + zp * cw_ref[1:2, :] + zp1 * cw_ref[2:3, :] + cb_ref[...]
    row_prev[...] = jnp.broadcast_to(zp[tm - 1:tm, :], row_prev.shape)
    z_prev[...] = z_hy


def _inproj(x2, g, w_bf, conv_w, conv_b, seq, tm=512):
    n_tok = x2.shape[0]
    n_hy, n_ret, n_g = 3 * HY_WIDTH, 4 * RET_WIDTH, 2 * D_MODEL
    tm = min(tm, seq)
    n_tiles = n_tok // tm
    cur = lambda i: (jnp.minimum(i, n_tiles - 1), 0)
    const = lambda i: (0, 0)
    return pl.pallas_call(
        functools.partial(_inproj_body, tiles_per_seq=seq // tm),
        grid=(n_tiles + 1,),
        in_specs=[pl.BlockSpec((tm, D_MODEL), cur),
                  pl.BlockSpec((1, D_MODEL), const),
                  pl.BlockSpec((D_MODEL, n_hy + n_ret + n_g), const, pipeline_mode=pl.Buffered(1)),
                  pl.BlockSpec((3, n_hy), const),
                  pl.BlockSpec((1, n_hy), const)],
        out_specs=[pl.BlockSpec((tm, n_hy), lambda i: (jnp.maximum(i - 1, 0), 0)),
                   pl.BlockSpec((tm, n_ret), cur),
                   pl.BlockSpec((tm, n_g), cur)],
        out_shape=[jax.ShapeDtypeStruct((n_tok, n_hy), F32),
                   jax.ShapeDtypeStruct((n_tok, n_ret), BF16),
                   jax.ShapeDtypeStruct((n_tok, n_g), BF16)],
        scratch_shapes=[pltpu.VMEM((tm, n_hy), F32), pltpu.VMEM((8, n_hy), F32)],
        compiler_params=_cparams(("arbitrary",)),
        name="inproj_shortconv",
    )(x2, g, w_bf, conv_w, conv_b)


def _fft_dims(seq):
    n = 2 * seq
    n1 = min(DFT_N1_MAX, 1 << (n.bit_length() // 2))
    n2 = n // n1
    assert n1 * n2 == n and n1 % 2 == 0
    return n, n1, n2


def _hi_lo_const(m):
    m = np.asarray(m, np.float64)
    hi = jnp.asarray(m, F32).astype(BF16)
    lo = (jnp.asarray(m, F32) - hi.astype(F32)).astype(BF16)
    return hi, lo


def _fft_tables(seq):
    n, n1, n2 = _fft_dims(seq)
    r1 = n1 // 2
    k1 = jnp.arange(n1, dtype=I32)
    j = jnp.arange(n2, dtype=I32)

    at = ((j[:, None] * k1[None, :]) % n).astype(F32) * (2.0 * math.pi / n)
    ct, st = jnp.cos(at), jnp.sin(at)

    def cos_sin(n1_count, k1_first):
        a1 = ((k1[:, None] * jnp.arange(n1_count, dtype=I32)[None, :]) % n1).astype(F32) * (2.0 * math.pi / n1)
        cf, sf = jnp.cos(a1), jnp.sin(a1)
        if k1_first:
            cf, sf, ctb, stb = cf[None], sf[None], ct[:, :, None], st[:, :, None]
        else:
            cf, sf, ctb, stb = cf.T[None], sf.T[None], ct[:, None, :], st[:, None, :]
        return cf * ctb - sf * stb, sf * ctb + cf * stb

    cat = jnp.concatenate
    c, s = cos_sin(r1, True)
    mat_a = cat([cat([c, s], axis=2), cat([-s, c], axis=2)], axis=1).astype(BF16)
    c, s = cos_sin(n1, True)
    mat_a_real = cat([c, -s], axis=1).astype(BF16)
    c, s = cos_sin(r1, False)
    mat_c = cat([cat([c, -s], axis=2), cat([s, c], axis=2)], axis=1).astype(BF16)
    a2 = 2.0 * np.pi * np.outer(np.arange(n2), np.arange(n2)) / n2
    gr, gi = np.cos(a2), -np.sin(a2)
    mat_f = np.block([[gr, -gi], [gi, gr]])
    mat_i = np.block([[gr, gi], [-gi, gr]])
    return dict(n=n, n1=n1, n2=n2, mat_a=mat_a, mat_a_real=mat_a_real, mat_c=mat_c,
                mat_f=_hi_lo_const(mat_f), mat_i=_hi_lo_const(mat_i))


def _filter_body(z_ref, w1_ref, b1_ref, f1_ref, w2_ref, b2_ref, f2_ref, w3_ref, dl_ref, h_ref, s_ref, *, seq):
    i = pl.program_id(0)
    tl = z_ref.shape[0]
    z = z_ref[...]
    a = jnp.sin(f1_ref[...] * (_dot3(z, w1_ref[...]) + b1_ref[...]))
    a = jnp.sin(f2_ref[...] * (_dot3(a, w2_ref[...]) + b2_ref[...]))
    h = _dot(a.astype(BF16), w3_ref[...].astype(BF16))
    win = jnp.exp(-z[:, 0:1] * dl_ref[...])
    h = h * jnp.concatenate([win] * HY_ORDER, axis=1)
    row = i * tl + lax.broadcasted_iota(I32, (tl, 1), 0)
    h = jnp.where(row == seq, 0.0, h)
    h_ref[...] = h

    @pl.when(i == 0)
    def _():
        s_ref[...] = jnp.zeros_like(s_ref)

    s_ref[...] += jnp.broadcast_to(jnp.sum(jnp.abs(h), axis=0, keepdims=True), s_ref.shape)


def _hyena_filter_time(seq, w1, b1, f1, w2, b2, f2, w3, tl=512):
    n = 2 * seq
    tl = min(tl, seq)
    idx = jnp.arange(n, dtype=I32)
    p = jnp.minimum(jnp.where(idx < seq, idx, n - idx), seq - 1).astype(F32)
    t = p / float(seq - 1)
    ang = 2.0 * math.pi * p / seq
    bands = jnp.linspace(1e-4, HY_BANDS - 1, HY_BANDS, dtype=F32)
    phase = ang[:, None] * bands[None, :]
    emb = jnp.concatenate([t[:, None], jnp.cos(phase), -jnp.sin(phase)], axis=-1)
    emb = jnp.pad(emb, ((0, 0), (0, EMB_PAD - emb.shape[1])))
    w1p = jnp.pad(w1.astype(F32), ((0, EMB_PAD - w1.shape[0]), (0, 0)))
    w3d = w3.astype(F32).reshape(HY_FILTER_WIDTH, HY_ORDER, 2, HY_WIDTH).transpose(2, 0, 1, 3)
    w3d = w3d.reshape(2, HY_FILTER_WIDTH, HY_ORDER * HY_WIDTH)
    deltas = jnp.abs(jnp.linspace(math.log(HY_DECAY_TARGET) / HY_SLOW_DECAY,
                                  math.log(HY_DECAY_TARGET) / HY_FAST_DECAY, HY_WIDTH, dtype=F32))[None, :]
    fw = HY_FILTER_WIDTH
    nblk_half = seq // tl
    const = lambda i: (0, 0)
    return pl.pallas_call(
        functools.partial(_filter_body, seq=seq),
        grid=(n // tl,),
        in_specs=[pl.BlockSpec((tl, EMB_PAD), lambda i: (i, 0)),
                  pl.BlockSpec((EMB_PAD, fw), const), pl.BlockSpec((1, fw), const), pl.BlockSpec((1, fw), const),
                  pl.BlockSpec((fw, fw), const), pl.BlockSpec((1, fw), const), pl.BlockSpec((1, fw), const),
                  pl.BlockSpec((None, fw, HY_ORDER * HY_WIDTH), lambda i: (i // nblk_half, 0, 0)),
                  pl.BlockSpec((1, HY_WIDTH), const)],
        out_specs=[pl.BlockSpec((tl, HY_ORDER * HY_WIDTH), lambda i: (i, 0)),
                   pl.BlockSpec((8, HY_ORDER * HY_WIDTH), const)],
        out_shape=[jax.ShapeDtypeStruct((n, HY_ORDER * HY_WIDTH), F32),
                   jax.ShapeDtypeStruct((8, HY_ORDER * HY_WIDTH), F32)],
        compiler_params=_cparams(("arbitrary",)),
        name="hyena_filter",
    )(emb, w1p, b1.astype(F32)[None], f1.astype(F32)[None], w2.astype(F32), b2.astype(F32)[None],
      f2.astype(F32)[None], w3d, deltas)


def _stage_a_real_body(h_ref, m_ref, a_ref):
    n1 = h_ref.shape[0]
    h_t = jnp.swapaxes(h_ref[...], 0, 1)
    rs = [_dot(m_ref[j], h_t[j].astype(BF16)) for j in range(h_ref.shape[1])]
    r = jnp.swapaxes(jnp.stack(rs, axis=0), 0, 1)
    a_ref[0] = r[:n1]
    a_ref[1] = r[n1:]


def _stage_a_real(h, tabs, n2c=8, cw=512):
    n1, n2 = tabs["n1"], tabs["n2"]
    c = h.shape[1]
    h3 = h.reshape(n1, n2, c)
    mat = tabs["mat_a_real"]
    return pl.pallas_call(
        _stage_a_real_body,
        grid=(n2 // n2c, c // cw),
        in_specs=[pl.BlockSpec((n1, n2c, cw), lambda j, k: (0, j, k)),
                  pl.BlockSpec((n2c,) + mat.shape[1:], lambda j, k: (j, 0, 0))],
        out_specs=pl.BlockSpec((2, n1, n2c, cw), lambda j, k: (0, 0, j, k)),
        out_shape=jax.ShapeDtypeStruct((2, n1, n2, c), F32),
        compiler_params=_cparams(("parallel", "parallel")),
        name="dft_a_filter",
    )(h3, mat)


def _pack_pair(re, im):
    hi = lax.bitcast_convert_type(re.astype(BF16).astype(F32), I32)
    lo = lax.bitcast_convert_type(im.astype(BF16).astype(F32), I32)
    return lax.bitcast_convert_type(hi | lax.shift_right_logical(lo, 16), F32)


def _unpack_pair(packed):
    word = lax.bitcast_convert_type(packed, I32)
    re = lax.bitcast_convert_type(word & jnp.int32(-65536), F32)
    im = lax.bitcast_convert_type(lax.shift_left(word, 16), F32)
    return re, im


def _stage_a_body(u_ref, m_ref, a_ref):
    _, r1, n2c, cw = u_ref.shape
    n1 = 2 * r1
    u_t = jnp.swapaxes(u_ref[...].reshape(n1, n2c, cw), 0, 1)
    packed = []
    for j in range(n2c):
        r = _dot(m_ref[j], u_t[j].astype(BF16))
        packed.append(_pack_pair(r[:n1], r[n1:]))
    a_ref[...] = jnp.swapaxes(jnp.stack(packed, axis=0), 0, 1)


def _stage_a(u5, col, tabs, n2c=8):
    n1, n2 = tabs["n1"], tabs["n2"]
    npair, _, r1, _, _ = u5.shape
    cw = HY_WIDTH
    mat = tabs["mat_a"]
    return pl.pallas_call(
        _stage_a_body,
        grid=(npair, n2 // n2c),
        in_specs=[pl.BlockSpec((None, 2, r1, n2c, cw), lambda p, j: (p, 0, 0, j, col)),
                  pl.BlockSpec((n2c,) + mat.shape[1:], lambda p, j: (j, 0, 0))],
        out_specs=pl.BlockSpec((None, n1, n2c, cw), lambda p, j: (p, 0, j, 0)),
        out_shape=jax.ShapeDtypeStruct((npair, n1, n2, cw), F32),
        compiler_params=_cparams(("parallel", "parallel")),
        name="dft_a",
    )(u5, mat)


def _stage_b_filter_body(a_ref, g_ref, s_ref, kr_ref, ki_ref, *, n):
    n2 = a_ref.shape[2]
    scale = 1.0 / (s_ref[0:1, :] * float(n))
    for kk in range(a_ref.shape[1]):
        d = _dot(g_ref[...], jnp.concatenate([a_ref[0, kk], a_ref[1, kk]], axis=0).astype(BF16))
        kr_ref[kk] = d[:n2] * scale
        ki_ref[kk] = d[n2:] * scale


def _stage_b_filter(a, s, tabs):
    n, n1, n2 = tabs["n"], tabs["n1"], tabs["n2"]
    c = a.shape[-1]
    kb = max(1, min(n1, 512 // n2))
    g = tabs["mat_f"][0]
    return pl.pallas_call(
        functools.partial(_stage_b_filter_body, n=n),
        grid=(n1 // kb,),
        in_specs=[pl.BlockSpec((2, kb, n2, c), lambda k: (0, k, 0, 0)),
                  pl.BlockSpec(g.shape, lambda k: (0, 0)),
                  pl.BlockSpec(s.shape, lambda k: (0, 0))],
        out_specs=[pl.BlockSpec((kb, n2, c), lambda k: (k, 0, 0)),
                   pl.BlockSpec((kb, n2, c), lambda k: (k, 0, 0))],
        out_shape=[jax.ShapeDtypeStruct((n1, n2, c), F32), jax.ShapeDtypeStruct((n1, n2, c), F32)],
        compiler_params=_cparams(("parallel",)),
        name="dft_b_filter",
    )(a, g, s)


def _stage_b_body(a_ref, kr_ref, ki_ref, gf_ref, gi_ref, c_ref):
    n2 = a_ref.shape[1]
    for kk in range(a_ref.shape[0]):
        d = _dot(gf_ref[...], jnp.concatenate(_unpack_pair(a_ref[kk]), axis=0).astype(BF16))
        dr, di = d[:n2], d[n2:]
        kr, ki = kr_ref[kk], ki_ref[kk]
        yr = dr * kr - di * ki
        yi = dr * ki + di * kr
        e = _dot(gi_ref[...], jnp.concatenate([yr, yi], axis=0).astype(BF16))
        c_ref[kk] = _pack_pair(e[:n2], e[n2:])


def _stage_b(a, kr, ki, order, tabs):
    n1, n2 = tabs["n1"], tabs["n2"]
    npair = a.shape[0]
    cw = HY_WIDTH
    kb = max(1, min(n1, 1024 // n2))
    gf = tabs["mat_f"][0]
    gi = tabs["mat_i"][0]
    const = lambda k, p: (0, 0)
    return pl.pallas_call(
        _stage_b_body,
        grid=(n1 // kb, npair),
        in_specs=[pl.BlockSpec((None, kb, n2, cw), lambda k, p: (p, k, 0, 0)),
                  pl.BlockSpec((kb, n2, cw), lambda k, p: (k, 0, order)),
                  pl.BlockSpec((kb, n2, cw), lambda k, p: (k, 0, order)),
                  pl.BlockSpec(gf.shape, const), pl.BlockSpec(gi.shape, const)],
        out_specs=pl.BlockSpec((None, kb, n2, cw), lambda k, p: (p, k, 0, 0)),
        out_shape=jax.ShapeDtypeStruct(a.shape, F32),
        compiler_params=_cparams(("parallel", "parallel")),
        name="dft_b",
    )(a, kr, ki, gf, gi)


def _stage_c_body(c_ref, u_ref, g_ref, skip_ref, mc_ref, ma_ref, y_ref, *rest, fuse_a):
    _, r1, n2c, cw = u_ref.shape
    n1 = 2 * r1
    ys = []
    c_t = jnp.swapaxes(c_ref[...], 0, 1)
    for j in range(n2c):
        cc = jnp.concatenate(_unpack_pair(c_t[j]), axis=0)
        ys.append(_dot(mc_ref[j], cc.astype(BF16)))
    y = jnp.swapaxes(jnp.stack(ys, axis=0), 0, 1)
    u = u_ref[...].reshape(n1, n2c, cw)
    g = g_ref[...].reshape(n1, n2c, cw)
    yo = g * (y + u * skip_ref[...].reshape(1, 1, cw))
    y_ref[...] = yo.reshape(2, r1, n2c, cw).astype(y_ref.dtype)
    if fuse_a:
        a_ref = rest[0]
        yo_t = jnp.swapaxes(yo, 0, 1)
        packed = []
        for j in range(n2c):
            r = _dot(ma_ref[j], yo_t[j].astype(BF16))
            packed.append(_pack_pair(r[:n1], r[n1:]))
        a_ref[...] = jnp.swapaxes(jnp.stack(packed, axis=0), 0, 1)


def _stage_c(c, u5, u_col, g5, g_col, skip, tabs, fuse_a, out_dtype, n2c=8):
    n1, n2 = tabs["n1"], tabs["n2"]
    npair, _, r1, _, _ = u5.shape
    cw = HY_WIDTH
    mc = tabs["mat_c"]
    ma = tabs["mat_a"]
    const = lambda p, j: (0, 0)
    per_j = lambda m: pl.BlockSpec((n2c,) + m.shape[1:], lambda p, j: (j, 0, 0))
    out_shape = [jax.ShapeDtypeStruct((npair, 2, r1, n2, cw), out_dtype)]
    out_specs = [pl.BlockSpec((None, 2, r1, n2c, cw), lambda p, j: (p, 0, 0, j, 0))]
    if fuse_a:
        out_shape.append(jax.ShapeDtypeStruct((npair, n1, n2, cw), F32))
        out_specs.append(pl.BlockSpec((None, n1, n2c, cw), lambda p, j: (p, 0, j, 0)))
    return pl.pallas_call(
        functools.partial(_stage_c_body, fuse_a=fuse_a),
        grid=(npair, n2 // n2c),
        in_specs=[pl.BlockSpec((None, n1, n2c, cw), lambda p, j: (p, 0, j, 0)),
                  pl.BlockSpec((None, 2, r1, n2c, cw), lambda p, j: (p, 0, 0, j, u_col)),
                  pl.BlockSpec((None, 2, r1, n2c, cw), lambda p, j: (p, 0, 0, j, g_col)),
                  pl.BlockSpec((1, cw), const), per_j(mc), per_j(ma)],
        out_specs=out_specs,
        out_shape=out_shape,
        compiler_params=_cparams(("parallel", "parallel")),
        name="dft_c",
    )(c, u5, g5, skip, mc, ma)


def _hyena(zc3, kr, ki, skip, tabs):
    bsz, seq, _ = zc3.shape
    n1, n2 = tabs["n1"], tabs["n2"]
    r1 = n1 // 2
    z5 = zc3.reshape(bsz // 2, 2, r1, n2, 3 * HY_WIDTH)
    a = _stage_a(z5, 2, tabs)
    c = _stage_b(a, kr, ki, 0, tabs)
    y1, a2 = _stage_c(c, z5, 2, z5, 0, skip[0:1].astype(F32), tabs, True, F32)
    c2 = _stage_b(a2, kr, ki, 1, tabs)
    (y2,) = _stage_c(c2, y1, 0, z5, 1, skip[1:2].astype(F32), tabs, False, BF16)
    return y2.reshape(bsz * seq, HY_WIDTH)


def _retention_body(sc_ref, qf_ref, kf_ref, vf_ref, cf_ref, sf_ref, qb_ref, kb_ref, vb_ref, cb_ref, sb_ref,
                    of_ref, ob_ref, state, dmat, qwt, kwt):
    ch = RET_CHUNK
    dh = RET_HEAD_DIM

    @pl.when(pl.program_id(1) == 0)
    def _():
        state[...] = jnp.zeros_like(state)
        ci = lax.broadcasted_iota(I32, (ch, ch), 0).astype(F32)
        mi = lax.broadcasted_iota(I32, (ch, ch), 1).astype(F32)
        lag = ci - mi
        for d in range(2):
            for h in range(RET_HEADS):
                lg = sc_ref[d * RET_HEADS + h]
                if d == 0:
                    dmat[d, h] = jnp.where(lag >= 0, jnp.exp(lg * jnp.maximum(lag, 0.0)), 0.0)
                    qwt[d, h] = jnp.exp(lg * (ci + 1.0))
                    kwt[d, h] = jnp.exp(lg * (ch - 1.0 - ci))
                else:
                    dmat[d, h] = jnp.where(lag < 0, jnp.exp(lg * jnp.maximum(-lag, 0.0)), 0.0)
                    qwt[d, h] = jnp.exp(lg * (ch - ci))
                    kwt[d, h] = jnp.exp(lg * ci)

    scale = RET_HEAD_DIM ** -0.5
    refs = ((qf_ref, kf_ref, vf_ref, cf_ref, sf_ref, of_ref), (qb_ref, kb_ref, vb_ref, cb_ref, sb_ref, ob_ref))
    work = [(d, h) for d in range(2) for h in range(RET_HEADS)]
    per_step = qf_ref.shape[0] // ch
    for c in range(per_step):
        rows = (slice(c * ch, (c + 1) * ch), slice((per_step - 1 - c) * ch, (per_step - c) * ch))
        qs, ks, qws, kws, vs = {}, {}, {}, {}, {}
        for d, h in work:
            q_ref, k_ref, v_ref, c_ref, s_ref, _ = refs[d]
            sl = slice(h * dh, (h + 1) * dh)
            cosf = c_ref[rows[d], :]
            sinf = s_ref[rows[d], :]
            q = q_ref[rows[d], sl].astype(F32)
            k = k_ref[rows[d], sl].astype(F32)
            q = q * cosf + pltpu.roll(q, dh // 2, axis=1) * sinf
            k = (k * cosf + pltpu.roll(k, dh // 2, axis=1) * sinf) * scale
            qs[d, h], ks[d, h] = q.astype(BF16), k.astype(BF16)
            qws[d, h], kws[d, h] = (q * qwt[d, h]).astype(BF16), (k * kwt[d, h]).astype(BF16)
            vs[d, h] = v_ref[rows[d], sl]
        scores = {dh_: _dot_nt(qs[dh_], ks[dh_]) for dh_ in work}
        for d, h in work:
            lhs = jnp.concatenate([(scores[d, h] * dmat[d, h]).astype(BF16), qws[d, h]], axis=1)
            rhs = jnp.concatenate([vs[d, h], state[d, h].astype(BF16)], axis=0)
            refs[d][5][rows[d], h * dh:(h + 1) * dh] = _dot(lhs, rhs).astype(BF16)
        for d, h in work:
            cdec = sc_ref[2 * RET_HEADS + d * RET_HEADS + h]
            state[d, h] = state[d, h] * cdec + _dot_tn(kws[d, h], vs[d, h])


def _retention(zret3, decay_logit):
    bsz, seq, _ = zret3.shape
    ch = RET_CHUNK
    nc = seq // ch
    half = RET_HEAD_DIM // 2
    inv_freq = ROPE_BASE ** (-jnp.arange(half, dtype=F32) / half)
    ang = jnp.arange(seq, dtype=F32)[:, None] * inv_freq[None, :]
    cosf = jnp.concatenate([jnp.cos(ang), jnp.cos(ang)], axis=1)
    sinf = jnp.concatenate([-jnp.sin(ang), jnp.sin(ang)], axis=1)
    log_g = jax.nn.log_sigmoid(decay_logit.astype(F32)).reshape(-1)
    scal = jnp.concatenate([log_g, jnp.exp(log_g * ch)])
    w = RET_WIDTH
    per_step = next(c for c in (4, 2, 1) if nc % c == 0)
    nb = nc // per_step
    blk = per_step * ch
    fwd = lambda col: pl.BlockSpec((None, blk, w), lambda b, n, sc: (b, n, col))
    bwd = lambda col: pl.BlockSpec((None, blk, w), lambda b, n, sc: (b, nb - 1 - n, col))
    rope_f = pl.BlockSpec((blk, RET_HEAD_DIM), lambda b, n, sc: (n, 0))
    rope_b = pl.BlockSpec((blk, RET_HEAD_DIM), lambda b, n, sc: (nb - 1 - n, 0))
    grid_spec = pltpu.PrefetchScalarGridSpec(
        num_scalar_prefetch=1,
        grid=(bsz, nb),
        in_specs=[fwd(0), fwd(1), fwd(2), rope_f, rope_f, bwd(0), bwd(1), bwd(2), rope_b, rope_b],
        out_specs=[pl.BlockSpec((None, blk, w), lambda b, n, sc: (b, n, 0)),
                   pl.BlockSpec((None, blk, w), lambda b, n, sc: (b, nb - 1 - n, 0))],
        scratch_shapes=[pltpu.VMEM((2, RET_HEADS, RET_HEAD_DIM, RET_HEAD_DIM), F32),
                        pltpu.VMEM((2, RET_HEADS, ch, ch), F32),
                        pltpu.VMEM((2, RET_HEADS, ch, RET_HEAD_DIM), F32),
                        pltpu.VMEM((2, RET_HEADS, ch, RET_HEAD_DIM), F32)],
    )
    return pl.pallas_call(
        _retention_body,
        grid_spec=grid_spec,
        out_shape=[jax.ShapeDtypeStruct((bsz, seq, w), BF16), jax.ShapeDtypeStruct((bsz, seq, w), BF16)],
        compiler_params=_cparams(("arbitrary", "arbitrary")),
        name="retention",
    )(scal, zret3, zret3, zret3, cosf, sinf, zret3, zret3, zret3, cosf, sinf)


def _outproj_body(x_ref, yhy_ref, of_ref, ob_ref, gr_ref, ghy_ref, gret_ref, whyo_ref, wreto_ref, wo_ref,
                  n2g_ref, wrt_ref, x1_ref, pt_ref):
    dh = RET_HEAD_DIM
    o = of_ref[...].astype(F32) + ob_ref[...].astype(F32)
    parts = []
    for h in range(RET_HEADS):
        oh = o[:, h * dh:(h + 1) * dh]
        parts.append(oh * lax.rsqrt(jnp.mean(oh * oh, axis=-1, keepdims=True) + EPS))
    on = jnp.concatenate(parts, axis=1)
    gr = gr_ref[...].astype(F32)
    ret = (gr * jax.nn.sigmoid(gr)) * on
    y_ret = _dot(ret.astype(BF16), wreto_ref[...])
    y_hy = _dot(yhy_ref[...], whyo_ref[...])
    merged = jax.nn.sigmoid(ghy_ref[...].astype(F32)) * y_hy + jax.nn.sigmoid(gret_ref[...].astype(F32)) * y_ret
    x1 = x_ref[...] + _dot(merged.astype(BF16), wo_ref[...])
    x1_ref[...] = x1
    h2 = x1 * lax.rsqrt(jnp.mean(x1 * x1, axis=-1, keepdims=True) + EPS) * n2g_ref[...]
    logits = _dot_nt(wrt_ref[...], h2.astype(BF16))
    m = jnp.max(logits, axis=0, keepdims=True)
    e = jnp.exp(logits - m)
    pt_ref[...] = e / jnp.sum(e, axis=0, keepdims=True)


def _outproj(x2, yhy, o_f, o_b, zret, zg, whyo, wreto, wo, n2g, wrt, tm=512):
    n_tok = x2.shape[0]
    d = D_MODEL
    row = lambda w, col=0: pl.BlockSpec((tm, w), lambda i: (i, col))
    const = lambda shape: pl.BlockSpec(shape, lambda i: (0, 0))
    return pl.pallas_call(
        _outproj_body,
        grid=(n_tok // tm,),
        in_specs=[row(d), row(HY_WIDTH), row(RET_WIDTH), row(RET_WIDTH), row(RET_WIDTH, 3), row(d, 0), row(d, 1),
                  const((HY_WIDTH, d)), const((RET_WIDTH, d)), const((d, d)), const((1, d)), const((N_EXPERTS, d))],
        out_specs=[row(d), pl.BlockSpec((N_EXPERTS, tm), lambda i: (0, i))],
        out_shape=[jax.ShapeDtypeStruct((n_tok, d), F32), jax.ShapeDtypeStruct((N_EXPERTS, n_tok), F32)],
        compiler_params=_cparams(("parallel",)),
        name="outproj_router",
    )(x2, yhy, o_f, o_b, zret, zg, zg, whyo, wreto, wo, n2g, wrt)


def _select_body(p_ref, upper_ref, lower_ref, lowinc_ref, eye_ref, pos_ref, idx_ref, gate_ref, lo_ref, *, cap):
    rows = p_ref.shape[0]
    p = p_ref[...]
    bits = lax.bitcast_convert_type(p, I32)

    def count(mask):
        return jnp.sum(jnp.sum(mask.astype(F32), axis=1, keepdims=True), axis=0, keepdims=True)

    def bit_step(i, thr):
        cand = thr | jnp.left_shift(jnp.int32(1), 30 - i)
        return jnp.where(count(bits >= cand) >= cap, cand, thr)

    thr = lax.fori_loop(0, 31, bit_step, jnp.zeros((1, 1), I32))
    gt = bits > thr
    eq = bits == thr
    need = cap - count(gt)

    def prefix(mask_f):
        incl = _dot(mask_f.astype(BF16), upper_ref[...])
        tot = jnp.broadcast_to(incl[:, LANES - 1:LANES], incl.shape)
        base = _dot(lower_ref[...], tot.astype(BF16))
        return incl, base, tot

    eq_f = eq.astype(F32)
    incl_e, base_e, _ = prefix(eq_f)
    sel = gt | (eq & (base_e + incl_e - eq_f < need))
    sel_f = sel.astype(F32)
    incl, base, tot = prefix(sel_f)
    pos_ref[...] = jnp.where(sel, (base + incl - 1.0).astype(I32), -1)
    lo_ref[...] = base.astype(I32)

    rowend = base + tot
    incl_t = _dot_nt(lowinc_ref[...], sel_f.astype(BF16)).astype(BF16)
    p_t = []
    rem = p
    for _ in range(3):
        part = rem.astype(BF16)
        rem = rem - part.astype(F32)
        p_t.append(_dot_nt(eye_ref[...], part).astype(BF16))
    table = jnp.concatenate([incl_t] + p_t, axis=0)
    groups = 2 if (cap // LANES) % 2 == 0 else 1
    sw = groups * LANES
    wide = lambda a: jnp.concatenate([a] * groups, axis=1)
    rowend_w, tot_w = wide(rowend), wide(tot)
    r_iota = lax.broadcasted_iota(I32, (rows, sw), 0).astype(F32)
    lane_iota = lax.broadcasted_iota(I32, (LANES, sw), 0).astype(F32)

    def slot_tile(ts, carry):
        s = (ts * sw + lax.broadcasted_iota(I32, (1, sw), 1)).astype(F32)
        done = rowend_w <= s
        row = jnp.sum(done.astype(F32), axis=0, keepdims=True)
        before = jnp.sum(jnp.where(done, tot_w, 0.0), axis=0, keepdims=True)
        onehot_t = (r_iota == row).astype(BF16)
        got = _dot(table, onehot_t)
        g_t = got[:LANES]
        lane = jnp.sum((g_t <= s - before).astype(F32), axis=0, keepdims=True)
        tok = (row * LANES + lane).astype(I32)
        p_row = got[LANES:2 * LANES] + got[2 * LANES:3 * LANES] + got[3 * LANES:]
        gate = jnp.sum(jnp.where(lane_iota == lane, p_row, 0.0), axis=0, keepdims=True)
        for k in range(groups):
            idx_ref[pl.ds(ts * groups + k, 1), :] = tok[:, k * LANES:(k + 1) * LANES]
            gate_ref[pl.ds(ts * groups + k, 1), :] = gate[:, k * LANES:(k + 1) * LANES]
        return carry

    lax.fori_loop(0, cap // sw, slot_tile, 0)


def _select(pt3, cap):
    n_e, rows, _ = pt3.shape
    ii = np.arange(LANES)
    upper = jnp.asarray(ii[:, None] <= ii[None, :], BF16)
    lowinc = jnp.asarray(ii[None, :] <= ii[:, None], BF16)
    eye = jnp.asarray(ii[None, :] == ii[:, None], BF16)
    rr = np.arange(rows)
    lower = jnp.asarray(rr[None, :] < rr[:, None], BF16)
    const = lambda shape: pl.BlockSpec(shape, lambda e: (0, 0))
    tok_spec = pl.BlockSpec((None, rows, LANES), lambda e: (e, 0, 0))
    slot_spec = pl.BlockSpec((None, cap // LANES, LANES), lambda e: (e, 0, 0))
    return pl.pallas_call(
        functools.partial(_select_body, cap=cap),
        grid=(n_e,),
        in_specs=[tok_spec, const((LANES, LANES)), const((rows, rows)), const((LANES, LANES)), const((LANES, LANES))],
        out_specs=[tok_spec, slot_spec, slot_spec, tok_spec],
        out_shape=[jax.ShapeDtypeStruct((n_e, rows, LANES), I32),
                   jax.ShapeDtypeStruct((n_e, cap // LANES, LANES), I32),
                   jax.ShapeDtypeStruct((n_e, cap // LANES, LANES), F32),
                   jax.ShapeDtypeStruct((n_e, rows, LANES), I32)],
        compiler_params=_cparams(("parallel",)),
        name="expert_select",
    )(pt3, upper, lower, lowinc, eye)


def _ffn_body(idx_ref, h_hbm, n2g_ref, gate_ref, wg_ref, wu_ref, wd_ref, ye_ref, xbuf, xb, sems, *,
              tiles_per_expert):
    s = xbuf.shape[1]
    step = pl.program_id(0) * tiles_per_expert + pl.program_id(1)
    n_steps = pl.num_programs(0) * tiles_per_expert
    cur = step % 2

    def row_copy(st, i, buf):
        tok = idx_ref[st * s + i]
        return pltpu.make_async_copy(h_hbm.at[pl.ds(tok, 1), :], xbuf.at[buf, pl.ds(i, 1), :], sems.at[buf])

    def wait_rows(buf):
        pltpu.make_async_copy(h_hbm.at[pl.ds(0, s), :], xbuf.at[buf], sems.at[buf]).wait()

    @pl.when(step == 0)
    def _():
        def one(i, carry):
            row_copy(0, i, 0).start()
            return carry
        lax.fori_loop(0, s, one, 0, unroll=8)

    @pl.when(step + 1 < n_steps)
    def _():
        for i in range(s):
            row_copy(step + 1, i, 1 - cur).start()

    wait_rows(cur)
    x = xbuf[cur]
    xb[...] = (x * lax.rsqrt(jnp.mean(x * x, axis=-1, keepdims=True) + EPS) * n2g_ref[...]).astype(BF16)
    y = None
    for f0 in range(0, EXPERT_FF, FF_CHUNK):
        f1 = min(f0 + FF_CHUNK, EXPERT_FF)
        a = _dot(xb[...], wg_ref[:, f0:f1])
        b = _dot(xb[...], wu_ref[:, f0:f1])
        hid = ((a * jax.nn.sigmoid(a)) * b).astype(BF16)
        part = _dot(hid, wd_ref[f0:f1, :])
        y = part if y is None else y + part
    for k in range(s // LANES):
        col = jnp.transpose(jnp.broadcast_to(gate_ref[k:k + 1, :], (LANES, LANES)))[:, 0:1]
        yk = y[k * LANES:(k + 1) * LANES] * col
        ye_ref[k * LANES:(k + 1) * LANES, :] = _pack_pair(yk[:, :D_MODEL // 2], yk[:, D_MODEL // 2:])


def _expert_ffn(idx_flat, x1, n2g, gates, wg, wu, wd, cap, s=512):
    s = min(s, cap)
    tiles = cap // s
    g3 = gates.reshape(N_EXPERTS * tiles, s // LANES, LANES)
    grid_spec = pltpu.PrefetchScalarGridSpec(
        num_scalar_prefetch=1,
        grid=(N_EXPERTS, tiles),
        in_specs=[pl.BlockSpec(memory_space=pl.ANY),
                  pl.BlockSpec((1, D_MODEL), lambda e, j, idx: (0, 0)),
                  pl.BlockSpec((None, s // LANES, LANES), lambda e, j, idx: (e * tiles + j, 0, 0)),
                  pl.BlockSpec((None, D_MODEL, EXPERT_FF), lambda e, j, idx: (e, 0, 0)),
                  pl.BlockSpec((None, D_MODEL, EXPERT_FF), lambda e, j, idx: (e, 0, 0)),
                  pl.BlockSpec((None, EXPERT_FF, D_MODEL), lambda e, j, idx: (e, 0, 0))],
        out_specs=pl.BlockSpec((s, D_MODEL // 2), lambda e, j, idx: (e * tiles + j, 0)),
        scratch_shapes=[pltpu.VMEM((2, s, D_MODEL), F32), pltpu.VMEM((s, D_MODEL), BF16),
                        pltpu.SemaphoreType.DMA((2,))],
    )
    return pl.pallas_call(
        functools.partial(_ffn_body, tiles_per_expert=tiles),
        grid_spec=grid_spec,
        out_shape=jax.ShapeDtypeStruct((N_EXPERTS * cap, D_MODEL // 2), F32),
        compiler_params=_cparams(("arbitrary", "arbitrary")),
        name="expert_ffn",
    )(idx_flat, x1, n2g, g3, wg, wu, wd)


def _combine_body(lo_ref, np_ref, x1_ref, pos_ref, nfg_ref, expand_ref, lov_ref, ye_hbm, o_ref, win, sems, *, cap,
                  rows_total, n_rows):
    step = pl.program_id(0)
    n_steps = pl.num_programs(0)
    groups = x1_ref.shape[0] // LANES
    w = COMBINE_WIN
    cur = step % 2

    def copies(rr, m, buf, g):
        out = []
        for e in range(N_EXPERTS):
            first = e * cap + lo_ref[e * n_rows + rr]
            intended = (first // COMBINE_ALIGN) * COMBINE_ALIGN + m * w
            actual = pl.multiple_of(jnp.minimum(intended, rows_total - w), COMBINE_ALIGN)
            out.append(pltpu.make_async_copy(ye_hbm.at[pl.ds(actual, w), :], win.at[buf, g, pl.ds(e * w, w), :],
                                             sems.at[buf, g, e]))
        return out

    def wait_windows(buf, g):
        for e in range(N_EXPERTS):
            pltpu.make_async_copy(ye_hbm.at[pl.ds(0, w), :], win.at[buf, g, pl.ds(e * w, w), :],
                                  sems.at[buf, g, e]).wait()

    def contribution(rr, m, buf, g):
        lane = lax.broadcasted_iota(I32, (1, N_EXPERTS), 1)
        first = lane * cap + lov_ref[pl.ds(rr, 1), :]
        intended = (first & jnp.int32(-COMBINE_ALIGN)) + m * w
        actual = jnp.minimum(intended, rows_total - w)
        pos = pos_ref[g * LANES:(g + 1) * LANES, :]
        glob = pos + lane * cap
        rel = glob - intended
        valid = (pos >= 0) & (rel >= 0) & (rel < w)
        local = jnp.where(valid, glob - actual, -1).astype(F32).astype(BF16)
        spread = _dot(local, expand_ref[...])
        col = (lax.broadcasted_iota(I32, (1, N_EXPERTS * w), 1) & (w - 1)).astype(F32)
        onehot = (spread == col).astype(BF16)
        lo_cols, hi_cols = _unpack_pair(win[buf, g])
        return jnp.concatenate([_dot(onehot, lo_cols.astype(BF16)), _dot(onehot, hi_cols.astype(BF16))], axis=1)

    @pl.when(step == 0)
    def _():
        for g in range(groups):
            for cp in copies(g, 0, 0, g):
                cp.start()

    @pl.when(step + 1 < n_steps)
    def _():
        for g in range(groups):
            for cp in copies((step + 1) * groups + g, 0, 1 - cur, g):
                cp.start()

    for g in range(groups):
        rr = step * groups + g
        wait_windows(cur, g)
        acc = x1_ref[g * LANES:(g + 1) * LANES, :] + contribution(rr, 0, cur, g)

        def extra_pass(m, acc, rr=rr, g=g):
            for cp in copies(rr, m, cur, g):
                cp.start()
            wait_windows(cur, g)
            return acc + contribution(rr, m, cur, g)

        acc = lax.fori_loop(1, np_ref[rr], extra_pass, acc)
        o_ref[g * LANES:(g + 1) * LANES, :] = (
            acc * lax.rsqrt(jnp.mean(acc * acc, axis=-1, keepdims=True) + EPS) * nfg_ref[...])


def _combine(lo, x1, pos_t, nfg, ye, cap):
    n_tok = x1.shape[0]
    w = COMBINE_WIN
    rows_total = ye.shape[0]
    n_rows = n_tok // LANES
    groups = next(c for c in (4, 2, 1) if n_rows % c == 0)
    tm = groups * LANES
    nxt =jnp.concatenate([lo[:, 1:], jnp.full((N_EXPERTS, 1), cap, I32)], axis=1)
    span = lo % COMBINE_ALIGN + (nxt - lo)
    n_pass = jnp.maximum(jnp.max((span + w - 1) // w, axis=0), 1).astype(I32)
    ee = np.arange(N_EXPERTS)
    expand = jnp.asarray(ee[:, None] == (np.arange(N_EXPERTS * w) // w)[None, :], BF16)
    grid_spec = pltpu.PrefetchScalarGridSpec(
        num_scalar_prefetch=2,
        grid=(n_rows // groups,),
        in_specs=[pl.BlockSpec((tm, D_MODEL), lambda i, lo_, np_: (i, 0)),
                  pl.BlockSpec((tm, N_EXPERTS), lambda i, lo_, np_: (i, 0)),
                  pl.BlockSpec((1, D_MODEL), lambda i, lo_, np_: (0, 0)),
                  pl.BlockSpec((N_EXPERTS, N_EXPERTS * w), lambda i, lo_, np_: (0, 0)),
                  pl.BlockSpec((n_rows, N_EXPERTS), lambda i, lo_, np_: (0, 0)),
                  pl.BlockSpec(memory_space=pl.ANY)],
        out_specs=pl.BlockSpec((tm, D_MODEL), lambda i, lo_, np_: (i, 0)),
        scratch_shapes=[pltpu.VMEM((2, groups, N_EXPERTS * w, D_MODEL // 2), F32),
                        pltpu.SemaphoreType.DMA((2, groups, N_EXPERTS))],
    )
    return pl.pallas_call(
        functools.partial(_combine_body, cap=cap, rows_total=rows_total, n_rows=n_rows),
        grid_spec=grid_spec,
        out_shape=jax.ShapeDtypeStruct((n_tok, D_MODEL), F32),
        compiler_params=_cparams(("arbitrary",)),
        name="moe_combine_norm",
    )(lo.reshape(-1), n_pass, x1, pos_t, nfg, expand, lo.T, ye)


def _trunk(x, w):
    bsz, seq, d = x.shape
    n_tok = bsz * seq
    x2 = x.reshape(n_tok, d)
    tabs = _fft_tables(seq)

    zc, zret, zg = _inproj(x2, w["norm1_g"], w["w_in"], w["hy_conv_w"], w["hy_conv_b"], seq)
    zc = zc.reshape(bsz, seq, -1)

    h_time, h_abs = _hyena_filter_time(seq, w["hy_w1"], w["hy_b1"], w["hy_freq1"], w["hy_w2"], w["hy_b2"],
                                       w["hy_freq2"], w["hy_w3"])
    kr, ki = _stage_b_filter(_stage_a_real(h_time, tabs), h_abs, tabs)
    yhy = _hyena(zc, kr, ki, w["hy_skip"], tabs)

    o_f, o_b = _retention(zret.reshape(bsz, seq, -1), w["ret_decay_logit"])
    x1, pt = _outproj(x2, yhy, o_f.reshape(n_tok, -1), o_b.reshape(n_tok, -1), zret, zg,
                      w["w_hy_out"], w["w_ret_out"], w["w_o"], w["norm2_g"], w["w_router_t"])

    cap = CAPACITY_FACTOR * n_tok // N_EXPERTS
    rows = n_tok // LANES
    pos, idx, gates, lo = _select(pt.reshape(N_EXPERTS, rows, LANES), cap)
    ye = _expert_ffn(idx.reshape(-1), x1, w["norm2_g"], gates, w["w_gate"], w["w_up"], w["w_down"], cap)
    pos_t = pos.reshape(N_EXPERTS, n_tok).T
    y = _combine(lo[:, :, 0], x1, pos_t, w["norm_f_g"], ye, cap)
    return y.reshape(bsz, seq, d)


def kernel(x_prompt, x_sample, norm1_g, w_in, hy_conv_w, hy_conv_b, hy_w1, hy_b1, hy_freq1, hy_w2, hy_b2, hy_freq2,
           hy_w3, hy_skip, ret_decay_logit, w_hy_out, w_ret_out, w_o, norm2_g, w_router, w_gate, w_up, w_down,
           norm_f_g):
    layer = 0
    w = dict(
        norm1_g=norm1_g[layer].astype(F32)[None], w_in=w_in[layer].astype(BF16),
        hy_conv_w=hy_conv_w[layer].astype(F32), hy_conv_b=hy_conv_b[layer].astype(F32)[None],
        hy_w1=hy_w1[layer], hy_b1=hy_b1[layer], hy_freq1=hy_freq1[layer], hy_w2=hy_w2[layer], hy_b2=hy_b2[layer],
        hy_freq2=hy_freq2[layer], hy_w3=hy_w3[layer], hy_skip=hy_skip[layer],
        ret_decay_logit=ret_decay_logit[layer],
        w_hy_out=w_hy_out[layer].astype(BF16), w_ret_out=w_ret_out[layer].astype(BF16), w_o=w_o[layer].astype(BF16),
        norm2_g=norm2_g[layer].astype(F32)[None], w_router_t=w_router[layer].T.astype(BF16),
        w_gate=w_gate[layer].astype(BF16), w_up=w_up[layer].astype(BF16), w_down=w_down[layer].astype(BF16),
        norm_f_g=norm_f_g.astype(F32)[None],
    )
    return _trunk(x_prompt, w), _trunk(x_sample, w)
```

```python
import functools
import math

import numpy as np
import jax
import jax.numpy as jnp
from jax import lax
from jax.experimental import pallas as pl
from jax.experimental.pallas import tpu as pltpu

F32 = jnp.float32
BF16 = jnp.bfloat16
I32 = jnp.int32

D_MODEL = 1024
HY_WIDTH = 512
HY_ORDER = 2
HY_BANDS = 16
HY_FILTER_WIDTH = 64
HY_FAST_DECAY = 0.3
HY_SLOW_DECAY = 1.5
HY_DECAY_TARGET = 1e-2
RET_WIDTH = 512
RET_HEADS = 4
RET_HEAD_DIM = 128
RET_CHUNK = 128
ROPE_BASE = 10000.0
N_EXPERTS = 16
EXPERT_FF = 1408
CAPACITY_FACTOR = 2
EPS = 1e-6

LANES = 128
SUBLANES = 8
EMB_PAD = 128
DFT_N1_MAX = 128
FF_CHUNK = 256
COMBINE_ALIGN = SUBLANES
COMBINE_WIN = 32


def _cparams(sem, vmem_mb=48):
    return pltpu.CompilerParams(dimension_semantics=sem, vmem_limit_bytes=vmem_mb * 1024 * 1024)


def _dot(a, b):
    return jnp.dot(a, b, preferred_element_type=F32)


def _dot_nt(a, b):
    return lax.dot_general(a, b, (((1,), (1,)), ((), ())), preferred_element_type=F32)


def _dot_tn(a, b):
    return lax.dot_general(a, b, (((0,), (0,)), ((), ())), preferred_element_type=F32)


def _split(a):
    hi = a.astype(BF16)
    lo = (a - hi.astype(F32)).astype(BF16)
    return hi, lo


def _dot_const(m_hi, m_lo, x, passes):
    xh = x.astype(BF16)
    r = _dot(m_hi, xh)
    if passes >= 3:
        xl = (x - xh.astype(F32)).astype(BF16)
        r = r + _dot(m_lo, xh) + _dot(m_hi, xl)
    return r


def _dot3(a, b):
    ah, al = _split(a)
    bh, bl = _split(b)
    return _dot(ah, bh) + _dot(al, bh) + _dot(ah, bl)


def _inproj_body(x_ref, g_ref, w_ref, cw_ref, cb_ref, zc_ref, zret_ref, zg_ref, z_prev, row_prev, *, tiles_per_seq):
    i = pl.program_id(0)
    tm = x_ref.shape[0]
    n_hy = zc_ref.shape[1]
    n_ret = zret_ref.shape[1]

    @pl.when(i == 0)
    def _():
        z_prev[...] = jnp.zeros_like(z_prev)
        row_prev[...] = jnp.zeros_like(row_prev)

    x = x_ref[...]
    ms = jnp.mean(x * x, axis=-1, keepdims=True)
    h = (x * lax.rsqrt(ms + EPS) * g_ref[...]).astype(BF16)
    z_hy = _dot(h, w_ref[:, :n_hy])
    zret_ref[...] = _dot(h, w_ref[:, n_hy:n_hy + n_ret]).astype(BF16)
    zg_ref[...] = _dot(h, w_ref[:, n_hy + n_ret:]).astype(BF16)

    t_prev = (i - 1) % tiles_per_seq
    zp = z_prev[...]
    before = jnp.where(t_prev == 0, 0.0, row_prev[0:1, :])
    after = jnp.where(t_prev == tiles_per_seq - 1, 0.0, z_hy[0:1, :])
    row = lax.broadcasted_iota(I32, (tm, 1), 0)
    zm1 = jnp.where(row == 0, before, pltpu.roll(zp, 1, axis=0))
    zp1 = jnp.where(row == tm - 1, after, pltpu.roll(zp, tm - 1, axis=0))
    zc_ref[...] = zm1 * cw_ref[0:1, :] + zp * cw_ref[1:2, :] + zp1 * cw_ref[2:3, :] + cb_ref[...]
    row_prev[...] = jnp.broadcast_to(zp[tm - 1:tm, :], row_prev.shape)
    z_prev[...] = z_hy


def _inproj(x2, g, w_bf, conv_w, conv_b, seq, tm=512):
    n_tok = x2.shape[0]
    n_hy, n_ret, n_g = 3 * HY_WIDTH, 4 * RET_WIDTH, 2 * D_MODEL
    tm = min(tm, seq)
    n_tiles = n_tok // tm
    cur = lambda i: (jnp.minimum(i, n_tiles - 1), 0)
    const = lambda i: (0, 0)
    return pl.pallas_call(
        functools.partial(_inproj_body, tiles_per_seq=seq // tm),
        grid=(n_tiles + 1,),
        in_specs=[pl.BlockSpec((tm, D_MODEL), cur),
                  pl.BlockSpec((1, D_MODEL), const),
                  pl.BlockSpec((D_MODEL, n_hy + n_ret + n_g), const, pipeline_mode=pl.Buffered(1)),
                  pl.BlockSpec((3, n_hy), const),
                  pl.BlockSpec((1, n_hy), const)],
        out_specs=[pl.BlockSpec((tm, n_hy), lambda i: (jnp.maximum(i - 1, 0), 0)),
                   pl.BlockSpec((tm, n_ret), cur),
                   pl.BlockSpec((tm, n_g), cur)],
        out_shape=[jax.ShapeDtypeStruct((n_tok, n_hy), F32),
                   jax.ShapeDtypeStruct((n_tok, n_ret), BF16),
                   jax.ShapeDtypeStruct((n_tok, n_g), BF16)],
        scratch_shapes=[pltpu.VMEM((tm, n_hy), F32), pltpu.VMEM((8, n_hy), F32)],
        compiler_params=_cparams(("arbitrary",)),
        name="inproj_shortconv",
    )(x2, g, w_bf, conv_w, conv_b)


def _fft_dims(seq):
    n = 2 * seq
    n1 = min(DFT_N1_MAX, 1 << (n.bit_length() // 2))
    n2 = n // n1
    assert n1 * n2 == n and n1 % 2 == 0
    return n, n1, n2


def _hi_lo_const(m):
    m = np.asarray(m, np.float64)
    hi = jnp.asarray(m, F32).astype(BF16)
    lo = (jnp.asarray(m, F32) - hi.astype(F32)).astype(BF16)
    return hi, lo


def _fft_tables(seq):
    n, n1, n2 = _fft_dims(seq)
    r1 = n1 // 2
    k1 = jnp.arange(n1, dtype=I32)
    j = jnp.arange(n2, dtype=I32)

    at = ((j[:, None] * k1[None, :]) % n).astype(F32) * (2.0 * math.pi / n)
    ct, st = jnp.cos(at), jnp.sin(at)

    def cos_sin(n1_count, k1_first):
        a1 = ((k1[:, None] * jnp.arange(n1_count, dtype=I32)[None, :]) % n1).astype(F32) * (2.0 * math.pi / n1)
        cf, sf = jnp.cos(a1), jnp.sin(a1)
        if k1_first:
            cf, sf, ctb, stb = cf[None], sf[None], ct[:, :, None], st[:, :, None]
        else:
            cf, sf, ctb, stb = cf.T[None], sf.T[None], ct[:, None, :], st[:, None, :]
        return cf * ctb - sf * stb, sf * ctb + cf * stb

    cat = jnp.concatenate
    c, s = cos_sin(r1, True)
    mat_a = cat([cat([c, s], axis=2), cat([-s, c], axis=2)], axis=1).astype(BF16)
    c, s = cos_sin(n1, True)
    mat_a_real = cat([c, -s], axis=1).astype(BF16)
    c, s = cos_sin(r1, False)
    mat_c = cat([cat([c, -s], axis=2), cat([s, c], axis=2)], axis=1).astype(BF16)
    a2 = 2.0 * np.pi * np.outer(np.arange(n2), np.arange(n2)) / n2
    gr, gi = np.cos(a2), -np.sin(a2)
    mat_f = np.block([[gr, -gi], [gi, gr]])
    mat_i = np.block([[gr, gi], [-gi, gr]])
    return dict(n=n, n1=n1, n2=n2, mat_a=mat_a, mat_a_real=mat_a_real, mat_c=mat_c,
                mat_f=_hi_lo_const(mat_f), mat_i=_hi_lo_const(mat_i))


def _filter_body(z_ref, w1_ref, b1_ref, f1_ref, w2_ref, b2_ref, f2_ref, w3_ref, dl_ref, h_ref, s_ref, *, seq):
    i = pl.program_id(0)
    tl = z_ref.shape[0]
    z = z_ref[...]
    a = jnp.sin(f1_ref[...] * (_dot3(z, w1_ref[...]) + b1_ref[...]))
    a = jnp.sin(f2_ref[...] * (_dot3(a, w2_ref[...]) + b2_ref[...]))
    h = _dot(a.astype(BF16), w3_ref[...].astype(BF16))
    win = jnp.exp(-z[:, 0:1] * dl_ref[...])
    h = h * jnp.concatenate([win] * HY_ORDER, axis=1)
    row = i * tl + lax.broadcasted_iota(I32, (tl, 1), 0)
    h = jnp.where(row == seq, 0.0, h)
    h_ref[...] = h

    @pl.when(i == 0)
    def _():
        s_ref[...] = jnp.zeros_like(s_ref)

    s_ref[...] += jnp.broadcast_to(jnp.sum(jnp.abs(h), axis=0, keepdims=True), s_ref.shape)


def _hyena_filter_time(seq, w1, b1, f1, w2, b2, f2, w3, tl=512):
    n = 2 * seq
    tl = min(tl, seq)
    idx = jnp.arange(n, dtype=I32)
    p = jnp.minimum(jnp.where(idx < seq, idx, n - idx), seq - 1).astype(F32)
    t = p / float(seq - 1)
    ang = 2.0 * math.pi * p / seq
    bands = jnp.linspace(1e-4, HY_BANDS - 1, HY_BANDS, dtype=F32)
    phase = ang[:, None] * bands[None, :]
    emb = jnp.concatenate([t[:, None], jnp.cos(phase), -jnp.sin(phase)], axis=-1)
    emb = jnp.pad(emb, ((0, 0), (0, EMB_PAD - emb.shape[1])))
    w1p = jnp.pad(w1.astype(F32), ((0, EMB_PAD - w1.shape[0]), (0, 0)))
    w3d = w3.astype(F32).reshape(HY_FILTER_WIDTH, HY_ORDER, 2, HY_WIDTH).transpose(2, 0, 1, 3)
    w3d = w3d.reshape(2, HY_FILTER_WIDTH, HY_ORDER * HY_WIDTH)
    deltas = jnp.abs(jnp.linspace(math.log(HY_DECAY_TARGET) / HY_SLOW_DECAY,
                                  math.log(HY_DECAY_TARGET) / HY_FAST_DECAY, HY_WIDTH, dtype=F32))[None, :]
    fw = HY_FILTER_WIDTH
    nblk_half = seq // tl
    const = lambda i: (0, 0)
    return pl.pallas_call(
        functools.partial(_filter_body, seq=seq),
        grid=(n // tl,),
        in_specs=[pl.BlockSpec((tl, EMB_PAD), lambda i: (i, 0)),
                  pl.BlockSpec((EMB_PAD, fw), const), pl.BlockSpec((1, fw), const), pl.BlockSpec((1, fw), const),
                  pl.BlockSpec((fw, fw), const), pl.BlockSpec((1, fw), const), pl.BlockSpec((1, fw), const),
                  pl.BlockSpec((None, fw, HY_ORDER * HY_WIDTH), lambda i: (i // nblk_half, 0, 0)),
                  pl.BlockSpec((1, HY_WIDTH), const)],
        out_specs=[pl.BlockSpec((tl, HY_ORDER * HY_WIDTH), lambda i: (i, 0)),
                   pl.BlockSpec((8, HY_ORDER * HY_WIDTH), const)],
        out_shape=[jax.ShapeDtypeStruct((n, HY_ORDER * HY_WIDTH), F32),
                   jax.ShapeDtypeStruct((8, HY_ORDER * HY_WIDTH), F32)],
        compiler_params=_cparams(("arbitrary",)),
        name="hyena_filter",
    )(emb, w1p, b1.astype(F32)[None], f1.astype(F32)[None], w2.astype(F32), b2.astype(F32)[None],
      f2.astype(F32)[None], w3d, deltas)


def _stage_a_real_body(h_ref, m_ref, a_ref):
    n1 = h_ref.shape[0]
    h_t = jnp.swapaxes(h_ref[...], 0, 1)
    rs = [_dot(m_ref[j], h_t[j].astype(BF16)) for j in range(h_ref.shape[1])]
    r = jnp.swapaxes(jnp.stack(rs, axis=0), 0, 1)
    a_ref[0] = r[:n1]
    a_ref[1] = r[n1:]


def _stage_a_real(h, tabs, n2c=8, cw=512):
    n1, n2 = tabs["n1"], tabs["n2"]
    c = h.shape[1]
    h3 = h.reshape(n1, n2, c)
    mat = tabs["mat_a_real"]
    return pl.pallas_call(
        _stage_a_real_body,
        grid=(n2 // n2c, c // cw),
        in_specs=[pl.BlockSpec((n1, n2c, cw), lambda j, k: (0, j, k)),
                  pl.BlockSpec((n2c,) + mat.shape[1:], lambda j, k: (j, 0, 0))],
        out_specs=pl.BlockSpec((2, n1, n2c, cw), lambda j, k: (0, 0, j, k)),
        out_shape=jax.ShapeDtypeStruct((2, n1, n2, c), F32),
        compiler_params=_cparams(("parallel", "parallel")),
        name="dft_a_filter",
    )(h3, mat)


def _pack_pair(re, im):
    hi = lax.bitcast_convert_type(re.astype(BF16).astype(F32), I32)
    lo = lax.bitcast_convert_type(im.astype(BF16).astype(F32), I32)
    return lax.bitcast_convert_type(hi | lax.shift_right_logical(lo, 16), F32)


def _unpack_pair(packed):
    word = lax.bitcast_convert_type(packed, I32)
    re = lax.bitcast_convert_type(word & jnp.int32(-65536), F32)
    im = lax.bitcast_convert_type(lax.shift_left(word, 16), F32)
    return re, im


def _stage_a_body(u_ref, m_ref, a_ref):
    _, r1, n2c, cw = u_ref.shape
    n1 = 2 * r1
    u_t = jnp.swapaxes(u_ref[...].reshape(n1, n2c, cw), 0, 1)
    packed = []
    for j in range(n2c):
        r = _dot(m_ref[j], u_t[j].astype(BF16))
        packed.append(_pack_pair(r[:n1], r[n1:]))
    a_ref[...] = jnp.swapaxes(jnp.stack(packed, axis=0), 0, 1)


def _stage_a(u5, col, tabs, n2c=8):
    n1, n2 = tabs["n1"], tabs["n2"]
    npair, _, r1, _, _ = u5.shape
    cw = HY_WIDTH
    mat = tabs["mat_a"]
    return pl.pallas_call(
        _stage_a_body,
        grid=(npair, n2 // n2c),
        in_specs=[pl.BlockSpec((None, 2, r1, n2c, cw), lambda p, j: (p, 0, 0, j, col)),
                  pl.BlockSpec((n2c,) + mat.shape[1:], lambda p, j: (j, 0, 0))],
        out_specs=pl.BlockSpec((None, n1, n2c, cw), lambda p, j: (p, 0, j, 0)),
        out_shape=jax.ShapeDtypeStruct((npair, n1, n2, cw), F32),
        compiler_params=_cparams(("parallel", "parallel")),
        name="dft_a",
    )(u5, mat)


def _stage_b_filter_body(a_ref, g_ref, s_ref, kr_ref, ki_ref, *, n):
    n2 = a_ref.shape[2]
    scale = 1.0 / (s_ref[0:1, :] * float(n))
    for kk in range(a_ref.shape[1]):
        d = _dot(g_ref[...], jnp.concatenate([a_ref[0, kk], a_ref[1, kk]], axis=0).astype(BF16))
        kr_ref[kk] = d[:n2] * scale
        ki_ref[kk] = d[n2:] * scale


def _stage_b_filter(a, s, tabs):
    n, n1, n2 = tabs["n"], tabs["n1"], tabs["n2"]
    c = a.shape[-1]
    kb = max(1, min(n1, 512 // n2))
    g = tabs["mat_f"][0]
    return pl.pallas_call(
        functools.partial(_stage_b_filter_body, n=n),
        grid=(n1 // kb,),
        in_specs=[pl.BlockSpec((2, kb, n2, c), lambda k: (0, k, 0, 0)),
                  pl.BlockSpec(g.shape, lambda k: (0, 0)),
                  pl.BlockSpec(s.shape, lambda k: (0, 0))],
        out_specs=[pl.BlockSpec((kb, n2, c), lambda k: (k, 0, 0)),
                   pl.BlockSpec((kb, n2, c), lambda k: (k, 0, 0))],
        out_shape=[jax.ShapeDtypeStruct((n1, n2, c), F32), jax.ShapeDtypeStruct((n1, n2, c), F32)],
        compiler_params=_cparams(("parallel",)),
        name="dft_b_filter",
    )(a, g, s)


def _stage_b_body(a_ref, kr_ref, ki_ref, gf_ref, gi_ref, c_ref):
    n2 = a_ref.shape[1]
    for kk in range(a_ref.shape[0]):
        d = _dot(gf_ref[...], jnp.concatenate(_unpack_pair(a_ref[kk]), axis=0).astype(BF16))
        dr, di = d[:n2], d[n2:]
        kr, ki = kr_ref[kk], ki_ref[kk]
        yr = dr * kr - di * ki
        yi = dr * ki + di * kr
        e = _dot(gi_ref[...], jnp.concatenate([yr, yi], axis=0).astype(BF16))
        c_ref[kk] = _pack_pair(e[:n2], e[n2:])


def _stage_b(a, kr, ki, order, tabs):
    n1, n2 = tabs["n1"], tabs["n2"]
    npair = a.shape[0]
    cw = HY_WIDTH
    kb = max(1, min(n1, 1024 // n2))
    gf = tabs["mat_f"][0]
    gi = tabs["mat_i"][0]
    const = lambda k, p: (0, 0)
    return pl.pallas_call(
        _stage_b_body,
        grid=(n1 // kb, npair),
        in_specs=[pl.BlockSpec((None, kb, n2, cw), lambda k, p: (p, k, 0, 0)),
                  pl.BlockSpec((kb, n2, cw), lambda k, p: (k, 0, order)),
                  pl.BlockSpec((kb, n2, cw), lambda k, p: (k, 0, order)),
                  pl.BlockSpec(gf.shape, const), pl.BlockSpec(gi.shape, const)],
        out_specs=pl.BlockSpec((None, kb, n2, cw), lambda k, p: (p, k, 0, 0)),
        out_shape=jax.ShapeDtypeStruct(a.shape, F32),
        compiler_params=_cparams(("parallel", "parallel")),
        name="dft_b",
    )(a, kr, ki, gf, gi)


def _stage_c_body(c_ref, u_ref, g_ref, skip_ref, mc_ref, ma_ref, y_ref, *rest, fuse_a):
    _, r1, n2c, cw = u_ref.shape
    n1 = 2 * r1
    ys = []
    c_t = jnp.swapaxes(c_ref[...], 0, 1)
    for j in range(n2c):
        cc = jnp.concatenate(_unpack_pair(c_t[j]), axis=0)
        ys.append(_dot(mc_ref[j], cc.astype(BF16)))
    y = jnp.swapaxes(jnp.stack(ys, axis=0), 0, 1)
    u = u_ref[...].reshape(n1, n2c, cw)
    g = g_ref[...].reshape(n1, n2c, cw)
    yo = g * (y + u * skip_ref[...].reshape(1, 1, cw))
    y_ref[...] = yo.reshape(2, r1, n2c, cw).astype(y_ref.dtype)
    if fuse_a:
        a_ref = rest[0]
        yo_t = jnp.swapaxes(yo, 0, 1)
        packed = []
        for j in range(n2c):
            r = _dot(ma_ref[j], yo_t[j].astype(BF16))
            packed.append(_pack_pair(r[:n1], r[n1:]))
        a_ref[...] = jnp.swapaxes(jnp.stack(packed, axis=0), 0, 1)


def _stage_c(c, u5, u_col, g5, g_col, skip, tabs, fuse_a, out_dtype, n2c=8):
    n1, n2 = tabs["n1"], tabs["n2"]
    npair, _, r1, _, _ = u5.shape
    cw = HY_WIDTH
    mc = tabs["mat_c"]
    ma = tabs["mat_a"]
    const = lambda p, j: (0, 0)
    per_j = lambda m: pl.BlockSpec((n2c,) + m.shape[1:], lambda p, j: (j, 0, 0))
    out_shape = [jax.ShapeDtypeStruct((npair, 2, r1, n2, cw), out_dtype)]
    out_specs = [pl.BlockSpec((None, 2, r1, n2c, cw), lambda p, j: (p, 0, 0, j, 0))]
    if fuse_a:
        out_shape.append(jax.ShapeDtypeStruct((npair, n1, n2, cw), F32))
        out_specs.append(pl.BlockSpec((None, n1, n2c, cw), lambda p, j: (p, 0, j, 0)))
    return pl.pallas_call(
        functools.partial(_stage_c_body, fuse_a=fuse_a),
        grid=(npair, n2 // n2c),
        in_specs=[pl.BlockSpec((None, n1, n2c, cw), lambda p, j: (p, 0, j, 0)),
                  pl.BlockSpec((None, 2, r1, n2c, cw), lambda p, j: (p, 0, 0, j, u_col)),
                  pl.BlockSpec((None, 2, r1, n2c, cw), lambda p, j: (p, 0, 0, j, g_col)),
                  pl.BlockSpec((1, cw), const), per_j(mc), per_j(ma)],
        out_specs=out_specs,
        out_shape=out_shape,
        compiler_params=_cparams(("parallel", "parallel")),
        name="dft_c",
    )(c, u5, g5, skip, mc, ma)


def _hyena(zc3, kr, ki, skip, tabs):
    bsz, seq, _ = zc3.shape
    n1, n2 = tabs["n1"], tabs["n2"]
    r1 = n1 // 2
    z5 = zc3.reshape(bsz // 2, 2, r1, n2, 3 * HY_WIDTH)
    a = _stage_a(z5, 2, tabs)
    c = _stage_b(a, kr, ki, 0, tabs)
    y1, a2 = _stage_c(c, z5, 2, z5, 0, skip[0:1].astype(F32), tabs, True, F32)
    c2 = _stage_b(a2, kr, ki, 1, tabs)
    (y2,) = _stage_c(c2, y1, 0, z5, 1, skip[1:2].astype(F32), tabs, False, BF16)
    return y2.reshape(bsz * seq, HY_WIDTH)


def _retention_body(sc_ref, qf_ref, kf_ref, vf_ref, cf_ref, sf_ref, qb_ref, kb_ref, vb_ref, cb_ref, sb_ref,
                    of_ref, ob_ref, state, dmat, qwt, kwt):
    ch = RET_CHUNK
    dh = RET_HEAD_DIM

    @pl.when(pl.program_id(1) == 0)
    def _():
        state[...] = jnp.zeros_like(state)
        ci = lax.broadcasted_iota(I32, (ch, ch), 0).astype(F32)
        mi = lax.broadcasted_iota(I32, (ch, ch), 1).astype(F32)
        lag = ci - mi
        for d in range(2):
            for h in range(RET_HEADS):
                lg = sc_ref[d * RET_HEADS + h]
                if d == 0:
                    dmat[d, h] = jnp.where(lag >= 0, jnp.exp(lg * jnp.maximum(lag, 0.0)), 0.0)
                    qwt[d, h] = jnp.exp(lg * (ci + 1.0))
                    kwt[d, h] = jnp.exp(lg * (ch - 1.0 - ci))
                else:
                    dmat[d, h] = jnp.where(lag < 0, jnp.exp(lg * jnp.maximum(-lag, 0.0)), 0.0)
                    qwt[d, h] = jnp.exp(lg * (ch - ci))
                    kwt[d, h] = jnp.exp(lg * ci)

    scale = RET_HEAD_DIM ** -0.5
    refs = ((qf_ref, kf_ref, vf_ref, cf_ref, sf_ref, of_ref), (qb_ref, kb_ref, vb_ref, cb_ref, sb_ref, ob_ref))
    work = [(d, h) for d in range(2) for h in range(RET_HEADS)]
    per_step = qf_ref.shape[0] // ch
    for c in range(per_step):
        rows = (slice(c * ch, (c + 1) * ch), slice((per_step - 1 - c) * ch, (per_step - c) * ch))
        qs, ks, qws, kws, vs = {}, {}, {}, {}, {}
        for d, h in work:
            q_ref, k_ref, v_ref, c_ref, s_ref, _ = refs[d]
            sl = slice(h * dh, (h + 1) * dh)
            cosf = c_ref[rows[d], :]
            sinf = s_ref[rows[d], :]
            q = q_ref[rows[d], sl].astype(F32)
            k = k_ref[rows[d], sl].astype(F32)
            q = q * cosf + pltpu.roll(q, dh // 2, axis=1) * sinf
            k = (k * cosf + pltpu.roll(k, dh // 2, axis=1) * sinf) * scale
            qs[d, h], ks[d, h] = q.astype(BF16), k.astype(BF16)
            qws[d, h], kws[d, h] = (q * qwt[d, h]).astype(BF16), (k * kwt[d, h]).astype(BF16)
            vs[d, h] = v_ref[rows[d], sl]
        scores = {dh_: _dot_nt(qs[dh_], ks[dh_]) for dh_ in work}
        for d, h in work:
            lhs = jnp.concatenate([(scores[d, h] * dmat[d, h]).astype(BF16), qws[d, h]], axis=1)
            rhs = jnp.concatenate([vs[d, h], state[d, h].astype(BF16)], axis=0)
            refs[d][5][rows[d], h * dh:(h + 1) * dh] = _dot(lhs, rhs).astype(BF16)
        for d, h in work:
            cdec = sc_ref[2 * RET_HEADS + d * RET_HEADS + h]
            state[d, h] = state[d, h] * cdec + _dot_tn(kws[d, h], vs[d, h])


def _retention(zret3, decay_logit):
    bsz, seq, _ = zret3.shape
    ch = RET_CHUNK
    nc = seq // ch
    half = RET_HEAD_DIM // 2
    inv_freq = ROPE_BASE ** (-jnp.arange(half, dtype=F32) / half)
    ang = jnp.arange(seq, dtype=F32)[:, None] * inv_freq[None, :]
    cosf = jnp.concatenate([jnp.cos(ang), jnp.cos(ang)], axis=1)
    sinf = jnp.concatenate([-jnp.sin(ang), jnp.sin(ang)], axis=1)
    log_g = jax.nn.log_sigmoid(decay_logit.astype(F32)).reshape(-1)
    scal = jnp.concatenate([log_g, jnp.exp(log_g * ch)])
    w = RET_WIDTH
    per_step = next(c for c in (4, 2, 1) if nc % c == 0)
    nb = nc // per_step
    blk = per_step * ch
    fwd = lambda col: pl.BlockSpec((None, blk, w), lambda b, n, sc: (b, n, col))
    bwd = lambda col: pl.BlockSpec((None, blk, w), lambda b, n, sc: (b, nb - 1 - n, col))
    rope_f = pl.BlockSpec((blk, RET_HEAD_DIM), lambda b, n, sc: (n, 0))
    rope_b = pl.BlockSpec((blk, RET_HEAD_DIM), lambda b, n, sc: (nb - 1 - n, 0))
    grid_spec = pltpu.PrefetchScalarGridSpec(
        num_scalar_prefetch=1,
        grid=(bsz, nb),
        in_specs=[fwd(0), fwd(1), fwd(2), rope_f, rope_f, bwd(0), bwd(1), bwd(2), rope_b, rope_b],
        out_specs=[pl.BlockSpec((None, blk, w), lambda b, n, sc: (b, n, 0)),
                   pl.BlockSpec((None, blk, w), lambda b, n, sc: (b, nb - 1 - n, 0))],
        scratch_shapes=[pltpu.VMEM((2, RET_HEADS, RET_HEAD_DIM, RET_HEAD_DIM), F32),
                        pltpu.VMEM((2, RET_HEADS, ch, ch), F32),
                        pltpu.VMEM((2, RET_HEADS, ch, RET_HEAD_DIM), F32),
                        pltpu.VMEM((2, RET_HEADS, ch, RET_HEAD_DIM), F32)],
    )
    return pl.pallas_call(
        _retention_body,
        grid_spec=grid_spec,
        out_shape=[jax.ShapeDtypeStruct((bsz, seq, w), BF16), jax.ShapeDtypeStruct((bsz, seq, w), BF16)],
        compiler_params=_cparams(("arbitrary", "arbitrary")),
        name="retention",
    )(scal, zret3, zret3, zret3, cosf, sinf, zret3, zret3, zret3, cosf, sinf)


def _outproj_body(x_ref, yhy_ref, of_ref, ob_ref, gr_ref, ghy_ref, gret_ref, whyo_ref, wreto_ref, wo_ref,
                  n2g_ref, wrt_ref, x1_ref, pt_ref):
    dh = RET_HEAD_DIM
    o = of_ref[...].astype(F32) + ob_ref[...].astype(F32)
    parts = []
    for h in range(RET_HEADS):
        oh = o[:, h * dh:(h + 1) * dh]
        parts.append(oh * lax.rsqrt(jnp.mean(oh * oh, axis=-1, keepdims=True) + EPS))
    on = jnp.concatenate(parts, axis=1)
    gr = gr_ref[...].astype(F32)
    ret = (gr * jax.nn.sigmoid(gr)) * on
    y_ret = _dot(ret.astype(BF16), wreto_ref[...])
    y_hy = _dot(yhy_ref[...], whyo_ref[...])
    merged = jax.nn.sigmoid(ghy_ref[...].astype(F32)) * y_hy + jax.nn.sigmoid(gret_ref[...].astype(F32)) * y_ret
    x1 = x_ref[...] + _dot(merged.astype(BF16), wo_ref[...])
    x1_ref[...] = x1
    h2 = x1 * lax.rsqrt(jnp.mean(x1 * x1, axis=-1, keepdims=True) + EPS) * n2g_ref[...]
    logits = _dot_nt(wrt_ref[...], h2.astype(BF16))
    m = jnp.max(logits, axis=0, keepdims=True)
    e = jnp.exp(logits - m)
    pt_ref[...] = e / jnp.sum(e, axis=0, keepdims=True)


def _outproj(x2, yhy, o_f, o_b, zret, zg, whyo, wreto, wo, n2g, wrt, tm=512):
    n_tok = x2.shape[0]
    d = D_MODEL
    row = lambda w, col=0: pl.BlockSpec((tm, w), lambda i: (i, col))
    const = lambda shape: pl.BlockSpec(shape, lambda i: (0, 0))
    return pl.pallas_call(
        _outproj_body,
        grid=(n_tok // tm,),
        in_specs=[row(d), row(HY_WIDTH), row(RET_WIDTH), row(RET_WIDTH), row(RET_WIDTH, 3), row(d, 0), row(d, 1),
                  const((HY_WIDTH, d)), const((RET_WIDTH, d)), const((d, d)), const((1, d)), const((N_EXPERTS, d))],
        out_specs=[row(d), pl.BlockSpec((N_EXPERTS, tm), lambda i: (0, i))],
        out_shape=[jax.ShapeDtypeStruct((n_tok, d), F32), jax.ShapeDtypeStruct((N_EXPERTS, n_tok), F32)],
        compiler_params=_cparams(("parallel",)),
        name="outproj_router",
    )(x2, yhy, o_f, o_b, zret, zg, zg, whyo, wreto, wo, n2g, wrt)


def _select_body(p_ref, upper_ref, lower_ref, lowinc_ref, eye_ref, pos_ref, idx_ref, gate_ref, lo_ref, *, cap):
    rows = p_ref.shape[0]
    p = p_ref[...]
    bits = lax.bitcast_convert_type(p, I32)

    def count(mask):
        return jnp.sum(jnp.sum(mask.astype(F32), axis=1, keepdims=True), axis=0, keepdims=True)

    def bit_step(i, thr):
        cand = thr | jnp.left_shift(jnp.int32(1), 30 - i)
        return jnp.where(count(bits >= cand) >= cap, cand, thr)

    thr = lax.fori_loop(0, 31, bit_step, jnp.zeros((1, 1), I32))
    gt = bits > thr
    eq = bits == thr
    need = cap - count(gt)

    def prefix(mask_f):
        incl = _dot(mask_f.astype(BF16), upper_ref[...])
        tot = jnp.broadcast_to(incl[:, LANES - 1:LANES], incl.shape)
        base = _dot(lower_ref[...], tot.astype(BF16))
        return incl, base, tot

    eq_f = eq.astype(F32)
    incl_e, base_e, _ = prefix(eq_f)
    sel = gt | (eq & (base_e + incl_e - eq_f < need))
    sel_f = sel.astype(F32)
    incl, base, tot = prefix(sel_f)
    pos_ref[...] = jnp.where(sel, (base + incl - 1.0).astype(I32), -1)
    lo_ref[...] = base.astype(I32)

    rowend = base + tot
    incl_t = _dot_nt(lowinc_ref[...], sel_f.astype(BF16)).astype(BF16)
    p_t = []
    rem = p
    for _ in range(3):
        part = rem.astype(BF16)
        rem = rem - part.astype(F32)
        p_t.append(_dot_nt(eye_ref[...], part).astype(BF16))
    table = jnp.concatenate([incl_t] + p_t, axis=0)
    groups = 2 if (cap // LANES) % 2 == 0 else 1
    sw = groups * LANES
    wide = lambda a: jnp.concatenate([a] * groups, axis=1)
    rowend_w, tot_w = wide(rowend), wide(tot)
    r_iota = lax.broadcasted_iota(I32, (rows, sw), 0).astype(F32)
    lane_iota = lax.broadcasted_iota(I32, (LANES, sw), 0).astype(F32)

    def slot_tile(ts, carry):
        s = (ts * sw + lax.broadcasted_iota(I32, (1, sw), 1)).astype(F32)
        done = rowend_w <= s
        row = jnp.sum(done.astype(F32), axis=0, keepdims=True)
        before = jnp.sum(jnp.where(done, tot_w, 0.0), axis=0, keepdims=True)
        onehot_t = (r_iota == row).astype(BF16)
        got = _dot(table, onehot_t)
        g_t = got[:LANES]
        lane = jnp.sum((g_t <= s - before).astype(F32), axis=0, keepdims=True)
        tok = (row * LANES + lane).astype(I32)
        p_row = got[LANES:2 * LANES] + got[2 * LANES:3 * LANES] + got[3 * LANES:]
        gate = jnp.sum(jnp.where(lane_iota == lane, p_row, 0.0), axis=0, keepdims=True)
        for k in range(groups):
            idx_ref[pl.ds(ts * groups + k, 1), :] = tok[:, k * LANES:(k + 1) * LANES]
            gate_ref[pl.ds(ts * groups + k, 1), :] = gate[:, k * LANES:(k + 1) * LANES]
        return carry

    lax.fori_loop(0, cap // sw, slot_tile, 0)


def _select(pt3, cap):
    n_e, rows, _ = pt3.shape
    ii = np.arange(LANES)
    upper = jnp.asarray(ii[:, None] <= ii[None, :], BF16)
    lowinc = jnp.asarray(ii[None, :] <= ii[:, None], BF16)
    eye = jnp.asarray(ii[None, :] == ii[:, None], BF16)
    rr = np.arange(rows)
    lower = jnp.asarray(rr[None, :] < rr[:, None], BF16)
    const = lambda shape: pl.BlockSpec(shape, lambda e: (0, 0))
    tok_spec = pl.BlockSpec((None, rows, LANES), lambda e: (e, 0, 0))
    slot_spec = pl.BlockSpec((None, cap // LANES, LANES), lambda e: (e, 0, 0))
    return pl.pallas_call(
        functools.partial(_select_body, cap=cap),
        grid=(n_e,),
        in_specs=[tok_spec, const((LANES, LANES)), const((rows, rows)), const((LANES, LANES)), const((LANES, LANES))],
        out_specs=[tok_spec, slot_spec, slot_spec, tok_spec],
        out_shape=[jax.ShapeDtypeStruct((n_e, rows, LANES), I32),
                   jax.ShapeDtypeStruct((n_e, cap // LANES, LANES), I32),
                   jax.ShapeDtypeStruct((n_e, cap // LANES, LANES), F32),
                   jax.ShapeDtypeStruct((n_e, rows, LANES), I32)],
        compiler_params=_cparams(("parallel",)),
        name="expert_select",
    )(pt3, upper, lower, lowinc, eye)


def _ffn_body(idx_ref, h_hbm, n2g_ref, gate_ref, wg_ref, wu_ref, wd_ref, ye_ref, xbuf, xb, sems, *,
              pairs_per_expert):
    s = xbuf.shape[1]
    step = pl.program_id(0) * pairs_per_expert + pl.program_id(1)
    n_steps = pl.num_programs(0) * pairs_per_expert

    def row_copy(tile, i, buf):
        tok = idx_ref[tile * s + i]
        return pltpu.make_async_copy(h_hbm.at[pl.ds(tok, 1), :], xbuf.at[buf, pl.ds(i, 1), :], sems.at[buf])

    def wait_rows(buf):
        pltpu.make_async_copy(h_hbm.at[pl.ds(0, s), :], xbuf.at[buf], sems.at[buf]).wait()

    def ffn(buf):
        x = xbuf[buf]
        xb[...] = (x * lax.rsqrt(jnp.mean(x * x, axis=-1, keepdims=True) + EPS) * n2g_ref[...]).astype(BF16)
        y = None
        for f0 in range(0, EXPERT_FF, FF_CHUNK):
            f1 = min(f0 + FF_CHUNK, EXPERT_FF)
            a = _dot(xb[...], wg_ref[:, f0:f1])
            b = _dot(xb[...], wu_ref[:, f0:f1])
            hid = ((a * jax.nn.sigmoid(a)) * b).astype(BF16)
            part = _dot(hid, wd_ref[f0:f1, :])
            y = part if y is None else y + part
        for k in range(s // LANES):
            row0 = buf * s + k * LANES
            gate = gate_ref[row0 // LANES:row0 // LANES + 1, :]
            col = jnp.transpose(jnp.broadcast_to(gate, (LANES, LANES)))[:, 0:1]
            yk = y[k * LANES:(k + 1) * LANES] * col
            ye_ref[row0:row0 + LANES, :] = _pack_pair(yk[:, :D_MODEL // 2], yk[:, D_MODEL // 2:])

    @pl.when(step == 0)
    def _():
        def one(i, carry):
            row_copy(0, i, 0).start()
            return carry
        lax.fori_loop(0, s, one, 0, unroll=8)

    for i in range(s):
        row_copy(2 * step + 1, i, 1).start()
    wait_rows(0)
    ffn(0)

    @pl.when(step + 1 < n_steps)
    def _():
        for i in range(s):
            row_copy(2 * step + 2, i, 0).start()

    wait_rows(1)
    ffn(1)


def _expert_ffn(idx_flat, x1, n2g, gates, wg, wu, wd, cap, s=512):
    s = min(s, cap // 2)
    assert s % LANES == 0 and cap % (2 * s) == 0
    pairs = cap // (2 * s)
    g3 = gates.reshape(N_EXPERTS * pairs, 2 * s // LANES, LANES)
    grid_spec = pltpu.PrefetchScalarGridSpec(
        num_scalar_prefetch=1,
        grid=(N_EXPERTS, pairs),
        in_specs=[pl.BlockSpec(memory_space=pl.ANY),
                  pl.BlockSpec((1, D_MODEL), lambda e, j, idx: (0, 0)),
                  pl.BlockSpec((None, 2 * s // LANES, LANES), lambda e, j, idx: (e * pairs + j, 0, 0)),
                  pl.BlockSpec((None, D_MODEL, EXPERT_FF), lambda e, j, idx: (e, 0, 0)),
                  pl.BlockSpec((None, D_MODEL, EXPERT_FF), lambda e, j, idx: (e, 0, 0)),
                  pl.BlockSpec((None, EXPERT_FF, D_MODEL), lambda e, j, idx: (e, 0, 0))],
        out_specs=pl.BlockSpec((2 * s, D_MODEL // 2), lambda e, j, idx: (e * pairs + j, 0)),
        scratch_shapes=[pltpu.VMEM((2, s, D_MODEL), F32), pltpu.VMEM((s, D_MODEL), BF16),
                        pltpu.SemaphoreType.DMA((2,))],
    )
    return pl.pallas_call(
        functools.partial(_ffn_body, pairs_per_expert=pairs),
        grid_spec=grid_spec,
        out_shape=jax.ShapeDtypeStruct((N_EXPERTS * cap, D_MODEL // 2), F32),
        compiler_params=_cparams(("arbitrary", "arbitrary")),
        name="expert_ffn",
    )(idx_flat, x1, n2g, g3, wg, wu, wd)


def _combine_body(lo_ref, np_ref, x1_ref, pos_ref, nfg_ref, expand_ref, lov_ref, ye_hbm, o_ref, win, sems, *, cap,
                  rows_total, n_rows):
    step = pl.program_id(0)
    n_steps = pl.num_programs(0)
    groups = x1_ref.shape[0] // LANES
    w = COMBINE_WIN
    cur = step % 2

    def copies(rr, m, buf, g):
        out = []
        for e in range(N_EXPERTS):
            first = e * cap + lo_ref[e * n_rows + rr]
            intended = (first // COMBINE_ALIGN) * COMBINE_ALIGN + m * w
            actual = pl.multiple_of(jnp.minimum(intended, rows_total - w), COMBINE_ALIGN)
            out.append(pltpu.make_async_copy(ye_hbm.at[pl.ds(actual, w), :], win.at[buf, g, pl.ds(e * w, w), :],
                                             sems.at[buf, g, e]))
        return out

    def wait_windows(buf, g):
        for e in range(N_EXPERTS):
            pltpu.make_async_copy(ye_hbm.at[pl.ds(0, w), :], win.at[buf, g, pl.ds(e * w, w), :],
                                  sems.at[buf, g, e]).wait()

    def contribution(rr, m, buf, g):
        lane = lax.broadcasted_iota(I32, (1, N_EXPERTS), 1)
        first = lane * cap + lov_ref[pl.ds(rr, 1), :]
        intended = (first & jnp.int32(-COMBINE_ALIGN)) + m * w
        actual = jnp.minimum(intended, rows_total - w)
        pos = pos_ref[g * LANES:(g + 1) * LANES, :]
        glob = pos + lane * cap
        rel = glob - intended
        valid = (pos >= 0) & (rel >= 0) & (rel < w)
        local = jnp.where(valid, glob - actual, -1).astype(F32).astype(BF16)
        spread = _dot(local, expand_ref[...])
        col = (lax.broadcasted_iota(I32, (1, N_EXPERTS * w), 1) & (w - 1)).astype(F32)
        onehot = (spread == col).astype(BF16)
        lo_cols, hi_cols = _unpack_pair(win[buf, g])
        return jnp.concatenate([_dot(onehot, lo_cols.astype(BF16)), _dot(onehot, hi_cols.astype(BF16))], axis=1)

    @pl.when(step == 0)
    def _():
        for g in range(groups):
            for cp in copies(g, 0, 0, g):
                cp.start()

    @pl.when(step + 1 < n_steps)
    def _():
        for g in range(groups):
            for cp in copies((step + 1) * groups + g, 0, 1 - cur, g):
                cp.start()

    for g in range(groups):
        rr = step * groups + g
        wait_windows(cur, g)
        acc = x1_ref[g * LANES:(g + 1) * LANES, :] + contribution(rr, 0, cur, g)

        def extra_pass(m, acc, rr=rr, g=g):
            for cp in copies(rr, m, cur, g):
                cp.start()
            wait_windows(cur, g)
            return acc + contribution(rr, m, cur, g)

        acc = lax.fori_loop(1, np_ref[rr], extra_pass, acc)
        o_ref[g * LANES:(g + 1) * LANES, :] = (
            acc * lax.rsqrt(jnp.mean(acc * acc, axis=-1, keepdims=True) + EPS) * nfg_ref[...])


def _combine(lo, x1, pos_t, nfg, ye, cap):
    n_tok = x1.shape[0]
    w = COMBINE_WIN
    rows_total = ye.shape[0]
    n_rows = n_tok // LANES
    groups = next(c for c in (4, 2, 1) if n_rows % c == 0)
    tm = groups * LANES
    nxt =jnp.concatenate([lo[:, 1:], jnp.full((N_EXPERTS, 1), cap, I32)], axis=1)
    span = lo % COMBINE_ALIGN + (nxt - lo)
    n_pass = jnp.maximum(jnp.max((span + w - 1) // w, axis=0), 1).astype(I32)
    ee = np.arange(N_EXPERTS)
    expand = jnp.asarray(ee[:, None] == (np.arange(N_EXPERTS * w) // w)[None, :], BF16)
    grid_spec = pltpu.PrefetchScalarGridSpec(
        num_scalar_prefetch=2,
        grid=(n_rows // groups,),
        in_specs=[pl.BlockSpec((tm, D_MODEL), lambda i, lo_, np_: (i, 0)),
                  pl.BlockSpec((tm, N_EXPERTS), lambda i, lo_, np_: (i, 0)),
                  pl.BlockSpec((1, D_MODEL), lambda i, lo_, np_: (0, 0)),
                  pl.BlockSpec((N_EXPERTS, N_EXPERTS * w), lambda i, lo_, np_: (0, 0)),
                  pl.BlockSpec((n_rows, N_EXPERTS), lambda i, lo_, np_: (0, 0)),
                  pl.BlockSpec(memory_space=pl.ANY)],
        out_specs=pl.BlockSpec((tm, D_MODEL), lambda i, lo_, np_: (i, 0)),
        scratch_shapes=[pltpu.VMEM((2, groups, N_EXPERTS * w, D_MODEL // 2), F32),
                        pltpu.SemaphoreType.DMA((2, groups, N_EXPERTS))],
    )
    return pl.pallas_call(
        functools.partial(_combine_body, cap=cap, rows_total=rows_total, n_rows=n_rows),
        grid_spec=grid_spec,
        out_shape=jax.ShapeDtypeStruct((n_tok, D_MODEL), F32),
        compiler_params=_cparams(("arbitrary",)),
        name="moe_combine_norm",
    )(lo.reshape(-1), n_pass, x1, pos_t, nfg, expand, lo.T, ye)


def _trunk(x, w):
    bsz, seq, d = x.shape
    n_tok = bsz * seq
    x2 = x.reshape(n_tok, d)
    tabs = _fft_tables(seq)

    zc, zret, zg = _inproj(x2, w["norm1_g"], w["w_in"], w["hy_conv_w"], w["hy_conv_b"], seq)
    zc = zc.reshape(bsz, seq, -1)

    h_time, h_abs = _hyena_filter_time(seq, w["hy_w1"], w["hy_b1"], w["hy_freq1"], w["hy_w2"], w["hy_b2"],
                                       w["hy_freq2"], w["hy_w3"])
    kr, ki = _stage_b_filter(_stage_a_real(h_time, tabs), h_abs, tabs)
    yhy = _hyena(zc, kr, ki, w["hy_skip"], tabs)

    o_f, o_b = _retention(zret.reshape(bsz, seq, -1), w["ret_decay_logit"])
    x1, pt = _outproj(x2, yhy, o_f.reshape(n_tok, -1), o_b.reshape(n_tok, -1), zret, zg,
                      w["w_hy_out"], w["w_ret_out"], w["w_o"], w["norm2_g"], w["w_router_t"])

    cap = CAPACITY_FACTOR * n_tok // N_EXPERTS
    rows = n_tok // LANES
    pos, idx, gates, lo = _select(pt.reshape(N_EXPERTS, rows, LANES), cap)
    ye = _expert_ffn(idx.reshape(-1), x1, w["norm2_g"], gates, w["w_gate"], w["w_up"], w["w_down"], cap)
    pos_t = pos.reshape(N_EXPERTS, n_tok).T
    y = _combine(lo[:, :, 0], x1, pos_t, w["norm_f_g"], ye, cap)
    return y.reshape(bsz, seq, d)


def kernel(x_prompt, x_sample, norm1_g, w_in, hy_conv_w, hy_conv_b, hy_w1, hy_b1, hy_freq1, hy_w2, hy_b2, hy_freq2,
           hy_w3, hy_skip, ret_decay_logit, w_hy_out, w_ret_out, w_o, norm2_g, w_router, w_gate, w_up, w_down,
           norm_f_g):
    layer = 0
    w = dict(
        norm1_g=norm1_g[layer].astype(F32)[None], w_in=w_in[layer].astype(BF16),
        hy_conv_w=hy_conv_w[layer].astype(F32), hy_conv_b=hy_conv_b[layer].astype(F32)[None],
        hy_w1=hy_w1[layer], hy_b1=hy_b1[layer], hy_freq1=hy_freq1[layer], hy_w2=hy_w2[layer], hy_b2=hy_b2[layer],
        hy_freq2=hy_freq2[layer], hy_w3=hy_w3[layer], hy_skip=hy_skip[layer],
        ret_decay_logit=ret_decay_logit[layer],
        w_hy_out=w_hy_out[layer].astype(BF16), w_ret_out=w_ret_out[layer].astype(BF16), w_o=w_o[layer].astype(BF16),
        norm2_g=norm2_g[layer].astype(F32)[None], w_router_t=w_router[layer].T.astype(BF16),
        w_gate=w_gate[layer].astype(BF16), w_up=w_up[layer].astype(BF16), w_down=w_down[layer].astype(BF16),
        norm_f_g=norm_f_g.astype(F32)[None],
    )
    return _trunk(x_prompt, w), _trunk(x_sample, w)
```

```python
import functools
import math

import numpy as np
import jax
import jax.numpy as jnp
from jax import lax
from jax.experimental import pallas as pl
from jax.experimental.pallas import tpu as pltpu

F32 = jnp.float32
BF16 = jnp.bfloat16
I32 = jnp.int32

D_MODEL = 1024
HY_WIDTH = 512
HY_ORDER = 2
HY_BANDS = 16
HY_FILTER_WIDTH = 64
HY_FAST_DECAY = 0.3
HY_SLOW_DECAY = 1.5
HY_DECAY_TARGET = 1e-2
RET_WIDTH = 512
RET_HEADS = 4
RET_HEAD_DIM = 128
RET_CHUNK = 128
ROPE_BASE = 10000.0
N_EXPERTS = 16
EXPERT_FF = 1408
CAPACITY_FACTOR = 2
EPS = 1e-6

LANES = 128
SUBLANES = 8
EMB_PAD = 128
DFT_N1_MAX = 128
FF_CHUNK = 256
COMBINE_ALIGN = SUBLANES
COMBINE_WIN = 32


def _cparams(sem, vmem_mb=48):
    return pltpu.CompilerParams(dimension_semantics=sem, vmem_limit_bytes=vmem_mb * 1024 * 1024)


def _dot(a, b):
    return jnp.dot(a, b, preferred_element_type=F32)


def _dot_nt(a, b):
    return lax.dot_general(a, b, (((1,), (1,)), ((), ())), preferred_element_type=F32)


def _dot_tn(a, b):
    return lax.dot_general(a, b, (((0,), (0,)), ((), ())), preferred_element_type=F32)


def _split(a):
    hi = a.astype(BF16)
    lo = (a - hi.astype(F32)).astype(BF16)
    return hi, lo


def _dot_const(m_hi, m_lo, x, passes):
    xh = x.astype(BF16)
    r = _dot(m_hi, xh)
    if passes >= 3:
        xl = (x - xh.astype(F32)).astype(BF16)
        r = r + _dot(m_lo, xh) + _dot(m_hi, xl)
    return r


def _dot3(a, b):
    ah, al = _split(a)
    bh, bl = _split(b)
    return _dot(ah, bh) + _dot(al, bh) + _dot(ah, bl)


def _inproj_body(x_ref, g_ref, w_ref, cw_ref, cb_ref, zc_ref, zret_ref, zg_ref, z_prev, row_prev, *, tiles_per_seq):
    i = pl.program_id(0)
    tm = x_ref.shape[0]
    n_hy = zc_ref.shape[1]
    n_ret = zret_ref.shape[1]

    @pl.when(i == 0)
    def _():
        z_prev[...] = jnp.zeros_like(z_prev)
        row_prev[...] = jnp.zeros_like(row_prev)

    x = x_ref[...]
    ms = jnp.mean(x * x, axis=-1, keepdims=True)
    h = (x * lax.rsqrt(ms + EPS) * g_ref[...]).astype(BF16)
    z_hy = _dot(h, w_ref[:, :n_hy])
    zret_ref[...] = _dot(h, w_ref[:, n_hy:n_hy + n_ret]).astype(BF16)
    zg_ref[...] = _dot(h, w_ref[:, n_hy + n_ret:]).astype(BF16)

    t_prev = (i - 1) % tiles_per_seq
    zp = z_prev[...]
    before = jnp.where(t_prev == 0, 0.0, row_prev[0:1, :])
    after = jnp.where(t_prev == tiles_per_seq - 1, 0.0, z_hy[0:1, :])
    row = lax.broadcasted_iota(I32, (tm, 1), 0)
    zm1 = jnp.where(row == 0, before, pltpu.roll(zp, 1, axis=0))
    zp1 = jnp.where(row == tm - 1, after, pltpu.roll(zp, tm - 1, axis=0))
    zc_ref[...] = zm1 * cw_ref[0:1, :] + zp * cw_ref[1:2, :] + zp1 * cw_ref[2:3, :] + cb_ref[...]
    row_prev[...] = jnp.broadcast_to(zp[tm - 1:tm, :], row_prev.shape)
    z_prev[...] = z_hy


def _inproj(x2, g, w_bf, conv_w, conv_b, seq, tm=512):
    n_tok = x2.shape[0]
    n_hy, n_ret, n_g = 3 * HY_WIDTH, 4 * RET_WIDTH, 2 * D_MODEL
    tm = min(tm, seq)
    n_tiles = n_tok // tm
    cur = lambda i: (jnp.minimum(i, n_tiles - 1), 0)
    const = lambda i: (0, 0)
    return pl.pallas_call(
        functools.partial(_inproj_body, tiles_per_seq=seq // tm),
        grid=(n_tiles + 1,),
        in_specs=[pl.BlockSpec((tm, D_MODEL), cur),
                  pl.BlockSpec((1, D_MODEL), const),
                  pl.BlockSpec((D_MODEL, n_hy + n_ret + n_g), const, pipeline_mode=pl.Buffered(1)),
                  pl.BlockSpec((3, n_hy), const),
                  pl.BlockSpec((1, n_hy), const)],
        out_specs=[pl.BlockSpec((tm, n_hy), lambda i: (jnp.maximum(i - 1, 0), 0)),
                   pl.BlockSpec((tm, n_ret), cur),
                   pl.BlockSpec((tm, n_g), cur)],
        out_shape=[jax.ShapeDtypeStruct((n_tok, n_hy), F32),
                   jax.ShapeDtypeStruct((n_tok, n_ret), BF16),
                   jax.ShapeDtypeStruct((n_tok, n_g), BF16)],
        scratch_shapes=[pltpu.VMEM((tm, n_hy), F32), pltpu.VMEM((8, n_hy), F32)],
        compiler_params=_cparams(("arbitrary",)),
        name="inproj_shortconv",
    )(x2, g, w_bf, conv_w, conv_b)


def _fft_dims(seq):
    n = 2 * seq
    n1 = min(DFT_N1_MAX, 1 << (n.bit_length() // 2))
    n2 = n // n1
    assert n1 * n2 == n and n1 % 2 == 0
    return n, n1, n2


def _hi_lo_const(m):
    m = np.asarray(m, np.float64)
    hi = jnp.asarray(m, F32).astype(BF16)
    lo = (jnp.asarray(m, F32) - hi.astype(F32)).astype(BF16)
    return hi, lo


def _fft_tables(seq):
    n, n1, n2 = _fft_dims(seq)
    r1 = n1 // 2
    k1 = jnp.arange(n1, dtype=I32)
    j = jnp.arange(n2, dtype=I32)

    at = ((j[:, None] * k1[None, :]) % n).astype(F32) * (2.0 * math.pi / n)
    ct, st = jnp.cos(at), jnp.sin(at)

    def cos_sin(n1_count, k1_first):
        a1 = ((k1[:, None] * jnp.arange(n1_count, dtype=I32)[None, :]) % n1).astype(F32) * (2.0 * math.pi / n1)
        cf, sf = jnp.cos(a1), jnp.sin(a1)
        if k1_first:
            cf, sf, ctb, stb = cf[None], sf[None], ct[:, :, None], st[:, :, None]
        else:
            cf, sf, ctb, stb = cf.T[None], sf.T[None], ct[:, None, :], st[:, None, :]
        return cf * ctb - sf * stb, sf * ctb + cf * stb

    cat = jnp.concatenate
    c, s = cos_sin(r1, True)
    mat_a = cat([cat([c, s], axis=2), cat([-s, c], axis=2)], axis=1).astype(BF16)
    c, s = cos_sin(n1, True)
    mat_a_real = cat([c, -s], axis=1).astype(BF16)
    c, s = cos_sin(r1, False)
    mat_c = cat([cat([c, -s], axis=2), cat([s, c], axis=2)], axis=1).astype(BF16)
    a2 = 2.0 * np.pi * np.outer(np.arange(n2), np.arange(n2)) / n2
    gr, gi = np.cos(a2), -np.sin(a2)
    mat_f = np.block([[gr, -gi], [gi, gr]])
    mat_i = np.block([[gr, gi], [-gi, gr]])
    return dict(n=n, n1=n1, n2=n2, mat_a=mat_a, mat_a_real=mat_a_real, mat_c=mat_c,
                mat_f=_hi_lo_const(mat_f), mat_i=_hi_lo_const(mat_i))


def _filter_body(z_ref, zt_ref, w1t_ref, b1_ref, f1_ref, w2t_ref, b2_ref, f2_ref, w3_ref, dl_ref, h_ref, s_ref, *,
                 seq):
    i = pl.program_id(0)
    tl = z_ref.shape[0]
    a = jnp.sin(f1_ref[...] * (_dot3(w1t_ref[...], zt_ref[...]) + b1_ref[...]))
    a = jnp.sin(f2_ref[...] * (_dot3(w2t_ref[...], a) + b2_ref[...]))
    h = _dot_tn(a.astype(BF16), w3_ref[...].astype(BF16))
    win = jnp.exp(-z_ref[:, 0:1] * dl_ref[...])
    h = h * jnp.concatenate([win] * HY_ORDER, axis=1)
    row = i * tl + lax.broadcasted_iota(I32, (tl, 1), 0)
    h = jnp.where(row == seq, 0.0, h)
    h_ref[...] = h

    @pl.when(i == 0)
    def _():
        s_ref[...] = jnp.zeros_like(s_ref)

    s_ref[...] += jnp.broadcast_to(jnp.sum(jnp.abs(h), axis=0, keepdims=True), s_ref.shape)


def _hyena_filter_time(seq, w1, b1, f1, w2, b2, f2, w3, tl=512):
    n = 2 * seq
    tl = min(tl, seq)
    idx = jnp.arange(n, dtype=I32)
    p = jnp.minimum(jnp.where(idx < seq, idx, n - idx), seq - 1).astype(F32)
    t = p / float(seq - 1)
    ang = 2.0 * math.pi * p / seq
    bands = jnp.linspace(1e-4, HY_BANDS - 1, HY_BANDS, dtype=F32)
    phase = ang[:, None] * bands[None, :]
    emb = jnp.concatenate([t[:, None], jnp.cos(phase), -jnp.sin(phase)], axis=-1)
    emb = jnp.pad(emb, ((0, 0), (0, EMB_PAD - emb.shape[1])))
    w1p = jnp.pad(w1.astype(F32), ((0, EMB_PAD - w1.shape[0]), (0, 0)))
    w3d = w3.astype(F32).reshape(HY_FILTER_WIDTH, HY_ORDER, 2, HY_WIDTH).transpose(2, 0, 1, 3)
    w3d = w3d.reshape(2, HY_FILTER_WIDTH, HY_ORDER * HY_WIDTH)
    deltas = jnp.abs(jnp.linspace(math.log(HY_DECAY_TARGET) / HY_SLOW_DECAY,
                                  math.log(HY_DECAY_TARGET) / HY_FAST_DECAY, HY_WIDTH, dtype=F32))[None, :]
    fw = HY_FILTER_WIDTH
    nblk_half = seq // tl
    const = lambda i: (0, 0)
    col = lambda v: v.astype(F32)[:, None]
    return pl.pallas_call(
        functools.partial(_filter_body, seq=seq),
        grid=(n // tl,),
        in_specs=[pl.BlockSpec((tl, EMB_PAD), lambda i: (i, 0)),
                  pl.BlockSpec((EMB_PAD, tl), lambda i: (0, i)),
                  pl.BlockSpec((fw, EMB_PAD), const), pl.BlockSpec((fw, 1), const), pl.BlockSpec((fw, 1), const),
                  pl.BlockSpec((fw, fw), const), pl.BlockSpec((fw, 1), const), pl.BlockSpec((fw, 1), const),
                  pl.BlockSpec((None, fw, HY_ORDER * HY_WIDTH), lambda i: (i // nblk_half, 0, 0)),
                  pl.BlockSpec((1, HY_WIDTH), const)],
        out_specs=[pl.BlockSpec((tl, HY_ORDER * HY_WIDTH), lambda i: (i, 0)),
                   pl.BlockSpec((8, HY_ORDER * HY_WIDTH), const)],
        out_shape=[jax.ShapeDtypeStruct((n, HY_ORDER * HY_WIDTH), F32),
                   jax.ShapeDtypeStruct((8, HY_ORDER * HY_WIDTH), F32)],
        compiler_params=_cparams(("arbitrary",)),
        name="hyena_filter",
    )(emb, emb.T, w1p.T, col(b1), col(f1), w2.astype(F32).T, col(b2), col(f2), w3d, deltas)


def _stage_a_real_body(h_ref, m_ref, a_ref):
    n1 = h_ref.shape[0]
    h_t = jnp.swapaxes(h_ref[...], 0, 1)
    packed = []
    for j in range(h_ref.shape[1]):
        r = _dot(m_ref[j], h_t[j].astype(BF16))
        packed.append(_pack_pair(r[:n1], r[n1:]))
    a_ref[...] = jnp.swapaxes(jnp.stack(packed, axis=0), 0, 1)


def _stage_a_real(h, tabs, n2c=8, cw=512):
    n1, n2 = tabs["n1"], tabs["n2"]
    c = h.shape[1]
    h3 = h.reshape(n1, n2, c)
    mat = tabs["mat_a_real"]
    return pl.pallas_call(
        _stage_a_real_body,
        grid=(n2 // n2c, c // cw),
        in_specs=[pl.BlockSpec((n1, n2c, cw), lambda j, k: (0, j, k)),
                  pl.BlockSpec((n2c,) + mat.shape[1:], lambda j, k: (j, 0, 0))],
        out_specs=pl.BlockSpec((n1, n2c, cw), lambda j, k: (0, j, k)),
        out_shape=jax.ShapeDtypeStruct((n1, n2, c), F32),
        compiler_params=_cparams(("parallel", "parallel")),
        name="dft_a_filter",
    )(h3, mat)


def _pack_pair(re, im):
    hi = lax.bitcast_convert_type(re.astype(BF16).astype(F32), I32)
    lo = lax.bitcast_convert_type(im.astype(BF16).astype(F32), I32)
    return lax.bitcast_convert_type(hi | lax.shift_right_logical(lo, 16), F32)


def _unpack_pair(packed):
    word = lax.bitcast_convert_type(packed, I32)
    re = lax.bitcast_convert_type(word & jnp.int32(-65536), F32)
    im = lax.bitcast_convert_type(lax.shift_left(word, 16), F32)
    return re, im


def _stage_a_body(u_ref, m_ref, a_ref):
    _, r1, n2c, cw = u_ref.shape
    n1 = 2 * r1
    u_t = jnp.swapaxes(u_ref[...].reshape(n1, n2c, cw), 0, 1)
    packed = []
    for j in range(n2c):
        r = _dot(m_ref[j], u_t[j].astype(BF16))
        packed.append(_pack_pair(r[:n1], r[n1:]))
    a_ref[...] = jnp.swapaxes(jnp.stack(packed, axis=0), 0, 1)


def _stage_a(u5, col, tabs, n2c=8):
    n1, n2 = tabs["n1"], tabs["n2"]
    npair, _, r1, _, _ = u5.shape
    cw = HY_WIDTH
    mat = tabs["mat_a"]
    return pl.pallas_call(
        _stage_a_body,
        grid=(npair, n2 // n2c),
        in_specs=[pl.BlockSpec((None, 2, r1, n2c, cw), lambda p, j: (p, 0, 0, j, col)),
                  pl.BlockSpec((n2c,) + mat.shape[1:], lambda p, j: (j, 0, 0))],
        out_specs=pl.BlockSpec((None, n1, n2c, cw), lambda p, j: (p, 0, j, 0)),
        out_shape=jax.ShapeDtypeStruct((npair, n1, n2, cw), F32),
        compiler_params=_cparams(("parallel", "parallel")),
        name="dft_a",
    )(u5, mat)


def _stage_b_filter_body(a_ref, g_ref, s_ref, kr_ref, ki_ref, *, n):
    n2 = a_ref.shape[1]
    scale = 1.0 / (s_ref[0:1, :] * float(n))
    for kk in range(a_ref.shape[0]):
        d = _dot(g_ref[...], jnp.concatenate(_unpack_pair(a_ref[kk]), axis=0).astype(BF16))
        kr_ref[kk] = d[:n2] * scale
        ki_ref[kk] = d[n2:] * scale


def _stage_b_filter(a, s, tabs):
    n, n1, n2 = tabs["n"], tabs["n1"], tabs["n2"]
    c = a.shape[-1]
    kb = max(1, min(n1, 512 // n2))
    g = tabs["mat_f"][0]
    return pl.pallas_call(
        functools.partial(_stage_b_filter_body, n=n),
        grid=(n1 // kb,),
        in_specs=[pl.BlockSpec((kb, n2, c), lambda k: (k, 0, 0)),
                  pl.BlockSpec(g.shape, lambda k: (0, 0)),
                  pl.BlockSpec(s.shape, lambda k: (0, 0))],
        out_specs=[pl.BlockSpec((kb, n2, c), lambda k: (k, 0, 0)),
                   pl.BlockSpec((kb, n2, c), lambda k: (k, 0, 0))],
        out_shape=[jax.ShapeDtypeStruct((n1, n2, c), F32), jax.ShapeDtypeStruct((n1, n2, c), F32)],
        compiler_params=_cparams(("parallel",)),
        name="dft_b_filter",
    )(a, g, s)


def _stage_b_body(a_ref, kr_ref, ki_ref, gf_ref, gi_ref, c_ref):
    n2 = a_ref.shape[1]
    for kk in range(a_ref.shape[0]):
        d = _dot(gf_ref[...], jnp.concatenate(_unpack_pair(a_ref[kk]), axis=0).astype(BF16))
        dr, di = d[:n2], d[n2:]
        kr, ki = kr_ref[kk], ki_ref[kk]
        yr = dr * kr - di * ki
        yi = dr * ki + di * kr
        e = _dot(gi_ref[...], jnp.concatenate([yr, yi], axis=0).astype(BF16))
        c_ref[kk] = _pack_pair(e[:n2], e[n2:])


def _stage_b(a, kr, ki, order, tabs):
    n1, n2 = tabs["n1"], tabs["n2"]
    npair = a.shape[0]
    cw = HY_WIDTH
    kb = max(1, min(n1, 1024 // n2))
    gf = tabs["mat_f"][0]
    gi = tabs["mat_i"][0]
    const = lambda k, p: (0, 0)
    return pl.pallas_call(
        _stage_b_body,
        grid=(n1 // kb, npair),
        in_specs=[pl.BlockSpec((None, kb, n2, cw), lambda k, p: (p, k, 0, 0)),
                  pl.BlockSpec((kb, n2, cw), lambda k, p: (k, 0, order)),
                  pl.BlockSpec((kb, n2, cw), lambda k, p: (k, 0, order)),
                  pl.BlockSpec(gf.shape, const), pl.BlockSpec(gi.shape, const)],
        out_specs=pl.BlockSpec((None, kb, n2, cw), lambda k, p: (p, k, 0, 0)),
        out_shape=jax.ShapeDtypeStruct(a.shape, F32),
        compiler_params=_cparams(("parallel", "parallel")),
        name="dft_b",
    )(a, kr, ki, gf, gi)


def _stage_c_body(c_ref, u_ref, g_ref, skip_ref, mc_ref, ma_ref, y_ref, *rest, fuse_a):
    _, r1, n2c, cw = u_ref.shape
    n1 = 2 * r1
    ys = []
    c_t = jnp.swapaxes(c_ref[...], 0, 1)
    for j in range(n2c):
        cc = jnp.concatenate(_unpack_pair(c_t[j]), axis=0)
        ys.append(_dot(mc_ref[j], cc.astype(BF16)))
    y = jnp.swapaxes(jnp.stack(ys, axis=0), 0, 1)
    u = u_ref[...].reshape(n1, n2c, cw)
    g = g_ref[...].reshape(n1, n2c, cw)
    yo = g * (y + u * skip_ref[...].reshape(1, 1, cw))
    y_ref[...] = yo.reshape(2, r1, n2c, cw).astype(y_ref.dtype)
    if fuse_a:
        a_ref = rest[0]
        yo_t = jnp.swapaxes(yo, 0, 1)
        packed = []
        for j in range(n2c):
            r = _dot(ma_ref[j], yo_t[j].astype(BF16))
            packed.append(_pack_pair(r[:n1], r[n1:]))
        a_ref[...] = jnp.swapaxes(jnp.stack(packed, axis=0), 0, 1)


def _stage_c(c, u5, u_col, g5, g_col, skip, tabs, fuse_a, out_dtype, n2c=8):
    n1, n2 = tabs["n1"], tabs["n2"]
    npair, _, r1, _, _ = u5.shape
    cw = HY_WIDTH
    mc = tabs["mat_c"]
    ma = tabs["mat_a"]
    const = lambda p, j: (0, 0)
    per_j = lambda m: pl.BlockSpec((n2c,) + m.shape[1:], lambda p, j: (j, 0, 0))
    out_shape = [jax.ShapeDtypeStruct((npair, 2, r1, n2, cw), out_dtype)]
    out_specs = [pl.BlockSpec((None, 2, r1, n2c, cw), lambda p, j: (p, 0, 0, j, 0))]
    if fuse_a:
        out_shape.append(jax.ShapeDtypeStruct((npair, n1, n2, cw), F32))
        out_specs.append(pl.BlockSpec((None, n1, n2c, cw), lambda p, j: (p, 0, j, 0)))
    return pl.pallas_call(
        functools.partial(_stage_c_body, fuse_a=fuse_a),
        grid=(npair, n2 // n2c),
        in_specs=[pl.BlockSpec((None, n1, n2c, cw), lambda p, j: (p, 0, j, 0)),
                  pl.BlockSpec((None, 2, r1, n2c, cw), lambda p, j: (p, 0, 0, j, u_col)),
                  pl.BlockSpec((None, 2, r1, n2c, cw), lambda p, j: (p, 0, 0, j, g_col)),
                  pl.BlockSpec((1, cw), const), per_j(mc), per_j(ma)],
        out_specs=out_specs,
        out_shape=out_shape,
        compiler_params=_cparams(("parallel", "parallel")),
        name="dft_c",
    )(c, u5, g5, skip, mc, ma)


def _hyena(zc3, kr, ki, skip, tabs):
    bsz, seq, _ = zc3.shape
    n1, n2 = tabs["n1"], tabs["n2"]
    r1 = n1 // 2
    z5 = zc3.reshape(bsz // 2, 2, r1, n2, 3 * HY_WIDTH)
    a = _stage_a(z5, 2, tabs)
    c = _stage_b(a, kr, ki, 0, tabs)
    y1, a2 = _stage_c(c, z5, 2, z5, 0, skip[0:1].astype(F32), tabs, True, F32)
    c2 = _stage_b(a2, kr, ki, 1, tabs)
    (y2,) = _stage_c(c2, y1, 0, z5, 1, skip[1:2].astype(F32), tabs, False, BF16)
    return y2.reshape(bsz * seq, HY_WIDTH)


def _retention_body(sc_ref, qf_ref, kf_ref, vf_ref, cf_ref, sf_ref, qb_ref, kb_ref, vb_ref, cb_ref, sb_ref,
                    of_ref, ob_ref, state, dmat, qwt, kwt):
    ch = RET_CHUNK
    dh = RET_HEAD_DIM

    @pl.when(pl.program_id(1) == 0)
    def _():
        state[...] = jnp.zeros_like(state)
        ci = lax.broadcasted_iota(I32, (ch, ch), 0).astype(F32)
        mi = lax.broadcasted_iota(I32, (ch, ch), 1).astype(F32)
        lag = ci - mi
        for d in range(2):
            for h in range(RET_HEADS):
                lg = sc_ref[d * RET_HEADS + h]
                if d == 0:
                    dmat[d, h] = jnp.where(lag >= 0, jnp.exp(lg * jnp.maximum(lag, 0.0)), 0.0)
                    qwt[d, h] = jnp.exp(lg * (ci + 1.0))
                    kwt[d, h] = jnp.exp(lg * (ch - 1.0 - ci))
                else:
                    dmat[d, h] = jnp.where(lag < 0, jnp.exp(lg * jnp.maximum(-lag, 0.0)), 0.0)
                    qwt[d, h] = jnp.exp(lg * (ch - ci))
                    kwt[d, h] = jnp.exp(lg * ci)

    scale = RET_HEAD_DIM ** -0.5
    refs = ((qf_ref, kf_ref, vf_ref, cf_ref, sf_ref, of_ref), (qb_ref, kb_ref, vb_ref, cb_ref, sb_ref, ob_ref))
    work = [(d, h) for d in range(2) for h in range(RET_HEADS)]
    per_step = qf_ref.shape[0] // ch
    for c in range(per_step):
        rows = (slice(c * ch, (c + 1) * ch), slice((per_step - 1 - c) * ch, (per_step - c) * ch))
        qs, ks, qws, kws, vs = {}, {}, {}, {}, {}
        for d, h in work:
            q_ref, k_ref, v_ref, c_ref, s_ref, _ = refs[d]
            sl = slice(h * dh, (h + 1) * dh)
            cosf = c_ref[rows[d], :]
            sinf = s_ref[rows[d], :]
            q = q_ref[rows[d], sl].astype(F32)
            k = k_ref[rows[d], sl].astype(F32)
            q = q * cosf + pltpu.roll(q, dh // 2, axis=1) * sinf
            k = (k * cosf + pltpu.roll(k, dh // 2, axis=1) * sinf) * scale
            qs[d, h], ks[d, h] = q.astype(BF16), k.astype(BF16)
            qws[d, h], kws[d, h] = (q * qwt[d, h]).astype(BF16), (k * kwt[d, h]).astype(BF16)
            vs[d, h] = v_ref[rows[d], sl]
        scores = {dh_: _dot_nt(qs[dh_], ks[dh_]) for dh_ in work}
        for d, h in work:
            lhs = jnp.concatenate([(scores[d, h] * dmat[d, h]).astype(BF16), qws[d, h]], axis=1)
            rhs = jnp.concatenate([vs[d, h], state[d, h].astype(BF16)], axis=0)
            refs[d][5][rows[d], h * dh:(h + 1) * dh] = _dot(lhs, rhs).astype(BF16)
        for d, h in work:
            cdec = sc_ref[2 * RET_HEADS + d * RET_HEADS + h]
            state[d, h] = state[d, h] * cdec + _dot_tn(kws[d, h], vs[d, h])


def _retention(zret3, decay_logit):
    bsz, seq, _ = zret3.shape
    ch = RET_CHUNK
    nc = seq // ch
    half = RET_HEAD_DIM // 2
    inv_freq = ROPE_BASE ** (-jnp.arange(half, dtype=F32) / half)
    ang = jnp.arange(seq, dtype=F32)[:, None] * inv_freq[None, :]
    cosf = jnp.concatenate([jnp.cos(ang), jnp.cos(ang)], axis=1)
    sinf = jnp.concatenate([-jnp.sin(ang), jnp.sin(ang)], axis=1)
    log_g = jax.nn.log_sigmoid(decay_logit.astype(F32)).reshape(-1)
    scal = jnp.concatenate([log_g, jnp.exp(log_g * ch)])
    w = RET_WIDTH
    per_step = next(c for c in (4, 2, 1) if nc % c == 0)
    nb = nc // per_step
    blk = per_step * ch
    fwd = lambda col: pl.BlockSpec((None, blk, w), lambda b, n, sc: (b, n, col))
    bwd = lambda col: pl.BlockSpec((None, blk, w), lambda b, n, sc: (b, nb - 1 - n, col))
    rope_f = pl.BlockSpec((blk, RET_HEAD_DIM), lambda b, n, sc: (n, 0))
    rope_b = pl.BlockSpec((blk, RET_HEAD_DIM), lambda b, n, sc: (nb - 1 - n, 0))
    grid_spec = pltpu.PrefetchScalarGridSpec(
        num_scalar_prefetch=1,
        grid=(bsz, nb),
        in_specs=[fwd(0), fwd(1), fwd(2), rope_f, rope_f, bwd(0), bwd(1), bwd(2), rope_b, rope_b],
        out_specs=[pl.BlockSpec((None, blk, w), lambda b, n, sc: (b, n, 0)),
                   pl.BlockSpec((None, blk, w), lambda b, n, sc: (b, nb - 1 - n, 0))],
        scratch_shapes=[pltpu.VMEM((2, RET_HEADS, RET_HEAD_DIM, RET_HEAD_DIM), F32),
                        pltpu.VMEM((2, RET_HEADS, ch, ch), F32),
                        pltpu.VMEM((2, RET_HEADS, ch, RET_HEAD_DIM), F32),
                        pltpu.VMEM((2, RET_HEADS, ch, RET_HEAD_DIM), F32)],
    )
    return pl.pallas_call(
        _retention_body,
        grid_spec=grid_spec,
        out_shape=[jax.ShapeDtypeStruct((bsz, seq, w), BF16), jax.ShapeDtypeStruct((bsz, seq, w), BF16)],
        compiler_params=_cparams(("arbitrary", "arbitrary")),
        name="retention",
    )(scal, zret3, zret3, zret3, cosf, sinf, zret3, zret3, zret3, cosf, sinf)


def _outproj_body(x_ref, yhy_ref, of_ref, ob_ref, gr_ref, ghy_ref, gret_ref, whyo_ref, wreto_ref, wo_ref,
                  n2g_ref, wrt_ref, x1_ref, pt_ref):
    dh = RET_HEAD_DIM
    o = of_ref[...].astype(F32) + ob_ref[...].astype(F32)
    parts = []
    for h in range(RET_HEADS):
        oh = o[:, h * dh:(h + 1) * dh]
        parts.append(oh * lax.rsqrt(jnp.mean(oh * oh, axis=-1, keepdims=True) + EPS))
    on = jnp.concatenate(parts, axis=1)
    gr = gr_ref[...].astype(F32)
    ret = (gr * jax.nn.sigmoid(gr)) * on
    y_ret = _dot(ret.astype(BF16), wreto_ref[...])
    y_hy = _dot(yhy_ref[...], whyo_ref[...])
    merged = jax.nn.sigmoid(ghy_ref[...].astype(F32)) * y_hy + jax.nn.sigmoid(gret_ref[...].astype(F32)) * y_ret
    x1 = x_ref[...] + _dot(merged.astype(BF16), wo_ref[...])
    x1_ref[...] = x1
    h2 = x1 * lax.rsqrt(jnp.mean(x1 * x1, axis=-1, keepdims=True) + EPS) * n2g_ref[...]
    logits = _dot_nt(wrt_ref[...], h2.astype(BF16))
    m = jnp.max(logits, axis=0, keepdims=True)
    e = jnp.exp(logits - m)
    pt_ref[...] = e / jnp.sum(e, axis=0, keepdims=True)


def _outproj(x2, yhy, o_f, o_b, zret, zg, whyo, wreto, wo, n2g, wrt, tm=512):
    n_tok = x2.shape[0]
    d = D_MODEL
    row = lambda w, col=0: pl.BlockSpec((tm, w), lambda i: (i, col))
    const = lambda shape: pl.BlockSpec(shape, lambda i: (0, 0))
    return pl.pallas_call(
        _outproj_body,
        grid=(n_tok // tm,),
        in_specs=[row(d), row(HY_WIDTH), row(RET_WIDTH), row(RET_WIDTH), row(RET_WIDTH, 3), row(d, 0), row(d, 1),
                  const((HY_WIDTH, d)), const((RET_WIDTH, d)), const((d, d)), const((1, d)), const((N_EXPERTS, d))],
        out_specs=[row(d), pl.BlockSpec((N_EXPERTS, tm), lambda i: (0, i))],
        out_shape=[jax.ShapeDtypeStruct((n_tok, d), F32), jax.ShapeDtypeStruct((N_EXPERTS, n_tok), F32)],
        compiler_params=_cparams(("parallel",)),
        name="outproj_router",
    )(x2, yhy, o_f, o_b, zret, zg, zg, whyo, wreto, wo, n2g, wrt)


def _select_body(p_ref, upper_ref, lower_ref, lowinc_ref, eye_ref, pos_ref, idx_ref, gate_ref, lo_ref, *, cap):
    rows = p_ref.shape[0]
    p = p_ref[...]
    bits = lax.bitcast_convert_type(p, I32)

    def count(mask):
        return jnp.sum(jnp.sum(mask.astype(F32), axis=1, keepdims=True), axis=0, keepdims=True)

    def bit_step(i, thr):
        cand = thr | jnp.left_shift(jnp.int32(1), 30 - i)
        return jnp.where(count(bits >= cand) >= cap, cand, thr)

    thr = lax.fori_loop(0, 31, bit_step, jnp.zeros((1, 1), I32))
    gt = bits > thr
    eq = bits == thr
    need = cap - count(gt)

    def prefix(mask_f):
        incl = _dot(mask_f.astype(BF16), upper_ref[...])
        tot = jnp.broadcast_to(incl[:, LANES - 1:LANES], incl.shape)
        base = _dot(lower_ref[...], tot.astype(BF16))
        return incl, base, tot

    eq_f = eq.astype(F32)
    incl_e, base_e, _ = prefix(eq_f)
    sel = gt | (eq & (base_e + incl_e - eq_f < need))
    sel_f = sel.astype(F32)
    incl, base, tot = prefix(sel_f)
    pos_ref[...] = jnp.where(sel, (base + incl - 1.0).astype(I32), -1)
    lo_ref[...] = base.astype(I32)

    rowend = base + tot
    incl_t = _dot_nt(lowinc_ref[...], sel_f.astype(BF16)).astype(BF16)
    p_t = []
    rem = p
    for _ in range(3):
        part = rem.astype(BF16)
        rem = rem - part.astype(F32)
        p_t.append(_dot_nt(eye_ref[...], part).astype(BF16))
    table = jnp.concatenate([incl_t] + p_t, axis=0)
    groups = 2 if (cap // LANES) % 2 == 0 else 1
    sw = groups * LANES
    wide = lambda a: jnp.concatenate([a] * groups, axis=1)
    rowend_w, tot_w = wide(rowend), wide(tot)
    r_iota = lax.broadcasted_iota(I32, (rows, sw), 0).astype(F32)
    lane_iota = lax.broadcasted_iota(I32, (LANES, sw), 0).astype(F32)

    def slot_tile(ts, carry):
        s = (ts * sw + lax.broadcasted_iota(I32, (1, sw), 1)).astype(F32)
        done = rowend_w <= s
        row = jnp.sum(done.astype(F32), axis=0, keepdims=True)
        before = jnp.sum(jnp.where(done, tot_w, 0.0), axis=0, keepdims=True)
        onehot_t = (r_iota == row).astype(BF16)
        got = _dot(table, onehot_t)
        g_t = got[:LANES]
        lane = jnp.sum((g_t <= s - before).astype(F32), axis=0, keepdims=True)
        tok = (row * LANES + lane).astype(I32)
        p_row = got[LANES:2 * LANES] + got[2 * LANES:3 * LANES] + got[3 * LANES:]
        gate = jnp.sum(jnp.where(lane_iota == lane, p_row, 0.0), axis=0, keepdims=True)
        for k in range(groups):
            idx_ref[pl.ds(ts * groups + k, 1), :] = tok[:, k * LANES:(k + 1) * LANES]
            gate_ref[pl.ds(ts * groups + k, 1), :] = gate[:, k * LANES:(k + 1) * LANES]
        return carry

    lax.fori_loop(0, cap // sw, slot_tile, 0)


def _select(pt3, cap):
    n_e, rows, _ = pt3.shape
    ii = np.arange(LANES)
    upper = jnp.asarray(ii[:, None] <= ii[None, :], BF16)
    lowinc = jnp.asarray(ii[None, :] <= ii[:, None], BF16)
    eye = jnp.asarray(ii[None, :] == ii[:, None], BF16)
    rr = np.arange(rows)
    lower = jnp.asarray(rr[None, :] < rr[:, None], BF16)
    const = lambda shape: pl.BlockSpec(shape, lambda e: (0, 0))
    tok_spec = pl.BlockSpec((None, rows, LANES), lambda e: (e, 0, 0))
    slot_spec = pl.BlockSpec((None, cap // LANES, LANES), lambda e: (e, 0, 0))
    return pl.pallas_call(
        functools.partial(_select_body, cap=cap),
        grid=(n_e,),
        in_specs=[tok_spec, const((LANES, LANES)), const((rows, rows)), const((LANES, LANES)), const((LANES, LANES))],
        out_specs=[tok_spec, slot_spec, slot_spec, tok_spec],
        out_shape=[jax.ShapeDtypeStruct((n_e, rows, LANES), I32),
                   jax.ShapeDtypeStruct((n_e, cap // LANES, LANES), I32),
                   jax.ShapeDtypeStruct((n_e, cap // LANES, LANES), F32),
                   jax.ShapeDtypeStruct((n_e, rows, LANES), I32)],
        compiler_params=_cparams(("parallel",)),
        name="expert_select",
    )(pt3, upper, lower, lowinc, eye)


def _ffn_body(idx_ref, h_hbm, n2g_ref, gate_ref, wg_ref, wu_ref, wd_ref, ye_ref, xbuf, xb, sems, *,
              pairs_per_expert):
    s = xbuf.shape[1]
    step = pl.program_id(0) * pairs_per_expert + pl.program_id(1)
    n_steps = pl.num_programs(0) * pairs_per_expert

    def row_copy(tile, i, buf):
        tok = idx_ref[tile * s + i]
        return pltpu.make_async_copy(h_hbm.at[pl.ds(tok, 1), :], xbuf.at[buf, pl.ds(i, 1), :], sems.at[buf])

    def wait_rows(buf):
        pltpu.make_async_copy(h_hbm.at[pl.ds(0, s), :], xbuf.at[buf], sems.at[buf]).wait()

    def ffn(buf):
        x = xbuf[buf]
        xb[...] = (x * lax.rsqrt(jnp.mean(x * x, axis=-1, keepdims=True) + EPS) * n2g_ref[...]).astype(BF16)
        y = None
        for f0 in range(0, EXPERT_FF, FF_CHUNK):
            f1 = min(f0 + FF_CHUNK, EXPERT_FF)
            a = _dot(xb[...], wg_ref[:, f0:f1])
            b = _dot(xb[...], wu_ref[:, f0:f1])
            hid = ((a * jax.nn.sigmoid(a)) * b).astype(BF16)
            part = _dot(hid, wd_ref[f0:f1, :])
            y = part if y is None else y + part
        for k in range(s // LANES):
            row0 = buf * s + k * LANES
            gate = gate_ref[row0 // LANES:row0 // LANES + 1, :]
            col = jnp.transpose(jnp.broadcast_to(gate, (LANES, LANES)))[:, 0:1]
            yk = y[k * LANES:(k + 1) * LANES] * col
            ye_ref[row0:row0 + LANES, :] = _pack_pair(yk[:, :D_MODEL // 2], yk[:, D_MODEL // 2:])

    @pl.when(step == 0)
    def _():
        def one(i, carry):
            row_copy(0, i, 0).start()
            return carry
        lax.fori_loop(0, s, one, 0, unroll=8)

    for i in range(s):
        row_copy(2 * step + 1, i, 1).start()
    wait_rows(0)
    ffn(0)

    @pl.when(step + 1 < n_steps)
    def _():
        for i in range(s):
            row_copy(2 * step + 2, i, 0).start()

    wait_rows(1)
    ffn(1)


def _expert_ffn(idx_flat, x1, n2g, gates, wg, wu, wd, cap, s=512):
    s = min(s, cap // 2)
    assert s % LANES == 0 and cap % (2 * s) == 0
    pairs = cap // (2 * s)
    g3 = gates.reshape(N_EXPERTS * pairs, 2 * s // LANES, LANES)
    grid_spec = pltpu.PrefetchScalarGridSpec(
        num_scalar_prefetch=1,
        grid=(N_EXPERTS, pairs),
        in_specs=[pl.BlockSpec(memory_space=pl.ANY),
                  pl.BlockSpec((1, D_MODEL), lambda e, j, idx: (0, 0)),
                  pl.BlockSpec((None, 2 * s // LANES, LANES), lambda e, j, idx: (e * pairs + j, 0, 0)),
                  pl.BlockSpec((None, D_MODEL, EXPERT_FF), lambda e, j, idx: (e, 0, 0)),
                  pl.BlockSpec((None, D_MODEL, EXPERT_FF), lambda e, j, idx: (e, 0, 0)),
                  pl.BlockSpec((None, EXPERT_FF, D_MODEL), lambda e, j, idx: (e, 0, 0))],
        out_specs=pl.BlockSpec((2 * s, D_MODEL // 2), lambda e, j, idx: (e * pairs + j, 0)),
        scratch_shapes=[pltpu.VMEM((2, s, D_MODEL), F32), pltpu.VMEM((s, D_MODEL), BF16),
                        pltpu.SemaphoreType.DMA((2,))],
    )
    return pl.pallas_call(
        functools.partial(_ffn_body, pairs_per_expert=pairs),
        grid_spec=grid_spec,
        out_shape=jax.ShapeDtypeStruct((N_EXPERTS * cap, D_MODEL // 2), F32),
        compiler_params=_cparams(("arbitrary", "arbitrary")),
        name="expert_ffn",
    )(idx_flat, x1, n2g, g3, wg, wu, wd)


def _combine_body(lo_ref, np_ref, x1_ref, pos_ref, nfg_ref, expand_ref, lov_ref, ye_hbm, o_ref, win, sems, *, cap,
                  rows_total, n_rows):
    step = pl.program_id(0)
    n_steps = pl.num_programs(0)
    groups = x1_ref.shape[0] // LANES
    w = COMBINE_WIN
    cur = step % 2

    def copies(rr, m, buf, g):
        out = []
        for e in range(N_EXPERTS):
            first = e * cap + lo_ref[e * n_rows + rr]
            intended = (first // COMBINE_ALIGN) * COMBINE_ALIGN + m * w
            actual = pl.multiple_of(jnp.minimum(intended, rows_total - w), COMBINE_ALIGN)
            out.append(pltpu.make_async_copy(ye_hbm.at[pl.ds(actual, w), :], win.at[buf, g, pl.ds(e * w, w), :],
                                             sems.at[buf, g, e]))
        return out

    def wait_windows(buf, g):
        for e in range(N_EXPERTS):
            pltpu.make_async_copy(ye_hbm.at[pl.ds(0, w), :], win.at[buf, g, pl.ds(e * w, w), :],
                                  sems.at[buf, g, e]).wait()

    def contribution(rr, m, buf, g):
        lane = lax.broadcasted_iota(I32, (1, N_EXPERTS), 1)
        first = lane * cap + lov_ref[pl.ds(rr, 1), :]
        intended = (first & jnp.int32(-COMBINE_ALIGN)) + m * w
        actual = jnp.minimum(intended, rows_total - w)
        pos = pos_ref[g * LANES:(g + 1) * LANES, :]
        glob = pos + lane * cap
        rel = glob - intended
        valid = (pos >= 0) & (rel >= 0) & (rel < w)
        local = jnp.where(valid, glob - actual, -1).astype(F32).astype(BF16)
        spread = _dot(local, expand_ref[...])
        col = (lax.broadcasted_iota(I32, (1, N_EXPERTS * w), 1) & (w - 1)).astype(F32)
        onehot = (spread == col).astype(BF16)
        lo_cols, hi_cols = _unpack_pair(win[buf, g])
        return jnp.concatenate([_dot(onehot, lo_cols.astype(BF16)), _dot(onehot, hi_cols.astype(BF16))], axis=1)

    @pl.when(step == 0)
    def _():
        for g in range(groups):
            for cp in copies(g, 0, 0, g):
                cp.start()

    @pl.when(step + 1 < n_steps)
    def _():
        for g in range(groups):
            for cp in copies((step + 1) * groups + g, 0, 1 - cur, g):
                cp.start()

    for g in range(groups):
        rr = step * groups + g
        wait_windows(cur, g)
        acc = x1_ref[g * LANES:(g + 1) * LANES, :] + contribution(rr, 0, cur, g)

        def extra_pass(m, acc, rr=rr, g=g):
            for cp in copies(rr, m, cur, g):
                cp.start()
            wait_windows(cur, g)
            return acc + contribution(rr, m, cur, g)

        acc = lax.fori_loop(1, np_ref[rr], extra_pass, acc)
        o_ref[g * LANES:(g + 1) * LANES, :] = (
            acc * lax.rsqrt(jnp.mean(acc * acc, axis=-1, keepdims=True) + EPS) * nfg_ref[...])


def _combine(lo, x1, pos_t, nfg, ye, cap):
    n_tok = x1.shape[0]
    w = COMBINE_WIN
    rows_total = ye.shape[0]
    n_rows = n_tok // LANES
    groups = next(c for c in (4, 2, 1) if n_rows % c == 0)
    tm = groups * LANES
    nxt =jnp.concatenate([lo[:, 1:], jnp.full((N_EXPERTS, 1), cap, I32)], axis=1)
    span = lo % COMBINE_ALIGN + (nxt - lo)
    n_pass = jnp.maximum(jnp.max((span + w - 1) // w, axis=0), 1).astype(I32)
    ee = np.arange(N_EXPERTS)
    expand = jnp.asarray(ee[:, None] == (np.arange(N_EXPERTS * w) // w)[None, :], BF16)
    grid_spec = pltpu.PrefetchScalarGridSpec(
        num_scalar_prefetch=2,
        grid=(n_rows // groups,),
        in_specs=[pl.BlockSpec((tm, D_MODEL), lambda i, lo_, np_: (i, 0)),
                  pl.BlockSpec((tm, N_EXPERTS), lambda i, lo_, np_: (i, 0)),
                  pl.BlockSpec((1, D_MODEL), lambda i, lo_, np_: (0, 0)),
                  pl.BlockSpec((N_EXPERTS, N_EXPERTS * w), lambda i, lo_, np_: (0, 0)),
                  pl.BlockSpec((n_rows, N_EXPERTS), lambda i, lo_, np_: (0, 0)),
                  pl.BlockSpec(memory_space=pl.ANY)],
        out_specs=pl.BlockSpec((tm, D_MODEL), lambda i, lo_, np_: (i, 0)),
        scratch_shapes=[pltpu.VMEM((2, groups, N_EXPERTS * w, D_MODEL // 2), F32),
                        pltpu.SemaphoreType.DMA((2, groups, N_EXPERTS))],
    )
    return pl.pallas_call(
        functools.partial(_combine_body, cap=cap, rows_total=rows_total, n_rows=n_rows),
        grid_spec=grid_spec,
        out_shape=jax.ShapeDtypeStruct((n_tok, D_MODEL), F32),
        compiler_params=_cparams(("arbitrary",)),
        name="moe_combine_norm",
    )(lo.reshape(-1), n_pass, x1, pos_t, nfg, expand, lo.T, ye)


def _trunk(x, w):
    bsz, seq, d = x.shape
    n_tok = bsz * seq
    x2 = x.reshape(n_tok, d)
    tabs = _fft_tables(seq)

    zc, zret, zg = _inproj(x2, w["norm1_g"], w["w_in"], w["hy_conv_w"], w["hy_conv_b"], seq)
    zc = zc.reshape(bsz, seq, -1)

    h_time, h_abs = _hyena_filter_time(seq, w["hy_w1"], w["hy_b1"], w["hy_freq1"], w["hy_w2"], w["hy_b2"],
                                       w["hy_freq2"], w["hy_w3"])
    kr, ki = _stage_b_filter(_stage_a_real(h_time, tabs), h_abs, tabs)
    yhy = _hyena(zc, kr, ki, w["hy_skip"], tabs)

    o_f, o_b = _retention(zret.reshape(bsz, seq, -1), w["ret_decay_logit"])
    x1, pt = _outproj(x2, yhy, o_f.reshape(n_tok, -1), o_b.reshape(n_tok, -1), zret, zg,
                      w["w_hy_out"], w["w_ret_out"], w["w_o"], w["norm2_g"], w["w_router_t"])

    cap = CAPACITY_FACTOR * n_tok // N_EXPERTS
    rows = n_tok // LANES
    pos, idx, gates, lo = _select(pt.reshape(N_EXPERTS, rows, LANES), cap)
    ye = _expert_ffn(idx.reshape(-1), x1, w["norm2_g"], gates, w["w_gate"], w["w_up"], w["w_down"], cap)
    pos_t = pos.reshape(N_EXPERTS, n_tok).T
    y = _combine(lo[:, :, 0], x1, pos_t, w["norm_f_g"], ye, cap)
    return y.reshape(bsz, seq, d)


def kernel(x_prompt, x_sample, norm1_g, w_in, hy_conv_w, hy_conv_b, hy_w1, hy_b1, hy_freq1, hy_w2, hy_b2, hy_freq2,
           hy_w3, hy_skip, ret_decay_logit, w_hy_out, w_ret_out, w_o, norm2_g, w_router, w_gate, w_up, w_down,
           norm_f_g):
    layer = 0
    w = dict(
        norm1_g=norm1_g[layer].astype(F32)[None], w_in=w_in[layer].astype(BF16),
        hy_conv_w=hy_conv_w[layer].astype(F32), hy_conv_b=hy_conv_b[layer].astype(F32)[None],
        hy_w1=hy_w1[layer], hy_b1=hy_b1[layer], hy_freq1=hy_freq1[layer], hy_w2=hy_w2[layer], hy_b2=hy_b2[layer],
        hy_freq2=hy_freq2[layer], hy_w3=hy_w3[layer], hy_skip=hy_skip[layer],
        ret_decay_logit=ret_decay_logit[layer],
        w_hy_out=w_hy_out[layer].astype(BF16), w_ret_out=w_ret_out[layer].astype(BF16), w_o=w_o[layer].astype(BF16),
        norm2_g=norm2_g[layer].astype(F32)[None], w_router_t=w_router[layer].T.astype(BF16),
        w_gate=w_gate[layer].astype(BF16), w_up=w_up[layer].astype(BF16), w_down=w_down[layer].astype(BF16),
        norm_f_g=norm_f_g.astype(F32)[None],
    )
    return _trunk(x_prompt, w), _trunk(x_sample, w)
```

```python
import functools
import math

import numpy as np
import jax
import jax.numpy as jnp
from jax import lax
from jax.experimental import pallas as pl
from jax.experimental.pallas import tpu as pltpu

F32 = jnp.float32
BF16 = jnp.bfloat16
I32 = jnp.int32

D_MODEL = 1024
HY_WIDTH = 512
HY_ORDER = 2
HY_BANDS = 16
HY_FILTER_WIDTH = 64
HY_FAST_DECAY = 0.3
HY_SLOW_DECAY = 1.5
HY_DECAY_TARGET = 1e-2
RET_WIDTH = 512
RET_HEADS = 4
RET_HEAD_DIM = 128
RET_CHUNK = 128
ROPE_BASE = 10000.0
N_EXPERTS = 16
EXPERT_FF = 1408
CAPACITY_FACTOR = 2
EPS = 1e-6

VMEM_LIMIT_MIB = 48
LANES = 128
SUBLANES = 8
EMB_PAD = 128
DFT_N1_MAX = 128
FF_CHUNK = 256
COMBINE_ALIGN = SUBLANES
COMBINE_WIN = 32


def _cparams(sem):
    return pltpu.CompilerParams(dimension_semantics=sem, vmem_limit_bytes=VMEM_LIMIT_MIB * 1024 * 1024)


def _dot(a, b):
    return jnp.dot(a, b, preferred_element_type=F32)


def _dot_nt(a, b):
    return lax.dot_general(a, b, (((1,), (1,)), ((), ())), preferred_element_type=F32)


def _dot_tn(a, b):
    return lax.dot_general(a, b, (((0,), (0,)), ((), ())), preferred_element_type=F32)


def _split(a):
    hi = a.astype(BF16)
    lo = (a - hi.astype(F32)).astype(BF16)
    return hi, lo


def _dot3(a, b):
    ah, al = _split(a)
    bh, bl = _split(b)
    return _dot(ah, bh) + _dot(al, bh) + _dot(ah, bl)


def _inproj_body(x_ref, g_ref, w_ref, cw_ref, cb_ref, zc_ref, zret_ref, zg_ref, z_prev, row_prev, *, tiles_per_seq):
    i = pl.program_id(0)
    tm = x_ref.shape[0]
    n_hy = zc_ref.shape[1]
    n_ret = zret_ref.shape[1]

    @pl.when(i == 0)
    def _():
        z_prev[...] = jnp.zeros_like(z_prev)
        row_prev[...] = jnp.zeros_like(row_prev)

    x = x_ref[...]
    ms = jnp.mean(x * x, axis=-1, keepdims=True)
    h = (x * lax.rsqrt(ms + EPS) * g_ref[...]).astype(BF16)
    z_hy = _dot(h, w_ref[:, :n_hy])
    zret_ref[...] = _dot(h, w_ref[:, n_hy:n_hy + n_ret]).astype(BF16)
    zg_ref[...] = _dot(h, w_ref[:, n_hy + n_ret:]).astype(BF16)

    t_prev = (i - 1) % tiles_per_seq
    zp = z_prev[...]
    before = jnp.where(t_prev == 0, 0.0, row_prev[0:1, :])
    after = jnp.where(t_prev == tiles_per_seq - 1, 0.0, z_hy[0:1, :])
    row = lax.broadcasted_iota(I32, (tm, 1), 0)
    zm1 = jnp.where(row == 0, before, pltpu.roll(zp, 1, axis=0))
    zp1 = jnp.where(row == tm - 1, after, pltpu.roll(zp, tm - 1, axis=0))
    zc_ref[...] = zm1 * cw_ref[0:1, :] + zp * cw_ref[1:2, :] + zp1 * cw_ref[2:3, :] + cb_ref[...]
    row_prev[...] = jnp.broadcast_to(zp[tm - 1:tm, :], row_prev.shape)
    z_prev[...] = z_hy


def _inproj(x2, g, w_bf, conv_w, conv_b, seq, tm=512):
    n_tok = x2.shape[0]
    n_hy, n_ret, n_g = 3 * HY_WIDTH, 4 * RET_WIDTH, 2 * D_MODEL
    tm = min(tm, seq)
    n_tiles = n_tok // tm
    cur = lambda i: (jnp.minimum(i, n_tiles - 1), 0)
    const = lambda i: (0, 0)
    return pl.pallas_call(
        functools.partial(_inproj_body, tiles_per_seq=seq // tm),
        grid=(n_tiles + 1,),
        in_specs=[pl.BlockSpec((tm, D_MODEL), cur),
                  pl.BlockSpec((1, D_MODEL), const),
                  pl.BlockSpec((D_MODEL, n_hy + n_ret + n_g), const, pipeline_mode=pl.Buffered(1)),
                  pl.BlockSpec((3, n_hy), const),
                  pl.BlockSpec((1, n_hy), const)],
        out_specs=[pl.BlockSpec((tm, n_hy), lambda i: (jnp.maximum(i - 1, 0), 0)),
                   pl.BlockSpec((tm, n_ret), cur),
                   pl.BlockSpec((tm, n_g), cur)],
        out_shape=[jax.ShapeDtypeStruct((n_tok, n_hy), F32),
                   jax.ShapeDtypeStruct((n_tok, n_ret), BF16),
                   jax.ShapeDtypeStruct((n_tok, n_g), BF16)],
        scratch_shapes=[pltpu.VMEM((tm, n_hy), F32), pltpu.VMEM((8, n_hy), F32)],
        compiler_params=_cparams(("arbitrary",)),
        name="inproj_shortconv",
    )(x2, g, w_bf, conv_w, conv_b)


def _fft_dims(seq):
    n = 2 * seq
    n1 = min(DFT_N1_MAX, 1 << (n.bit_length() // 2))
    n2 = n // n1
    assert n1 * n2 == n and n1 % 2 == 0
    return n, n1, n2


def _fft_tables(seq):
    n, n1, n2 = _fft_dims(seq)
    r1 = n1 // 2
    k1 = jnp.arange(n1, dtype=I32)
    j = jnp.arange(n2, dtype=I32)

    at = ((j[:, None] * k1[None, :]) % n).astype(F32) * (2.0 * math.pi / n)
    ct, st = jnp.cos(at), jnp.sin(at)

    def cos_sin(n1_count, k1_first):
        a1 = ((k1[:, None] * jnp.arange(n1_count, dtype=I32)[None, :]) % n1).astype(F32) * (2.0 * math.pi / n1)
        cf, sf = jnp.cos(a1), jnp.sin(a1)
        if k1_first:
            cf, sf, ctb, stb = cf[None], sf[None], ct[:, :, None], st[:, :, None]
        else:
            cf, sf, ctb, stb = cf.T[None], sf.T[None], ct[:, None, :], st[:, None, :]
        return cf * ctb - sf * stb, sf * ctb + cf * stb

    cat = jnp.concatenate
    c, s = cos_sin(r1, True)
    mat_a = cat([cat([c, s], axis=2), cat([-s, c], axis=2)], axis=1).astype(BF16)
    c, s = cos_sin(n1, True)
    mat_a_real = cat([c, -s], axis=1).astype(BF16)
    c, s = cos_sin(r1, False)
    mat_c = cat([cat([c, -s], axis=2), cat([s, c], axis=2)], axis=1).astype(BF16)
    a2 = 2.0 * np.pi * np.outer(np.arange(n2), np.arange(n2)) / n2
    gr, gi = np.cos(a2), -np.sin(a2)
    mat_f = np.block([[gr, -gi], [gi, gr]])
    mat_i = np.block([[gr, gi], [-gi, gr]])
    return dict(n=n, n1=n1, n2=n2, mat_a=mat_a, mat_a_real=mat_a_real, mat_c=mat_c,
                mat_f=jnp.asarray(mat_f, F32).astype(BF16), mat_i=jnp.asarray(mat_i, F32).astype(BF16))


def _filter_body(z_ref, zt_ref, w1t_ref, b1_ref, f1_ref, w2t_ref, b2_ref, f2_ref, w3_ref, dl_ref, h_ref, s_ref, *,
                 seq):
    i = pl.program_id(0)
    tl = z_ref.shape[0]
    a = jnp.sin(f1_ref[...] * (_dot3(w1t_ref[...], zt_ref[...]) + b1_ref[...]))
    a = jnp.sin(f2_ref[...] * (_dot3(w2t_ref[...], a) + b2_ref[...]))
    h = _dot_tn(a.astype(BF16), w3_ref[...].astype(BF16))
    win = jnp.exp(-z_ref[:, 0:1] * dl_ref[...])
    h = h * jnp.concatenate([win] * HY_ORDER, axis=1)
    row = i * tl + lax.broadcasted_iota(I32, (tl, 1), 0)
    h = jnp.where(row == seq, 0.0, h)
    h_ref[...] = h

    @pl.when(i == 0)
    def _():
        s_ref[...] = jnp.zeros_like(s_ref)

    s_ref[...] += jnp.broadcast_to(jnp.sum(jnp.abs(h), axis=0, keepdims=True), s_ref.shape)


def _hyena_filter_time(seq, w1, b1, f1, w2, b2, f2, w3, tl=512):
    n = 2 * seq
    tl = min(tl, seq)
    idx = jnp.arange(n, dtype=I32)
    p = jnp.minimum(jnp.where(idx < seq, idx, n - idx), seq - 1).astype(F32)
    t = p / float(seq - 1)
    ang = 2.0 * math.pi * p / seq
    bands = jnp.linspace(1e-4, HY_BANDS - 1, HY_BANDS, dtype=F32)
    phase = ang[:, None] * bands[None, :]
    emb = jnp.concatenate([t[:, None], jnp.cos(phase), -jnp.sin(phase)], axis=-1)
    emb = jnp.pad(emb, ((0, 0), (0, EMB_PAD - emb.shape[1])))
    w1p = jnp.pad(w1.astype(F32), ((0, EMB_PAD - w1.shape[0]), (0, 0)))
    w3d = w3.astype(F32).reshape(HY_FILTER_WIDTH, HY_ORDER, 2, HY_WIDTH).transpose(2, 0, 1, 3)
    w3d = w3d.reshape(2, HY_FILTER_WIDTH, HY_ORDER * HY_WIDTH)
    deltas = jnp.abs(jnp.linspace(math.log(HY_DECAY_TARGET) / HY_SLOW_DECAY,
                                  math.log(HY_DECAY_TARGET) / HY_FAST_DECAY, HY_WIDTH, dtype=F32))[None, :]
    fw = HY_FILTER_WIDTH
    nblk_half = seq // tl
    const = lambda i: (0, 0)
    col = lambda v: v.astype(F32)[:, None]
    return pl.pallas_call(
        functools.partial(_filter_body, seq=seq),
        grid=(n // tl,),
        in_specs=[pl.BlockSpec((tl, EMB_PAD), lambda i: (i, 0)),
                  pl.BlockSpec((EMB_PAD, tl), lambda i: (0, i)),
                  pl.BlockSpec((fw, EMB_PAD), const), pl.BlockSpec((fw, 1), const), pl.BlockSpec((fw, 1), const),
                  pl.BlockSpec((fw, fw), const), pl.BlockSpec((fw, 1), const), pl.BlockSpec((fw, 1), const),
                  pl.BlockSpec((None, fw, HY_ORDER * HY_WIDTH), lambda i: (i // nblk_half, 0, 0)),
                  pl.BlockSpec((1, HY_WIDTH), const)],
        out_specs=[pl.BlockSpec((tl, HY_ORDER * HY_WIDTH), lambda i: (i, 0)),
                   pl.BlockSpec((8, HY_ORDER * HY_WIDTH), const)],
        out_shape=[jax.ShapeDtypeStruct((n, HY_ORDER * HY_WIDTH), F32),
                   jax.ShapeDtypeStruct((8, HY_ORDER * HY_WIDTH), F32)],
        compiler_params=_cparams(("arbitrary",)),
        name="hyena_filter",
    )(emb, emb.T, w1p.T, col(b1), col(f1), w2.astype(F32).T, col(b2), col(f2), w3d, deltas)


def _stage_a_real_body(h_ref, m_ref, a_ref):
    n1 = h_ref.shape[0]
    h_t = jnp.swapaxes(h_ref[...], 0, 1)
    packed = []
    for j in range(h_ref.shape[1]):
        r = _dot(m_ref[j], h_t[j].astype(BF16))
        packed.append(_pack_pair(r[:n1], r[n1:]))
    a_ref[...] = jnp.swapaxes(jnp.stack(packed, axis=0), 0, 1)


def _stage_a_real(h, tabs, n2c=8, cw=512):
    n1, n2 = tabs["n1"], tabs["n2"]
    c = h.shape[1]
    h3 = h.reshape(n1, n2, c)
    mat = tabs["mat_a_real"]
    return pl.pallas_call(
        _stage_a_real_body,
        grid=(n2 // n2c, c // cw),
        in_specs=[pl.BlockSpec((n1, n2c, cw), lambda j, k: (0, j, k)),
                  pl.BlockSpec((n2c,) + mat.shape[1:], lambda j, k: (j, 0, 0))],
        out_specs=pl.BlockSpec((n1, n2c, cw), lambda j, k: (0, j, k)),
        out_shape=jax.ShapeDtypeStruct((n1, n2, c), F32),
        compiler_params=_cparams(("parallel", "parallel")),
        name="dft_a_filter",
    )(h3, mat)


def _pack_pair(re, im):
    hi = lax.bitcast_convert_type(re.astype(BF16).astype(F32), I32)
    lo = lax.bitcast_convert_type(im.astype(BF16).astype(F32), I32)
    return lax.bitcast_convert_type(hi | lax.shift_right_logical(lo, 16), F32)


def _unpack_pair(packed):
    word = lax.bitcast_convert_type(packed, I32)
    re = lax.bitcast_convert_type(word & jnp.int32(-65536), F32)
    im = lax.bitcast_convert_type(lax.shift_left(word, 16), F32)
    return re, im


def _stage_a_body(u_ref, m_ref, a_ref):
    _, r1, n2c, cw = u_ref.shape
    n1 = 2 * r1
    u_t = jnp.swapaxes(u_ref[...].reshape(n1, n2c, cw), 0, 1)
    packed = []
    for j in range(n2c):
        r = _dot(m_ref[j], u_t[j].astype(BF16))
        packed.append(_pack_pair(r[:n1], r[n1:]))
    a_ref[...] = jnp.swapaxes(jnp.stack(packed, axis=0), 0, 1)


def _stage_a(u5, col, tabs, n2c=8):
    n1, n2 = tabs["n1"], tabs["n2"]
    npair, _, r1, _, _ = u5.shape
    cw = HY_WIDTH
    mat = tabs["mat_a"]
    return pl.pallas_call(
        _stage_a_body,
        grid=(npair, n2 // n2c),
        in_specs=[pl.BlockSpec((None, 2, r1, n2c, cw), lambda p, j: (p, 0, 0, j, col)),
                  pl.BlockSpec((n2c,) + mat.shape[1:], lambda p, j: (j, 0, 0))],
        out_specs=pl.BlockSpec((None, n1, n2c, cw), lambda p, j: (p, 0, j, 0)),
        out_shape=jax.ShapeDtypeStruct((npair, n1, n2, cw), F32),
        compiler_params=_cparams(("parallel", "parallel")),
        name="dft_a",
    )(u5, mat)


def _stage_b_filter_body(a_ref, g_ref, s_ref, kr_ref, ki_ref, *, n):
    n2 = a_ref.shape[1]
    scale = 1.0 / (s_ref[0:1, :] * float(n))
    for kk in range(a_ref.shape[0]):
        d = _dot(g_ref[...], jnp.concatenate(_unpack_pair(a_ref[kk]), axis=0).astype(BF16))
        kr_ref[kk] = d[:n2] * scale
        ki_ref[kk] = d[n2:] * scale


def _stage_b_filter(a, s, tabs):
    n, n1, n2 = tabs["n"], tabs["n1"], tabs["n2"]
    c = a.shape[-1]
    kb = max(1, min(n1, 512 // n2))
    g = tabs["mat_f"]
    return pl.pallas_call(
        functools.partial(_stage_b_filter_body, n=n),
        grid=(n1 // kb,),
        in_specs=[pl.BlockSpec((kb, n2, c), lambda k: (k, 0, 0)),
                  pl.BlockSpec(g.shape, lambda k: (0, 0)),
                  pl.BlockSpec(s.shape, lambda k: (0, 0))],
        out_specs=[pl.BlockSpec((kb, n2, c), lambda k: (k, 0, 0)),
                   pl.BlockSpec((kb, n2, c), lambda k: (k, 0, 0))],
        out_shape=[jax.ShapeDtypeStruct((n1, n2, c), F32), jax.ShapeDtypeStruct((n1, n2, c), F32)],
        compiler_params=_cparams(("parallel",)),
        name="dft_b_filter",
    )(a, g, s)


def _stage_b_body(a_ref, kr_ref, ki_ref, gf_ref, gi_ref, c_ref):
    n2 = a_ref.shape[1]
    for kk in range(a_ref.shape[0]):
        d = _dot(gf_ref[...], jnp.concatenate(_unpack_pair(a_ref[kk]), axis=0).astype(BF16))
        dr, di = d[:n2], d[n2:]
        kr, ki = kr_ref[kk], ki_ref[kk]
        yr = dr * kr - di * ki
        yi = dr * ki + di * kr
        e = _dot(gi_ref[...], jnp.concatenate([yr, yi], axis=0).astype(BF16))
        c_ref[kk] = _pack_pair(e[:n2], e[n2:])


def _stage_b(a, kr, ki, order, tabs):
    n1, n2 = tabs["n1"], tabs["n2"]
    npair = a.shape[0]
    cw = HY_WIDTH
    kb = max(1, min(n1, 1024 // n2))
    gf = tabs["mat_f"]
    gi = tabs["mat_i"]
    const = lambda k, p: (0, 0)
    return pl.pallas_call(
        _stage_b_body,
        grid=(n1 // kb, npair),
        in_specs=[pl.BlockSpec((None, kb, n2, cw), lambda k, p: (p, k, 0, 0)),
                  pl.BlockSpec((kb, n2, cw), lambda k, p: (k, 0, order)),
                  pl.BlockSpec((kb, n2, cw), lambda k, p: (k, 0, order)),
                  pl.BlockSpec(gf.shape, const), pl.BlockSpec(gi.shape, const)],
        out_specs=pl.BlockSpec((None, kb, n2, cw), lambda k, p: (p, k, 0, 0)),
        out_shape=jax.ShapeDtypeStruct(a.shape, F32),
        compiler_params=_cparams(("parallel", "parallel")),
        name="dft_b",
    )(a, kr, ki, gf, gi)


def _stage_c_body(c_ref, u_ref, g_ref, skip_ref, mc_ref, ma_ref, y_ref, *rest, fuse_a):
    _, r1, n2c, cw = u_ref.shape
    n1 = 2 * r1
    ys = []
    c_t = jnp.swapaxes(c_ref[...], 0, 1)
    for j in range(n2c):
        cc = jnp.concatenate(_unpack_pair(c_t[j]), axis=0)
        ys.append(_dot(mc_ref[j], cc.astype(BF16)))
    y = jnp.swapaxes(jnp.stack(ys, axis=0), 0, 1)
    u = u_ref[...].reshape(n1, n2c, cw)
    g = g_ref[...].reshape(n1, n2c, cw)
    yo = g * (y + u * skip_ref[...].reshape(1, 1, cw))
    y_ref[...] = yo.reshape(2, r1, n2c, cw).astype(y_ref.dtype)
    if fuse_a:
        a_ref = rest[0]
        yo_t = jnp.swapaxes(yo, 0, 1)
        packed = []
        for j in range(n2c):
            r = _dot(ma_ref[j], yo_t[j].astype(BF16))
            packed.append(_pack_pair(r[:n1], r[n1:]))
        a_ref[...] = jnp.swapaxes(jnp.stack(packed, axis=0), 0, 1)


def _stage_c(c, u5, u_col, g5, g_col, skip, tabs, fuse_a, out_dtype, n2c=8):
    n1, n2 = tabs["n1"], tabs["n2"]
    npair, _, r1, _, _ = u5.shape
    cw = HY_WIDTH
    mc = tabs["mat_c"]
    ma = tabs["mat_a"]
    const = lambda p, j: (0, 0)
    per_j = lambda m: pl.BlockSpec((n2c,) + m.shape[1:], lambda p, j: (j, 0, 0))
    out_shape = [jax.ShapeDtypeStruct((npair, 2, r1, n2, cw), out_dtype)]
    out_specs = [pl.BlockSpec((None, 2, r1, n2c, cw), lambda p, j: (p, 0, 0, j, 0))]
    if fuse_a:
        out_shape.append(jax.ShapeDtypeStruct((npair, n1, n2, cw), F32))
        out_specs.append(pl.BlockSpec((None, n1, n2c, cw), lambda p, j: (p, 0, j, 0)))
    return pl.pallas_call(
        functools.partial(_stage_c_body, fuse_a=fuse_a),
        grid=(npair, n2 // n2c),
        in_specs=[pl.BlockSpec((None, n1, n2c, cw), lambda p, j: (p, 0, j, 0)),
                  pl.BlockSpec((None, 2, r1, n2c, cw), lambda p, j: (p, 0, 0, j, u_col)),
                  pl.BlockSpec((None, 2, r1, n2c, cw), lambda p, j: (p, 0, 0, j, g_col)),
                  pl.BlockSpec((1, cw), const), per_j(mc), per_j(ma)],
        out_specs=out_specs,
        out_shape=out_shape,
        compiler_params=_cparams(("parallel", "parallel")),
        name="dft_c",
    )(c, u5, g5, skip, mc, ma)


def _hyena(zc3, kr, ki, skip, tabs):
    bsz, seq, _ = zc3.shape
    n1, n2 = tabs["n1"], tabs["n2"]
    r1 = n1 // 2
    z5 = zc3.reshape(bsz // 2, 2, r1, n2, 3 * HY_WIDTH)
    a = _stage_a(z5, 2, tabs)
    c = _stage_b(a, kr, ki, 0, tabs)
    y1, a2 = _stage_c(c, z5, 2, z5, 0, skip[0:1].astype(F32), tabs, True, F32)
    c2 = _stage_b(a2, kr, ki, 1, tabs)
    (y2,) = _stage_c(c2, y1, 0, z5, 1, skip[1:2].astype(F32), tabs, False, BF16)
    return y2.reshape(bsz * seq, HY_WIDTH)


def _retention_body(sc_ref, qf_ref, kf_ref, vf_ref, cf_ref, sf_ref, qb_ref, kb_ref, vb_ref, cb_ref, sb_ref,
                    of_ref, ob_ref, state, dmat, qwt, kwt):
    ch = RET_CHUNK
    dh = RET_HEAD_DIM

    @pl.when(pl.program_id(1) == 0)
    def _():
        state[...] = jnp.zeros_like(state)
        ci = lax.broadcasted_iota(I32, (ch, ch), 0).astype(F32)
        mi = lax.broadcasted_iota(I32, (ch, ch), 1).astype(F32)
        lag = ci - mi
        for d in range(2):
            for h in range(RET_HEADS):
                lg = sc_ref[d * RET_HEADS + h]
                if d == 0:
                    dmat[d, h] = jnp.where(lag >= 0, jnp.exp(lg * jnp.maximum(lag, 0.0)), 0.0)
                    qwt[d, h] = jnp.exp(lg * (ci + 1.0))
                    kwt[d, h] = jnp.exp(lg * (ch - 1.0 - ci))
                else:
                    dmat[d, h] = jnp.where(lag < 0, jnp.exp(lg * jnp.maximum(-lag, 0.0)), 0.0)
                    qwt[d, h] = jnp.exp(lg * (ch - ci))
                    kwt[d, h] = jnp.exp(lg * ci)

    scale = RET_HEAD_DIM ** -0.5
    refs = ((qf_ref, kf_ref, vf_ref, cf_ref, sf_ref, of_ref), (qb_ref, kb_ref, vb_ref, cb_ref, sb_ref, ob_ref))
    work = [(d, h) for d in range(2) for h in range(RET_HEADS)]
    per_step = qf_ref.shape[0] // ch
    for c in range(per_step):
        rows = (slice(c * ch, (c + 1) * ch), slice((per_step - 1 - c) * ch, (per_step - c) * ch))
        qs, ks, qws, kws, vs = {}, {}, {}, {}, {}
        for d, h in work:
            q_ref, k_ref, v_ref, c_ref, s_ref, _ = refs[d]
            sl = slice(h * dh, (h + 1) * dh)
            cosf = c_ref[rows[d], :]
            sinf = s_ref[rows[d], :]
            q = q_ref[rows[d], sl].astype(F32)
            k = k_ref[rows[d], sl].astype(F32)
            q = q * cosf + pltpu.roll(q, dh // 2, axis=1) * sinf
            k = (k * cosf + pltpu.roll(k, dh // 2, axis=1) * sinf) * scale
            qs[d, h], ks[d, h] = q.astype(BF16), k.astype(BF16)
            qws[d, h], kws[d, h] = (q * qwt[d, h]).astype(BF16), (k * kwt[d, h]).astype(BF16)
            vs[d, h] = v_ref[rows[d], sl]
        scores = {dh_: _dot_nt(qs[dh_], ks[dh_]) for dh_ in work}
        for d, h in work:
            lhs = jnp.concatenate([(scores[d, h] * dmat[d, h]).astype(BF16), qws[d, h]], axis=1)
            rhs = jnp.concatenate([vs[d, h], state[d, h].astype(BF16)], axis=0)
            refs[d][5][rows[d], h * dh:(h + 1) * dh] = _dot(lhs, rhs).astype(BF16)
        for d, h in work:
            cdec = sc_ref[2 * RET_HEADS + d * RET_HEADS + h]
            state[d, h] = state[d, h] * cdec + _dot_tn(kws[d, h], vs[d, h])


def _retention(zret3, decay_logit):
    bsz, seq, _ = zret3.shape
    ch = RET_CHUNK
    nc = seq // ch
    half = RET_HEAD_DIM // 2
    inv_freq = ROPE_BASE ** (-jnp.arange(half, dtype=F32) / half)
    ang = jnp.arange(seq, dtype=F32)[:, None] * inv_freq[None, :]
    cosf = jnp.concatenate([jnp.cos(ang), jnp.cos(ang)], axis=1)
    sinf = jnp.concatenate([-jnp.sin(ang), jnp.sin(ang)], axis=1)
    log_g = jax.nn.log_sigmoid(decay_logit.astype(F32)).reshape(-1)
    scal = jnp.concatenate([log_g, jnp.exp(log_g * ch)])
    w = RET_WIDTH
    per_step = next(c for c in (4, 2, 1) if nc % c == 0)
    nb = nc // per_step
    blk = per_step * ch
    fwd = lambda col: pl.BlockSpec((None, blk, w), lambda b, n, sc: (b, n, col))
    bwd = lambda col: pl.BlockSpec((None, blk, w), lambda b, n, sc: (b, nb - 1 - n, col))
    rope_f = pl.BlockSpec((blk, RET_HEAD_DIM), lambda b, n, sc: (n, 0))
    rope_b = pl.BlockSpec((blk, RET_HEAD_DIM), lambda b, n, sc: (nb - 1 - n, 0))
    grid_spec = pltpu.PrefetchScalarGridSpec(
        num_scalar_prefetch=1,
        grid=(bsz, nb),
        in_specs=[fwd(0), fwd(1), fwd(2), rope_f, rope_f, bwd(0), bwd(1), bwd(2), rope_b, rope_b],
        out_specs=[pl.BlockSpec((None, blk, w), lambda b, n, sc: (b, n, 0)),
                   pl.BlockSpec((None, blk, w), lambda b, n, sc: (b, nb - 1 - n, 0))],
        scratch_shapes=[pltpu.VMEM((2, RET_HEADS, RET_HEAD_DIM, RET_HEAD_DIM), F32),
                        pltpu.VMEM((2, RET_HEADS, ch, ch), F32),
                        pltpu.VMEM((2, RET_HEADS, ch, RET_HEAD_DIM), F32),
                        pltpu.VMEM((2, RET_HEADS, ch, RET_HEAD_DIM), F32)],
    )
    return pl.pallas_call(
        _retention_body,
        grid_spec=grid_spec,
        out_shape=[jax.ShapeDtypeStruct((bsz, seq, w), BF16), jax.ShapeDtypeStruct((bsz, seq, w), BF16)],
        compiler_params=_cparams(("arbitrary", "arbitrary")),
        name="retention",
    )(scal, zret3, zret3, zret3, cosf, sinf, zret3, zret3, zret3, cosf, sinf)


def _outproj_body(x_ref, yhy_ref, of_ref, ob_ref, gr_ref, ghy_ref, gret_ref, whyo_ref, wreto_ref, wo_ref,
                  n2g_ref, wrt_ref, x1_ref, pt_ref):
    dh = RET_HEAD_DIM
    o = of_ref[...].astype(F32) + ob_ref[...].astype(F32)
    parts = []
    for h in range(RET_HEADS):
        oh = o[:, h * dh:(h + 1) * dh]
        parts.append(oh * lax.rsqrt(jnp.mean(oh * oh, axis=-1, keepdims=True) + EPS))
    on = jnp.concatenate(parts, axis=1)
    gr = gr_ref[...].astype(F32)
    ret = (gr * jax.nn.sigmoid(gr)) * on
    y_ret = _dot(ret.astype(BF16), wreto_ref[...])
    y_hy = _dot(yhy_ref[...], whyo_ref[...])
    merged = jax.nn.sigmoid(ghy_ref[...].astype(F32)) * y_hy + jax.nn.sigmoid(gret_ref[...].astype(F32)) * y_ret
    x1 = x_ref[...] + _dot(merged.astype(BF16), wo_ref[...])
    x1_ref[...] = x1
    h2 = x1 * lax.rsqrt(jnp.mean(x1 * x1, axis=-1, keepdims=True) + EPS) * n2g_ref[...]
    logits = _dot_nt(wrt_ref[...], h2.astype(BF16))
    m = jnp.max(logits, axis=0, keepdims=True)
    e = jnp.exp(logits - m)
    pt_ref[...] = e / jnp.sum(e, axis=0, keepdims=True)


def _outproj(x2, yhy, o_f, o_b, zret, zg, whyo, wreto, wo, n2g, wrt, tm=512):
    n_tok = x2.shape[0]
    d = D_MODEL
    row = lambda w, col=0: pl.BlockSpec((tm, w), lambda i: (i, col))
    const = lambda shape: pl.BlockSpec(shape, lambda i: (0, 0))
    return pl.pallas_call(
        _outproj_body,
        grid=(n_tok // tm,),
        in_specs=[row(d), row(HY_WIDTH), row(RET_WIDTH), row(RET_WIDTH), row(RET_WIDTH, 3), row(d, 0), row(d, 1),
                  const((HY_WIDTH, d)), const((RET_WIDTH, d)), const((d, d)), const((1, d)), const((N_EXPERTS, d))],
        out_specs=[row(d), pl.BlockSpec((N_EXPERTS, tm), lambda i: (0, i))],
        out_shape=[jax.ShapeDtypeStruct((n_tok, d), F32), jax.ShapeDtypeStruct((N_EXPERTS, n_tok), F32)],
        compiler_params=_cparams(("parallel",)),
        name="outproj_router",
    )(x2, yhy, o_f, o_b, zret, zg, zg, whyo, wreto, wo, n2g, wrt)


def _select_body(p_ref, upper_ref, lower_ref, lowinc_ref, eye_ref, pos_ref, idx_ref, gate_ref, lo_ref, *, cap):
    rows = p_ref.shape[0]
    p = p_ref[...]
    bits = lax.bitcast_convert_type(p, I32)

    def count(mask):
        return jnp.sum(jnp.sum(mask.astype(F32), axis=1, keepdims=True), axis=0, keepdims=True)

    def bit_step(i, thr):
        cand = thr | jnp.left_shift(jnp.int32(1), 30 - i)
        return jnp.where(count(bits >= cand) >= cap, cand, thr)

    thr = lax.fori_loop(0, 31, bit_step, jnp.zeros((1, 1), I32))
    gt = bits > thr
    eq = bits == thr
    need = cap - count(gt)

    def prefix(mask_f):
        incl = _dot(mask_f.astype(BF16), upper_ref[...])
        tot = jnp.broadcast_to(incl[:, LANES - 1:LANES], incl.shape)
        base = _dot(lower_ref[...], tot.astype(BF16))
        return incl, base, tot

    eq_f = eq.astype(F32)
    incl_e, base_e, _ = prefix(eq_f)
    sel = gt | (eq & (base_e + incl_e - eq_f < need))
    sel_f = sel.astype(F32)
    incl, base, tot = prefix(sel_f)
    pos_ref[...] = jnp.where(sel, (base + incl - 1.0).astype(I32), -1)
    lo_ref[...] = base.astype(I32)

    rowend = base + tot
    incl_t = _dot_nt(lowinc_ref[...], sel_f.astype(BF16)).astype(BF16)
    p_t = []
    rem = p
    for _ in range(3):
        part = rem.astype(BF16)
        rem = rem - part.astype(F32)
        p_t.append(_dot_nt(eye_ref[...], part).astype(BF16))
    table = jnp.concatenate([incl_t] + p_t, axis=0)
    groups = 2 if (cap // LANES) % 2 == 0 else 1
    sw = groups * LANES
    wide = lambda a: jnp.concatenate([a] * groups, axis=1)
    rowend_w, tot_w = wide(rowend), wide(tot)
    r_iota = lax.broadcasted_iota(I32, (rows, sw), 0).astype(F32)
    lane_iota = lax.broadcasted_iota(I32, (LANES, sw), 0).astype(F32)

    def slot_tile(ts, carry):
        s = (ts * sw + lax.broadcasted_iota(I32, (1, sw), 1)).astype(F32)
        done = rowend_w <= s
        row = jnp.sum(done.astype(F32), axis=0, keepdims=True)
        before = jnp.sum(jnp.where(done, tot_w, 0.0), axis=0, keepdims=True)
        onehot_t = (r_iota == row).astype(BF16)
        got = _dot(table, onehot_t)
        g_t = got[:LANES]
        lane = jnp.sum((g_t <= s - before).astype(F32), axis=0, keepdims=True)
        tok = (row * LANES + lane).astype(I32)
        p_row = got[LANES:2 * LANES] + got[2 * LANES:3 * LANES] + got[3 * LANES:]
        gate = jnp.sum(jnp.where(lane_iota == lane, p_row, 0.0), axis=0, keepdims=True)
        for k in range(groups):
            idx_ref[pl.ds(ts * groups + k, 1), :] = tok[:, k * LANES:(k + 1) * LANES]
            gate_ref[pl.ds(ts * groups + k, 1), :] = gate[:, k * LANES:(k + 1) * LANES]
        return carry

    lax.fori_loop(0, cap // sw, slot_tile, 0)


def _select(pt3, cap):
    n_e, rows, _ = pt3.shape
    ii = np.arange(LANES)
    upper = jnp.asarray(ii[:, None] <= ii[None, :], BF16)
    lowinc = jnp.asarray(ii[None, :] <= ii[:, None], BF16)
    eye = jnp.asarray(ii[None, :] == ii[:, None], BF16)
    rr = np.arange(rows)
    lower = jnp.asarray(rr[None, :] < rr[:, None], BF16)
    const = lambda shape: pl.BlockSpec(shape, lambda e: (0, 0))
    tok_spec = pl.BlockSpec((None, rows, LANES), lambda e: (e, 0, 0))
    slot_spec = pl.BlockSpec((None, cap // LANES, LANES), lambda e: (e, 0, 0))
    return pl.pallas_call(
        functools.partial(_select_body, cap=cap),
        grid=(n_e,),
        in_specs=[tok_spec, const((LANES, LANES)), const((rows, rows)), const((LANES, LANES)), const((LANES, LANES))],
        out_specs=[tok_spec, slot_spec, slot_spec, tok_spec],
        out_shape=[jax.ShapeDtypeStruct((n_e, rows, LANES), I32),
                   jax.ShapeDtypeStruct((n_e, cap // LANES, LANES), I32),
                   jax.ShapeDtypeStruct((n_e, cap // LANES, LANES), F32),
                   jax.ShapeDtypeStruct((n_e, rows, LANES), I32)],
        compiler_params=_cparams(("parallel",)),
        name="expert_select",
    )(pt3, upper, lower, lowinc, eye)


def _ffn_body(idx_ref, h_hbm, n2g_ref, gate_ref, wg_ref, wu_ref, wd_ref, ye_ref, xbuf, xb, sems, *,
              pairs_per_expert):
    s = xbuf.shape[1]
    step = pl.program_id(0) * pairs_per_expert + pl.program_id(1)
    n_steps = pl.num_programs(0) * pairs_per_expert

    def row_copy(tile, i, buf):
        tok = idx_ref[tile * s + i]
        return pltpu.make_async_copy(h_hbm.at[pl.ds(tok, 1), :], xbuf.at[buf, pl.ds(i, 1), :], sems.at[buf])

    def wait_rows(buf):
        pltpu.make_async_copy(h_hbm.at[pl.ds(0, s), :], xbuf.at[buf], sems.at[buf]).wait()

    def ffn(buf):
        x = xbuf[buf]
        xb[...] = (x * lax.rsqrt(jnp.mean(x * x, axis=-1, keepdims=True) + EPS) * n2g_ref[...]).astype(BF16)
        y = None
        for f0 in range(0, EXPERT_FF, FF_CHUNK):
            f1 = min(f0 + FF_CHUNK, EXPERT_FF)
            a = _dot(xb[...], wg_ref[:, f0:f1])
            b = _dot(xb[...], wu_ref[:, f0:f1])
            hid = ((a * jax.nn.sigmoid(a)) * b).astype(BF16)
            part = _dot(hid, wd_ref[f0:f1, :])
            y = part if y is None else y + part
        for k in range(s // LANES):
            row0 = buf * s + k * LANES
            gate = gate_ref[row0 // LANES:row0 // LANES + 1, :]
            col = jnp.transpose(jnp.broadcast_to(gate, (LANES, LANES)))[:, 0:1]
            yk = y[k * LANES:(k + 1) * LANES] * col
            ye_ref[row0:row0 + LANES, :] = _pack_pair(yk[:, :D_MODEL // 2], yk[:, D_MODEL // 2:])

    @pl.when(step == 0)
    def _():
        def one(i, carry):
            row_copy(0, i, 0).start()
            return carry
        lax.fori_loop(0, s, one, 0, unroll=8)

    for i in range(s):
        row_copy(2 * step + 1, i, 1).start()
    wait_rows(0)
    ffn(0)

    @pl.when(step + 1 < n_steps)
    def _():
        for i in range(s):
            row_copy(2 * step + 2, i, 0).start()

    wait_rows(1)
    ffn(1)


def _expert_ffn(idx_flat, x1, n2g, gates, wg, wu, wd, cap, s=512):
    s = min(s, cap // 2)
    assert s % LANES == 0 and cap % (2 * s) == 0
    pairs = cap // (2 * s)
    g3 = gates.reshape(N_EXPERTS * pairs, 2 * s // LANES, LANES)
    grid_spec = pltpu.PrefetchScalarGridSpec(
        num_scalar_prefetch=1,
        grid=(N_EXPERTS, pairs),
        in_specs=[pl.BlockSpec(memory_space=pl.ANY),
                  pl.BlockSpec((1, D_MODEL), lambda e, j, idx: (0, 0)),
                  pl.BlockSpec((None, 2 * s // LANES, LANES), lambda e, j, idx: (e * pairs + j, 0, 0)),
                  pl.BlockSpec((None, D_MODEL, EXPERT_FF), lambda e, j, idx: (e, 0, 0)),
                  pl.BlockSpec((None, D_MODEL, EXPERT_FF), lambda e, j, idx: (e, 0, 0)),
                  pl.BlockSpec((None, EXPERT_FF, D_MODEL), lambda e, j, idx: (e, 0, 0))],
        out_specs=pl.BlockSpec((2 * s, D_MODEL // 2), lambda e, j, idx: (e * pairs + j, 0)),
        scratch_shapes=[pltpu.VMEM((2, s, D_MODEL), F32), pltpu.VMEM((s, D_MODEL), BF16),
                        pltpu.SemaphoreType.DMA((2,))],
    )
    return pl.pallas_call(
        functools.partial(_ffn_body, pairs_per_expert=pairs),
        grid_spec=grid_spec,
        out_shape=jax.ShapeDtypeStruct((N_EXPERTS * cap, D_MODEL // 2), F32),
        compiler_params=_cparams(("arbitrary", "arbitrary")),
        name="expert_ffn",
    )(idx_flat, x1, n2g, g3, wg, wu, wd)


def _combine_body(lo_ref, np_ref, x1_ref, pos_ref, nfg_ref, expand_ref, lov_ref, ye_hbm, o_ref, win, sems, *, cap,
                  rows_total, n_rows):
    step = pl.program_id(0)
    n_steps = pl.num_programs(0)
    groups = x1_ref.shape[0] // LANES
    w = COMBINE_WIN
    cur = step % 2

    def copies(rr, m, buf, g):
        out = []
        for e in range(N_EXPERTS):
            first = e * cap + lo_ref[e * n_rows + rr]
            intended = (first // COMBINE_ALIGN) * COMBINE_ALIGN + m * w
            actual = pl.multiple_of(jnp.minimum(intended, rows_total - w), COMBINE_ALIGN)
            out.append(pltpu.make_async_copy(ye_hbm.at[pl.ds(actual, w), :], win.at[buf, g, pl.ds(e * w, w), :],
                                             sems.at[buf, g, e]))
        return out

    def wait_windows(buf, g):
        for e in range(N_EXPERTS):
            pltpu.make_async_copy(ye_hbm.at[pl.ds(0, w), :], win.at[buf, g, pl.ds(e * w, w), :],
                                  sems.at[buf, g, e]).wait()

    def contribution(rr, m, buf, g):
        lane = lax.broadcasted_iota(I32, (1, N_EXPERTS), 1)
        first = lane * cap + lov_ref[pl.ds(rr, 1), :]
        intended = (first & jnp.int32(-COMBINE_ALIGN)) + m * w
        actual = jnp.minimum(intended, rows_total - w)
        pos = pos_ref[g * LANES:(g + 1) * LANES, :]
        glob = pos + lane * cap
        rel = glob - intended
        valid = (pos >= 0) & (rel >= 0) & (rel < w)
        local = jnp.where(valid, glob - actual, -1).astype(F32).astype(BF16)
        spread = _dot(local, expand_ref[...])
        col = (lax.broadcasted_iota(I32, (1, N_EXPERTS * w), 1) & (w - 1)).astype(F32)
        onehot = (spread == col).astype(BF16)
        lo_cols, hi_cols = _unpack_pair(win[buf, g])
        return jnp.concatenate([_dot(onehot, lo_cols.astype(BF16)), _dot(onehot, hi_cols.astype(BF16))], axis=1)

    @pl.when(step == 0)
    def _():
        for g in range(groups):
            for cp in copies(g, 0, 0, g):
                cp.start()

    @pl.when(step + 1 < n_steps)
    def _():
        for g in range(groups):
            for cp in copies((step + 1) * groups + g, 0, 1 - cur, g):
                cp.start()

    for g in range(groups):
        rr = step * groups + g
        wait_windows(cur, g)
        acc = x1_ref[g * LANES:(g + 1) * LANES, :] + contribution(rr, 0, cur, g)

        def extra_pass(m, acc, rr=rr, g=g):
            for cp in copies(rr, m, cur, g):
                cp.start()
            wait_windows(cur, g)
            return acc + contribution(rr, m, cur, g)

        acc = lax.fori_loop(1, np_ref[rr], extra_pass, acc)
        o_ref[g * LANES:(g + 1) * LANES, :] = (
            acc * lax.rsqrt(jnp.mean(acc * acc, axis=-1, keepdims=True) + EPS) * nfg_ref[...])


def _combine(lo, x1, pos_t, nfg, ye, cap):
    n_tok = x1.shape[0]
    w = COMBINE_WIN
    rows_total = ye.shape[0]
    n_rows = n_tok // LANES
    groups = next(c for c in (4, 2, 1) if n_rows % c == 0)
    tm = groups * LANES
    nxt =jnp.concatenate([lo[:, 1:], jnp.full((N_EXPERTS, 1), cap, I32)], axis=1)
    span = lo % COMBINE_ALIGN + (nxt - lo)
    n_pass = jnp.maximum(jnp.max((span + w - 1) // w, axis=0), 1).astype(I32)
    ee = np.arange(N_EXPERTS)
    expand = jnp.asarray(ee[:, None] == (np.arange(N_EXPERTS * w) // w)[None, :], BF16)
    grid_spec = pltpu.PrefetchScalarGridSpec(
        num_scalar_prefetch=2,
        grid=(n_rows // groups,),
        in_specs=[pl.BlockSpec((tm, D_MODEL), lambda i, lo_, np_: (i, 0)),
                  pl.BlockSpec((tm, N_EXPERTS), lambda i, lo_, np_: (i, 0)),
                  pl.BlockSpec((1, D_MODEL), lambda i, lo_, np_: (0, 0)),
                  pl.BlockSpec((N_EXPERTS, N_EXPERTS * w), lambda i, lo_, np_: (0, 0)),
                  pl.BlockSpec((n_rows, N_EXPERTS), lambda i, lo_, np_: (0, 0)),
                  pl.BlockSpec(memory_space=pl.ANY)],
        out_specs=pl.BlockSpec((tm, D_MODEL), lambda i, lo_, np_: (i, 0)),
        scratch_shapes=[pltpu.VMEM((2, groups, N_EXPERTS * w, D_MODEL // 2), F32),
                        pltpu.SemaphoreType.DMA((2, groups, N_EXPERTS))],
    )
    return pl.pallas_call(
        functools.partial(_combine_body, cap=cap, rows_total=rows_total, n_rows=n_rows),
        grid_spec=grid_spec,
        out_shape=jax.ShapeDtypeStruct((n_tok, D_MODEL), F32),
        compiler_params=_cparams(("arbitrary",)),
        name="moe_combine_norm",
    )(lo.reshape(-1), n_pass, x1, pos_t, nfg, expand, lo.T, ye)


def _trunk(x, w):
    bsz, seq, d = x.shape
    n_tok = bsz * seq
    x2 = x.reshape(n_tok, d)
    tabs = _fft_tables(seq)

    zc, zret, zg = _inproj(x2, w["norm1_g"], w["w_in"], w["hy_conv_w"], w["hy_conv_b"], seq)
    zc = zc.reshape(bsz, seq, -1)

    h_time, h_abs = _hyena_filter_time(seq, w["hy_w1"], w["hy_b1"], w["hy_freq1"], w["hy_w2"], w["hy_b2"],
                                       w["hy_freq2"], w["hy_w3"])
    kr, ki = _stage_b_filter(_stage_a_real(h_time, tabs), h_abs, tabs)
    yhy = _hyena(zc, kr, ki, w["hy_skip"], tabs)

    o_f, o_b = _retention(zret.reshape(bsz, seq, -1), w["ret_decay_logit"])
    x1, pt = _outproj(x2, yhy, o_f.reshape(n_tok, -1), o_b.reshape(n_tok, -1), zret, zg,
                      w["w_hy_out"], w["w_ret_out"], w["w_o"], w["norm2_g"], w["w_router_t"])

    cap = CAPACITY_FACTOR * n_tok // N_EXPERTS
    rows = n_tok // LANES
    pos, idx, gates, lo = _select(pt.reshape(N_EXPERTS, rows, LANES), cap)
    ye = _expert_ffn(idx.reshape(-1), x1, w["norm2_g"], gates, w["w_gate"], w["w_up"], w["w_down"], cap)
    pos_t = pos.reshape(N_EXPERTS, n_tok).T
    y = _combine(lo[:, :, 0], x1, pos_t, w["norm_f_g"], ye, cap)
    return y.reshape(bsz, seq, d)


def kernel(x_prompt, x_sample, norm1_g, w_in, hy_conv_w, hy_conv_b, hy_w1, hy_b1, hy_freq1, hy_w2, hy_b2, hy_freq2,
           hy_w3, hy_skip, ret_decay_logit, w_hy_out, w_ret_out, w_o, norm2_g, w_router, w_gate, w_up, w_down,
           norm_f_g):
    layer = 0
    w = dict(
        norm1_g=norm1_g[layer].astype(F32)[None], w_in=w_in[layer].astype(BF16),
        hy_conv_w=hy_conv_w[layer].astype(F32), hy_conv_b=hy_conv_b[layer].astype(F32)[None],
        hy_w1=hy_w1[layer], hy_b1=hy_b1[layer], hy_freq1=hy_freq1[layer], hy_w2=hy_w2[layer], hy_b2=hy_b2[layer],
        hy_freq2=hy_freq2[layer], hy_w3=hy_w3[layer], hy_skip=hy_skip[layer],
        ret_decay_logit=ret_decay_logit[layer],
        w_hy_out=w_hy_out[layer].astype(BF16), w_ret_out=w_ret_out[layer].astype(BF16), w_o=w_o[layer].astype(BF16),
        norm2_g=norm2_g[layer].astype(F32)[None], w_router_t=w_router[layer].T.astype(BF16),
        w_gate=w_gate[layer].astype(BF16), w_up=w_up[layer].astype(BF16), w_down=w_down[layer].astype(BF16),
        norm_f_g=norm_f_g.astype(F32)[None],
    )
    return _trunk(x_prompt, w), _trunk(x_sample, w)
```

```python
import functools
import math

import numpy as np
import jax
import jax.numpy as jnp
from jax import lax
from jax.experimental import pallas as pl
from jax.experimental.pallas import tpu as pltpu

F32 = jnp.float32
BF16 = jnp.bfloat16
I32 = jnp.int32

D_MODEL = 1024
HY_WIDTH = 512
HY_ORDER = 2
HY_BANDS = 16
HY_FILTER_WIDTH = 64
HY_FAST_DECAY = 0.3
HY_SLOW_DECAY = 1.5
HY_DECAY_TARGET = 1e-2
RET_WIDTH = 512
RET_HEADS = 4
RET_HEAD_DIM = 128
RET_CHUNK = 128
ROPE_BASE = 10000.0
N_EXPERTS = 16
EXPERT_FF = 1408
CAPACITY_FACTOR = 2
EPS = 1e-6

VMEM_LIMIT_MIB = 48
LANES = 128
SUBLANES = 8
EMB_PAD = 128
DFT_N1_MAX = 128
FF_CHUNK = 256
COMBINE_ALIGN = SUBLANES
COMBINE_WIN = 32


def _cparams(sem):
    return pltpu.CompilerParams(dimension_semantics=sem, vmem_limit_bytes=VMEM_LIMIT_MIB * 1024 * 1024)


def _dot(a, b):
    return jnp.dot(a, b, preferred_element_type=F32)


def _dot_nt(a, b):
    return lax.dot_general(a, b, (((1,), (1,)), ((), ())), preferred_element_type=F32)


def _dot_tn(a, b):
    return lax.dot_general(a, b, (((0,), (0,)), ((), ())), preferred_element_type=F32)


def _split(a):
    hi = a.astype(BF16)
    lo = (a - hi.astype(F32)).astype(BF16)
    return hi, lo


def _dot3(a, b):
    ah, al = _split(a)
    bh, bl = _split(b)
    return _dot(ah, bh) + _dot(al, bh) + _dot(ah, bl)


def _inproj_body(x_ref, g_ref, w_ref, cw_ref, cb_ref, zc_ref, zret_ref, zg_ref, z_prev, row_prev, *, tiles_per_seq):
    i = pl.program_id(0)
    tm = x_ref.shape[0]
    n_hy = zc_ref.shape[1]
    n_ret = zret_ref.shape[1]

    @pl.when(i == 0)
    def _():
        z_prev[...] = jnp.zeros_like(z_prev)
        row_prev[...] = jnp.zeros_like(row_prev)

    x = x_ref[...]
    ms = jnp.mean(x * x, axis=-1, keepdims=True)
    h = (x * lax.rsqrt(ms + EPS) * g_ref[...]).astype(BF16)
    z_hy = _dot(h, w_ref[:, :n_hy])
    zret_ref[...] = _dot(h, w_ref[:, n_hy:n_hy + n_ret]).astype(BF16)
    zg_ref[...] = _dot(h, w_ref[:, n_hy + n_ret:]).astype(BF16)

    t_prev = (i - 1) % tiles_per_seq
    zp = z_prev[...]
    before = jnp.where(t_prev == 0, 0.0, row_prev[0:1, :])
    after = jnp.where(t_prev == tiles_per_seq - 1, 0.0, z_hy[0:1, :])
    row = lax.broadcasted_iota(I32, (tm, 1), 0)
    zm1 = jnp.where(row == 0, before, pltpu.roll(zp, 1, axis=0))
    zp1 = jnp.where(row == tm - 1, after, pltpu.roll(zp, tm - 1, axis=0))
    zc_ref[...] = zm1 * cw_ref[0:1, :] + zp * cw_ref[1:2, :] + zp1 * cw_ref[2:3, :] + cb_ref[...]
    row_prev[...] = jnp.broadcast_to(zp[tm - 1:tm, :], row_prev.shape)
    z_prev[...] = z_hy


def _inproj(x2, g, w_bf, conv_w, conv_b, seq, tm=512):
    n_tok = x2.shape[0]
    n_hy, n_ret, n_g = 3 * HY_WIDTH, 4 * RET_WIDTH, 2 * D_MODEL
    tm = min(tm, seq)
    n_tiles = n_tok // tm
    cur = lambda i: (jnp.minimum(i, n_tiles - 1), 0)
    const = lambda i: (0, 0)
    return pl.pallas_call(
        functools.partial(_inproj_body, tiles_per_seq=seq // tm),
        grid=(n_tiles + 1,),
        in_specs=[pl.BlockSpec((tm, D_MODEL), cur),
                  pl.BlockSpec((1, D_MODEL), const),
                  pl.BlockSpec((D_MODEL, n_hy + n_ret + n_g), const, pipeline_mode=pl.Buffered(1)),
                  pl.BlockSpec((3, n_hy), const),
                  pl.BlockSpec((1, n_hy), const)],
        out_specs=[pl.BlockSpec((tm, n_hy), lambda i: (jnp.maximum(i - 1, 0), 0)),
                   pl.BlockSpec((tm, n_ret), cur),
                   pl.BlockSpec((tm, n_g), cur)],
        out_shape=[jax.ShapeDtypeStruct((n_tok, n_hy), F32),
                   jax.ShapeDtypeStruct((n_tok, n_ret), BF16),
                   jax.ShapeDtypeStruct((n_tok, n_g), BF16)],
        scratch_shapes=[pltpu.VMEM((tm, n_hy), F32), pltpu.VMEM((8, n_hy), F32)],
        compiler_params=_cparams(("arbitrary",)),
        name="inproj_shortconv",
    )(x2, g, w_bf, conv_w, conv_b)


def _fft_dims(seq):
    n = 2 * seq
    n1 = min(DFT_N1_MAX, 1 << (n.bit_length() // 2))
    n2 = n // n1
    assert n1 * n2 == n and n1 % 2 == 0
    return n, n1, n2


def _fft_tables(seq):
    n, n1, n2 = _fft_dims(seq)
    r1 = n1 // 2
    k1 = jnp.arange(n1, dtype=I32)
    j = jnp.arange(n2, dtype=I32)

    at = ((j[:, None] * k1[None, :]) % n).astype(F32) * (2.0 * math.pi / n)
    ct, st = jnp.cos(at), jnp.sin(at)

    def cos_sin(n1_count, k1_first):
        a1 = ((k1[:, None] * jnp.arange(n1_count, dtype=I32)[None, :]) % n1).astype(F32) * (2.0 * math.pi / n1)
        cf, sf = jnp.cos(a1), jnp.sin(a1)
        if k1_first:
            cf, sf, ctb, stb = cf[None], sf[None], ct[:, :, None], st[:, :, None]
        else:
            cf, sf, ctb, stb = cf.T[None], sf.T[None], ct[:, None, :], st[:, None, :]
        return cf * ctb - sf * stb, sf * ctb + cf * stb

    cat = jnp.concatenate
    c, s = cos_sin(r1, True)
    mat_a = cat([cat([c, s], axis=2), cat([-s, c], axis=2)], axis=1).astype(BF16)
    c, s = cos_sin(n1, True)
    mat_a_real = cat([c, -s], axis=1).astype(BF16)
    c, s = cos_sin(r1, False)
    mat_c = cat([cat([c, -s], axis=2), cat([s, c], axis=2)], axis=1).astype(BF16)
    a2 = 2.0 * np.pi * np.outer(np.arange(n2), np.arange(n2)) / n2
    gr, gi = np.cos(a2), -np.sin(a2)
    mat_f = np.block([[gr, -gi], [gi, gr]])
    mat_i = np.block([[gr, gi], [-gi, gr]])
    return dict(n=n, n1=n1, n2=n2, mat_a=mat_a, mat_a_real=mat_a_real, mat_c=mat_c,
                mat_f=jnp.asarray(mat_f, F32).astype(BF16), mat_i=jnp.asarray(mat_i, F32).astype(BF16))


def _filter_body(z_ref, zt_ref, w1t_ref, b1_ref, f1_ref, w2t_ref, b2_ref, f2_ref, w3_ref, dl_ref, h_ref, s_ref, *,
                 seq):
    i = pl.program_id(0)
    tl = z_ref.shape[0]
    a = jnp.sin(f1_ref[...] * (_dot3(w1t_ref[...], zt_ref[...]) + b1_ref[...]))
    a = jnp.sin(f2_ref[...] * (_dot3(w2t_ref[...], a) + b2_ref[...]))
    h = _dot_tn(a.astype(BF16), w3_ref[...].astype(BF16))
    win = jnp.exp(-z_ref[:, 0:1] * dl_ref[...])
    h = h * jnp.concatenate([win] * HY_ORDER, axis=1)
    row = i * tl + lax.broadcasted_iota(I32, (tl, 1), 0)
    h = jnp.where(row == seq, 0.0, h)
    h_ref[...] = h

    @pl.when(i == 0)
    def _():
        s_ref[...] = jnp.zeros_like(s_ref)

    s_ref[...] += jnp.broadcast_to(jnp.sum(jnp.abs(h), axis=0, keepdims=True), s_ref.shape)


def _hyena_filter_time(seq, w1, b1, f1, w2, b2, f2, w3, tl=512):
    n = 2 * seq
    tl = min(tl, seq)
    idx = jnp.arange(n, dtype=I32)
    p = jnp.minimum(jnp.where(idx < seq, idx, n - idx), seq - 1).astype(F32)
    t = p / float(seq - 1)
    ang = 2.0 * math.pi * p / seq
    bands = jnp.linspace(1e-4, HY_BANDS - 1, HY_BANDS, dtype=F32)
    phase = ang[:, None] * bands[None, :]
    emb = jnp.concatenate([t[:, None], jnp.cos(phase), -jnp.sin(phase)], axis=-1)
    emb = jnp.pad(emb, ((0, 0), (0, EMB_PAD - emb.shape[1])))
    w1p = jnp.pad(w1.astype(F32), ((0, EMB_PAD - w1.shape[0]), (0, 0)))
    w3d = w3.astype(F32).reshape(HY_FILTER_WIDTH, HY_ORDER, 2, HY_WIDTH).transpose(2, 0, 1, 3)
    w3d = w3d.reshape(2, HY_FILTER_WIDTH, HY_ORDER * HY_WIDTH)
    deltas = jnp.abs(jnp.linspace(math.log(HY_DECAY_TARGET) / HY_SLOW_DECAY,
                                  math.log(HY_DECAY_TARGET) / HY_FAST_DECAY, HY_WIDTH, dtype=F32))[None, :]
    fw = HY_FILTER_WIDTH
    nblk_half = seq // tl
    const = lambda i: (0, 0)
    col = lambda v: v.astype(F32)[:, None]
    return pl.pallas_call(
        functools.partial(_filter_body, seq=seq),
        grid=(n // tl,),
        in_specs=[pl.BlockSpec((tl, EMB_PAD), lambda i: (i, 0)),
                  pl.BlockSpec((EMB_PAD, tl), lambda i: (0, i)),
                  pl.BlockSpec((fw, EMB_PAD), const), pl.BlockSpec((fw, 1), const), pl.BlockSpec((fw, 1), const),
                  pl.BlockSpec((fw, fw), const), pl.BlockSpec((fw, 1), const), pl.BlockSpec((fw, 1), const),
                  pl.BlockSpec((None, fw, HY_ORDER * HY_WIDTH), lambda i: (i // nblk_half, 0, 0)),
                  pl.BlockSpec((1, HY_WIDTH), const)],
        out_specs=[pl.BlockSpec((tl, HY_ORDER * HY_WIDTH), lambda i: (i, 0)),
                   pl.BlockSpec((8, HY_ORDER * HY_WIDTH), const)],
        out_shape=[jax.ShapeDtypeStruct((n, HY_ORDER * HY_WIDTH), F32),
                   jax.ShapeDtypeStruct((8, HY_ORDER * HY_WIDTH), F32)],
        compiler_params=_cparams(("arbitrary",)),
        name="hyena_filter",
    )(emb, emb.T, w1p.T, col(b1), col(f1), w2.astype(F32).T, col(b2), col(f2), w3d, deltas)


def _stage_a_real_body(h_ref, m_ref, a_ref):
    n1 = h_ref.shape[0]
    h_t = jnp.swapaxes(h_ref[...], 0, 1)
    packed = []
    for j in range(h_ref.shape[1]):
        r = _dot(m_ref[j], h_t[j].astype(BF16))
        packed.append(_pack_pair(r[:n1], r[n1:]))
    a_ref[...] = jnp.swapaxes(jnp.stack(packed, axis=0), 0, 1)


def _stage_a_real(h, tabs, n2c=8, cw=512):
    n1, n2 = tabs["n1"], tabs["n2"]
    c = h.shape[1]
    h3 = h.reshape(n1, n2, c)
    mat = tabs["mat_a_real"]
    return pl.pallas_call(
        _stage_a_real_body,
        grid=(n2 // n2c, c // cw),
        in_specs=[pl.BlockSpec((n1, n2c, cw), lambda j, k: (0, j, k)),
                  pl.BlockSpec((n2c,) + mat.shape[1:], lambda j, k: (j, 0, 0))],
        out_specs=pl.BlockSpec((n1, n2c, cw), lambda j, k: (0, j, k)),
        out_shape=jax.ShapeDtypeStruct((n1, n2, c), F32),
        compiler_params=_cparams(("parallel", "parallel")),
        name="dft_a_filter",
    )(h3, mat)


def _pack_pair(re, im):
    hi = lax.bitcast_convert_type(re.astype(BF16).astype(F32), I32)
    lo = lax.bitcast_convert_type(im.astype(BF16).astype(F32), I32)
    return lax.bitcast_convert_type(hi | lax.shift_right_logical(lo, 16), F32)


def _unpack_pair(packed):
    word = lax.bitcast_convert_type(packed, I32)
    re = lax.bitcast_convert_type(word & jnp.int32(-65536), F32)
    im = lax.bitcast_convert_type(lax.shift_left(word, 16), F32)
    return re, im


def _stage_a_body(u_ref, m_ref, a_ref):
    _, r1, n2c, cw = u_ref.shape
    n1 = 2 * r1
    u_t = jnp.swapaxes(u_ref[...].reshape(n1, n2c, cw), 0, 1)
    packed = []
    for j in range(n2c):
        r = _dot(m_ref[j], u_t[j].astype(BF16))
        packed.append(_pack_pair(r[:n1], r[n1:]))
    a_ref[...] = jnp.swapaxes(jnp.stack(packed, axis=0), 0, 1)


def _stage_a(u5, col, tabs, n2c=8):
    n1, n2 = tabs["n1"], tabs["n2"]
    npair, _, r1, _, _ = u5.shape
    cw = HY_WIDTH
    mat = tabs["mat_a"]
    return pl.pallas_call(
        _stage_a_body,
        grid=(npair, n2 // n2c),
        in_specs=[pl.BlockSpec((None, 2, r1, n2c, cw), lambda p, j: (p, 0, 0, j, col)),
                  pl.BlockSpec((n2c,) + mat.shape[1:], lambda p, j: (j, 0, 0))],
        out_specs=pl.BlockSpec((None, n1, n2c, cw), lambda p, j: (p, 0, j, 0)),
        out_shape=jax.ShapeDtypeStruct((npair, n1, n2, cw), F32),
        compiler_params=_cparams(("parallel", "parallel")),
        name="dft_a",
    )(u5, mat)


def _stage_b_filter_body(a_ref, g_ref, s_ref, kr_ref, ki_ref, *, n):
    n2 = a_ref.shape[1]
    scale = 1.0 / (s_ref[0:1, :] * float(n))
    for kk in range(a_ref.shape[0]):
        d = _dot(g_ref[...], jnp.concatenate(_unpack_pair(a_ref[kk]), axis=0).astype(BF16))
        kr_ref[kk] = d[:n2] * scale
        ki_ref[kk] = d[n2:] * scale


def _stage_b_filter(a, s, tabs):
    n, n1, n2 = tabs["n"], tabs["n1"], tabs["n2"]
    c = a.shape[-1]
    kb = max(1, min(n1, 512 // n2))
    g = tabs["mat_f"]
    return pl.pallas_call(
        functools.partial(_stage_b_filter_body, n=n),
        grid=(n1 // kb,),
        in_specs=[pl.BlockSpec((kb, n2, c), lambda k: (k, 0, 0)),
                  pl.BlockSpec(g.shape, lambda k: (0, 0)),
                  pl.BlockSpec(s.shape, lambda k: (0, 0))],
        out_specs=[pl.BlockSpec((kb, n2, c), lambda k: (k, 0, 0)),
                   pl.BlockSpec((kb, n2, c), lambda k: (k, 0, 0))],
        out_shape=[jax.ShapeDtypeStruct((n1, n2, c), F32), jax.ShapeDtypeStruct((n1, n2, c), F32)],
        compiler_params=_cparams(("parallel",)),
        name="dft_b_filter",
    )(a, g, s)


def _stage_b_body(a_ref, kr_ref, ki_ref, gf_ref, gi_ref, c_ref):
    n2 = a_ref.shape[1]
    for kk in range(a_ref.shape[0]):
        d = _dot(gf_ref[...], jnp.concatenate(_unpack_pair(a_ref[kk]), axis=0).astype(BF16))
        dr, di = d[:n2], d[n2:]
        kr, ki = kr_ref[kk], ki_ref[kk]
        yr = dr * kr - di * ki
        yi = dr * ki + di * kr
        e = _dot(gi_ref[...], jnp.concatenate([yr, yi], axis=0).astype(BF16))
        c_ref[kk] = _pack_pair(e[:n2], e[n2:])


def _stage_b(a, kr, ki, order, tabs):
    n1, n2 = tabs["n1"], tabs["n2"]
    npair = a.shape[0]
    cw = HY_WIDTH
    kb = max(1, min(n1, 1024 // n2))
    gf = tabs["mat_f"]
    gi = tabs["mat_i"]
    const = lambda k, p: (0, 0)
    return pl.pallas_call(
        _stage_b_body,
        grid=(n1 // kb, npair),
        in_specs=[pl.BlockSpec((None, kb, n2, cw), lambda k, p: (p, k, 0, 0)),
                  pl.BlockSpec((kb, n2, cw), lambda k, p: (k, 0, order)),
                  pl.BlockSpec((kb, n2, cw), lambda k, p: (k, 0, order)),
                  pl.BlockSpec(gf.shape, const), pl.BlockSpec(gi.shape, const)],
        out_specs=pl.BlockSpec((None, kb, n2, cw), lambda k, p: (p, k, 0, 0)),
        out_shape=jax.ShapeDtypeStruct(a.shape, F32),
        compiler_params=_cparams(("parallel", "parallel")),
        name="dft_b",
    )(a, kr, ki, gf, gi)


def _stage_c_body(c_ref, u_ref, g_ref, skip_ref, mc_ref, ma_ref, y_ref, *rest, fuse_a):
    _, r1, n2c, cw = u_ref.shape
    n1 = 2 * r1
    ys = []
    c_t = jnp.swapaxes(c_ref[...], 0, 1)
    for j in range(n2c):
        cc = jnp.concatenate(_unpack_pair(c_t[j]), axis=0)
        ys.append(_dot(mc_ref[j], cc.astype(BF16)))
    y = jnp.swapaxes(jnp.stack(ys, axis=0), 0, 1)
    u = u_ref[...].reshape(n1, n2c, cw)
    g = g_ref[...].reshape(n1, n2c, cw)
    yo = g * (y + u * skip_ref[...].reshape(1, 1, cw))
    y_ref[...] = yo.reshape(2, r1, n2c, cw).astype(y_ref.dtype)
    if fuse_a:
        a_ref = rest[0]
        yo_t = jnp.swapaxes(yo, 0, 1)
        packed = []
        for j in range(n2c):
            r = _dot(ma_ref[j], yo_t[j].astype(BF16))
            packed.append(_pack_pair(r[:n1], r[n1:]))
        a_ref[...] = jnp.swapaxes(jnp.stack(packed, axis=0), 0, 1)


def _stage_c(c, u5, u_col, g5, g_col, skip, tabs, fuse_a, out_dtype, n2c=8):
    n1, n2 = tabs["n1"], tabs["n2"]
    npair, _, r1, _, _ = u5.shape
    cw = HY_WIDTH
    mc = tabs["mat_c"]
    ma = tabs["mat_a"]
    const = lambda p, j: (0, 0)
    per_j = lambda m: pl.BlockSpec((n2c,) + m.shape[1:], lambda p, j: (j, 0, 0))
    out_shape = [jax.ShapeDtypeStruct((npair, 2, r1, n2, cw), out_dtype)]
    out_specs = [pl.BlockSpec((None, 2, r1, n2c, cw), lambda p, j: (p, 0, 0, j, 0))]
    if fuse_a:
        out_shape.append(jax.ShapeDtypeStruct((npair, n1, n2, cw), F32))
        out_specs.append(pl.BlockSpec((None, n1, n2c, cw), lambda p, j: (p, 0, j, 0)))
    return pl.pallas_call(
        functools.partial(_stage_c_body, fuse_a=fuse_a),
        grid=(npair, n2 // n2c),
        in_specs=[pl.BlockSpec((None, n1, n2c, cw), lambda p, j: (p, 0, j, 0)),
                  pl.BlockSpec((None, 2, r1, n2c, cw), lambda p, j: (p, 0, 0, j, u_col)),
                  pl.BlockSpec((None, 2, r1, n2c, cw), lambda p, j: (p, 0, 0, j, g_col)),
                  pl.BlockSpec((1, cw), const), per_j(mc), per_j(ma)],
        out_specs=out_specs,
        out_shape=out_shape,
        compiler_params=_cparams(("parallel", "parallel")),
        name="dft_c",
    )(c, u5, g5, skip, mc, ma)


def _hyena(zc3, kr, ki, skip, tabs):
    bsz, seq, _ = zc3.shape
    n1, n2 = tabs["n1"], tabs["n2"]
    r1 = n1 // 2
    z5 = zc3.reshape(bsz // 2, 2, r1, n2, 3 * HY_WIDTH)
    a = _stage_a(z5, 2, tabs)
    c = _stage_b(a, kr, ki, 0, tabs)
    y1, a2 = _stage_c(c, z5, 2, z5, 0, skip[0:1].astype(F32), tabs, True, F32)
    c2 = _stage_b(a2, kr, ki, 1, tabs)
    (y2,) = _stage_c(c2, y1, 0, z5, 1, skip[1:2].astype(F32), tabs, False, BF16)
    return y2.reshape(bsz * seq, HY_WIDTH)


def _retention_body(sc_ref, qf_ref, kf_ref, vf_ref, cf_ref, sf_ref, qb_ref, kb_ref, vb_ref, cb_ref, sb_ref,
                    of_ref, ob_ref, state, dmat, qwt, kwt):
    ch = RET_CHUNK
    dh = RET_HEAD_DIM

    @pl.when(pl.program_id(1) == 0)
    def _():
        state[...] = jnp.zeros_like(state)
        ci = lax.broadcasted_iota(I32, (ch, ch), 0).astype(F32)
        mi = lax.broadcasted_iota(I32, (ch, ch), 1).astype(F32)
        lag = ci - mi
        for d in range(2):
            for h in range(RET_HEADS):
                lg = sc_ref[d * RET_HEADS + h]
                if d == 0:
                    dmat[d, h] = jnp.where(lag >= 0, jnp.exp(lg * jnp.maximum(lag, 0.0)), 0.0)
                    qwt[d, h] = jnp.exp(lg * (ci + 1.0))
                    kwt[d, h] = jnp.exp(lg * (ch - 1.0 - ci))
                else:
                    dmat[d, h] = jnp.where(lag < 0, jnp.exp(lg * jnp.maximum(-lag, 0.0)), 0.0)
                    qwt[d, h] = jnp.exp(lg * (ch - ci))
                    kwt[d, h] = jnp.exp(lg * ci)

    scale = RET_HEAD_DIM ** -0.5
    refs = ((qf_ref, kf_ref, vf_ref, cf_ref, sf_ref, of_ref), (qb_ref, kb_ref, vb_ref, cb_ref, sb_ref, ob_ref))
    work = [(d, h) for d in range(2) for h in range(RET_HEADS)]
    per_step = qf_ref.shape[0] // ch
    for c in range(per_step):
        rows = (slice(c * ch, (c + 1) * ch), slice((per_step - 1 - c) * ch, (per_step - c) * ch))
        qs, ks, qws, kws, vs = {}, {}, {}, {}, {}
        for d, h in work:
            q_ref, k_ref, v_ref, c_ref, s_ref, _ = refs[d]
            sl = slice(h * dh, (h + 1) * dh)
            cosf = c_ref[rows[d], :]
            sinf = s_ref[rows[d], :]
            q = q_ref[rows[d], sl].astype(F32)
            k = k_ref[rows[d], sl].astype(F32)
            q = q * cosf + pltpu.roll(q, dh // 2, axis=1) * sinf
            k = (k * cosf + pltpu.roll(k, dh // 2, axis=1) * sinf) * scale
            qs[d, h], ks[d, h] = q.astype(BF16), k.astype(BF16)
            qws[d, h], kws[d, h] = (q * qwt[d, h]).astype(BF16), (k * kwt[d, h]).astype(BF16)
            vs[d, h] = v_ref[rows[d], sl]
        scores = {dh_: _dot_nt(qs[dh_], ks[dh_]) for dh_ in work}
        for d, h in work:
            lhs = jnp.concatenate([(scores[d, h] * dmat[d, h]).astype(BF16), qws[d, h]], axis=1)
            rhs = jnp.concatenate([vs[d, h], state[d, h].astype(BF16)], axis=0)
            refs[d][5][rows[d], h * dh:(h + 1) * dh] = _dot(lhs, rhs).astype(BF16)
        for d, h in work:
            cdec = sc_ref[2 * RET_HEADS + d * RET_HEADS + h]
            state[d, h] = state[d, h] * cdec + _dot_tn(kws[d, h], vs[d, h])


def _retention(zret3, decay_logit):
    bsz, seq, _ = zret3.shape
    ch = RET_CHUNK
    nc = seq // ch
    half = RET_HEAD_DIM // 2
    inv_freq = ROPE_BASE ** (-jnp.arange(half, dtype=F32) / half)
    ang = jnp.arange(seq, dtype=F32)[:, None] * inv_freq[None, :]
    cosf = jnp.concatenate([jnp.cos(ang), jnp.cos(ang)], axis=1)
    sinf = jnp.concatenate([-jnp.sin(ang), jnp.sin(ang)], axis=1)
    log_g = jax.nn.log_sigmoid(decay_logit.astype(F32)).reshape(-1)
    scal = jnp.concatenate([log_g, jnp.exp(log_g * ch)])
    w = RET_WIDTH
    per_step = next(c for c in (8, 4, 2, 1) if nc % c == 0)
    nb = nc // per_step
    blk = per_step * ch
    fwd = lambda col: pl.BlockSpec((None, blk, w), lambda b, n, sc: (b, n, col))
    bwd = lambda col: pl.BlockSpec((None, blk, w), lambda b, n, sc: (b, nb - 1 - n, col))
    rope_f = pl.BlockSpec((blk, RET_HEAD_DIM), lambda b, n, sc: (n, 0))
    rope_b = pl.BlockSpec((blk, RET_HEAD_DIM), lambda b, n, sc: (nb - 1 - n, 0))
    grid_spec = pltpu.PrefetchScalarGridSpec(
        num_scalar_prefetch=1,
        grid=(bsz, nb),
        in_specs=[fwd(0), fwd(1), fwd(2), rope_f, rope_f, bwd(0), bwd(1), bwd(2), rope_b, rope_b],
        out_specs=[pl.BlockSpec((None, blk, w), lambda b, n, sc: (b, n, 0)),
                   pl.BlockSpec((None, blk, w), lambda b, n, sc: (b, nb - 1 - n, 0))],
        scratch_shapes=[pltpu.VMEM((2, RET_HEADS, RET_HEAD_DIM, RET_HEAD_DIM), F32),
                        pltpu.VMEM((2, RET_HEADS, ch, ch), F32),
                        pltpu.VMEM((2, RET_HEADS, ch, RET_HEAD_DIM), F32),
                        pltpu.VMEM((2, RET_HEADS, ch, RET_HEAD_DIM), F32)],
    )
    return pl.pallas_call(
        _retention_body,
        grid_spec=grid_spec,
        out_shape=[jax.ShapeDtypeStruct((bsz, seq, w), BF16), jax.ShapeDtypeStruct((bsz, seq, w), BF16)],
        compiler_params=_cparams(("arbitrary", "arbitrary")),
        name="retention",
    )(scal, zret3, zret3, zret3, cosf, sinf, zret3, zret3, zret3, cosf, sinf)


def _outproj_body(x_ref, yhy_ref, of_ref, ob_ref, gr_ref, ghy_ref, gret_ref, whyo_ref, wreto_ref, wo_ref,
                  n2g_ref, wrt_ref, x1_ref, pt_ref):
    dh = RET_HEAD_DIM
    o = of_ref[...].astype(F32) + ob_ref[...].astype(F32)
    parts = []
    for h in range(RET_HEADS):
        oh = o[:, h * dh:(h + 1) * dh]
        parts.append(oh * lax.rsqrt(jnp.mean(oh * oh, axis=-1, keepdims=True) + EPS))
    on = jnp.concatenate(parts, axis=1)
    gr = gr_ref[...].astype(F32)
    ret = (gr * jax.nn.sigmoid(gr)) * on
    y_ret = _dot(ret.astype(BF16), wreto_ref[...])
    y_hy = _dot(yhy_ref[...], whyo_ref[...])
    merged = jax.nn.sigmoid(ghy_ref[...].astype(F32)) * y_hy + jax.nn.sigmoid(gret_ref[...].astype(F32)) * y_ret
    x1 = x_ref[...] + _dot(merged.astype(BF16), wo_ref[...])
    x1_ref[...] = x1
    h2 = x1 * lax.rsqrt(jnp.mean(x1 * x1, axis=-1, keepdims=True) + EPS) * n2g_ref[...]
    logits = _dot_nt(wrt_ref[...], h2.astype(BF16))
    m = jnp.max(logits, axis=0, keepdims=True)
    e = jnp.exp(logits - m)
    pt_ref[...] = e / jnp.sum(e, axis=0, keepdims=True)


def _outproj(x2, yhy, o_f, o_b, zret, zg, whyo, wreto, wo, n2g, wrt, tm=512):
    n_tok = x2.shape[0]
    d = D_MODEL
    row = lambda w, col=0: pl.BlockSpec((tm, w), lambda i: (i, col))
    const = lambda shape: pl.BlockSpec(shape, lambda i: (0, 0))
    return pl.pallas_call(
        _outproj_body,
        grid=(n_tok // tm,),
        in_specs=[row(d), row(HY_WIDTH), row(RET_WIDTH), row(RET_WIDTH), row(RET_WIDTH, 3), row(d, 0), row(d, 1),
                  const((HY_WIDTH, d)), const((RET_WIDTH, d)), const((d, d)), const((1, d)), const((N_EXPERTS, d))],
        out_specs=[row(d), pl.BlockSpec((N_EXPERTS, tm), lambda i: (0, i))],
        out_shape=[jax.ShapeDtypeStruct((n_tok, d), F32), jax.ShapeDtypeStruct((N_EXPERTS, n_tok), F32)],
        compiler_params=_cparams(("parallel",)),
        name="outproj_router",
    )(x2, yhy, o_f, o_b, zret, zg, zg, whyo, wreto, wo, n2g, wrt)


def _select_body(p_ref, upper_ref, lower_ref, lowinc_ref, eye_ref, pos_ref, idx_ref, gate_ref, lo_ref, *, cap):
    rows = p_ref.shape[0]
    p = p_ref[...]
    bits = lax.bitcast_convert_type(p, I32)

    def count(mask):
        return jnp.sum(jnp.sum(mask.astype(F32), axis=1, keepdims=True), axis=0, keepdims=True)

    def bit_step(i, thr):
        cand = thr | jnp.left_shift(jnp.int32(1), 30 - i)
        return jnp.where(count(bits >= cand) >= cap, cand, thr)

    thr = lax.fori_loop(0, 31, bit_step, jnp.zeros((1, 1), I32))
    gt = bits > thr
    eq = bits == thr
    need = cap - count(gt)

    def prefix(mask_f):
        incl = _dot(mask_f.astype(BF16), upper_ref[...])
        tot = jnp.broadcast_to(incl[:, LANES - 1:LANES], incl.shape)
        base = _dot(lower_ref[...], tot.astype(BF16))
        return incl, base, tot

    eq_f = eq.astype(F32)
    incl_e, base_e, _ = prefix(eq_f)
    sel = gt | (eq & (base_e + incl_e - eq_f < need))
    sel_f = sel.astype(F32)
    incl, base, tot = prefix(sel_f)
    pos_ref[...] = jnp.where(sel, (base + incl - 1.0).astype(I32), -1)
    lo_ref[...] = base.astype(I32)

    rowend = base + tot
    incl_t = _dot_nt(lowinc_ref[...], sel_f.astype(BF16)).astype(BF16)
    p_t = []
    rem = p
    for _ in range(3):
        part = rem.astype(BF16)
        rem = rem - part.astype(F32)
        p_t.append(_dot_nt(eye_ref[...], part).astype(BF16))
    table = jnp.concatenate([incl_t] + p_t, axis=0)
    groups = 2 if (cap // LANES) % 2 == 0 else 1
    sw = groups * LANES
    wide = lambda a: jnp.concatenate([a] * groups, axis=1)
    rowend_w, tot_w = wide(rowend), wide(tot)
    r_iota = lax.broadcasted_iota(I32, (rows, sw), 0).astype(F32)
    lane_iota = lax.broadcasted_iota(I32, (LANES, sw), 0).astype(F32)

    def slot_tile(ts, carry):
        s = (ts * sw + lax.broadcasted_iota(I32, (1, sw), 1)).astype(F32)
        done = rowend_w <= s
        row = jnp.sum(done.astype(F32), axis=0, keepdims=True)
        before = jnp.sum(jnp.where(done, tot_w, 0.0), axis=0, keepdims=True)
        onehot_t = (r_iota == row).astype(BF16)
        got = _dot(table, onehot_t)
        g_t = got[:LANES]
        lane = jnp.sum((g_t <= s - before).astype(F32), axis=0, keepdims=True)
        tok = (row * LANES + lane).astype(I32)
        p_row = got[LANES:2 * LANES] + got[2 * LANES:3 * LANES] + got[3 * LANES:]
        gate = jnp.sum(jnp.where(lane_iota == lane, p_row, 0.0), axis=0, keepdims=True)
        for k in range(groups):
            idx_ref[pl.ds(ts * groups + k, 1), :] = tok[:, k * LANES:(k + 1) * LANES]
            gate_ref[pl.ds(ts * groups + k, 1), :] = gate[:, k * LANES:(k + 1) * LANES]
        return carry

    lax.fori_loop(0, cap // sw, slot_tile, 0)


def _select(pt3, cap):
    n_e, rows, _ = pt3.shape
    ii = np.arange(LANES)
    upper = jnp.asarray(ii[:, None] <= ii[None, :], BF16)
    lowinc = jnp.asarray(ii[None, :] <= ii[:, None], BF16)
    eye = jnp.asarray(ii[None, :] == ii[:, None], BF16)
    rr = np.arange(rows)
    lower = jnp.asarray(rr[None, :] < rr[:, None], BF16)
    const = lambda shape: pl.BlockSpec(shape, lambda e: (0, 0))
    tok_spec = pl.BlockSpec((None, rows, LANES), lambda e: (e, 0, 0))
    slot_spec = pl.BlockSpec((None, cap // LANES, LANES), lambda e: (e, 0, 0))
    return pl.pallas_call(
        functools.partial(_select_body, cap=cap),
        grid=(n_e,),
        in_specs=[tok_spec, const((LANES, LANES)), const((rows, rows)), const((LANES, LANES)), const((LANES, LANES))],
        out_specs=[tok_spec, slot_spec, slot_spec, tok_spec],
        out_shape=[jax.ShapeDtypeStruct((n_e, rows, LANES), I32),
                   jax.ShapeDtypeStruct((n_e, cap // LANES, LANES), I32),
                   jax.ShapeDtypeStruct((n_e, cap // LANES, LANES), F32),
                   jax.ShapeDtypeStruct((n_e, rows, LANES), I32)],
        compiler_params=_cparams(("parallel",)),
        name="expert_select",
    )(pt3, upper, lower, lowinc, eye)


def _ffn_body(idx_ref, h_hbm, n2g_ref, gate_ref, wg_ref, wu_ref, wd_ref, ye_ref, xbuf, xb, sems, *,
              pairs_per_expert):
    s = xbuf.shape[1]
    step = pl.program_id(0) * pairs_per_expert + pl.program_id(1)
    n_steps = pl.num_programs(0) * pairs_per_expert

    def row_copy(tile, i, buf):
        tok = idx_ref[tile * s + i]
        return pltpu.make_async_copy(h_hbm.at[pl.ds(tok, 1), :], xbuf.at[buf, pl.ds(i, 1), :], sems.at[buf])

    def wait_rows(buf):
        pltpu.make_async_copy(h_hbm.at[pl.ds(0, s), :], xbuf.at[buf], sems.at[buf]).wait()

    def ffn(buf):
        x = xbuf[buf]
        xb[...] = (x * lax.rsqrt(jnp.mean(x * x, axis=-1, keepdims=True) + EPS) * n2g_ref[...]).astype(BF16)
        y = None
        for f0 in range(0, EXPERT_FF, FF_CHUNK):
            f1 = min(f0 + FF_CHUNK, EXPERT_FF)
            a = _dot(xb[...], wg_ref[:, f0:f1])
            b = _dot(xb[...], wu_ref[:, f0:f1])
            hid = ((a * jax.nn.sigmoid(a)) * b).astype(BF16)
            part = _dot(hid, wd_ref[f0:f1, :])
            y = part if y is None else y + part
        for k in range(s // LANES):
            row0 = buf * s + k * LANES
            gate = gate_ref[row0 // LANES:row0 // LANES + 1, :]
            col = jnp.transpose(jnp.broadcast_to(gate, (LANES, LANES)))[:, 0:1]
            yk = y[k * LANES:(k + 1) * LANES] * col
            ye_ref[row0:row0 + LANES, :] = _pack_pair(yk[:, :D_MODEL // 2], yk[:, D_MODEL // 2:])

    @pl.when(step == 0)
    def _():
        def one(i, carry):
            row_copy(0, i, 0).start()
            return carry
        lax.fori_loop(0, s, one, 0, unroll=8)

    for i in range(s):
        row_copy(2 * step + 1, i, 1).start()
    wait_rows(0)
    ffn(0)

    @pl.when(step + 1 < n_steps)
    def _():
        for i in range(s):
            row_copy(2 * step + 2, i, 0).start()

    wait_rows(1)
    ffn(1)


def _expert_ffn(idx_flat, x1, n2g, gates, wg, wu, wd, cap, s=512):
    s = min(s, cap // 2)
    assert s % LANES == 0 and cap % (2 * s) == 0
    pairs = cap // (2 * s)
    g3 = gates.reshape(N_EXPERTS * pairs, 2 * s // LANES, LANES)
    grid_spec = pltpu.PrefetchScalarGridSpec(
        num_scalar_prefetch=1,
        grid=(N_EXPERTS, pairs),
        in_specs=[pl.BlockSpec(memory_space=pl.ANY),
                  pl.BlockSpec((1, D_MODEL), lambda e, j, idx: (0, 0)),
                  pl.BlockSpec((None, 2 * s // LANES, LANES), lambda e, j, idx: (e * pairs + j, 0, 0)),
                  pl.BlockSpec((None, D_MODEL, EXPERT_FF), lambda e, j, idx: (e, 0, 0)),
                  pl.BlockSpec((None, D_MODEL, EXPERT_FF), lambda e, j, idx: (e, 0, 0)),
                  pl.BlockSpec((None, EXPERT_FF, D_MODEL), lambda e, j, idx: (e, 0, 0))],
        out_specs=pl.BlockSpec((2 * s, D_MODEL // 2), lambda e, j, idx: (e * pairs + j, 0)),
        scratch_shapes=[pltpu.VMEM((2, s, D_MODEL), F32), pltpu.VMEM((s, D_MODEL), BF16),
                        pltpu.SemaphoreType.DMA((2,))],
    )
    return pl.pallas_call(
        functools.partial(_ffn_body, pairs_per_expert=pairs),
        grid_spec=grid_spec,
        out_shape=jax.ShapeDtypeStruct((N_EXPERTS * cap, D_MODEL // 2), F32),
        compiler_params=_cparams(("arbitrary", "arbitrary")),
        name="expert_ffn",
    )(idx_flat, x1, n2g, g3, wg, wu, wd)


def _combine_body(lo_ref, np_ref, x1_ref, pos_ref, nfg_ref, expand_ref, lov_ref, ye_hbm, o_ref, win, sems, *, cap,
                  rows_total, n_rows):
    step = pl.program_id(0)
    n_steps = pl.num_programs(0)
    groups = x1_ref.shape[0] // LANES
    w = COMBINE_WIN
    cur = step % 2

    def copies(rr, m, buf, g):
        out = []
        for e in range(N_EXPERTS):
            first = e * cap + lo_ref[e * n_rows + rr]
            intended = (first // COMBINE_ALIGN) * COMBINE_ALIGN + m * w
            actual = pl.multiple_of(jnp.minimum(intended, rows_total - w), COMBINE_ALIGN)
            out.append(pltpu.make_async_copy(ye_hbm.at[pl.ds(actual, w), :], win.at[buf, g, pl.ds(e * w, w), :],
                                             sems.at[buf, g, e]))
        return out

    def wait_windows(buf, g):
        for e in range(N_EXPERTS):
            pltpu.make_async_copy(ye_hbm.at[pl.ds(0, w), :], win.at[buf, g, pl.ds(e * w, w), :],
                                  sems.at[buf, g, e]).wait()

    def contribution(rr, m, buf, g):
        lane = lax.broadcasted_iota(I32, (1, N_EXPERTS), 1)
        first = lane * cap + lov_ref[pl.ds(rr, 1), :]
        intended = (first & jnp.int32(-COMBINE_ALIGN)) + m * w
        actual = jnp.minimum(intended, rows_total - w)
        pos = pos_ref[g * LANES:(g + 1) * LANES, :]
        glob = pos + lane * cap
        rel = glob - intended
        valid = (pos >= 0) & (rel >= 0) & (rel < w)
        local = jnp.where(valid, glob - actual, -1).astype(F32).astype(BF16)
        spread = _dot(local, expand_ref[...])
        col = (lax.broadcasted_iota(I32, (1, N_EXPERTS * w), 1) & (w - 1)).astype(F32)
        onehot = (spread == col).astype(BF16)
        lo_cols, hi_cols = _unpack_pair(win[buf, g])
        return jnp.concatenate([_dot(onehot, lo_cols.astype(BF16)), _dot(onehot, hi_cols.astype(BF16))], axis=1)

    @pl.when(step == 0)
    def _():
        for g in range(groups):
            for cp in copies(g, 0, 0, g):
                cp.start()

    @pl.when(step + 1 < n_steps)
    def _():
        for g in range(groups):
            for cp in copies((step + 1) * groups + g, 0, 1 - cur, g):
                cp.start()

    for g in range(groups):
        rr = step * groups + g
        wait_windows(cur, g)
        acc = x1_ref[g * LANES:(g + 1) * LANES, :] + contribution(rr, 0, cur, g)

        def extra_pass(m, acc, rr=rr, g=g):
            for cp in copies(rr, m, cur, g):
                cp.start()
            wait_windows(cur, g)
            return acc + contribution(rr, m, cur, g)

        acc = lax.fori_loop(1, np_ref[rr], extra_pass, acc)
        o_ref[g * LANES:(g + 1) * LANES, :] = (
            acc * lax.rsqrt(jnp.mean(acc * acc, axis=-1, keepdims=True) + EPS) * nfg_ref[...])


def _combine(lo, x1, pos_t, nfg, ye, cap):
    n_tok = x1.shape[0]
    w = COMBINE_WIN
    rows_total = ye.shape[0]
    n_rows = n_tok // LANES
    groups = next(c for c in (8, 4, 2, 1) if n_rows % c == 0)
    tm = groups * LANES
    nxt =jnp.concatenate([lo[:, 1:], jnp.full((N_EXPERTS, 1), cap, I32)], axis=1)
    span = lo % COMBINE_ALIGN + (nxt - lo)
    n_pass = jnp.maximum(jnp.max((span + w - 1) // w, axis=0), 1).astype(I32)
    ee = np.arange(N_EXPERTS)
    expand = jnp.asarray(ee[:, None] == (np.arange(N_EXPERTS * w) // w)[None, :], BF16)
    grid_spec = pltpu.PrefetchScalarGridSpec(
        num_scalar_prefetch=2,
        grid=(n_rows // groups,),
        in_specs=[pl.BlockSpec((tm, D_MODEL), lambda i, lo_, np_: (i, 0)),
                  pl.BlockSpec((tm, N_EXPERTS), lambda i, lo_, np_: (i, 0)),
                  pl.BlockSpec((1, D_MODEL), lambda i, lo_, np_: (0, 0)),
                  pl.BlockSpec((N_EXPERTS, N_EXPERTS * w), lambda i, lo_, np_: (0, 0)),
                  pl.BlockSpec((n_rows, N_EXPERTS), lambda i, lo_, np_: (0, 0)),
                  pl.BlockSpec(memory_space=pl.ANY)],
        out_specs=pl.BlockSpec((tm, D_MODEL), lambda i, lo_, np_: (i, 0)),
        scratch_shapes=[pltpu.VMEM((2, groups, N_EXPERTS * w, D_MODEL // 2), F32),
                        pltpu.SemaphoreType.DMA((2, groups, N_EXPERTS))],
    )
    return pl.pallas_call(
        functools.partial(_combine_body, cap=cap, rows_total=rows_total, n_rows=n_rows),
        grid_spec=grid_spec,
        out_shape=jax.ShapeDtypeStruct((n_tok, D_MODEL), F32),
        compiler_params=_cparams(("arbitrary",)),
        name="moe_combine_norm",
    )(lo.reshape(-1), n_pass, x1, pos_t, nfg, expand, lo.T, ye)


def _trunk(x, w):
    bsz, seq, d = x.shape
    n_tok = bsz * seq
    x2 = x.reshape(n_tok, d)
    tabs = _fft_tables(seq)

    zc, zret, zg = _inproj(x2, w["norm1_g"], w["w_in"], w["hy_conv_w"], w["hy_conv_b"], seq)
    zc = zc.reshape(bsz, seq, -1)

    h_time, h_abs = _hyena_filter_time(seq, w["hy_w1"], w["hy_b1"], w["hy_freq1"], w["hy_w2"], w["hy_b2"],
                                       w["hy_freq2"], w["hy_w3"])
    kr, ki = _stage_b_filter(_stage_a_real(h_time, tabs), h_abs, tabs)
    yhy = _hyena(zc, kr, ki, w["hy_skip"], tabs)

    o_f, o_b = _retention(zret.reshape(bsz, seq, -1), w["ret_decay_logit"])
    x1, pt = _outproj(x2, yhy, o_f.reshape(n_tok, -1), o_b.reshape(n_tok, -1), zret, zg,
                      w["w_hy_out"], w["w_ret_out"], w["w_o"], w["norm2_g"], w["w_router_t"])

    cap = CAPACITY_FACTOR * n_tok // N_EXPERTS
    rows = n_tok // LANES
    pos, idx, gates, lo = _select(pt.reshape(N_EXPERTS, rows, LANES), cap)
    ye = _expert_ffn(idx.reshape(-1), x1, w["norm2_g"], gates, w["w_gate"], w["w_up"], w["w_down"], cap)
    pos_t = pos.reshape(N_EXPERTS, n_tok).T
    y = _combine(lo[:, :, 0], x1, pos_t, w["norm_f_g"], ye, cap)
    return y.reshape(bsz, seq, d)


def kernel(x_prompt, x_sample, norm1_g, w_in, hy_conv_w, hy_conv_b, hy_w1, hy_b1, hy_freq1, hy_w2, hy_b2, hy_freq2,
           hy_w3, hy_skip, ret_decay_logit, w_hy_out, w_ret_out, w_o, norm2_g, w_router, w_gate, w_up, w_down,
           norm_f_g):
    layer = 0
    w = dict(
        norm1_g=norm1_g[layer].astype(F32)[None], w_in=w_in[layer].astype(BF16),
        hy_conv_w=hy_conv_w[layer].astype(F32), hy_conv_b=hy_conv_b[layer].astype(F32)[None],
        hy_w1=hy_w1[layer], hy_b1=hy_b1[layer], hy_freq1=hy_freq1[layer], hy_w2=hy_w2[layer], hy_b2=hy_b2[layer],
        hy_freq2=hy_freq2[layer], hy_w3=hy_w3[layer], hy_skip=hy_skip[layer],
        ret_decay_logit=ret_decay_logit[layer],
        w_hy_out=w_hy_out[layer].astype(BF16), w_ret_out=w_ret_out[layer].astype(BF16), w_o=w_o[layer].astype(BF16),
        norm2_g=norm2_g[layer].astype(F32)[None], w_router_t=w_router[layer].T.astype(BF16),
        w_gate=w_gate[layer].astype(BF16), w_up=w_up[layer].astype(BF16), w_down=w_down[layer].astype(BF16),
        norm_f_g=norm_f_g.astype(F32)[None],
    )
    return _trunk(x_prompt, w), _trunk(x_sample, w)
```

```python
import functools
import math

import numpy as np
import jax
import jax.numpy as jnp
from jax import lax
from jax.experimental import pallas as pl
from jax.experimental.pallas import tpu as pltpu

F32 = jnp.float32
BF16 = jnp.bfloat16
I32 = jnp.int32

D_MODEL = 1024
HY_WIDTH = 512
HY_ORDER = 2
HY_BANDS = 16
HY_FILTER_WIDTH = 64
HY_FAST_DECAY = 0.3
HY_SLOW_DECAY = 1.5
HY_DECAY_TARGET = 1e-2
RET_WIDTH = 512
RET_HEADS = 4
RET_HEAD_DIM = 128
RET_CHUNK = 128
ROPE_BASE = 10000.0
N_EXPERTS = 16
EXPERT_FF = 1408
CAPACITY_FACTOR = 2
EPS = 1e-6

VMEM_LIMIT_MIB = 48
LANES = 128
SUBLANES = 8
EMB_PAD = 128
DFT_N1_MAX = 128
FF_CHUNK = 256
COMBINE_ALIGN = SUBLANES
COMBINE_WIN = 32


def _cparams(sem):
    return pltpu.CompilerParams(dimension_semantics=sem, vmem_limit_bytes=VMEM_LIMIT_MIB * 1024 * 1024)


def _dot(a, b):
    return jnp.dot(a, b, preferred_element_type=F32)


def _dot_nt(a, b):
    return lax.dot_general(a, b, (((1,), (1,)), ((), ())), preferred_element_type=F32)


def _dot_tn(a, b):
    return lax.dot_general(a, b, (((0,), (0,)), ((), ())), preferred_element_type=F32)


def _split(a):
    hi = a.astype(BF16)
    lo = (a - hi.astype(F32)).astype(BF16)
    return hi, lo


def _dot3(a, b):
    ah, al = _split(a)
    bh, bl = _split(b)
    return _dot(ah, bh) + _dot(al, bh) + _dot(ah, bl)


def _inproj_body(x_ref, g_ref, w_ref, cw_ref, cb_ref, zc_ref, zret_ref, zg_ref, z_prev, row_prev, *, tiles_per_seq):
    i = pl.program_id(0)
    tm = x_ref.shape[0]
    n_hy = zc_ref.shape[1]
    n_ret = zret_ref.shape[1]

    @pl.when(i == 0)
    def _():
        z_prev[...] = jnp.zeros_like(z_prev)
        row_prev[...] = jnp.zeros_like(row_prev)

    x = x_ref[...]
    ms = jnp.mean(x * x, axis=-1, keepdims=True)
    h = (x * lax.rsqrt(ms + EPS) * g_ref[...]).astype(BF16)
    z_hy = _dot(h, w_ref[:, :n_hy])
    zret_ref[...] = _dot(h, w_ref[:, n_hy:n_hy + n_ret]).astype(BF16)
    zg_ref[...] = _dot(h, w_ref[:, n_hy + n_ret:]).astype(BF16)

    t_prev = (i - 1) % tiles_per_seq
    zp = z_prev[...]
    before = jnp.where(t_prev == 0, 0.0, row_prev[0:1, :])
    after = jnp.where(t_prev == tiles_per_seq - 1, 0.0, z_hy[0:1, :])
    row = lax.broadcasted_iota(I32, (tm, 1), 0)
    zm1 = jnp.where(row == 0, before, pltpu.roll(zp, 1, axis=0))
    zp1 = jnp.where(row == tm - 1, after, pltpu.roll(zp, tm - 1, axis=0))
    zc_ref[...] = zm1 * cw_ref[0:1, :] + zp * cw_ref[1:2, :] + zp1 * cw_ref[2:3, :] + cb_ref[...]
    row_prev[...] = jnp.broadcast_to(zp[tm - 1:tm, :], row_prev.shape)
    z_prev[...] = z_hy


def _inproj(x2, g, w_bf, conv_w, conv_b, seq, tm=512):
    n_tok = x2.shape[0]
    n_hy, n_ret, n_g = 3 * HY_WIDTH, 4 * RET_WIDTH, 2 * D_MODEL
    tm = min(tm, seq)
    n_tiles = n_tok // tm
    cur = lambda i: (jnp.minimum(i, n_tiles - 1), 0)
    const = lambda i: (0, 0)
    return pl.pallas_call(
        functools.partial(_inproj_body, tiles_per_seq=seq // tm),
        grid=(n_tiles + 1,),
        in_specs=[pl.BlockSpec((tm, D_MODEL), cur),
                  pl.BlockSpec((1, D_MODEL), const),
                  pl.BlockSpec((D_MODEL, n_hy + n_ret + n_g), const, pipeline_mode=pl.Buffered(1)),
                  pl.BlockSpec((3, n_hy), const),
                  pl.BlockSpec((1, n_hy), const)],
        out_specs=[pl.BlockSpec((tm, n_hy), lambda i: (jnp.maximum(i - 1, 0), 0)),
                   pl.BlockSpec((tm, n_ret), cur),
                   pl.BlockSpec((tm, n_g), cur)],
        out_shape=[jax.ShapeDtypeStruct((n_tok, n_hy), F32),
                   jax.ShapeDtypeStruct((n_tok, n_ret), BF16),
                   jax.ShapeDtypeStruct((n_tok, n_g), BF16)],
        scratch_shapes=[pltpu.VMEM((tm, n_hy), F32), pltpu.VMEM((8, n_hy), F32)],
        compiler_params=_cparams(("arbitrary",)),
        name="inproj_shortconv",
    )(x2, g, w_bf, conv_w, conv_b)


def _fft_dims(seq):
    n = 2 * seq
    n1 = min(DFT_N1_MAX, 1 << (n.bit_length() // 2))
    n2 = n // n1
    assert n1 * n2 == n and n1 % 2 == 0
    return n, n1, n2


def _fft_tables(seq):
    n, n1, n2 = _fft_dims(seq)
    r1 = n1 // 2
    k1 = jnp.arange(n1, dtype=I32)
    j = jnp.arange(n2, dtype=I32)

    at = ((j[:, None] * k1[None, :]) % n).astype(F32) * (2.0 * math.pi / n)
    ct, st = jnp.cos(at), jnp.sin(at)

    def cos_sin(n1_count, k1_first):
        a1 = ((k1[:, None] * jnp.arange(n1_count, dtype=I32)[None, :]) % n1).astype(F32) * (2.0 * math.pi / n1)
        cf, sf = jnp.cos(a1), jnp.sin(a1)
        if k1_first:
            cf, sf, ctb, stb = cf[None], sf[None], ct[:, :, None], st[:, :, None]
        else:
            cf, sf, ctb, stb = cf.T[None], sf.T[None], ct[:, None, :], st[:, None, :]
        return cf * ctb - sf * stb, sf * ctb + cf * stb

    cat = jnp.concatenate
    c, s = cos_sin(r1, True)
    mat_a = cat([cat([c, s], axis=2), cat([-s, c], axis=2)], axis=1).astype(BF16)
    c, s = cos_sin(n1, True)
    mat_a_real = cat([c, -s], axis=1).astype(BF16)
    c, s = cos_sin(r1, False)
    mat_c = cat([cat([c, -s], axis=2), cat([s, c], axis=2)], axis=1).astype(BF16)
    a2 = 2.0 * np.pi * np.outer(np.arange(n2), np.arange(n2)) / n2
    gr, gi = np.cos(a2), -np.sin(a2)
    mat_f = np.block([[gr, -gi], [gi, gr]])
    mat_i = np.block([[gr, gi], [-gi, gr]])
    return dict(n=n, n1=n1, n2=n2, mat_a=mat_a, mat_a_real=mat_a_real, mat_c=mat_c,
                mat_f=jnp.asarray(mat_f, F32).astype(BF16), mat_i=jnp.asarray(mat_i, F32).astype(BF16))


def _filter_body(z_ref, zt_ref, w1t_ref, b1_ref, f1_ref, w2t_ref, b2_ref, f2_ref, w3_ref, dl_ref, h_ref, s_ref, *,
                 seq):
    i = pl.program_id(0)
    tl = z_ref.shape[0]
    a = jnp.sin(f1_ref[...] * (_dot3(w1t_ref[...], zt_ref[...]) + b1_ref[...]))
    a = jnp.sin(f2_ref[...] * (_dot3(w2t_ref[...], a) + b2_ref[...]))
    h = _dot_tn(a.astype(BF16), w3_ref[...].astype(BF16))
    win = jnp.exp(-z_ref[:, 0:1] * dl_ref[...])
    h = h * jnp.concatenate([win] * HY_ORDER, axis=1)
    row = i * tl + lax.broadcasted_iota(I32, (tl, 1), 0)
    h = jnp.where(row == seq, 0.0, h)
    h_ref[...] = h

    @pl.when(i == 0)
    def _():
        s_ref[...] = jnp.zeros_like(s_ref)

    s_ref[...] += jnp.broadcast_to(jnp.sum(jnp.abs(h), axis=0, keepdims=True), s_ref.shape)


def _hyena_filter_time(seq, w1, b1, f1, w2, b2, f2, w3, tl=512):
    n = 2 * seq
    tl = min(tl, seq)
    idx = jnp.arange(n, dtype=I32)
    p = jnp.minimum(jnp.where(idx < seq, idx, n - idx), seq - 1).astype(F32)
    t = p / float(seq - 1)
    ang = 2.0 * math.pi * p / seq
    bands = jnp.linspace(1e-4, HY_BANDS - 1, HY_BANDS, dtype=F32)
    phase = ang[:, None] * bands[None, :]
    emb = jnp.concatenate([t[:, None], jnp.cos(phase), -jnp.sin(phase)], axis=-1)
    emb = jnp.pad(emb, ((0, 0), (0, EMB_PAD - emb.shape[1])))
    w1p = jnp.pad(w1.astype(F32), ((0, EMB_PAD - w1.shape[0]), (0, 0)))
    w3d = w3.astype(F32).reshape(HY_FILTER_WIDTH, HY_ORDER, 2, HY_WIDTH).transpose(2, 0, 1, 3)
    w3d = w3d.reshape(2, HY_FILTER_WIDTH, HY_ORDER * HY_WIDTH)
    deltas = jnp.abs(jnp.linspace(math.log(HY_DECAY_TARGET) / HY_SLOW_DECAY,
                                  math.log(HY_DECAY_TARGET) / HY_FAST_DECAY, HY_WIDTH, dtype=F32))[None, :]
    fw = HY_FILTER_WIDTH
    nblk_half = seq // tl
    const = lambda i: (0, 0)
    col = lambda v: v.astype(F32)[:, None]
    return pl.pallas_call(
        functools.partial(_filter_body, seq=seq),
        grid=(n // tl,),
        in_specs=[pl.BlockSpec((tl, EMB_PAD), lambda i: (i, 0)),
                  pl.BlockSpec((EMB_PAD, tl), lambda i: (0, i)),
                  pl.BlockSpec((fw, EMB_PAD), const), pl.BlockSpec((fw, 1), const), pl.BlockSpec((fw, 1), const),
                  pl.BlockSpec((fw, fw), const), pl.BlockSpec((fw, 1), const), pl.BlockSpec((fw, 1), const),
                  pl.BlockSpec((None, fw, HY_ORDER * HY_WIDTH), lambda i: (i // nblk_half, 0, 0)),
                  pl.BlockSpec((1, HY_WIDTH), const)],
        out_specs=[pl.BlockSpec((tl, HY_ORDER * HY_WIDTH), lambda i: (i, 0)),
                   pl.BlockSpec((8, HY_ORDER * HY_WIDTH), const)],
        out_shape=[jax.ShapeDtypeStruct((n, HY_ORDER * HY_WIDTH), F32),
                   jax.ShapeDtypeStruct((8, HY_ORDER * HY_WIDTH), F32)],
        compiler_params=_cparams(("arbitrary",)),
        name="hyena_filter",
    )(emb, emb.T, w1p.T, col(b1), col(f1), w2.astype(F32).T, col(b2), col(f2), w3d, deltas)


def _stage_a_real_body(h_ref, m_ref, a_ref):
    n1 = h_ref.shape[0]
    h_t = jnp.swapaxes(h_ref[...], 0, 1)
    packed = []
    for j in range(h_ref.shape[1]):
        r = _dot(m_ref[j], h_t[j].astype(BF16))
        packed.append(_pack_pair(r[:n1], r[n1:]))
    a_ref[...] = jnp.swapaxes(jnp.stack(packed, axis=0), 0, 1)


def _stage_a_real(h, tabs, n2c=8, cw=512):
    n1, n2 = tabs["n1"], tabs["n2"]
    c = h.shape[1]
    h3 = h.reshape(n1, n2, c)
    mat = tabs["mat_a_real"]
    return pl.pallas_call(
        _stage_a_real_body,
        grid=(n2 // n2c, c // cw),
        in_specs=[pl.BlockSpec((n1, n2c, cw), lambda j, k: (0, j, k)),
                  pl.BlockSpec((n2c,) + mat.shape[1:], lambda j, k: (j, 0, 0))],
        out_specs=pl.BlockSpec((n1, n2c, cw), lambda j, k: (0, j, k)),
        out_shape=jax.ShapeDtypeStruct((n1, n2, c), F32),
        compiler_params=_cparams(("parallel", "parallel")),
        name="dft_a_filter",
    )(h3, mat)


def _pack_pair(re, im):
    hi = lax.bitcast_convert_type(re.astype(BF16).astype(F32), I32)
    lo = lax.bitcast_convert_type(im.astype(BF16).astype(F32), I32)
    return lax.bitcast_convert_type(hi | lax.shift_right_logical(lo, 16), F32)


def _unpack_pair(packed):
    word = lax.bitcast_convert_type(packed, I32)
    re = lax.bitcast_convert_type(word & jnp.int32(-65536), F32)
    im = lax.bitcast_convert_type(lax.shift_left(word, 16), F32)
    return re, im


def _stage_a_body(u_ref, m_ref, a_ref):
    _, r1, n2c, cw = u_ref.shape
    n1 = 2 * r1
    u_t = jnp.swapaxes(u_ref[...].reshape(n1, n2c, cw), 0, 1)
    packed = []
    for j in range(n2c):
        r = _dot(m_ref[j], u_t[j].astype(BF16))
        packed.append(_pack_pair(r[:n1], r[n1:]))
    a_ref[...] = jnp.swapaxes(jnp.stack(packed, axis=0), 0, 1)


def _stage_a(u5, col, tabs, n2c=8):
    n1, n2 = tabs["n1"], tabs["n2"]
    npair, _, r1, _, _ = u5.shape
    cw = HY_WIDTH
    mat = tabs["mat_a"]
    return pl.pallas_call(
        _stage_a_body,
        grid=(npair, n2 // n2c),
        in_specs=[pl.BlockSpec((None, 2, r1, n2c, cw), lambda p, j: (p, 0, 0, j, col)),
                  pl.BlockSpec((n2c,) + mat.shape[1:], lambda p, j: (j, 0, 0))],
        out_specs=pl.BlockSpec((None, n1, n2c, cw), lambda p, j: (p, 0, j, 0)),
        out_shape=jax.ShapeDtypeStruct((npair, n1, n2, cw), F32),
        compiler_params=_cparams(("parallel", "parallel")),
        name="dft_a",
    )(u5, mat)


def _stage_b_filter_body(a_ref, g_ref, s_ref, kr_ref, ki_ref, *, n):
    n2 = a_ref.shape[1]
    scale = 1.0 / (s_ref[0:1, :] * float(n))
    for kk in range(a_ref.shape[0]):
        d = _dot(g_ref[...], jnp.concatenate(_unpack_pair(a_ref[kk]), axis=0).astype(BF16))
        kr_ref[kk] = d[:n2] * scale
        ki_ref[kk] = d[n2:] * scale


def _stage_b_filter(a, s, tabs):
    n, n1, n2 = tabs["n"], tabs["n1"], tabs["n2"]
    c = a.shape[-1]
    kb = max(1, min(n1, 512 // n2))
    g = tabs["mat_f"]
    return pl.pallas_call(
        functools.partial(_stage_b_filter_body, n=n),
        grid=(n1 // kb,),
        in_specs=[pl.BlockSpec((kb, n2, c), lambda k: (k, 0, 0)),
                  pl.BlockSpec(g.shape, lambda k: (0, 0)),
                  pl.BlockSpec(s.shape, lambda k: (0, 0))],
        out_specs=[pl.BlockSpec((kb, n2, c), lambda k: (k, 0, 0)),
                   pl.BlockSpec((kb, n2, c), lambda k: (k, 0, 0))],
        out_shape=[jax.ShapeDtypeStruct((n1, n2, c), F32), jax.ShapeDtypeStruct((n1, n2, c), F32)],
        compiler_params=_cparams(("parallel",)),
        name="dft_b_filter",
    )(a, g, s)


def _stage_b_body(a_ref, kr_ref, ki_ref, gf_ref, gi_ref, c_ref):
    n2 = a_ref.shape[1]
    for kk in range(a_ref.shape[0]):
        d = _dot(gf_ref[...], jnp.concatenate(_unpack_pair(a_ref[kk]), axis=0).astype(BF16))
        dr, di = d[:n2], d[n2:]
        kr, ki = kr_ref[kk], ki_ref[kk]
        yr = dr * kr - di * ki
        yi = dr * ki + di * kr
        e = _dot(gi_ref[...], jnp.concatenate([yr, yi], axis=0).astype(BF16))
        c_ref[kk] = _pack_pair(e[:n2], e[n2:])


def _stage_b(a, kr, ki, order, tabs):
    n1, n2 = tabs["n1"], tabs["n2"]
    npair = a.shape[0]
    cw = HY_WIDTH
    kb = max(1, min(n1, 1024 // n2))
    gf = tabs["mat_f"]
    gi = tabs["mat_i"]
    const = lambda k, p: (0, 0)
    return pl.pallas_call(
        _stage_b_body,
        grid=(n1 // kb, npair),
        in_specs=[pl.BlockSpec((None, kb, n2, cw), lambda k, p: (p, k, 0, 0)),
                  pl.BlockSpec((kb, n2, cw), lambda k, p: (k, 0, order)),
                  pl.BlockSpec((kb, n2, cw), lambda k, p: (k, 0, order)),
                  pl.BlockSpec(gf.shape, const), pl.BlockSpec(gi.shape, const)],
        out_specs=pl.BlockSpec((None, kb, n2, cw), lambda k, p: (p, k, 0, 0)),
        out_shape=jax.ShapeDtypeStruct(a.shape, F32),
        compiler_params=_cparams(("parallel", "parallel")),
        name="dft_b",
    )(a, kr, ki, gf, gi)


def _stage_c_body(c_ref, u_ref, g_ref, skip_ref, mc_ref, ma_ref, y_ref, *rest, fuse_a):
    _, r1, n2c, cw = u_ref.shape
    n1 = 2 * r1
    ys = []
    c_t = jnp.swapaxes(c_ref[...], 0, 1)
    for j in range(n2c):
        cc = jnp.concatenate(_unpack_pair(c_t[j]), axis=0)
        ys.append(_dot(mc_ref[j], cc.astype(BF16)))
    y = jnp.swapaxes(jnp.stack(ys, axis=0), 0, 1)
    u = u_ref[...].reshape(n1, n2c, cw)
    g = g_ref[...].reshape(n1, n2c, cw)
    yo = g * (y + u * skip_ref[...].reshape(1, 1, cw))
    y_ref[...] = yo.reshape(2, r1, n2c, cw).astype(y_ref.dtype)
    if fuse_a:
        a_ref = rest[0]
        yo_t = jnp.swapaxes(yo, 0, 1)
        packed = []
        for j in range(n2c):
            r = _dot(ma_ref[j], yo_t[j].astype(BF16))
            packed.append(_pack_pair(r[:n1], r[n1:]))
        a_ref[...] = jnp.swapaxes(jnp.stack(packed, axis=0), 0, 1)


def _stage_c(c, u5, u_col, g5, g_col, skip, tabs, fuse_a, out_dtype, n2c=8):
    n1, n2 = tabs["n1"], tabs["n2"]
    npair, _, r1, _, _ = u5.shape
    cw = HY_WIDTH
    mc = tabs["mat_c"]
    ma = tabs["mat_a"]
    const = lambda p, j: (0, 0)
    per_j = lambda m: pl.BlockSpec((n2c,) + m.shape[1:], lambda p, j: (j, 0, 0))
    out_shape = [jax.ShapeDtypeStruct((npair, 2, r1, n2, cw), out_dtype)]
    out_specs = [pl.BlockSpec((None, 2, r1, n2c, cw), lambda p, j: (p, 0, 0, j, 0))]
    if fuse_a:
        out_shape.append(jax.ShapeDtypeStruct((npair, n1, n2, cw), F32))
        out_specs.append(pl.BlockSpec((None, n1, n2c, cw), lambda p, j: (p, 0, j, 0)))
    return pl.pallas_call(
        functools.partial(_stage_c_body, fuse_a=fuse_a),
        grid=(npair, n2 // n2c),
        in_specs=[pl.BlockSpec((None, n1, n2c, cw), lambda p, j: (p, 0, j, 0)),
                  pl.BlockSpec((None, 2, r1, n2c, cw), lambda p, j: (p, 0, 0, j, u_col)),
                  pl.BlockSpec((None, 2, r1, n2c, cw), lambda p, j: (p, 0, 0, j, g_col)),
                  pl.BlockSpec((1, cw), const), per_j(mc), per_j(ma)],
        out_specs=out_specs,
        out_shape=out_shape,
        compiler_params=_cparams(("parallel", "parallel")),
        name="dft_c",
    )(c, u5, g5, skip, mc, ma)


def _hyena(zc3, kr, ki, skip, tabs):
    bsz, seq, _ = zc3.shape
    n1, n2 = tabs["n1"], tabs["n2"]
    r1 = n1 // 2
    z5 = zc3.reshape(bsz // 2, 2, r1, n2, 3 * HY_WIDTH)
    a = _stage_a(z5, 2, tabs)
    c = _stage_b(a, kr, ki, 0, tabs)
    y1, a2 = _stage_c(c, z5, 2, z5, 0, skip[0:1].astype(F32), tabs, True, F32)
    c2 = _stage_b(a2, kr, ki, 1, tabs)
    (y2,) = _stage_c(c2, y1, 0, z5, 1, skip[1:2].astype(F32), tabs, False, BF16)
    return y2.reshape(bsz * seq, HY_WIDTH)


def _retention_body(sc_ref, qf_ref, kf_ref, vf_ref, cf_ref, sf_ref, qb_ref, kb_ref, vb_ref, cb_ref, sb_ref,
                    of_ref, ob_ref, state, dmat, qwt, kwt):
    ch = RET_CHUNK
    dh = RET_HEAD_DIM

    @pl.when(pl.program_id(1) == 0)
    def _():
        state[...] = jnp.zeros_like(state)
        ci = lax.broadcasted_iota(I32, (ch, ch), 0).astype(F32)
        mi = lax.broadcasted_iota(I32, (ch, ch), 1).astype(F32)
        lag = ci - mi
        for d in range(2):
            for h in range(RET_HEADS):
                lg = sc_ref[d * RET_HEADS + h]
                if d == 0:
                    dmat[d, h] = jnp.where(lag >= 0, jnp.exp(lg * jnp.maximum(lag, 0.0)), 0.0)
                    qwt[d, h] = jnp.exp(lg * (ci + 1.0))
                    kwt[d, h] = jnp.exp(lg * (ch - 1.0 - ci))
                else:
                    dmat[d, h] = jnp.where(lag < 0, jnp.exp(lg * jnp.maximum(-lag, 0.0)), 0.0)
                    qwt[d, h] = jnp.exp(lg * (ch - ci))
                    kwt[d, h] = jnp.exp(lg * ci)

    scale = RET_HEAD_DIM ** -0.5
    refs = ((qf_ref, kf_ref, vf_ref, cf_ref, sf_ref, of_ref), (qb_ref, kb_ref, vb_ref, cb_ref, sb_ref, ob_ref))
    work = [(d, h) for d in range(2) for h in range(RET_HEADS)]
    per_step = qf_ref.shape[0] // ch
    for c in range(per_step):
        rows = (slice(c * ch, (c + 1) * ch), slice((per_step - 1 - c) * ch, (per_step - c) * ch))
        qs, ks, qws, kws, vs = {}, {}, {}, {}, {}
        for d, h in work:
            q_ref, k_ref, v_ref, c_ref, s_ref, _ = refs[d]
            sl = slice(h * dh, (h + 1) * dh)
            cosf = c_ref[rows[d], :]
            sinf = s_ref[rows[d], :]
            q = q_ref[rows[d], sl].astype(F32)
            k = k_ref[rows[d], sl].astype(F32)
            q = q * cosf + pltpu.roll(q, dh // 2, axis=1) * sinf
            k = (k * cosf + pltpu.roll(k, dh // 2, axis=1) * sinf) * scale
            qs[d, h], ks[d, h] = q.astype(BF16), k.astype(BF16)
            qws[d, h], kws[d, h] = (q * qwt[d, h]).astype(BF16), (k * kwt[d, h]).astype(BF16)
            vs[d, h] = v_ref[rows[d], sl]
        scores = {dh_: _dot_nt(qs[dh_], ks[dh_]) for dh_ in work}
        for d, h in work:
            lhs = jnp.concatenate([(scores[d, h] * dmat[d, h]).astype(BF16), qws[d, h]], axis=1)
            rhs = jnp.concatenate([vs[d, h], state[d, h].astype(BF16)], axis=0)
            refs[d][5][rows[d], h * dh:(h + 1) * dh] = _dot(lhs, rhs).astype(BF16)
        for d, h in work:
            cdec = sc_ref[2 * RET_HEADS + d * RET_HEADS + h]
            state[d, h] = state[d, h] * cdec + _dot_tn(kws[d, h], vs[d, h])


def _retention(zret3, decay_logit):
    bsz, seq, _ = zret3.shape
    ch = RET_CHUNK
    nc = seq // ch
    half = RET_HEAD_DIM // 2
    inv_freq = ROPE_BASE ** (-jnp.arange(half, dtype=F32) / half)
    ang = jnp.arange(seq, dtype=F32)[:, None] * inv_freq[None, :]
    cosf = jnp.concatenate([jnp.cos(ang), jnp.cos(ang)], axis=1)
    sinf = jnp.concatenate([-jnp.sin(ang), jnp.sin(ang)], axis=1)
    log_g = jax.nn.log_sigmoid(decay_logit.astype(F32)).reshape(-1)
    scal = jnp.concatenate([log_g, jnp.exp(log_g * ch)])
    w = RET_WIDTH
    per_step = next(c for c in (4, 2, 1) if nc % c == 0)
    nb = nc // per_step
    blk = per_step * ch
    fwd = lambda col: pl.BlockSpec((None, blk, w), lambda b, n, sc: (b, n, col))
    bwd = lambda col: pl.BlockSpec((None, blk, w), lambda b, n, sc: (b, nb - 1 - n, col))
    rope_f = pl.BlockSpec((blk, RET_HEAD_DIM), lambda b, n, sc: (n, 0))
    rope_b = pl.BlockSpec((blk, RET_HEAD_DIM), lambda b, n, sc: (nb - 1 - n, 0))
    grid_spec = pltpu.PrefetchScalarGridSpec(
        num_scalar_prefetch=1,
        grid=(bsz, nb),
        in_specs=[fwd(0), fwd(1), fwd(2), rope_f, rope_f, bwd(0), bwd(1), bwd(2), rope_b, rope_b],
        out_specs=[pl.BlockSpec((None, blk, w), lambda b, n, sc: (b, n, 0)),
                   pl.BlockSpec((None, blk, w), lambda b, n, sc: (b, nb - 1 - n, 0))],
        scratch_shapes=[pltpu.VMEM((2, RET_HEADS, RET_HEAD_DIM, RET_HEAD_DIM), F32),
                        pltpu.VMEM((2, RET_HEADS, ch, ch), F32),
                        pltpu.VMEM((2, RET_HEADS, ch, RET_HEAD_DIM), F32),
                        pltpu.VMEM((2, RET_HEADS, ch, RET_HEAD_DIM), F32)],
    )
    return pl.pallas_call(
        _retention_body,
        grid_spec=grid_spec,
        out_shape=[jax.ShapeDtypeStruct((bsz, seq, w), BF16), jax.ShapeDtypeStruct((bsz, seq, w), BF16)],
        compiler_params=_cparams(("arbitrary", "arbitrary")),
        name="retention",
    )(scal, zret3, zret3, zret3, cosf, sinf, zret3, zret3, zret3, cosf, sinf)


def _outproj_body(x_ref, yhy_ref, of_ref, ob_ref, gr_ref, ghy_ref, gret_ref, whyo_ref, wreto_ref, wo_ref,
                  n2g_ref, wrt_ref, x1_ref, pt_ref):
    dh = RET_HEAD_DIM
    o = of_ref[...].astype(F32) + ob_ref[...].astype(F32)
    parts = []
    for h in range(RET_HEADS):
        oh = o[:, h * dh:(h + 1) * dh]
        parts.append(oh * lax.rsqrt(jnp.mean(oh * oh, axis=-1, keepdims=True) + EPS))
    on = jnp.concatenate(parts, axis=1)
    gr = gr_ref[...].astype(F32)
    ret = (gr * jax.nn.sigmoid(gr)) * on
    y_ret = _dot(ret.astype(BF16), wreto_ref[...])
    y_hy = _dot(yhy_ref[...], whyo_ref[...])
    merged = jax.nn.sigmoid(ghy_ref[...].astype(F32)) * y_hy + jax.nn.sigmoid(gret_ref[...].astype(F32)) * y_ret
    x1 = x_ref[...] + _dot(merged.astype(BF16), wo_ref[...])
    x1_ref[...] = x1
    h2 = x1 * lax.rsqrt(jnp.mean(x1 * x1, axis=-1, keepdims=True) + EPS) * n2g_ref[...]
    logits = _dot_nt(wrt_ref[...], h2.astype(BF16))
    m = jnp.max(logits, axis=0, keepdims=True)
    e = jnp.exp(logits - m)
    pt_ref[...] = e / jnp.sum(e, axis=0, keepdims=True)


def _outproj(x2, yhy, o_f, o_b, zret, zg, whyo, wreto, wo, n2g, wrt, tm=512):
    n_tok = x2.shape[0]
    d = D_MODEL
    row = lambda w, col=0: pl.BlockSpec((tm, w), lambda i: (i, col))
    const = lambda shape: pl.BlockSpec(shape, lambda i: (0, 0))
    return pl.pallas_call(
        _outproj_body,
        grid=(n_tok // tm,),
        in_specs=[row(d), row(HY_WIDTH), row(RET_WIDTH), row(RET_WIDTH), row(RET_WIDTH, 3), row(d, 0), row(d, 1),
                  const((HY_WIDTH, d)), const((RET_WIDTH, d)), const((d, d)), const((1, d)), const((N_EXPERTS, d))],
        out_specs=[row(d), pl.BlockSpec((N_EXPERTS, tm), lambda i: (0, i))],
        out_shape=[jax.ShapeDtypeStruct((n_tok, d), F32), jax.ShapeDtypeStruct((N_EXPERTS, n_tok), F32)],
        compiler_params=_cparams(("parallel",)),
        name="outproj_router",
    )(x2, yhy, o_f, o_b, zret, zg, zg, whyo, wreto, wo, n2g, wrt)


def _select_body(p_ref, upper_ref, lower_ref, lowinc_ref, eye_ref, pos_ref, idx_ref, gate_ref, lo_ref, *, cap):
    rows = p_ref.shape[0]
    p = p_ref[...]
    bits = lax.bitcast_convert_type(p, I32)

    def count(mask):
        return jnp.sum(jnp.sum(mask.astype(F32), axis=1, keepdims=True), axis=0, keepdims=True)

    def bit_step(i, thr):
        cand = thr | jnp.left_shift(jnp.int32(1), 30 - i)
        return jnp.where(count(bits >= cand) >= cap, cand, thr)

    thr = lax.fori_loop(0, 31, bit_step, jnp.zeros((1, 1), I32))
    gt = bits > thr
    eq = bits == thr
    need = cap - count(gt)

    def prefix(mask_f):
        incl = _dot(mask_f.astype(BF16), upper_ref[...])
        tot = jnp.broadcast_to(incl[:, LANES - 1:LANES], incl.shape)
        base = _dot(lower_ref[...], tot.astype(BF16))
        return incl, base, tot

    eq_f = eq.astype(F32)
    incl_e, base_e, _ = prefix(eq_f)
    sel = gt | (eq & (base_e + incl_e - eq_f < need))
    sel_f = sel.astype(F32)
    incl, base, tot = prefix(sel_f)
    pos_ref[...] = jnp.where(sel, (base + incl - 1.0).astype(I32), -1)
    lo_ref[...] = base.astype(I32)

    rowend = base + tot
    incl_t = _dot_nt(lowinc_ref[...], sel_f.astype(BF16)).astype(BF16)
    p_t = []
    rem = p
    for _ in range(3):
        part = rem.astype(BF16)
        rem = rem - part.astype(F32)
        p_t.append(_dot_nt(eye_ref[...], part).astype(BF16))
    table = jnp.concatenate([incl_t] + p_t, axis=0)
    groups = 2 if (cap // LANES) % 2 == 0 else 1
    sw = groups * LANES
    wide = lambda a: jnp.concatenate([a] * groups, axis=1)
    rowend_w, tot_w = wide(rowend), wide(tot)
    r_iota = lax.broadcasted_iota(I32, (rows, sw), 0).astype(F32)
    lane_iota = lax.broadcasted_iota(I32, (LANES, sw), 0).astype(F32)

    def slot_tile(ts, carry):
        s = (ts * sw + lax.broadcasted_iota(I32, (1, sw), 1)).astype(F32)
        done = rowend_w <= s
        row = jnp.sum(done.astype(F32), axis=0, keepdims=True)
        before = jnp.sum(jnp.where(done, tot_w, 0.0), axis=0, keepdims=True)
        onehot_t = (r_iota == row).astype(BF16)
        got = _dot(table, onehot_t)
        g_t = got[:LANES]
        lane = jnp.sum((g_t <= s - before).astype(F32), axis=0, keepdims=True)
        tok = (row * LANES + lane).astype(I32)
        p_row = got[LANES:2 * LANES] + got[2 * LANES:3 * LANES] + got[3 * LANES:]
        gate = jnp.sum(jnp.where(lane_iota == lane, p_row, 0.0), axis=0, keepdims=True)
        for k in range(groups):
            idx_ref[pl.ds(ts * groups + k, 1), :] = tok[:, k * LANES:(k + 1) * LANES]
            gate_ref[pl.ds(ts * groups + k, 1), :] = gate[:, k * LANES:(k + 1) * LANES]
        return carry

    lax.fori_loop(0, cap // sw, slot_tile, 0)


def _select(pt3, cap):
    n_e, rows, _ = pt3.shape
    ii = np.arange(LANES)
    upper = jnp.asarray(ii[:, None] <= ii[None, :], BF16)
    lowinc = jnp.asarray(ii[None, :] <= ii[:, None], BF16)
    eye = jnp.asarray(ii[None, :] == ii[:, None], BF16)
    rr = np.arange(rows)
    lower = jnp.asarray(rr[None, :] < rr[:, None], BF16)
    const = lambda shape: pl.BlockSpec(shape, lambda e: (0, 0))
    tok_spec = pl.BlockSpec((None, rows, LANES), lambda e: (e, 0, 0))
    slot_spec = pl.BlockSpec((None, cap // LANES, LANES), lambda e: (e, 0, 0))
    return pl.pallas_call(
        functools.partial(_select_body, cap=cap),
        grid=(n_e,),
        in_specs=[tok_spec, const((LANES, LANES)), const((rows, rows)), const((LANES, LANES)), const((LANES, LANES))],
        out_specs=[tok_spec, slot_spec, slot_spec, tok_spec],
        out_shape=[jax.ShapeDtypeStruct((n_e, rows, LANES), I32),
                   jax.ShapeDtypeStruct((n_e, cap // LANES, LANES), I32),
                   jax.ShapeDtypeStruct((n_e, cap // LANES, LANES), F32),
                   jax.ShapeDtypeStruct((n_e, rows, LANES), I32)],
        compiler_params=_cparams(("parallel",)),
        name="expert_select",
    )(pt3, upper, lower, lowinc, eye)


def _ffn_body(idx_ref, h_hbm, n2g_ref, gate_ref, wg_ref, wu_ref, wd_ref, ye_ref, xbuf, xb, sems, *,
              pairs_per_expert):
    s = xbuf.shape[1]
    step = pl.program_id(0) * pairs_per_expert + pl.program_id(1)
    n_steps = pl.num_programs(0) * pairs_per_expert

    def row_copy(tile, i, buf):
        tok = idx_ref[tile * s + i]
        return pltpu.make_async_copy(h_hbm.at[pl.ds(tok, 1), :], xbuf.at[buf, pl.ds(i, 1), :], sems.at[buf])

    def wait_rows(buf):
        pltpu.make_async_copy(h_hbm.at[pl.ds(0, s), :], xbuf.at[buf], sems.at[buf]).wait()

    def ffn(buf):
        x = xbuf[buf]
        xb[...] = (x * lax.rsqrt(jnp.mean(x * x, axis=-1, keepdims=True) + EPS) * n2g_ref[...]).astype(BF16)
        y = None
        for f0 in range(0, EXPERT_FF, FF_CHUNK):
            f1 = min(f0 + FF_CHUNK, EXPERT_FF)
            a = _dot(xb[...], wg_ref[:, f0:f1])
            b = _dot(xb[...], wu_ref[:, f0:f1])
            hid = ((a * jax.nn.sigmoid(a)) * b).astype(BF16)
            part = _dot(hid, wd_ref[f0:f1, :])
            y = part if y is None else y + part
        for k in range(s // LANES):
            row0 = buf * s + k * LANES
            gate = gate_ref[row0 // LANES:row0 // LANES + 1, :]
            col = jnp.transpose(jnp.broadcast_to(gate, (LANES, LANES)))[:, 0:1]
            yk = y[k * LANES:(k + 1) * LANES] * col
            ye_ref[row0:row0 + LANES, :] = _pack_pair(yk[:, :D_MODEL // 2], yk[:, D_MODEL // 2:])

    @pl.when(step == 0)
    def _():
        def one(i, carry):
            row_copy(0, i, 0).start()
            return carry
        lax.fori_loop(0, s, one, 0, unroll=8)

    for i in range(s):
        row_copy(2 * step + 1, i, 1).start(priority=i % 2)
    wait_rows(0)
    ffn(0)

    @pl.when(step + 1 < n_steps)
    def _():
        for i in range(s):
            row_copy(2 * step + 2, i, 0).start(priority=i % 2)

    wait_rows(1)
    ffn(1)


def _expert_ffn(idx_flat, x1, n2g, gates, wg, wu, wd, cap, s=512):
    s = min(s, cap // 2)
    assert s % LANES == 0 and cap % (2 * s) == 0
    pairs = cap // (2 * s)
    g3 = gates.reshape(N_EXPERTS * pairs, 2 * s // LANES, LANES)
    grid_spec = pltpu.PrefetchScalarGridSpec(
        num_scalar_prefetch=1,
        grid=(N_EXPERTS, pairs),
        in_specs=[pl.BlockSpec(memory_space=pl.ANY),
                  pl.BlockSpec((1, D_MODEL), lambda e, j, idx: (0, 0)),
                  pl.BlockSpec((None, 2 * s // LANES, LANES), lambda e, j, idx: (e * pairs + j, 0, 0)),
                  pl.BlockSpec((None, D_MODEL, EXPERT_FF), lambda e, j, idx: (e, 0, 0)),
                  pl.BlockSpec((None, D_MODEL, EXPERT_FF), lambda e, j, idx: (e, 0, 0)),
                  pl.BlockSpec((None, EXPERT_FF, D_MODEL), lambda e, j, idx: (e, 0, 0))],
        out_specs=pl.BlockSpec((2 * s, D_MODEL // 2), lambda e, j, idx: (e * pairs + j, 0)),
        scratch_shapes=[pltpu.VMEM((2, s, D_MODEL), F32), pltpu.VMEM((s, D_MODEL), BF16),
                        pltpu.SemaphoreType.DMA((2,))],
    )
    return pl.pallas_call(
        functools.partial(_ffn_body, pairs_per_expert=pairs),
        grid_spec=grid_spec,
        out_shape=jax.ShapeDtypeStruct((N_EXPERTS * cap, D_MODEL // 2), F32),
        compiler_params=_cparams(("arbitrary", "arbitrary")),
        name="expert_ffn",
    )(idx_flat, x1, n2g, g3, wg, wu, wd)


def _combine_body(lo_ref, np_ref, x1_ref, pos_ref, nfg_ref, expand_ref, lov_ref, ye_hbm, o_ref, win, sems, *, cap,
                  rows_total, n_rows):
    step = pl.program_id(0)
    n_steps = pl.num_programs(0)
    groups = x1_ref.shape[0] // LANES
    w = COMBINE_WIN
    cur = step % 2

    def copies(rr, m, buf, g):
        out = []
        for e in range(N_EXPERTS):
            first = e * cap + lo_ref[e * n_rows + rr]
            intended = (first & (-COMBINE_ALIGN)) + m * w
            actual = pl.multiple_of(jnp.minimum(intended, rows_total - w), COMBINE_ALIGN)
            out.append(pltpu.make_async_copy(ye_hbm.at[pl.ds(actual, w), :], win.at[buf, g, pl.ds(e * w, w), :],
                                             sems.at[buf, g, e]))
        return out

    def wait_windows(buf, g):
        for e in range(N_EXPERTS):
            pltpu.make_async_copy(ye_hbm.at[pl.ds(0, w), :], win.at[buf, g, pl.ds(e * w, w), :],
                                  sems.at[buf, g, e]).wait()

    def contribution(rr, m, buf, g):
        lane = lax.broadcasted_iota(I32, (1, N_EXPERTS), 1)
        first = lane * cap + lov_ref[pl.ds(rr, 1), :]
        intended = (first & jnp.int32(-COMBINE_ALIGN)) + m * w
        actual = jnp.minimum(intended, rows_total - w)
        pos = pos_ref[g * LANES:(g + 1) * LANES, :]
        glob = pos + lane * cap
        rel = glob - intended
        valid = (pos >= 0) & (rel >= 0) & (rel < w)
        local = jnp.where(valid, glob - actual, -1).astype(F32).astype(BF16)
        spread = _dot(local, expand_ref[...])
        col = (lax.broadcasted_iota(I32, (1, N_EXPERTS * w), 1) & (w - 1)).astype(F32)
        onehot = (spread == col).astype(BF16)
        lo_cols, hi_cols = _unpack_pair(win[buf, g])
        return jnp.concatenate([_dot(onehot, lo_cols.astype(BF16)), _dot(onehot, hi_cols.astype(BF16))], axis=1)

    @pl.when(step == 0)
    def _():
        for g in range(groups):
            for cp in copies(g, 0, 0, g):
                cp.start()

    @pl.when(step + 1 < n_steps)
    def _():
        for g in range(groups):
            for cp in copies((step + 1) * groups + g, 0, 1 - cur, g):
                cp.start()

    for g in range(groups):
        rr = step * groups + g
        wait_windows(cur, g)
        acc = x1_ref[g * LANES:(g + 1) * LANES, :] + contribution(rr, 0, cur, g)

        def extra_pass(m, acc, rr=rr, g=g):
            for cp in copies(rr, m, cur, g):
                cp.start()
            wait_windows(cur, g)
            return acc + contribution(rr, m, cur, g)

        acc = lax.fori_loop(1, np_ref[rr], extra_pass, acc)
        o_ref[g * LANES:(g + 1) * LANES, :] = (
            acc * lax.rsqrt(jnp.mean(acc * acc, axis=-1, keepdims=True) + EPS) * nfg_ref[...])


def _combine(lo, x1, pos_t, nfg, ye, cap):
    n_tok = x1.shape[0]
    w = COMBINE_WIN
    rows_total = ye.shape[0]
    n_rows = n_tok // LANES
    groups = next(c for c in (4, 2, 1) if n_rows % c == 0)
    tm = groups * LANES
    nxt =jnp.concatenate([lo[:, 1:], jnp.full((N_EXPERTS, 1), cap, I32)], axis=1)
    span = lo % COMBINE_ALIGN + (nxt - lo)
    n_pass = jnp.maximum(jnp.max((span + w - 1) // w, axis=0), 1).astype(I32)
    ee = np.arange(N_EXPERTS)
    expand = jnp.asarray(ee[:, None] == (np.arange(N_EXPERTS * w) // w)[None, :], BF16)
    grid_spec = pltpu.PrefetchScalarGridSpec(
        num_scalar_prefetch=2,
        grid=(n_rows // groups,),
        in_specs=[pl.BlockSpec((tm, D_MODEL), lambda i, lo_, np_: (i, 0)),
                  pl.BlockSpec((tm, N_EXPERTS), lambda i, lo_, np_: (i, 0)),
                  pl.BlockSpec((1, D_MODEL), lambda i, lo_, np_: (0, 0)),
                  pl.BlockSpec((N_EXPERTS, N_EXPERTS * w), lambda i, lo_, np_: (0, 0)),
                  pl.BlockSpec((n_rows, N_EXPERTS), lambda i, lo_, np_: (0, 0)),
                  pl.BlockSpec(memory_space=pl.ANY)],
        out_specs=pl.BlockSpec((tm, D_MODEL), lambda i, lo_, np_: (i, 0)),
        scratch_shapes=[pltpu.VMEM((2, groups, N_EXPERTS * w, D_MODEL // 2), F32),
                        pltpu.SemaphoreType.DMA((2, groups, N_EXPERTS))],
    )
    return pl.pallas_call(
        functools.partial(_combine_body, cap=cap, rows_total=rows_total, n_rows=n_rows),
        grid_spec=grid_spec,
        out_shape=jax.ShapeDtypeStruct((n_tok, D_MODEL), F32),
        compiler_params=_cparams(("arbitrary",)),
        name="moe_combine_norm",
    )(lo.reshape(-1), n_pass, x1, pos_t, nfg, expand, lo.T, ye)


def _trunk(x, w):
    bsz, seq, d = x.shape
    n_tok = bsz * seq
    x2 = x.reshape(n_tok, d)
    tabs = _fft_tables(seq)

    zc, zret, zg = _inproj(x2, w["norm1_g"], w["w_in"], w["hy_conv_w"], w["hy_conv_b"], seq)
    zc = zc.reshape(bsz, seq, -1)

    h_time, h_abs = _hyena_filter_time(seq, w["hy_w1"], w["hy_b1"], w["hy_freq1"], w["hy_w2"], w["hy_b2"],
                                       w["hy_freq2"], w["hy_w3"])
    kr, ki = _stage_b_filter(_stage_a_real(h_time, tabs), h_abs, tabs)
    yhy = _hyena(zc, kr, ki, w["hy_skip"], tabs)

    o_f, o_b = _retention(zret.reshape(bsz, seq, -1), w["ret_decay_logit"])
    x1, pt = _outproj(x2, yhy, o_f.reshape(n_tok, -1), o_b.reshape(n_tok, -1), zret, zg,
                      w["w_hy_out"], w["w_ret_out"], w["w_o"], w["norm2_g"], w["w_router_t"])

    cap = CAPACITY_FACTOR * n_tok // N_EXPERTS
    rows = n_tok // LANES
    pos, idx, gates, lo = _select(pt.reshape(N_EXPERTS, rows, LANES), cap)
    ye = _expert_ffn(idx.reshape(-1), x1, w["norm2_g"], gates, w["w_gate"], w["w_up"], w["w_down"], cap)
    pos_t = pos.reshape(N_EXPERTS, n_tok).T
    y = _combine(lo[:, :, 0], x1, pos_t, w["norm_f_g"], ye, cap)
    return y.reshape(bsz, seq, d)


def kernel(x_prompt, x_sample, norm1_g, w_in, hy_conv_w, hy_conv_b, hy_w1, hy_b1, hy_freq1, hy_w2, hy_b2, hy_freq2,
           hy_w3, hy_skip, ret_decay_logit, w_hy_out, w_ret_out, w_o, norm2_g, w_router, w_gate, w_up, w_down,
           norm_f_g):
    layer = 0
    w = dict(
        norm1_g=norm1_g[layer].astype(F32)[None], w_in=w_in[layer].astype(BF16),
        hy_conv_w=hy_conv_w[layer].astype(F32), hy_conv_b=hy_conv_b[layer].astype(F32)[None],
        hy_w1=hy_w1[layer], hy_b1=hy_b1[layer], hy_freq1=hy_freq1[layer], hy_w2=hy_w2[layer], hy_b2=hy_b2[layer],
        hy_freq2=hy_freq2[layer], hy_w3=hy_w3[layer], hy_skip=hy_skip[layer],
        ret_decay_logit=ret_decay_logit[layer],
        w_hy_out=w_hy_out[layer].astype(BF16), w_ret_out=w_ret_out[layer].astype(BF16), w_o=w_o[layer].astype(BF16),
        norm2_g=norm2_g[layer].astype(F32)[None], w_router_t=w_router[layer].T.astype(BF16),
        w_gate=w_gate[layer].astype(BF16), w_up=w_up[layer].astype(BF16), w_down=w_down[layer].astype(BF16),
        norm_f_g=norm_f_g.astype(F32)[None],
    )
    return _trunk(x_prompt, w), _trunk(x_sample, w)
```
